```python
import jax, jax.numpy as jnp
from jax import lax
import numpy as np

D_MODEL = 2048
BATCH = 1
SEQ = 8192
DEPTH = 1

D_POOL = D_MODEL // 2
POOL_WINDOWS = (2, 4, 8, 16)
POOL_GROUPS = len(POOL_WINDOWS)
POOL_GW = D_POOL // POOL_GROUPS
D_HGRN = D_MODEL // 2
HGRN_HEAD_DIM = 128
HGRN_HEADS = D_HGRN // HGRN_HEAD_DIM
CHUNK = 64
D_IN = D_POOL + 4 * D_HGRN + 2 * D_MODEL
D_FF = ((8 * D_MODEL // 3 + 127) // 128) * 128
CONV_WIDTH = 3
PLE_DIM = 256
EPS = 1e-6

kernel_name = "hybrid_pool_hgrn2_gated_block"


def rms_norm(x, w):
    xf = x.astype(jnp.float32)
    y = xf * lax.rsqrt(jnp.mean(xf * xf, axis=-1, keepdims=True) + EPS)
    return (y * w.astype(jnp.float32)).astype(x.dtype)


def multiscale_pool(u, w_mix, scale):
    B, S, _ = u.shape
    uf = u.astype(jnp.float32)
    cs0 = jnp.concatenate([jnp.zeros((B, 1, D_POOL), jnp.float32), jnp.cumsum(uf, axis=1)], axis=1)
    pos = jnp.arange(1, S + 1, dtype=jnp.float32)
    outs = []
    for g, w in enumerate(POOL_WINDOWS):
        c0 = cs0[..., g * POOL_GW:(g + 1) * POOL_GW]
        upper = c0[:, 1:]
        lower = jnp.concatenate([jnp.zeros((B, w - 1, POOL_GW), jnp.float32), c0[:, :S - w + 1]], axis=1)
        cnt = jnp.minimum(pos, float(w))[None, :, None]
        outs.append((upper - lower) / cnt)
    pooled = jnp.stack(outs, axis=2)
    d = (pooled - uf.reshape(B, S, POOL_GROUPS, POOL_GW)).astype(u.dtype)
    y = jnp.einsum('bsgc,gcd->bsgd', d, w_mix).reshape(B, S, D_POOL)
    return y * scale


def hgrn2(q, f_logit, inp, g, lb, norm_w):
    B, S, _ = q.shape
    H, Dh, C = HGRN_HEADS, HGRN_HEAD_DIM, CHUNK
    N = S // C
    f32 = jnp.float32
    lbf = lb.astype(f32)
    fl = f_logit.astype(f32)
    f = lbf + (1.0 - lbf) * jax.nn.sigmoid(fl)
    log_f = jnp.log(f)
    k = (1.0 - lbf) * jax.nn.sigmoid(-fl)

    def to_chunks(t):
        t = t.astype(f32).reshape(B, N, C, H, Dh)
        return jnp.transpose(t, (1, 0, 3, 2, 4))

    qc, kc, vc = to_chunks(q), to_chunks(k), to_chunks(inp)
    bc = jnp.cumsum(to_chunks(log_f), axis=-2)
    mask = jnp.tril(jnp.ones((C, C), dtype=bool))[:, :, None]

    def step(state, xs):
        q_, k_, v_, b_ = xs
        o_inter = jnp.einsum('bhtk,bhkv->bhtv', q_ * jnp.exp(b_), state)
        diff = b_[:, :, :, None, :] - b_[:, :, None, :, :]
        decay = jnp.exp(jnp.where(mask, diff, -jnp.inf))
        scores = jnp.einsum('bhtk,bhtsk,bhsk->bhts', q_, decay, k_)
        o = o_inter + jnp.einsum('bhts,bhsv->bhtv', scores, v_)
        b_last = b_[:, :, -1:, :]
        new_state = jnp.exp(b_last[:, :, 0, :])[..., None] * state + \
            jnp.einsum('bhsk,bhsv->bhkv', k_ * jnp.exp(b_last - b_), v_)
        return new_state, o

    state0 = jnp.zeros((B, H, Dh, Dh), f32)
    _, o = lax.scan(step, state0, (qc, kc, vc, bc))
    o = jnp.transpose(o, (1, 0, 3, 2, 4)).reshape(B, S, H, Dh)
    o = o * lax.rsqrt(jnp.mean(o * o, axis=-1, keepdims=True) + EPS)
    o = o * norm_w.astype(f32).reshape(H, Dh)
    o = o.reshape(B, S, D_HGRN) * jax.nn.silu(g.astype(f32))
    return o.astype(q.dtype)


def causal_depthwise_conv(z, w, b):
    S = z.shape[1]
    zp = jnp.pad(z, ((0, 0), (CONV_WIDTH - 1, 0), (0, 0)))
    y = b
    for j in range(CONV_WIDTH):
        y = y + w[j] * zp[:, j:j + S]
    return y


def setup_inputs(seed: int = 0) -> dict:
    key = jax.random.key(seed)
    ks = jax.random.split(key, 24)
    f32 = jnp.float32

    def nrm(k, shape, scale):
        return jax.random.normal(k, shape, f32) * scale

    L = DEPTH
    return {
        "x": nrm(ks[0], (BATCH, SEQ, D_MODEL), 1.0),
        "p": nrm(ks[1], (DEPTH, BATCH, SEQ, PLE_DIM), 1.0),
        "ln_mix_w": 1.0 + nrm(ks[2], (L, D_MODEL), 0.02),
        "w_in": nrm(ks[3], (L, D_MODEL, D_IN), D_MODEL ** -0.5),
        "pool_mix_w": nrm(ks[4], (L, POOL_GROUPS, POOL_GW, POOL_GW), POOL_GW ** -0.5),
        "pool_scale": 1.0 + nrm(ks[5], (L, D_POOL), 0.02),
        "hgrn_lb_logits": nrm(ks[6], (DEPTH + 1, D_HGRN), 0.1),
        "hgrn_norm_w": 1.0 + nrm(ks[7], (L, D_HGRN), 0.02),
        "w_branch_a": nrm(ks[8], (L, D_POOL, D_MODEL), D_POOL ** -0.5),
        "w_branch_b": nrm(ks[9], (L, D_HGRN, D_MODEL), D_HGRN ** -0.5),
        "w_out": nrm(ks[10], (L, D_MODEL, D_MODEL), D_MODEL ** -0.5),
        "ln_ffn_w": 1.0 + nrm(ks[11], (L, D_MODEL), 0.02),
        "w_up": nrm(ks[12], (L, D_MODEL, 2 * D_FF), D_MODEL ** -0.5),
        "conv_w": nrm(ks[13], (L, CONV_WIDTH, 2 * D_FF), CONV_WIDTH ** -0.5),
        "conv_b": nrm(ks[14], (L, 2 * D_FF), 0.01),
        "w_down": nrm(ks[15], (L, D_FF, D_MODEL), D_FF ** -0.5),
        "ln_ple_w": 1.0 + nrm(ks[16], (L, D_MODEL), 0.02),
        "w_ple_gate": nrm(ks[17], (L, D_MODEL, D_MODEL), D_MODEL ** -0.5),
        "w_ple": nrm(ks[18], (L, PLE_DIM, D_MODEL), PLE_DIM ** -0.5),
        "ln_final_w": 1.0 + nrm(ks[19], (D_MODEL,), 0.02),
    }


def reference(x, p, ln_mix_w, w_in, pool_mix_w, pool_scale, hgrn_lb_logits, hgrn_norm_w,
              w_branch_a, w_branch_b, w_out, ln_ffn_w, w_up, conv_w, conv_b, w_down,
              ln_ple_w, w_ple_gate, w_ple, ln_final_w):
    lb_all = jnp.cumsum(jax.nn.softmax(hgrn_lb_logits.astype(jnp.float32), axis=0), axis=0)
    split_idx = [D_POOL + j * D_HGRN for j in range(5)] + [D_POOL + 4 * D_HGRN + D_MODEL]
    for i in range(DEPTH):
        h = rms_norm(x, ln_mix_w[i])
        proj = h @ w_in[i]
        u_pool, q, f_logit, inp, g, gate_a, gate_b = jnp.split(proj, split_idx, axis=-1)
        y_a = multiscale_pool(u_pool, pool_mix_w[i], pool_scale[i]) @ w_branch_a[i]
        y_b = hgrn2(q, f_logit, inp, g, lb_all[i], hgrn_norm_w[i]) @ w_branch_b[i]
        merged = jax.nn.sigmoid(gate_a) * y_a + jax.nn.sigmoid(gate_b) * y_b
        x = x + merged @ w_out[i]
        h2 = rms_norm(x, ln_ffn_w[i])
        u = causal_depthwise_conv(h2 @ w_up[i], conv_w[i], conv_b[i])
        u_gate, u_val = jnp.split(u, 2, axis=-1)
        x = x + (jax.nn.silu(u_gate) * u_val) @ w_down[i]
        h3 = rms_norm(x, ln_ple_w[i])
        x = x + jax.nn.sigmoid(h3 @ w_ple_gate[i]) * (p[i] @ w_ple[i])
    return rms_norm(x, ln_final_w)
```

```python
import functools

import jax
import jax.numpy as jnp
from jax import lax
from jax.experimental import pallas as pl
from jax.experimental.pallas import tpu as pltpu

F32 = jnp.float32
BF16 = jnp.bfloat16

EPS = 1e-6
POOL_WINDOWS = (2, 4, 8, 16)
POOL_HALO = 16
HEAD_DIM = 128
CHUNK = 64
SUB = 16
NSUB = CHUNK // SUB
CONV_WIDTH = 3
CARRY_ROWS = 8
LANE = 128
V7X_VMEM_BYTES = 64 * 1024 * 1024
VMEM_LIMIT = V7X_VMEM_BYTES - 8 * 1024 * 1024


def _dot(a, b):
    return jnp.dot(a, b, preferred_element_type=F32)


def _dot_nt(a, b):
    return lax.dot_general(a, b, (((1,), (1,)), ((), ())), preferred_element_type=F32)


def _dot_tn(a, b):
    return lax.dot_general(a, b, (((0,), (0,)), ((), ())), preferred_element_type=F32)


def _rms(x, w):
    return x * lax.rsqrt(jnp.mean(x * x, axis=-1, keepdims=True) + EPS) * w


def _params(*sem):
    return pltpu.CompilerParams(dimension_semantics=sem, vmem_limit_bytes=VMEM_LIMIT)


def _resident(shape):
    return pl.BlockSpec(shape, lambda *_: (0,) * len(shape), pipeline_mode=pl.Buffered(1))


def _inproj_kernel(x_ref, lnw_ref, w_ref, o_ref, f_ref, h_ref, *, f_block):
    n = pl.program_id(1)

    @pl.when(n == 0)
    def _():
        h_ref[...] = _rms(x_ref[...], lnw_ref[...]).astype(BF16)

    acc = _dot(h_ref[...], w_ref[...])
    o_ref[...] = acc.astype(o_ref.dtype)

    @pl.when(n == f_block)
    def _():
        f_ref[...] = acc


def _inproj(x, ln_w, w_in, *, bm, bn, f_block):
    s, d = x.shape
    d_in = w_in.shape[1]
    return pl.pallas_call(
        functools.partial(_inproj_kernel, f_block=f_block),
        grid=(s // bm, d_in // bn),
        in_specs=[
            pl.BlockSpec((bm, d), lambda m, n: (m, 0)),
            pl.BlockSpec((1, d), lambda m, n: (0, 0)),
            pl.BlockSpec((d, bn), lambda m, n: (0, n)),
        ],
        out_specs=[
            pl.BlockSpec((bm, bn), lambda m, n: (m, n)),
            pl.BlockSpec((bm, bn), lambda m, n: (m, 0)),
        ],
        out_shape=[
            jax.ShapeDtypeStruct((s, d_in), BF16),
            jax.ShapeDtypeStruct((s, bn), F32),
        ],
        scratch_shapes=[pltpu.VMEM((bm, d), BF16)],
        compiler_params=_params("arbitrary", "arbitrary"),
        name="inproj",
    )(x, ln_w, w_in)


def _pool_kernel(u_ref, halo_ref, ga_ref, mix_ref, scale_ref, wa_ref, o_ref, feat_ref, *, bm):
    m = pl.program_id(0)
    n = pl.program_id(1)

    @pl.when(n == 0)
    def _():
        u = u_ref[...].astype(F32)
        halo = halo_ref[...].astype(F32) * (m > 0).astype(F32)
        ext = jnp.concatenate([halo, u], axis=0)
        gw = u.shape[1] // len(POOL_WINDOWS)
        pos = m * bm + lax.broadcasted_iota(jnp.int32, (bm, 1), 0) + 1
        run = ext
        width = 1
        feats = []
        for g, w in enumerate(POOL_WINDOWS):
            while width < w:
                run = run + pltpu.roll(run, width, axis=0)
                width *= 2
            cnt = jnp.minimum(pos, w).astype(F32)
            win = run[POOL_HALO:, g * gw:(g + 1) * gw]
            d = win / cnt - u[:, g * gw:(g + 1) * gw]
            y = _dot(d.astype(BF16), mix_ref[g])
            feats.append(y * scale_ref[:, g * gw:(g + 1) * gw])
        feat_ref[...] = jnp.concatenate(feats, axis=1).astype(BF16)

    y_a = _dot(feat_ref[...], wa_ref[...])
    o_ref[...] = (jax.nn.sigmoid(ga_ref[...].astype(F32)) * y_a).astype(o_ref.dtype)


def _pool_branch(proj, mix_w, scale, w_a, *, bm, bn, ga_block0):
    s = proj.shape[0]
    c = w_a.shape[0]
    d = w_a.shape[1]
    g, gw, _ = mix_w.shape
    halo_per_tile = bm // POOL_HALO
    return pl.pallas_call(
        functools.partial(_pool_kernel, bm=bm),
        grid=(s // bm, d // bn),
        in_specs=[
            pl.BlockSpec((bm, c), lambda m, n: (m, 0)),
            pl.BlockSpec((POOL_HALO, c), lambda m, n: (jnp.maximum(m * halo_per_tile - 1, 0), 0)),
            pl.BlockSpec((bm, bn), lambda m, n: (m, ga_block0 + n)),
            _resident((g, gw, gw)),
            _resident((1, c)),
            pl.BlockSpec((c, bn), lambda m, n: (0, n)),
        ],
        out_specs=pl.BlockSpec((bm, bn), lambda m, n: (m, n)),
        out_shape=jax.ShapeDtypeStruct((s, d), BF16),
        scratch_shapes=[pltpu.VMEM((bm, c), BF16)],
        compiler_params=_params("arbitrary", "arbitrary"),
        name="pool_branch",
    )(proj, proj, proj, mix_w, scale, w_a)


def _split3(x):
    hi = x.astype(BF16)
    r = x - hi.astype(F32)
    mid = r.astype(BF16)
    lo = (r - mid.astype(F32)).astype(BF16)
    return hi, mid, lo


def _hgrn_chunk(c, q_ref, f_ref, i_ref, g_ref, tri_ref, wsel_ref, o_ref, st_ref, lb, norm_w):
    r0 = pl.multiple_of(c * CHUNK, CHUNK)
    rows = pl.ds(r0, CHUNK)
    d = q_ref.shape[1]
    heads = d // HEAD_DIM

    fl = f_ref[rows, :]
    q = q_ref[rows, :].astype(F32)
    v = i_ref[rows, :]
    one_m_lb = 1.0 - lb
    fgate = lb + one_m_lb * jax.nn.sigmoid(fl)
    kk = one_m_lb * jax.nn.sigmoid(-fl)
    log_f = jnp.log(fgate)

    tri = tri_ref[...]
    hi, mid, lo = _split3(log_f)
    b = _dot(tri, hi) + _dot(tri, mid) + _dot(tri, lo)
    b_last = b[CHUNK - 1:CHUNK, :]

    q_in = (q * jnp.exp(b)).astype(BF16)
    k_st = (kk * jnp.exp(b_last - b)).astype(BF16)

    b_end = jnp.concatenate(
        [jnp.broadcast_to(b[(j + 1) * SUB - 1:(j + 1) * SUB, :], (SUB, d)) for j in range(NSUB)], axis=0)
    k_hat = kk * jnp.exp(b_end - b)
    sub_id = lax.broadcasted_iota(jnp.int32, (CHUNK, 1), 0) // SUB
    q_from = []
    for j in range(NSUB - 1):
        lo_row = (j + 1) * SUB
        bj = b[lo_row - 1:lo_row, :]
        part = q[lo_row:, :] * jnp.exp(b[lo_row:, :] - bj)
        q_from.append(jnp.concatenate([jnp.zeros((lo_row, d), F32), part], axis=0).astype(BF16))
    k_src = [jnp.where(sub_id == j, k_hat, 0.0).astype(BF16) for j in range(NSUB - 1)]

    per_head = [[] for _ in range(heads)]
    for r in range(NSUB):
        qb = q[r * SUB:(r + 1) * SUB, :]
        bb = b[r * SUB:(r + 1) * SUB, :]
        pieces = []
        for s in range(SUB):
            row = r * SUB + s
            e = jnp.exp(jnp.minimum(bb - b[row:row + 1, :], 0.0))
            pieces.append((qb * kk[row:row + 1, :] * e).astype(BF16))
        for h in range(heads):
            per_head[h].append(jnp.concatenate(
                [p[:, h * HEAD_DIM:(h + 1) * HEAD_DIM] for p in pieces], axis=1))
    a_big = jnp.concatenate([jnp.concatenate(blocks, axis=0) for blocks in per_head], axis=0)
    diag = _dot(a_big, wsel_ref[...])

    t_id = lax.broadcasted_iota(jnp.int32, (CHUNK, CHUNK), 0)
    s_id = lax.broadcasted_iota(jnp.int32, (CHUNK, CHUNK), 1)
    diag_mask = (t_id // SUB == s_id // SUB) & (s_id <= t_id)

    outs = []
    for h in range(heads):
        hs = slice(h * HEAD_DIM, (h + 1) * HEAD_DIM)
        q_cat = jnp.concatenate([qf[:, hs] for qf in q_from], axis=1)
        k_cat = jnp.concatenate([ks[:, hs] for ks in k_src], axis=1)
        scores = _dot_nt(q_cat, k_cat) + jnp.where(diag_mask, diag[h * CHUNK:(h + 1) * CHUNK, :], 0.0)
        v_h = v[:, hs]
        state_t = st_ref[h]
        o_h = _dot_nt(q_in[:, hs], state_t.astype(BF16)) + _dot(scores.astype(BF16), v_h)
        st_ref[h] = state_t * jnp.exp(b_last[:, hs]) + _dot_tn(v_h, k_st[:, hs])
        o_h = o_h * lax.rsqrt(jnp.mean(o_h * o_h, axis=-1, keepdims=True) + EPS)
        outs.append(o_h)
    o = jnp.concatenate(outs, axis=1) * norm_w
    g = g_ref[rows, :].astype(F32)
    o_ref[rows, :] = (o * (g * jax.nn.sigmoid(g))).astype(o_ref.dtype)


def _hgrn_kernel(q_ref, f_ref, i_ref, g_ref, lbl_ref, nw_ref, tri_ref, wsel_ref, o_ref, st_ref, *, layer):
    @pl.when(pl.program_id(0) == 0)
    def _():
        st_ref[...] = jnp.zeros_like(st_ref)

    logits = lbl_ref[...]
    e = jnp.exp(logits - jnp.max(logits, axis=0, keepdims=True))
    lb = jnp.sum(e[:layer + 1, :], axis=0, keepdims=True) / jnp.sum(e, axis=0, keepdims=True)
    norm_w = nw_ref[...]

    def body(c, carry):
        _hgrn_chunk(c, q_ref, f_ref, i_ref, g_ref, tri_ref, wsel_ref, o_ref, st_ref, lb, norm_w)
        return carry

    lax.fori_loop(0, q_ref.shape[0] // CHUNK, body, 0)


def _hgrn(proj, f_logit, lb_logits, norm_w, *, bt, layer, q_block, i_block, g_block):
    s = proj.shape[0]
    d = f_logit.shape[1]
    heads = d // HEAD_DIM
    tri = (lax.broadcasted_iota(jnp.int32, (CHUNK, CHUNK), 0)
           >= lax.broadcasted_iota(jnp.int32, (CHUNK, CHUNK), 1)).astype(BF16)
    src = lax.broadcasted_iota(jnp.int32, (SUB * HEAD_DIM, CHUNK), 0) // HEAD_DIM
    col = lax.broadcasted_iota(jnp.int32, (SUB * HEAD_DIM, CHUNK), 1) % SUB
    wsel = (src == col).astype(BF16)
    nl = lb_logits.shape[0]
    return pl.pallas_call(
        functools.partial(_hgrn_kernel, layer=layer),
        grid=(s // bt,),
        in_specs=[
            pl.BlockSpec((bt, d), lambda t: (t, q_block)),
            pl.BlockSpec((bt, d), lambda t: (t, 0)),
            pl.BlockSpec((bt, d), lambda t: (t, i_block)),
            pl.BlockSpec((bt, d), lambda t: (t, g_block)),
            _resident((nl, d)),
            _resident((1, d)),
            _resident((CHUNK, CHUNK)),
            _resident((SUB * HEAD_DIM, CHUNK)),
        ],
        out_specs=pl.BlockSpec((bt, d), lambda t: (t, 0)),
        out_shape=jax.ShapeDtypeStruct((s, d), BF16),
        scratch_shapes=[pltpu.VMEM((heads, HEAD_DIM, HEAD_DIM), F32)],
        compiler_params=_params("arbitrary"),
        name="hgrn",
    )(proj, f_logit, proj, proj, lb_logits, norm_w, tri, wsel)


def _merge_kernel(og_ref, gb0_ref, gb1_ref, ya_ref, x_ref, wb_ref, wo_ref, o_ref):
    y_b = _dot(og_ref[...], wb_ref[...])
    gate_b = jnp.concatenate([gb0_ref[...], gb1_ref[...]], axis=1).astype(F32)
    merged = ya_ref[...].astype(F32) + jax.nn.sigmoid(gate_b) * y_b
    o_ref[...] = x_ref[...] + _dot(merged.astype(BF16), wo_ref[...])


def _merge(og, proj, ya, x, w_b, w_out, *, bm, gb_block0):
    s, d = x.shape
    c = og.shape[1]
    half = d // 2
    return pl.pallas_call(
        _merge_kernel,
        grid=(s // bm,),
        in_specs=[
            pl.BlockSpec((bm, c), lambda m: (m, 0)),
            pl.BlockSpec((bm, half), lambda m: (m, gb_block0)),
            pl.BlockSpec((bm, half), lambda m: (m, gb_block0 + 1)),
            pl.BlockSpec((bm, d), lambda m: (m, 0)),
            pl.BlockSpec((bm, d), lambda m: (m, 0)),
            _resident((c, d)),
            _resident((d, d)),
        ],
        out_specs=pl.BlockSpec((bm, d), lambda m: (m, 0)),
        out_shape=jax.ShapeDtypeStruct((s, d), F32),
        compiler_params=_params("arbitrary"),
        name="merge_out",
    )(og, proj, proj, ya, x, w_b, w_out)


def _causal_conv(z, prev, w_ref, b_ref):
    ext = jnp.concatenate([prev, z], axis=0)
    y = b_ref[...] + w_ref[CONV_WIDTH - 1:CONV_WIDTH, :] * z
    for back in range(1, CONV_WIDTH):
        shifted = pltpu.roll(ext, back, axis=0)[CARRY_ROWS:, :]
        y = y + w_ref[CONV_WIDTH - 1 - back:CONV_WIDTH - back, :] * shifted
    return y


def _ffn_kernel(x_ref, lnw_ref, wug_ref, wuv_ref, cwg_ref, cwv_ref, cbg_ref, cbv_ref, wd_ref,
                o_ref, h_ref, cg_ref, cv_ref):
    m = pl.program_id(0)
    j = pl.program_id(1)

    @pl.when(j == 0)
    def _():
        x = x_ref[...]
        h_ref[...] = _rms(x, lnw_ref[...]).astype(BF16)
        o_ref[...] = x

    @pl.when(m == 0)
    def _():
        cg_ref[j] = jnp.zeros(cg_ref.shape[1:], F32)
        cv_ref[j] = jnp.zeros(cv_ref.shape[1:], F32)

    h = h_ref[...]
    zg = _dot(h, wug_ref[...])
    zv = _dot(h, wuv_ref[...])
    ug = _causal_conv(zg, cg_ref[j], cwg_ref, cbg_ref)
    uv = _causal_conv(zv, cv_ref[j], cwv_ref, cbv_ref)
    bm = zg.shape[0]
    cg_ref[j] = zg[bm - CARRY_ROWS:, :]
    cv_ref[j] = zv[bm - CARRY_ROWS:, :]
    act = (ug * jax.nn.sigmoid(ug) * uv).astype(BF16)
    o_ref[...] += _dot(act, wd_ref[...])


def _ffn(x, ln_w, w_up, conv_w, conv_b, w_down, *, bm, tf):
    s, d = x.shape
    dff = w_down.shape[0]
    nj = dff // tf
    return pl.pallas_call(
        _ffn_kernel,
        grid=(s // bm, nj),
        in_specs=[
            pl.BlockSpec((bm, d), lambda m, j: (m, 0)),
            pl.BlockSpec((1, d), lambda m, j: (0, 0)),
            pl.BlockSpec((d, tf), lambda m, j: (0, j)),
            pl.BlockSpec((d, tf), lambda m, j: (0, nj + j)),
            pl.BlockSpec((CONV_WIDTH, tf), lambda m, j: (0, j)),
            pl.BlockSpec((CONV_WIDTH, tf), lambda m, j: (0, nj + j)),
            pl.BlockSpec((1, tf), lambda m, j: (0, j)),
            pl.BlockSpec((1, tf), lambda m, j: (0, nj + j)),
            pl.BlockSpec((tf, d), lambda m, j: (j, 0)),
        ],
        out_specs=pl.BlockSpec((bm, d), lambda m, j: (m, 0)),
        out_shape=jax.ShapeDtypeStruct((s, d), F32),
        scratch_shapes=[
            pltpu.VMEM((bm, d), BF16),
            pltpu.VMEM((nj, CARRY_ROWS, tf), F32),
            pltpu.VMEM((nj, CARRY_ROWS, tf), F32),
        ],
        compiler_params=_params("arbitrary", "arbitrary"),
        name="ffn",
    )(x, ln_w, w_up, w_up, conv_w, conv_w, conv_b, conv_b, w_down)


def _ple_kernel(x_ref, p_ref, lnw_ref, wg_ref, wp_ref, lnf_ref, o_ref, *, final):
    x = x_ref[...]
    h = _rms(x, lnw_ref[...]).astype(BF16)
    gate = jax.nn.sigmoid(_dot(h, wg_ref[...]))
    emb = _dot(p_ref[...].astype(BF16), wp_ref[...])
    y = x + gate * emb
    o_ref[...] = _rms(y, lnf_ref[...]) if final else y


def _ple(x, p, ln_w, w_gate, w_ple, ln_final, *, bm, final):
    s, d = x.shape
    e = p.shape[1]
    return pl.pallas_call(
        functools.partial(_ple_kernel, final=final),
        grid=(s // bm,),
        in_specs=[
            pl.BlockSpec((bm, d), lambda m: (m, 0)),
            pl.BlockSpec((bm, e), lambda m: (m, 0)),
            _resident((1, d)),
            _resident((d, d)),
            _resident((e, d)),
            _resident((1, d)),
        ],
        out_specs=pl.BlockSpec((bm, d), lambda m: (m, 0)),
        out_shape=jax.ShapeDtypeStruct((s, d), F32),
        compiler_params=_params("arbitrary"),
        name="ple_final",
    )(x, p, ln_w, w_gate, w_ple, ln_final)


def _pad_ff(w_up, conv_w, conv_b, w_down, tf):
    dff = w_down.shape[0]
    pad = (-dff) % tf
    if pad == 0:
        return w_up, conv_w, conv_b, w_down

    def pad_halves(a):
        g, v = a[..., :dff], a[..., dff:]
        widths = [(0, 0)] * (a.ndim - 1) + [(0, pad)]
        return jnp.concatenate([jnp.pad(g, widths), jnp.pad(v, widths)], axis=-1)

    return pad_halves(w_up), pad_halves(conv_w), pad_halves(conv_b), jnp.pad(w_down, ((0, pad), (0, 0)))


def kernel(x, p, ln_mix_w, w_in, pool_mix_w, pool_scale, hgrn_lb_logits, hgrn_norm_w, w_branch_a,
           w_branch_b, w_out, ln_ffn_w, w_up, conv_w, conv_b, w_down, ln_ple_w, w_ple_gate, w_ple,
           ln_final_w):
    batch, seq, d = x.shape
    depth = w_in.shape[0]
    c_pool = w_branch_a.shape[1]
    c_hgrn = w_branch_b.shape[1]
    blk = c_hgrn
    assert c_pool == blk and d == 2 * blk, "column blocks of the combined projection must line up"
    assert seq % 1024 == 0
    tf = 512
    outs = []
    for bi in range(batch):
        xb = x[bi]
        for i in range(depth):
            proj, f_logit = _inproj(xb, ln_mix_w[i][None], w_in[i].astype(BF16),
                                    bm=1024, bn=blk, f_block=2)
            ya = _pool_branch(proj, pool_mix_w[i].astype(BF16), pool_scale[i][None],
                              w_branch_a[i].astype(BF16), bm=1024, bn=blk, ga_block0=5)
            og = _hgrn(proj, f_logit, hgrn_lb_logits, hgrn_norm_w[i][None], bt=256, layer=i,
                       q_block=1, i_block=3, g_block=4)
            xb = _merge(og, proj, ya, xb, w_branch_b[i].astype(BF16), w_out[i].astype(BF16),
                        bm=512, gb_block0=7)
            wu, cw, cb, wd = _pad_ff(w_up[i], conv_w[i], conv_b[i][None], w_down[i], tf)
            xb = _ffn(xb, ln_ffn_w[i][None], wu.astype(BF16), cw, cb, wd.astype(BF16), bm=512, tf=tf)
            xb = _ple(xb, p[i, bi], ln_ple_w[i][None], w_ple_gate[i].astype(BF16),
                      w_ple[i].astype(BF16), ln_final_w[None], bm=512, final=(i == depth - 1))
        outs.append(xb)
    return jnp.stack(outs, axis=0)
```

```python
import functools

import jax
import jax.numpy as jnp
from jax import lax
from jax.experimental import pallas as pl
from jax.experimental.pallas import tpu as pltpu

F32 = jnp.float32
BF16 = jnp.bfloat16

EPS = 1e-6
POOL_WINDOWS = (2, 4, 8, 16)
POOL_HALO = 16
HEAD_DIM = 128
CHUNK = 64
SUB = 16
NSUB = CHUNK // SUB
CONV_WIDTH = 3
CARRY_ROWS = 8
LANE = 128
LOG2_E = 1.4426950408889634
MILD_DECAY_LOG2 = -100.0
V7X_VMEM_BYTES = 64 * 1024 * 1024
VMEM_LIMIT = V7X_VMEM_BYTES - 8 * 1024 * 1024


def _dot(a, b):
    return jnp.dot(a, b, preferred_element_type=F32)


def _dot_nt(a, b):
    return lax.dot_general(a, b, (((1,), (1,)), ((), ())), preferred_element_type=F32)


def _dot_tn(a, b):
    return lax.dot_general(a, b, (((0,), (0,)), ((), ())), preferred_element_type=F32)


def _rms(x, w):
    return x * lax.rsqrt(jnp.mean(x * x, axis=-1, keepdims=True) + EPS) * w


def _params(*sem):
    return pltpu.CompilerParams(dimension_semantics=sem, vmem_limit_bytes=VMEM_LIMIT)


def _resident(shape):
    return pl.BlockSpec(shape, lambda *_: (0,) * len(shape), pipeline_mode=pl.Buffered(1))


def _inproj_kernel(x_ref, lnw_ref, w_ref, o_ref, f_ref, h_ref, *, f_block):
    n = pl.program_id(1)

    @pl.when(n == 0)
    def _():
        h_ref[...] = _rms(x_ref[...], lnw_ref[...]).astype(BF16)

    acc = _dot(h_ref[...], w_ref[...])
    o_ref[...] = acc.astype(o_ref.dtype)

    @pl.when(n == f_block)
    def _():
        f_ref[...] = acc


def _inproj(x, ln_w, w_in, *, bm, bn, f_block):
    s, d = x.shape
    d_in = w_in.shape[1]
    return pl.pallas_call(
        functools.partial(_inproj_kernel, f_block=f_block),
        grid=(s // bm, d_in // bn),
        in_specs=[
            pl.BlockSpec((bm, d), lambda m, n: (m, 0)),
            pl.BlockSpec((1, d), lambda m, n: (0, 0)),
            pl.BlockSpec((d, bn), lambda m, n: (0, n)),
        ],
        out_specs=[
            pl.BlockSpec((bm, bn), lambda m, n: (m, n)),
            pl.BlockSpec((bm, bn), lambda m, n: (m, 0)),
        ],
        out_shape=[
            jax.ShapeDtypeStruct((s, d_in), BF16),
            jax.ShapeDtypeStruct((s, bn), F32),
        ],
        scratch_shapes=[pltpu.VMEM((bm, d), BF16)],
        compiler_params=_params("arbitrary", "arbitrary"),
        name="inproj",
    )(x, ln_w, w_in)


def _pool_kernel(u_ref, halo_ref, ga_ref, mix_ref, scale_ref, wa_ref, o_ref, feat_ref, *, bm):
    m = pl.program_id(0)
    n = pl.program_id(1)

    @pl.when(n == 0)
    def _():
        u = u_ref[...].astype(F32)
        halo = halo_ref[...].astype(F32) * (m > 0).astype(F32)
        ext = jnp.concatenate([halo, u], axis=0)
        gw = u.shape[1] // len(POOL_WINDOWS)
        pos = m * bm + lax.broadcasted_iota(jnp.int32, (bm, 1), 0) + 1
        run = ext
        width = 1
        feats = []
        for g, w in enumerate(POOL_WINDOWS):
            while width < w:
                run = run + pltpu.roll(run, width, axis=0)
                width *= 2
            cnt = jnp.minimum(pos, w).astype(F32)
            win = run[POOL_HALO:, g * gw:(g + 1) * gw]
            d = win / cnt - u[:, g * gw:(g + 1) * gw]
            y = _dot(d.astype(BF16), mix_ref[g])
            feats.append(y * scale_ref[:, g * gw:(g + 1) * gw])
        feat_ref[...] = jnp.concatenate(feats, axis=1).astype(BF16)

    y_a = _dot(feat_ref[...], wa_ref[...])
    o_ref[...] = (jax.nn.sigmoid(ga_ref[...].astype(F32)) * y_a).astype(o_ref.dtype)


def _pool_branch(proj, mix_w, scale, w_a, *, bm, bn, ga_block0):
    s = proj.shape[0]
    c = w_a.shape[0]
    d = w_a.shape[1]
    g, gw, _ = mix_w.shape
    halo_per_tile = bm // POOL_HALO
    return pl.pallas_call(
        functools.partial(_pool_kernel, bm=bm),
        grid=(s // bm, d // bn),
        in_specs=[
            pl.BlockSpec((bm, c), lambda m, n: (m, 0)),
            pl.BlockSpec((POOL_HALO, c), lambda m, n: (jnp.maximum(m * halo_per_tile - 1, 0), 0)),
            pl.BlockSpec((bm, bn), lambda m, n: (m, ga_block0 + n)),
            _resident((g, gw, gw)),
            _resident((1, c)),
            pl.BlockSpec((c, bn), lambda m, n: (0, n)),
        ],
        out_specs=pl.BlockSpec((bm, bn), lambda m, n: (m, n)),
        out_shape=jax.ShapeDtypeStruct((s, d), BF16),
        scratch_shapes=[pltpu.VMEM((bm, c), BF16)],
        compiler_params=_params("arbitrary", "arbitrary"),
        name="pool_branch",
    )(proj, proj, proj, mix_w, scale, w_a)


def _split3(x):
    hi = x.astype(BF16)
    r = x - hi.astype(F32)
    mid = r.astype(BF16)
    lo = (r - mid.astype(F32)).astype(BF16)
    return hi, mid, lo


def _gates(fl, lb):
    one_m_lb = 1.0 - lb
    sg = jax.nn.sigmoid(fl)
    log2_f = jnp.log(lb + one_m_lb * sg) * LOG2_E
    return log2_f, one_m_lb * (1.0 - sg)


def _chunk_cumsum(tri, x):
    hi, mid, lo = _split3(x)
    return _dot(tri, hi) + _dot(tri, mid) + _dot(tri, lo)


def _head_norm_gate(o_heads, norm_w, g):
    o = jnp.concatenate(
        [o_h * lax.rsqrt(jnp.mean(o_h * o_h, axis=-1, keepdims=True) + EPS) for o_h in o_heads], axis=1)
    return o * norm_w * (g * jax.nn.sigmoid(g))


def _hgrn_chunk_any_decay(c, q_ref, f_ref, i_ref, g_ref, tri_ref, wsel_ref, o_ref, st_ref, lb, norm_w):
    r0 = pl.multiple_of(c * CHUNK, CHUNK)
    rows = pl.ds(r0, CHUNK)
    d = q_ref.shape[1]
    heads = d // HEAD_DIM

    q = q_ref[rows, :].astype(F32)
    v = i_ref[rows, :]
    log2_f, kk = _gates(f_ref[rows, :], lb)
    b = _chunk_cumsum(tri_ref[:CHUNK, :CHUNK], log2_f)
    b_last = b[CHUNK - 1:CHUNK, :]

    q_in = (q * jnp.exp2(b)).astype(BF16)
    k_st = (kk * jnp.exp2(b_last - b)).astype(BF16)

    b_end = jnp.concatenate(
        [jnp.broadcast_to(b[(j + 1) * SUB - 1:(j + 1) * SUB, :], (SUB, d)) for j in range(NSUB)], axis=0)
    k_hat = kk * jnp.exp2(b_end - b)
    sub_id = lax.broadcasted_iota(jnp.int32, (CHUNK, 1), 0) // SUB
    q_from = []
    for j in range(NSUB - 1):
        lo_row = (j + 1) * SUB
        bj = b[lo_row - 1:lo_row, :]
        part = q[lo_row:, :] * jnp.exp2(b[lo_row:, :] - bj)
        q_from.append(jnp.concatenate([jnp.zeros((lo_row, d), F32), part], axis=0).astype(BF16))
    k_src = [jnp.where(sub_id == j, k_hat, 0.0).astype(BF16) for j in range(NSUB - 1)]

    per_head = [[] for _ in range(heads)]
    for r in range(NSUB):
        qb = q[r * SUB:(r + 1) * SUB, :]
        bb = b[r * SUB:(r + 1) * SUB, :]
        pieces = []
        for s in range(SUB):
            row = r * SUB + s
            e = jnp.exp2(jnp.minimum(bb - b[row:row + 1, :], 0.0))
            pieces.append((qb * kk[row:row + 1, :] * e).astype(BF16))
        for h in range(heads):
            per_head[h].append(jnp.concatenate(
                [p[:, h * HEAD_DIM:(h + 1) * HEAD_DIM] for p in pieces], axis=1))
    a_big = jnp.concatenate([jnp.concatenate(blocks, axis=0) for blocks in per_head], axis=0)
    diag = _dot(a_big, wsel_ref[...])

    t_id = lax.broadcasted_iota(jnp.int32, (CHUNK, CHUNK), 0)
    s_id = lax.broadcasted_iota(jnp.int32, (CHUNK, CHUNK), 1)
    diag_mask = (t_id // SUB == s_id // SUB) & (s_id <= t_id)

    outs = []
    for h in range(heads):
        hs = slice(h * HEAD_DIM, (h + 1) * HEAD_DIM)
        q_cat = jnp.concatenate([qf[:, hs] for qf in q_from], axis=1)
        k_cat = jnp.concatenate([ks[:, hs] for ks in k_src], axis=1)
        scores = _dot_nt(q_cat, k_cat) + jnp.where(diag_mask, diag[h * CHUNK:(h + 1) * CHUNK, :], 0.0)
        v_h = v[:, hs]
        state_t = st_ref[h]
        o_h = _dot_nt(q_in[:, hs], state_t.astype(BF16)) + _dot(scores.astype(BF16), v_h)
        st_ref[h] = state_t * jnp.exp2(b_last[:, hs]) + _dot_tn(v_h, k_st[:, hs])
        outs.append(o_h)
    o_ref[rows, :] = _head_norm_gate(outs, norm_w, g_ref[rows, :].astype(F32)).astype(o_ref.dtype)


def _hgrn_tile_mild_decay(b, kk, q_ref, i_ref, g_ref, o_ref, st_ref, norm_w):
    bt, d = b.shape
    heads = d // HEAD_DIM
    n_chunks = bt // CHUNK
    q = q_ref[...].astype(F32)
    v = i_ref[...]
    b_last_rows = [b[(c + 1) * CHUNK - 1:(c + 1) * CHUNK, :] for c in range(n_chunks)]
    b_last = jnp.concatenate([jnp.broadcast_to(r, (CHUNK, d)) for r in b_last_rows], axis=0)
    q_in = (q * jnp.exp2(b)).astype(BF16)
    k_out = (kk * jnp.exp2(-b)).astype(BF16)
    k_st = (kk * jnp.exp2(b_last - b)).astype(BF16)
    causal = (lax.broadcasted_iota(jnp.int32, (CHUNK, CHUNK), 1)
              <= lax.broadcasted_iota(jnp.int32, (CHUNK, CHUNK), 0))
    states = [st_ref[h] for h in range(heads)]
    tile_out = []
    for c in range(n_chunks):
        rs = slice(c * CHUNK, (c + 1) * CHUNK)
        carry_decay = jnp.exp2(b_last_rows[c])
        outs = []
        for h in range(heads):
            hs = slice(h * HEAD_DIM, (h + 1) * HEAD_DIM)
            scores = jnp.where(causal, _dot_nt(q_in[rs, hs], k_out[rs, hs]), 0.0).astype(BF16)
            outs.append(_dot(scores, v[rs, hs]) + _dot_nt(q_in[rs, hs], states[h].astype(BF16)))
            states[h] = states[h] * carry_decay[:, hs] + _dot_tn(v[rs, hs], k_st[rs, hs])
        tile_out.append(jnp.concatenate(outs, axis=1))
    for h in range(heads):
        st_ref[h] = states[h]
    heads_out = jnp.concatenate(tile_out, axis=0)
    o_heads = [heads_out[:, h * HEAD_DIM:(h + 1) * HEAD_DIM] for h in range(heads)]
    o_ref[...] = _head_norm_gate(o_heads, norm_w, g_ref[...].astype(F32)).astype(o_ref.dtype)


def _hgrn_kernel(q_ref, f_ref, i_ref, g_ref, lbl_ref, nw_ref, tri_ref, wsel_ref, o_ref, st_ref, *, layer):
    @pl.when(pl.program_id(0) == 0)
    def _():
        st_ref[...] = jnp.zeros_like(st_ref)

    logits = lbl_ref[...]
    e = jnp.exp(logits - jnp.max(logits, axis=0, keepdims=True))
    lb = jnp.sum(e[:layer + 1, :], axis=0, keepdims=True) / jnp.sum(e, axis=0, keepdims=True)
    norm_w = nw_ref[...]

    log2_f, kk = _gates(f_ref[...], lb)
    b = _chunk_cumsum(tri_ref[...], log2_f)
    mild = jnp.min(b) >= MILD_DECAY_LOG2

    @pl.when(mild)
    def _():
        _hgrn_tile_mild_decay(b, kk, q_ref, i_ref, g_ref, o_ref, st_ref, norm_w)

    @pl.when(jnp.logical_not(mild))
    def _():
        def body(c, carry):
            _hgrn_chunk_any_decay(c, q_ref, f_ref, i_ref, g_ref, tri_ref, wsel_ref, o_ref, st_ref, lb, norm_w)
            return carry

        lax.fori_loop(0, q_ref.shape[0] // CHUNK, body, 0)


def _hgrn(proj, f_logit, lb_logits, norm_w, *, bt, layer, q_block, i_block, g_block):
    s = proj.shape[0]
    d = f_logit.shape[1]
    heads = d // HEAD_DIM
    t_id = lax.broadcasted_iota(jnp.int32, (bt, bt), 0)
    s_id = lax.broadcasted_iota(jnp.int32, (bt, bt), 1)
    tri = ((s_id <= t_id) & (s_id // CHUNK == t_id // CHUNK)).astype(BF16)
    src = lax.broadcasted_iota(jnp.int32, (SUB * HEAD_DIM, CHUNK), 0) // HEAD_DIM
    col = lax.broadcasted_iota(jnp.int32, (SUB * HEAD_DIM, CHUNK), 1) % SUB
    wsel = (src == col).astype(BF16)
    nl = lb_logits.shape[0]
    return pl.pallas_call(
        functools.partial(_hgrn_kernel, layer=layer),
        grid=(s // bt,),
        in_specs=[
            pl.BlockSpec((bt, d), lambda t: (t, q_block)),
            pl.BlockSpec((bt, d), lambda t: (t, 0)),
            pl.BlockSpec((bt, d), lambda t: (t, i_block)),
            pl.BlockSpec((bt, d), lambda t: (t, g_block)),
            _resident((nl, d)),
            _resident((1, d)),
            _resident((bt, bt)),
            _resident((SUB * HEAD_DIM, CHUNK)),
        ],
        out_specs=pl.BlockSpec((bt, d), lambda t: (t, 0)),
        out_shape=jax.ShapeDtypeStruct((s, d), BF16),
        scratch_shapes=[pltpu.VMEM((heads, HEAD_DIM, HEAD_DIM), F32)],
        compiler_params=_params("arbitrary"),
        name="hgrn",
    )(proj, f_logit, proj, proj, lb_logits, norm_w, tri, wsel)


def _merge_kernel(og_ref, gb0_ref, gb1_ref, ya_ref, x_ref, wb_ref, wo_ref, o_ref):
    y_b = _dot(og_ref[...], wb_ref[...])
    gate_b = jnp.concatenate([gb0_ref[...], gb1_ref[...]], axis=1).astype(F32)
    merged = ya_ref[...].astype(F32) + jax.nn.sigmoid(gate_b) * y_b
    o_ref[...] = x_ref[...] + _dot(merged.astype(BF16), wo_ref[...])


def _merge(og, proj, ya, x, w_b, w_out, *, bm, gb_block0):
    s, d = x.shape
    c = og.shape[1]
    half = d // 2
    return pl.pallas_call(
        _merge_kernel,
        grid=(s // bm,),
        in_specs=[
            pl.BlockSpec((bm, c), lambda m: (m, 0)),
            pl.BlockSpec((bm, half), lambda m: (m, gb_block0)),
            pl.BlockSpec((bm, half), lambda m: (m, gb_block0 + 1)),
            pl.BlockSpec((bm, d), lambda m: (m, 0)),
            pl.BlockSpec((bm, d), lambda m: (m, 0)),
            _resident((c, d)),
            _resident((d, d)),
        ],
        out_specs=pl.BlockSpec((bm, d), lambda m: (m, 0)),
        out_shape=jax.ShapeDtypeStruct((s, d), F32),
        compiler_params=_params("arbitrary"),
        name="merge_out",
    )(og, proj, proj, ya, x, w_b, w_out)


def _causal_conv(z, prev, w_ref, b_ref):
    ext = jnp.concatenate([prev, z], axis=0)
    y = b_ref[...] + w_ref[CONV_WIDTH - 1:CONV_WIDTH, :] * z
    for back in range(1, CONV_WIDTH):
        shifted = pltpu.roll(ext, back, axis=0)[CARRY_ROWS:, :]
        y = y + w_ref[CONV_WIDTH - 1 - back:CONV_WIDTH - back, :] * shifted
    return y


def _ffn_kernel(x_ref, lnw_ref, wug_ref, wuv_ref, cwg_ref, cwv_ref, cbg_ref, cbv_ref, wd_ref,
                o_ref, h_ref, cg_ref, cv_ref):
    m = pl.program_id(0)
    j = pl.program_id(1)

    @pl.when(j == 0)
    def _():
        x = x_ref[...]
        h_ref[...] = _rms(x, lnw_ref[...]).astype(BF16)
        o_ref[...] = x

    @pl.when(m == 0)
    def _():
        cg_ref[j] = jnp.zeros(cg_ref.shape[1:], F32)
        cv_ref[j] = jnp.zeros(cv_ref.shape[1:], F32)

    h = h_ref[...]
    zg = _dot(h, wug_ref[...])
    zv = _dot(h, wuv_ref[...])
    ug = _causal_conv(zg, cg_ref[j], cwg_ref, cbg_ref)
    uv = _causal_conv(zv, cv_ref[j], cwv_ref, cbv_ref)
    bm = zg.shape[0]
    cg_ref[j] = zg[bm - CARRY_ROWS:, :]
    cv_ref[j] = zv[bm - CARRY_ROWS:, :]
    act = (ug * jax.nn.sigmoid(ug) * uv).astype(BF16)
    o_ref[...] += _dot(act, wd_ref[...])


def _ffn(x, ln_w, w_up, conv_w, conv_b, w_down, *, bm, tf):
    s, d = x.shape
    dff = w_down.shape[0]
    nj = dff // tf
    return pl.pallas_call(
        _ffn_kernel,
        grid=(s // bm, nj),
        in_specs=[
            pl.BlockSpec((bm, d), lambda m, j: (m, 0)),
            pl.BlockSpec((1, d), lambda m, j: (0, 0)),
            pl.BlockSpec((d, tf), lambda m, j: (0, j)),
            pl.BlockSpec((d, tf), lambda m, j: (0, nj + j)),
            pl.BlockSpec((CONV_WIDTH, tf), lambda m, j: (0, j)),
            pl.BlockSpec((CONV_WIDTH, tf), lambda m, j: (0, nj + j)),
            pl.BlockSpec((1, tf), lambda m, j: (0, j)),
            pl.BlockSpec((1, tf), lambda m, j: (0, nj + j)),
            pl.BlockSpec((tf, d), lambda m, j: (j, 0)),
        ],
        out_specs=pl.BlockSpec((bm, d), lambda m, j: (m, 0)),
        out_shape=jax.ShapeDtypeStruct((s, d), F32),
        scratch_shapes=[
            pltpu.VMEM((bm, d), BF16),
            pltpu.VMEM((nj, CARRY_ROWS, tf), F32),
            pltpu.VMEM((nj, CARRY_ROWS, tf), F32),
        ],
        compiler_params=_params("arbitrary", "arbitrary"),
        name="ffn",
    )(x, ln_w, w_up, w_up, conv_w, conv_w, conv_b, conv_b, w_down)


def _ple_kernel(x_ref, p_ref, lnw_ref, wg_ref, wp_ref, lnf_ref, o_ref, *, final):
    x = x_ref[...]
    h = _rms(x, lnw_ref[...]).astype(BF16)
    gate = jax.nn.sigmoid(_dot(h, wg_ref[...]))
    emb = _dot(p_ref[...].astype(BF16), wp_ref[...])
    y = x + gate * emb
    o_ref[...] = _rms(y, lnf_ref[...]) if final else y


def _ple(x, p, ln_w, w_gate, w_ple, ln_final, *, bm, final):
    s, d = x.shape
    e = p.shape[1]
    return pl.pallas_call(
        functools.partial(_ple_kernel, final=final),
        grid=(s // bm,),
        in_specs=[
            pl.BlockSpec((bm, d), lambda m: (m, 0)),
            pl.BlockSpec((bm, e), lambda m: (m, 0)),
            _resident((1, d)),
            _resident((d, d)),
            _resident((e, d)),
            _resident((1, d)),
        ],
        out_specs=pl.BlockSpec((bm, d), lambda m: (m, 0)),
        out_shape=jax.ShapeDtypeStruct((s, d), F32),
        compiler_params=_params("arbitrary"),
        name="ple_final",
    )(x, p, ln_w, w_gate, w_ple, ln_final)


def _pad_ff(w_up, conv_w, conv_b, w_down, tf):
    dff = w_down.shape[0]
    pad = (-dff) % tf
    if pad == 0:
        return w_up, conv_w, conv_b, w_down

    def pad_halves(a):
        g, v = a[..., :dff], a[..., dff:]
        widths = [(0, 0)] * (a.ndim - 1) + [(0, pad)]
        return jnp.concatenate([jnp.pad(g, widths), jnp.pad(v, widths)], axis=-1)

    return pad_halves(w_up), pad_halves(conv_w), pad_halves(conv_b), jnp.pad(w_down, ((0, pad), (0, 0)))


def kernel(x, p, ln_mix_w, w_in, pool_mix_w, pool_scale, hgrn_lb_logits, hgrn_norm_w, w_branch_a,
           w_branch_b, w_out, ln_ffn_w, w_up, conv_w, conv_b, w_down, ln_ple_w, w_ple_gate, w_ple,
           ln_final_w):
    batch, seq, d = x.shape
    depth = w_in.shape[0]
    c_pool = w_branch_a.shape[1]
    c_hgrn = w_branch_b.shape[1]
    blk = c_hgrn
    assert c_pool == blk and d == 2 * blk, "column blocks of the combined projection must line up"
    assert seq % 1024 == 0
    tf = 512
    outs = []
    for bi in range(batch):
        xb = x[bi]
        for i in range(depth):
            proj, f_logit = _inproj(xb, ln_mix_w[i][None], w_in[i].astype(BF16),
                                    bm=1024, bn=blk, f_block=2)
            ya = _pool_branch(proj, pool_mix_w[i].astype(BF16), pool_scale[i][None],
                              w_branch_a[i].astype(BF16), bm=1024, bn=blk, ga_block0=5)
            og = _hgrn(proj, f_logit, hgrn_lb_logits, hgrn_norm_w[i][None], bt=256, layer=i,
                       q_block=1, i_block=3, g_block=4)
            xb = _merge(og, proj, ya, xb, w_branch_b[i].astype(BF16), w_out[i].astype(BF16),
                        bm=512, gb_block0=7)
            wu, cw, cb, wd = _pad_ff(w_up[i], conv_w[i], conv_b[i][None], w_down[i], tf)
            xb = _ffn(xb, ln_ffn_w[i][None], wu.astype(BF16), cw, cb, wd.astype(BF16), bm=512, tf=tf)
            xb = _ple(xb, p[i, bi], ln_ple_w[i][None], w_ple_gate[i].astype(BF16),
                      w_ple[i].astype(BF16), ln_final_w[None], bm=512, final=(i == depth - 1))
        outs.append(xb)
    return jnp.stack(outs, axis=0)
```

```python
import functools

import jax
import jax.numpy as jnp
from jax import lax
from jax.experimental import pallas as pl
from jax.experimental.pallas import tpu as pltpu

F32 = jnp.float32
BF16 = jnp.bfloat16

EPS = 1e-6
POOL_WINDOWS = (2, 4, 8, 16)
POOL_HALO = 16
HEAD_DIM = 128
CHUNK = 64
SUB = 16
NSUB = CHUNK // SUB
CONV_WIDTH = 3
CARRY_ROWS = 8
ACT_ROWS = 64
LANE = 128
LOG2_E = 1.4426950408889634
MILD_DECAY_LOG2 = -100.0
V7X_VMEM_BYTES = 64 * 1024 * 1024
VMEM_LIMIT = V7X_VMEM_BYTES - 8 * 1024 * 1024


def _dot(a, b):
    return jnp.dot(a, b, preferred_element_type=F32)


def _dot_nt(a, b):
    return lax.dot_general(a, b, (((1,), (1,)), ((), ())), preferred_element_type=F32)


def _dot_tn(a, b):
    return lax.dot_general(a, b, (((0,), (0,)), ((), ())), preferred_element_type=F32)


def _rms(x, w):
    return x * lax.rsqrt(jnp.mean(x * x, axis=-1, keepdims=True) + EPS) * w


def _params(*sem):
    return pltpu.CompilerParams(dimension_semantics=sem, vmem_limit_bytes=VMEM_LIMIT)


def _resident(shape):
    return pl.BlockSpec(shape, lambda *_: (0,) * len(shape), pipeline_mode=pl.Buffered(1))


def _inproj_kernel(x_ref, lnw_ref, w_ref, o_ref, f_ref, h_ref, *, f_block):
    n = pl.program_id(1)

    @pl.when(n == 0)
    def _():
        h_ref[...] = _rms(x_ref[...], lnw_ref[...]).astype(BF16)

    acc = _dot(h_ref[...], w_ref[...])
    o_ref[...] = acc.astype(o_ref.dtype)

    @pl.when(n == f_block)
    def _():
        f_ref[...] = acc


def _inproj(x, ln_w, w_in, *, bm, bn, f_block):
    s, d = x.shape
    d_in = w_in.shape[1]
    return pl.pallas_call(
        functools.partial(_inproj_kernel, f_block=f_block),
        grid=(s // bm, d_in // bn),
        in_specs=[
            pl.BlockSpec((bm, d), lambda m, n: (m, 0)),
            pl.BlockSpec((1, d), lambda m, n: (0, 0)),
            pl.BlockSpec((d, bn), lambda m, n: (0, n)),
        ],
        out_specs=[
            pl.BlockSpec((bm, bn), lambda m, n: (m, n)),
            pl.BlockSpec((bm, bn), lambda m, n: (m, 0)),
        ],
        out_shape=[
            jax.ShapeDtypeStruct((s, d_in), BF16),
            jax.ShapeDtypeStruct((s, bn), F32),
        ],
        scratch_shapes=[pltpu.VMEM((bm, d), BF16)],
        compiler_params=_params("arbitrary", "arbitrary"),
        name="inproj",
    )(x, ln_w, w_in)


def _pool_kernel(u_ref, halo_ref, ga_ref, mix_ref, scale_ref, wa_ref, o_ref, feat_ref, *, bm):
    m = pl.program_id(0)
    n = pl.program_id(1)

    @pl.when(n == 0)
    def _():
        u = u_ref[...].astype(F32)
        halo = halo_ref[...].astype(F32) * (m > 0).astype(F32)
        ext = jnp.concatenate([halo, u], axis=0)
        gw = u.shape[1] // len(POOL_WINDOWS)
        pos = m * bm + lax.broadcasted_iota(jnp.int32, (bm, 1), 0) + 1
        run = ext
        width = 1
        feats = []
        for g, w in enumerate(POOL_WINDOWS):
            while width < w:
                run = run + pltpu.roll(run, width, axis=0)
                width *= 2
            cnt = jnp.minimum(pos, w).astype(F32)
            win = run[POOL_HALO:, g * gw:(g + 1) * gw]
            d = win / cnt - u[:, g * gw:(g + 1) * gw]
            y = _dot(d.astype(BF16), mix_ref[g])
            feats.append(y * scale_ref[:, g * gw:(g + 1) * gw])
        feat_ref[...] = jnp.concatenate(feats, axis=1).astype(BF16)

    y_a = _dot(feat_ref[...], wa_ref[...])
    o_ref[...] = (jax.nn.sigmoid(ga_ref[...].astype(F32)) * y_a).astype(o_ref.dtype)


def _pool_branch(proj, mix_w, scale, w_a, *, bm, bn, ga_block0):
    s = proj.shape[0]
    c = w_a.shape[0]
    d = w_a.shape[1]
    g, gw, _ = mix_w.shape
    halo_per_tile = bm // POOL_HALO
    return pl.pallas_call(
        functools.partial(_pool_kernel, bm=bm),
        grid=(s // bm, d // bn),
        in_specs=[
            pl.BlockSpec((bm, c), lambda m, n: (m, 0)),
            pl.BlockSpec((POOL_HALO, c), lambda m, n: (jnp.maximum(m * halo_per_tile - 1, 0), 0)),
            pl.BlockSpec((bm, bn), lambda m, n: (m, ga_block0 + n)),
            _resident((g, gw, gw)),
            _resident((1, c)),
            pl.BlockSpec((c, bn), lambda m, n: (0, n)),
        ],
        out_specs=pl.BlockSpec((bm, bn), lambda m, n: (m, n)),
        out_shape=jax.ShapeDtypeStruct((s, d), BF16),
        scratch_shapes=[pltpu.VMEM((bm, c), BF16)],
        compiler_params=_params("arbitrary", "arbitrary"),
        name="pool_branch",
    )(proj, proj, proj, mix_w, scale, w_a)


def _split3(x):
    hi = x.astype(BF16)
    r = x - hi.astype(F32)
    mid = r.astype(BF16)
    lo = (r - mid.astype(F32)).astype(BF16)
    return hi, mid, lo


def _gates(fl, lb):
    one_m_lb = 1.0 - lb
    sg = jax.nn.sigmoid(fl)
    log2_f = jnp.log(lb + one_m_lb * sg) * LOG2_E
    return log2_f, one_m_lb * (1.0 - sg)


def _chunk_cumsum(tri, x):
    hi, mid, lo = _split3(x)
    return _dot(tri, hi) + _dot(tri, mid) + _dot(tri, lo)


def _head_norm_gate(o_heads, norm_w, g):
    o = jnp.concatenate(
        [o_h * lax.rsqrt(jnp.mean(o_h * o_h, axis=-1, keepdims=True) + EPS) for o_h in o_heads], axis=1)
    return o * norm_w * (g * jax.nn.sigmoid(g))


def _hgrn_chunk_any_decay(c, q_ref, f_ref, i_ref, g_ref, tri_ref, wsel_ref, o_ref, st_ref, lb, norm_w):
    r0 = pl.multiple_of(c * CHUNK, CHUNK)
    rows = pl.ds(r0, CHUNK)
    d = q_ref.shape[1]
    heads = d // HEAD_DIM

    q = q_ref[rows, :].astype(F32)
    v = i_ref[rows, :]
    log2_f, kk = _gates(f_ref[rows, :], lb)
    b = _chunk_cumsum(tri_ref[:CHUNK, :CHUNK], log2_f)
    b_last = b[CHUNK - 1:CHUNK, :]

    q_in = (q * jnp.exp2(b)).astype(BF16)
    k_st = (kk * jnp.exp2(b_last - b)).astype(BF16)

    b_end = jnp.concatenate(
        [jnp.broadcast_to(b[(j + 1) * SUB - 1:(j + 1) * SUB, :], (SUB, d)) for j in range(NSUB)], axis=0)
    k_hat = kk * jnp.exp2(b_end - b)
    sub_id = lax.broadcasted_iota(jnp.int32, (CHUNK, 1), 0) // SUB
    q_from = []
    for j in range(NSUB - 1):
        lo_row = (j + 1) * SUB
        bj = b[lo_row - 1:lo_row, :]
        part = q[lo_row:, :] * jnp.exp2(b[lo_row:, :] - bj)
        q_from.append(jnp.concatenate([jnp.zeros((lo_row, d), F32), part], axis=0).astype(BF16))
    k_src = [jnp.where(sub_id == j, k_hat, 0.0).astype(BF16) for j in range(NSUB - 1)]

    per_head = [[] for _ in range(heads)]
    for r in range(NSUB):
        qb = q[r * SUB:(r + 1) * SUB, :]
        bb = b[r * SUB:(r + 1) * SUB, :]
        pieces = []
        for s in range(SUB):
            row = r * SUB + s
            e = jnp.exp2(jnp.minimum(bb - b[row:row + 1, :], 0.0))
            pieces.append((qb * kk[row:row + 1, :] * e).astype(BF16))
        for h in range(heads):
            per_head[h].append(jnp.concatenate(
                [p[:, h * HEAD_DIM:(h + 1) * HEAD_DIM] for p in pieces], axis=1))
    a_big = jnp.concatenate([jnp.concatenate(blocks, axis=0) for blocks in per_head], axis=0)
    diag = _dot(a_big, wsel_ref[...])

    t_id = lax.broadcasted_iota(jnp.int32, (CHUNK, CHUNK), 0)
    s_id = lax.broadcasted_iota(jnp.int32, (CHUNK, CHUNK), 1)
    diag_mask = (t_id // SUB == s_id // SUB) & (s_id <= t_id)

    outs = []
    for h in range(heads):
        hs = slice(h * HEAD_DIM, (h + 1) * HEAD_DIM)
        q_cat = jnp.concatenate([qf[:, hs] for qf in q_from], axis=1)
        k_cat = jnp.concatenate([ks[:, hs] for ks in k_src], axis=1)
        scores = _dot_nt(q_cat, k_cat) + jnp.where(diag_mask, diag[h * CHUNK:(h + 1) * CHUNK, :], 0.0)
        v_h = v[:, hs]
        state_t = st_ref[h]
        o_h = _dot_nt(q_in[:, hs], state_t.astype(BF16)) + _dot(scores.astype(BF16), v_h)
        st_ref[h] = state_t * jnp.exp2(b_last[:, hs]) + _dot_tn(v_h, k_st[:, hs])
        outs.append(o_h)
    o_ref[rows, :] = _head_norm_gate(outs, norm_w, g_ref[rows, :].astype(F32)).astype(o_ref.dtype)


def _hgrn_tile_mild_decay(b, kk, q_ref, i_ref, g_ref, o_ref, st_ref, norm_w):
    bt, d = b.shape
    heads = d // HEAD_DIM
    n_chunks = bt // CHUNK
    q = q_ref[...].astype(F32)
    v = i_ref[...]
    b_last_rows = [b[(c + 1) * CHUNK - 1:(c + 1) * CHUNK, :] for c in range(n_chunks)]
    b_last = jnp.concatenate([jnp.broadcast_to(r, (CHUNK, d)) for r in b_last_rows], axis=0)
    q_in = (q * jnp.exp2(b)).astype(BF16)
    k_out = (kk * jnp.exp2(-b)).astype(BF16)
    k_st = (kk * jnp.exp2(b_last - b)).astype(BF16)
    causal = (lax.broadcasted_iota(jnp.int32, (CHUNK, CHUNK), 1)
              <= lax.broadcasted_iota(jnp.int32, (CHUNK, CHUNK), 0))
    states = [st_ref[h] for h in range(heads)]
    tile_out = []
    for c in range(n_chunks):
        rs = slice(c * CHUNK, (c + 1) * CHUNK)
        carry_decay = jnp.exp2(b_last_rows[c])
        outs = []
        for h in range(heads):
            hs = slice(h * HEAD_DIM, (h + 1) * HEAD_DIM)
            scores = jnp.where(causal, _dot_nt(q_in[rs, hs], k_out[rs, hs]), 0.0).astype(BF16)
            outs.append(_dot(scores, v[rs, hs]) + _dot_nt(q_in[rs, hs], states[h].astype(BF16)))
            states[h] = states[h] * carry_decay[:, hs] + _dot_tn(v[rs, hs], k_st[rs, hs])
        tile_out.append(jnp.concatenate(outs, axis=1))
    for h in range(heads):
        st_ref[h] = states[h]
    heads_out = jnp.concatenate(tile_out, axis=0)
    o_heads = [heads_out[:, h * HEAD_DIM:(h + 1) * HEAD_DIM] for h in range(heads)]
    o_ref[...] = _head_norm_gate(o_heads, norm_w, g_ref[...].astype(F32)).astype(o_ref.dtype)


def _hgrn_kernel(q_ref, f_ref, i_ref, g_ref, lbl_ref, nw_ref, tri_ref, wsel_ref, o_ref, st_ref, *, layer):
    @pl.when(pl.program_id(0) == 0)
    def _():
        st_ref[...] = jnp.zeros_like(st_ref)

    logits = lbl_ref[...]
    e = jnp.exp(logits - jnp.max(logits, axis=0, keepdims=True))
    lb = jnp.sum(e[:layer + 1, :], axis=0, keepdims=True) / jnp.sum(e, axis=0, keepdims=True)
    norm_w = nw_ref[...]

    log2_f, kk = _gates(f_ref[...], lb)
    b = _chunk_cumsum(tri_ref[...], log2_f)
    mild = jnp.min(b) >= MILD_DECAY_LOG2

    @pl.when(mild)
    def _():
        _hgrn_tile_mild_decay(b, kk, q_ref, i_ref, g_ref, o_ref, st_ref, norm_w)

    @pl.when(jnp.logical_not(mild))
    def _():
        def body(c, carry):
            _hgrn_chunk_any_decay(c, q_ref, f_ref, i_ref, g_ref, tri_ref, wsel_ref, o_ref, st_ref, lb, norm_w)
            return carry

        lax.fori_loop(0, q_ref.shape[0] // CHUNK, body, 0)


def _hgrn(proj, f_logit, lb_logits, norm_w, *, bt, layer, q_block, i_block, g_block):
    s = proj.shape[0]
    d = f_logit.shape[1]
    heads = d // HEAD_DIM
    t_id = lax.broadcasted_iota(jnp.int32, (bt, bt), 0)
    s_id = lax.broadcasted_iota(jnp.int32, (bt, bt), 1)
    tri = ((s_id <= t_id) & (s_id // CHUNK == t_id // CHUNK)).astype(BF16)
    src = lax.broadcasted_iota(jnp.int32, (SUB * HEAD_DIM, CHUNK), 0) // HEAD_DIM
    col = lax.broadcasted_iota(jnp.int32, (SUB * HEAD_DIM, CHUNK), 1) % SUB
    wsel = (src == col).astype(BF16)
    nl = lb_logits.shape[0]
    return pl.pallas_call(
        functools.partial(_hgrn_kernel, layer=layer),
        grid=(s // bt,),
        in_specs=[
            pl.BlockSpec((bt, d), lambda t: (t, q_block)),
            pl.BlockSpec((bt, d), lambda t: (t, 0)),
            pl.BlockSpec((bt, d), lambda t: (t, i_block)),
            pl.BlockSpec((bt, d), lambda t: (t, g_block)),
            _resident((nl, d)),
            _resident((1, d)),
            _resident((bt, bt)),
            _resident((SUB * HEAD_DIM, CHUNK)),
        ],
        out_specs=pl.BlockSpec((bt, d), lambda t: (t, 0)),
        out_shape=jax.ShapeDtypeStruct((s, d), BF16),
        scratch_shapes=[pltpu.VMEM((heads, HEAD_DIM, HEAD_DIM), F32)],
        compiler_params=_params("arbitrary"),
        name="hgrn",
    )(proj, f_logit, proj, proj, lb_logits, norm_w, tri, wsel)


def _merge_kernel(og_ref, gb0_ref, gb1_ref, ya_ref, x_ref, wb_ref, wo_ref, o_ref):
    y_b = _dot(og_ref[...], wb_ref[...])
    gate_b = jnp.concatenate([gb0_ref[...], gb1_ref[...]], axis=1).astype(F32)
    merged = ya_ref[...].astype(F32) + jax.nn.sigmoid(gate_b) * y_b
    o_ref[...] = x_ref[...] + _dot(merged.astype(BF16), wo_ref[...])


def _merge(og, proj, ya, x, w_b, w_out, *, bm, gb_block0):
    s, d = x.shape
    c = og.shape[1]
    half = d // 2
    return pl.pallas_call(
        _merge_kernel,
        grid=(s // bm,),
        in_specs=[
            pl.BlockSpec((bm, c), lambda m: (m, 0)),
            pl.BlockSpec((bm, half), lambda m: (m, gb_block0)),
            pl.BlockSpec((bm, half), lambda m: (m, gb_block0 + 1)),
            pl.BlockSpec((bm, d), lambda m: (m, 0)),
            pl.BlockSpec((bm, d), lambda m: (m, 0)),
            _resident((c, d)),
            _resident((d, d)),
        ],
        out_specs=pl.BlockSpec((bm, d), lambda m: (m, 0)),
        out_shape=jax.ShapeDtypeStruct((s, d), F32),
        compiler_params=_params("arbitrary"),
        name="merge_out",
    )(og, proj, proj, ya, x, w_b, w_out)


def _causal_conv(z_ref, row0, cols, w_ref, b_ref):
    y = b_ref[:, cols]
    for tap in range(CONV_WIDTH):
        start = row0 + CARRY_ROWS - (CONV_WIDTH - 1) + tap
        y = y + w_ref[tap:tap + 1, cols] * z_ref[pl.ds(start, ACT_ROWS), cols]
    return y


def _ffn_kernel(x_ref, lnw_ref, wug_ref, wuv_ref, cwg_ref, cwv_ref, cbg_ref, cbv_ref, wd_ref,
                o_ref, h_ref, zg0_ref, zv0_ref, zg1_ref, zv1_ref, a0_ref, a1_ref, acc_ref, cg_ref, cv_ref,
                *, nj):
    m = pl.program_id(0)
    j = pl.program_id(1)
    bm = h_ref.shape[0]
    jp = jnp.clip(j - 1, 0, nj - 1)

    @pl.when(j == 0)
    def _():
        x = x_ref[...]
        h_ref[...] = _rms(x, lnw_ref[...]).astype(BF16)
        acc_ref[...] = x

    @pl.when((j == 0) & (m == 0))
    def _():
        cg_ref[...] = jnp.zeros_like(cg_ref)
        cv_ref[...] = jnp.zeros_like(cv_ref)

    pairs = ((zg0_ref, zv0_ref), (zg1_ref, zv1_ref))

    acts = (a0_ref, a1_ref)

    def up(slot):
        zg_ref, zv_ref = pairs[slot]
        h = h_ref[...]
        zg_ref[CARRY_ROWS:, :] = _dot(h, wug_ref[...])
        zv_ref[CARRY_ROWS:, :] = _dot(h, wuv_ref[...])

    def conv_gate(slot):
        zg_ref, zv_ref = pairs[slot]
        zg_ref[:CARRY_ROWS, :] = cg_ref[jp]
        zv_ref[:CARRY_ROWS, :] = cv_ref[jp]
        for c in range(zg_ref.shape[1] // LANE):
            cols = slice(c * LANE, (c + 1) * LANE)
            for r in range(bm // ACT_ROWS):
                ug = _causal_conv(zg_ref, r * ACT_ROWS, cols, cwg_ref, cbg_ref)
                uv = _causal_conv(zv_ref, r * ACT_ROWS, cols, cwv_ref, cbv_ref)
                acts[slot][r * ACT_ROWS:(r + 1) * ACT_ROWS, cols] = (
                    ug * jax.nn.sigmoid(ug) * uv).astype(BF16)
        cg_ref[jp] = zg_ref[bm:, :]
        cv_ref[jp] = zv_ref[bm:, :]

    def down(slot, last=False):
        total = acc_ref[...] + _dot(acts[slot][...], wd_ref[...])
        if last:
            o_ref[...] = total
        else:
            acc_ref[...] = total

    def stages(slot, do_up, do_conv, do_down):
        if do_conv:
            conv_gate(1 - slot)
        if do_up:
            up(slot)
        if do_down:
            down(slot)

    @pl.when(j == 0)
    def _():
        stages(0, True, False, False)

    @pl.when(j == 1)
    def _():
        stages(1, True, True, False)

    for parity in range(2):
        @pl.when((j >= 2) & (j < nj) & (j % 2 == parity))
        def _():
            stages(parity, True, True, True)

    @pl.when(j == nj)
    def _():
        stages(nj % 2, False, True, True)

    @pl.when(j == nj + 1)
    def _():
        down((nj + 1) % 2, last=True)


def _ffn(x, ln_w, w_up, conv_w, conv_b, w_down, *, bm, tf):
    s, d = x.shape
    dff = w_down.shape[0]
    nj = dff // tf
    def tile(j, lag):
        return jnp.clip(j - lag, 0, nj - 1)

    return pl.pallas_call(
        functools.partial(_ffn_kernel, nj=nj),
        grid=(s // bm, nj + 2),
        in_specs=[
            pl.BlockSpec((bm, d), lambda m, j: (m, 0)),
            pl.BlockSpec((1, d), lambda m, j: (0, 0)),
            pl.BlockSpec((d, tf), lambda m, j: (0, tile(j, 0))),
            pl.BlockSpec((d, tf), lambda m, j: (0, nj + tile(j, 0))),
            pl.BlockSpec((CONV_WIDTH, tf), lambda m, j: (0, tile(j, 1))),
            pl.BlockSpec((CONV_WIDTH, tf), lambda m, j: (0, nj + tile(j, 1))),
            pl.BlockSpec((1, tf), lambda m, j: (0, tile(j, 1))),
            pl.BlockSpec((1, tf), lambda m, j: (0, nj + tile(j, 1))),
            pl.BlockSpec((tf, d), lambda m, j: (tile(j, 2), 0)),
        ],
        out_specs=pl.BlockSpec((bm, d), lambda m, j: (m, 0)),
        out_shape=jax.ShapeDtypeStruct((s, d), F32),
        scratch_shapes=[
            pltpu.VMEM((bm, d), BF16),
            pltpu.VMEM((CARRY_ROWS + bm, tf), F32),
            pltpu.VMEM((CARRY_ROWS + bm, tf), F32),
            pltpu.VMEM((CARRY_ROWS + bm, tf), F32),
            pltpu.VMEM((CARRY_ROWS + bm, tf), F32),
            pltpu.VMEM((bm, tf), BF16),
            pltpu.VMEM((bm, tf), BF16),
            pltpu.VMEM((bm, d), F32),
            pltpu.VMEM((nj, CARRY_ROWS, tf), F32),
            pltpu.VMEM((nj, CARRY_ROWS, tf), F32),
        ],
        compiler_params=_params("arbitrary", "arbitrary"),
        name="ffn",
    )(x, ln_w, w_up, w_up, conv_w, conv_w, conv_b, conv_b, w_down)


def _ple_kernel(x_ref, p_ref, lnw_ref, wg_ref, wp_ref, lnf_ref, o_ref, *, final):
    x = x_ref[...]
    h = _rms(x, lnw_ref[...]).astype(BF16)
    gate = jax.nn.sigmoid(_dot(h, wg_ref[...]))
    emb = _dot(p_ref[...].astype(BF16), wp_ref[...])
    y = x + gate * emb
    o_ref[...] = _rms(y, lnf_ref[...]) if final else y


def _ple(x, p, ln_w, w_gate, w_ple, ln_final, *, bm, final):
    s, d = x.shape
    e = p.shape[1]
    return pl.pallas_call(
        functools.partial(_ple_kernel, final=final),
        grid=(s // bm,),
        in_specs=[
            pl.BlockSpec((bm, d), lambda m: (m, 0)),
            pl.BlockSpec((bm, e), lambda m: (m, 0)),
            _resident((1, d)),
            _resident((d, d)),
            _resident((e, d)),
            _resident((1, d)),
        ],
        out_specs=pl.BlockSpec((bm, d), lambda m: (m, 0)),
        out_shape=jax.ShapeDtypeStruct((s, d), F32),
        compiler_params=_params("arbitrary"),
        name="ple_final",
    )(x, p, ln_w, w_gate, w_ple, ln_final)


def _pad_ff(w_up, conv_w, conv_b, w_down, tf):
    dff = w_down.shape[0]
    pad = (-dff) % tf
    if pad == 0:
        return w_up, conv_w, conv_b, w_down

    def pad_halves(a):
        g, v = a[..., :dff], a[..., dff:]
        widths = [(0, 0)] * (a.ndim - 1) + [(0, pad)]
        return jnp.concatenate([jnp.pad(g, widths), jnp.pad(v, widths)], axis=-1)

    return pad_halves(w_up), pad_halves(conv_w), pad_halves(conv_b), jnp.pad(w_down, ((0, pad), (0, 0)))


def kernel(x, p, ln_mix_w, w_in, pool_mix_w, pool_scale, hgrn_lb_logits, hgrn_norm_w, w_branch_a,
           w_branch_b, w_out, ln_ffn_w, w_up, conv_w, conv_b, w_down, ln_ple_w, w_ple_gate, w_ple,
           ln_final_w):
    batch, seq, d = x.shape
    depth = w_in.shape[0]
    c_pool = w_branch_a.shape[1]
    c_hgrn = w_branch_b.shape[1]
    blk = c_hgrn
    assert c_pool == blk and d == 2 * blk, "column blocks of the combined projection must line up"
    assert seq % 1024 == 0
    tf = 512
    outs = []
    for bi in range(batch):
        xb = x[bi]
        for i in range(depth):
            proj, f_logit = _inproj(xb, ln_mix_w[i][None], w_in[i].astype(BF16),
                                    bm=1024, bn=blk, f_block=2)
            ya = _pool_branch(proj, pool_mix_w[i].astype(BF16), pool_scale[i][None],
                              w_branch_a[i].astype(BF16), bm=1024, bn=blk, ga_block0=5)
            og = _hgrn(proj, f_logit, hgrn_lb_logits, hgrn_norm_w[i][None], bt=256, layer=i,
                       q_block=1, i_block=3, g_block=4)
            xb = _merge(og, proj, ya, xb, w_branch_b[i].astype(BF16), w_out[i].astype(BF16),
                        bm=512, gb_block0=7)
            wu, cw, cb, wd = _pad_ff(w_up[i], conv_w[i], conv_b[i][None], w_down[i], tf)
            xb = _ffn(xb, ln_ffn_w[i][None], wu.astype(BF16), cw, cb, wd.astype(BF16), bm=512, tf=tf)
            xb = _ple(xb, p[i, bi], ln_ple_w[i][None], w_ple_gate[i].astype(BF16),
                      w_ple[i].astype(BF16), ln_final_w[None], bm=512, final=(i == depth - 1))
        outs.append(xb)
    return jnp.stack(outs, axis=0)
```

```python
import functools

import jax
import jax.numpy as jnp
from jax import lax
from jax.experimental import pallas as pl
from jax.experimental.pallas import tpu as pltpu

F32 = jnp.float32
BF16 = jnp.bfloat16

EPS = 1e-6
POOL_WINDOWS = (2, 4, 8, 16)
POOL_HALO = 16
HEAD_DIM = 128
CHUNK = 64
SUB = 16
NSUB = CHUNK // SUB
CONV_WIDTH = 3
CARRY_ROWS = 8
ACT_ROWS = 64
LANE = 128
LOG2_E = 1.4426950408889634
MILD_DECAY_LOG2 = -100.0
V7X_VMEM_BYTES = 64 * 1024 * 1024
VMEM_LIMIT = V7X_VMEM_BYTES - 8 * 1024 * 1024


def _dot(a, b):
    return jnp.dot(a, b, preferred_element_type=F32)


def _dot_nt(a, b):
    return lax.dot_general(a, b, (((1,), (1,)), ((), ())), preferred_element_type=F32)


def _dot_tn(a, b):
    return lax.dot_general(a, b, (((0,), (0,)), ((), ())), preferred_element_type=F32)


def _rms(x, w):
    return x * lax.rsqrt(jnp.mean(x * x, axis=-1, keepdims=True) + EPS) * w


def _params(*sem):
    return pltpu.CompilerParams(dimension_semantics=sem, vmem_limit_bytes=VMEM_LIMIT)


def _resident(shape):
    return pl.BlockSpec(shape, lambda *_: (0,) * len(shape), pipeline_mode=pl.Buffered(1))


def _inproj_kernel(x_ref, lnw_ref, w_ref, o_ref, f_ref, h_ref, *, f_block):
    n = pl.program_id(1)

    @pl.when(n == 0)
    def _():
        h_ref[...] = _rms(x_ref[...], lnw_ref[...]).astype(BF16)

    acc = _dot(h_ref[...], w_ref[...])
    o_ref[...] = acc.astype(o_ref.dtype)

    @pl.when(n == f_block)
    def _():
        f_ref[...] = acc


def _inproj(x, ln_w, w_in, *, bm, bn, f_block):
    s, d = x.shape
    d_in = w_in.shape[1]
    return pl.pallas_call(
        functools.partial(_inproj_kernel, f_block=f_block),
        grid=(s // bm, d_in // bn),
        in_specs=[
            pl.BlockSpec((bm, d), lambda m, n: (m, 0)),
            pl.BlockSpec((1, d), lambda m, n: (0, 0)),
            pl.BlockSpec((d, bn), lambda m, n: (0, n)),
        ],
        out_specs=[
            pl.BlockSpec((bm, bn), lambda m, n: (m, n)),
            pl.BlockSpec((bm, bn), lambda m, n: (m, 0)),
        ],
        out_shape=[
            jax.ShapeDtypeStruct((s, d_in), BF16),
            jax.ShapeDtypeStruct((s, bn), F32),
        ],
        scratch_shapes=[pltpu.VMEM((bm, d), BF16)],
        compiler_params=_params("arbitrary", "arbitrary"),
        name="inproj",
    )(x, ln_w, w_in)


def _pool_kernel(u_ref, halo_ref, ga_ref, mix_ref, scale_ref, wa_ref, o_ref, feat_ref, *, bm):
    m = pl.program_id(0)
    n = pl.program_id(1)

    @pl.when(n == 0)
    def _():
        u = u_ref[...].astype(F32)
        halo = halo_ref[...].astype(F32) * (m > 0).astype(F32)
        ext = jnp.concatenate([halo, u], axis=0)
        gw = u.shape[1] // len(POOL_WINDOWS)
        pos = m * bm + lax.broadcasted_iota(jnp.int32, (bm, 1), 0) + 1
        run = ext
        width = 1
        feats = []
        for g, w in enumerate(POOL_WINDOWS):
            while width < w:
                run = run + pltpu.roll(run, width, axis=0)
                width *= 2
            cnt = jnp.minimum(pos, w).astype(F32)
            win = run[POOL_HALO:, g * gw:(g + 1) * gw]
            d = win / cnt - u[:, g * gw:(g + 1) * gw]
            y = _dot(d.astype(BF16), mix_ref[g])
            feats.append(y * scale_ref[:, g * gw:(g + 1) * gw])
        feat_ref[...] = jnp.concatenate(feats, axis=1).astype(BF16)

    y_a = _dot(feat_ref[...], wa_ref[...])
    o_ref[...] = (jax.nn.sigmoid(ga_ref[...].astype(F32)) * y_a).astype(o_ref.dtype)


def _pool_branch(proj, mix_w, scale, w_a, *, bm, bn, ga_block0):
    s = proj.shape[0]
    c = w_a.shape[0]
    d = w_a.shape[1]
    g, gw, _ = mix_w.shape
    halo_per_tile = bm // POOL_HALO
    return pl.pallas_call(
        functools.partial(_pool_kernel, bm=bm),
        grid=(s // bm, d // bn),
        in_specs=[
            pl.BlockSpec((bm, c), lambda m, n: (m, 0)),
            pl.BlockSpec((POOL_HALO, c), lambda m, n: (jnp.maximum(m * halo_per_tile - 1, 0), 0)),
            pl.BlockSpec((bm, bn), lambda m, n: (m, ga_block0 + n)),
            _resident((g, gw, gw)),
            _resident((1, c)),
            pl.BlockSpec((c, bn), lambda m, n: (0, n)),
        ],
        out_specs=pl.BlockSpec((bm, bn), lambda m, n: (m, n)),
        out_shape=jax.ShapeDtypeStruct((s, d), BF16),
        scratch_shapes=[pltpu.VMEM((bm, c), BF16)],
        compiler_params=_params("arbitrary", "arbitrary"),
        name="pool_branch",
    )(proj, proj, proj, mix_w, scale, w_a)


def _split3(x):
    hi = x.astype(BF16)
    r = x - hi.astype(F32)
    mid = r.astype(BF16)
    lo = (r - mid.astype(F32)).astype(BF16)
    return hi, mid, lo


def _gates(fl, lb):
    one_m_lb = 1.0 - lb
    sg = jax.nn.sigmoid(fl)
    log2_f = jnp.log(lb + one_m_lb * sg) * LOG2_E
    return log2_f, one_m_lb * (1.0 - sg)


def _chunk_cumsum(tri, x):
    hi, mid, lo = _split3(x)
    return _dot(tri, hi) + _dot(tri, mid) + _dot(tri, lo)


def _head_norm_gate(o_heads, norm_w, g):
    o = jnp.concatenate(
        [o_h * lax.rsqrt(jnp.mean(o_h * o_h, axis=-1, keepdims=True) + EPS) for o_h in o_heads], axis=1)
    return o * norm_w * (g * jax.nn.sigmoid(g))


def _hgrn_chunk_any_decay(c, q_ref, f_ref, i_ref, g_ref, tri_ref, wsel_ref, o_ref, st_ref, lb, norm_w):
    r0 = pl.multiple_of(c * CHUNK, CHUNK)
    rows = pl.ds(r0, CHUNK)
    d = q_ref.shape[1]
    heads = d // HEAD_DIM

    q = q_ref[rows, :].astype(F32)
    v = i_ref[rows, :]
    log2_f, kk = _gates(f_ref[rows, :], lb)
    b = _chunk_cumsum(tri_ref[:CHUNK, :CHUNK], log2_f)
    b_last = b[CHUNK - 1:CHUNK, :]

    q_in = (q * jnp.exp2(b)).astype(BF16)
    k_st = (kk * jnp.exp2(b_last - b)).astype(BF16)

    b_end = jnp.concatenate(
        [jnp.broadcast_to(b[(j + 1) * SUB - 1:(j + 1) * SUB, :], (SUB, d)) for j in range(NSUB)], axis=0)
    k_hat = kk * jnp.exp2(b_end - b)
    sub_id = lax.broadcasted_iota(jnp.int32, (CHUNK, 1), 0) // SUB
    q_from = []
    for j in range(NSUB - 1):
        lo_row = (j + 1) * SUB
        bj = b[lo_row - 1:lo_row, :]
        part = q[lo_row:, :] * jnp.exp2(b[lo_row:, :] - bj)
        q_from.append(jnp.concatenate([jnp.zeros((lo_row, d), F32), part], axis=0).astype(BF16))
    k_src = [jnp.where(sub_id == j, k_hat, 0.0).astype(BF16) for j in range(NSUB - 1)]

    per_head = [[] for _ in range(heads)]
    for r in range(NSUB):
        qb = q[r * SUB:(r + 1) * SUB, :]
        bb = b[r * SUB:(r + 1) * SUB, :]
        pieces = []
        for s in range(SUB):
            row = r * SUB + s
            e = jnp.exp2(jnp.minimum(bb - b[row:row + 1, :], 0.0))
            pieces.append((qb * kk[row:row + 1, :] * e).astype(BF16))
        for h in range(heads):
            per_head[h].append(jnp.concatenate(
                [p[:, h * HEAD_DIM:(h + 1) * HEAD_DIM] for p in pieces], axis=1))
    a_big = jnp.concatenate([jnp.concatenate(blocks, axis=0) for blocks in per_head], axis=0)
    diag = _dot(a_big, wsel_ref[...])

    t_id = lax.broadcasted_iota(jnp.int32, (CHUNK, CHUNK), 0)
    s_id = lax.broadcasted_iota(jnp.int32, (CHUNK, CHUNK), 1)
    diag_mask = (t_id // SUB == s_id // SUB) & (s_id <= t_id)

    outs = []
    for h in range(heads):
        hs = slice(h * HEAD_DIM, (h + 1) * HEAD_DIM)
        q_cat = jnp.concatenate([qf[:, hs] for qf in q_from], axis=1)
        k_cat = jnp.concatenate([ks[:, hs] for ks in k_src], axis=1)
        scores = _dot_nt(q_cat, k_cat) + jnp.where(diag_mask, diag[h * CHUNK:(h + 1) * CHUNK, :], 0.0)
        v_h = v[:, hs]
        state_t = st_ref[h]
        o_h = _dot_nt(q_in[:, hs], state_t.astype(BF16)) + _dot(scores.astype(BF16), v_h)
        st_ref[h] = state_t * jnp.exp2(b_last[:, hs]) + _dot_tn(v_h, k_st[:, hs])
        outs.append(o_h)
    o_ref[rows, :] = _head_norm_gate(outs, norm_w, g_ref[rows, :].astype(F32)).astype(o_ref.dtype)


def _hgrn_tile_mild_decay(b, kk, q_ref, i_ref, g_ref, o_ref, st_ref, norm_w):
    bt, d = b.shape
    heads = d // HEAD_DIM
    n_chunks = bt // CHUNK
    q = q_ref[...].astype(F32)
    v = i_ref[...]
    b_last_rows = [b[(c + 1) * CHUNK - 1:(c + 1) * CHUNK, :] for c in range(n_chunks)]
    b_last = jnp.concatenate([jnp.broadcast_to(r, (CHUNK, d)) for r in b_last_rows], axis=0)
    q_in = (q * jnp.exp2(b)).astype(BF16)
    k_out = (kk * jnp.exp2(-b)).astype(BF16)
    k_st = (kk * jnp.exp2(b_last - b)).astype(BF16)
    causal = (lax.broadcasted_iota(jnp.int32, (CHUNK, CHUNK), 1)
              <= lax.broadcasted_iota(jnp.int32, (CHUNK, CHUNK), 0))
    states = [st_ref[h] for h in range(heads)]
    tile_out = []
    for c in range(n_chunks):
        rs = slice(c * CHUNK, (c + 1) * CHUNK)
        carry_decay = jnp.exp2(b_last_rows[c])
        outs = []
        for h in range(heads):
            hs = slice(h * HEAD_DIM, (h + 1) * HEAD_DIM)
            scores = jnp.where(causal, _dot_nt(q_in[rs, hs], k_out[rs, hs]), 0.0).astype(BF16)
            outs.append(_dot(scores, v[rs, hs]) + _dot_nt(q_in[rs, hs], states[h].astype(BF16)))
            states[h] = states[h] * carry_decay[:, hs] + _dot_tn(v[rs, hs], k_st[rs, hs])
        tile_out.append(jnp.concatenate(outs, axis=1))
    for h in range(heads):
        st_ref[h] = states[h]
    heads_out = jnp.concatenate(tile_out, axis=0)
    o_heads = [heads_out[:, h * HEAD_DIM:(h + 1) * HEAD_DIM] for h in range(heads)]
    o_ref[...] = _head_norm_gate(o_heads, norm_w, g_ref[...].astype(F32)).astype(o_ref.dtype)


def _hgrn_kernel(q_ref, f_ref, i_ref, g_ref, lbl_ref, nw_ref, tri_ref, wsel_ref, o_ref, st_ref, *, layer):
    @pl.when(pl.program_id(0) == 0)
    def _():
        st_ref[...] = jnp.zeros_like(st_ref)

    logits = lbl_ref[...]
    e = jnp.exp(logits - jnp.max(logits, axis=0, keepdims=True))
    lb = jnp.sum(e[:layer + 1, :], axis=0, keepdims=True) / jnp.sum(e, axis=0, keepdims=True)
    norm_w = nw_ref[...]

    log2_f, kk = _gates(f_ref[...], lb)
    b = _chunk_cumsum(tri_ref[...], log2_f)
    mild = jnp.min(b) >= MILD_DECAY_LOG2

    @pl.when(mild)
    def _():
        _hgrn_tile_mild_decay(b, kk, q_ref, i_ref, g_ref, o_ref, st_ref, norm_w)

    @pl.when(jnp.logical_not(mild))
    def _():
        def body(c, carry):
            _hgrn_chunk_any_decay(c, q_ref, f_ref, i_ref, g_ref, tri_ref, wsel_ref, o_ref, st_ref, lb, norm_w)
            return carry

        lax.fori_loop(0, q_ref.shape[0] // CHUNK, body, 0)


def _hgrn(proj, f_logit, lb_logits, norm_w, *, bt, layer, q_block, i_block, g_block):
    s = proj.shape[0]
    d = f_logit.shape[1]
    heads = d // HEAD_DIM
    t_id = lax.broadcasted_iota(jnp.int32, (bt, bt), 0)
    s_id = lax.broadcasted_iota(jnp.int32, (bt, bt), 1)
    tri = ((s_id <= t_id) & (s_id // CHUNK == t_id // CHUNK)).astype(BF16)
    src = lax.broadcasted_iota(jnp.int32, (SUB * HEAD_DIM, CHUNK), 0) // HEAD_DIM
    col = lax.broadcasted_iota(jnp.int32, (SUB * HEAD_DIM, CHUNK), 1) % SUB
    wsel = (src == col).astype(BF16)
    nl = lb_logits.shape[0]
    return pl.pallas_call(
        functools.partial(_hgrn_kernel, layer=layer),
        grid=(s // bt,),
        in_specs=[
            pl.BlockSpec((bt, d), lambda t: (t, q_block)),
            pl.BlockSpec((bt, d), lambda t: (t, 0)),
            pl.BlockSpec((bt, d), lambda t: (t, i_block)),
            pl.BlockSpec((bt, d), lambda t: (t, g_block)),
            _resident((nl, d)),
            _resident((1, d)),
            _resident((bt, bt)),
            _resident((SUB * HEAD_DIM, CHUNK)),
        ],
        out_specs=pl.BlockSpec((bt, d), lambda t: (t, 0)),
        out_shape=jax.ShapeDtypeStruct((s, d), BF16),
        scratch_shapes=[pltpu.VMEM((heads, HEAD_DIM, HEAD_DIM), F32)],
        compiler_params=_params("arbitrary"),
        name="hgrn",
    )(proj, f_logit, proj, proj, lb_logits, norm_w, tri, wsel)


def _merge_kernel(og_ref, gb0_ref, gb1_ref, ya_ref, x_ref, wb_ref, wo_ref, o_ref):
    y_b = _dot(og_ref[...], wb_ref[...])
    gate_b = jnp.concatenate([gb0_ref[...], gb1_ref[...]], axis=1).astype(F32)
    merged = ya_ref[...].astype(F32) + jax.nn.sigmoid(gate_b) * y_b
    o_ref[...] = x_ref[...] + _dot(merged.astype(BF16), wo_ref[...])


def _merge(og, proj, ya, x, w_b, w_out, *, bm, gb_block0):
    s, d = x.shape
    c = og.shape[1]
    half = d // 2
    return pl.pallas_call(
        _merge_kernel,
        grid=(s // bm,),
        in_specs=[
            pl.BlockSpec((bm, c), lambda m: (m, 0)),
            pl.BlockSpec((bm, half), lambda m: (m, gb_block0)),
            pl.BlockSpec((bm, half), lambda m: (m, gb_block0 + 1)),
            pl.BlockSpec((bm, d), lambda m: (m, 0)),
            pl.BlockSpec((bm, d), lambda m: (m, 0)),
            _resident((c, d)),
            _resident((d, d)),
        ],
        out_specs=pl.BlockSpec((bm, d), lambda m: (m, 0)),
        out_shape=jax.ShapeDtypeStruct((s, d), F32),
        compiler_params=_params("arbitrary"),
        name="merge_out",
    )(og, proj, proj, ya, x, w_b, w_out)


def _causal_conv(z_ref, row0, cols, w_ref, b_ref):
    y = b_ref[:, cols]
    for tap in range(CONV_WIDTH):
        start = row0 + CARRY_ROWS - (CONV_WIDTH - 1) + tap
        y = y + w_ref[tap:tap + 1, cols] * z_ref[pl.ds(start, ACT_ROWS), cols]
    return y


def _ffn_kernel(x_ref, lnw_ref, wug_ref, wuv_ref, cwg_ref, cwv_ref, cbg_ref, cbv_ref, wd_ref,
                o_ref, h_ref, zg0_ref, zv0_ref, zg1_ref, zv1_ref, a0_ref, a1_ref, cg_ref, cv_ref, *, nj):
    m = pl.program_id(0)
    j = pl.program_id(1)
    bm = h_ref.shape[0]
    jp = jnp.clip(j - 1, 0, nj - 1)

    @pl.when(j == 0)
    def _():
        x = x_ref[...]
        h_ref[...] = _rms(x, lnw_ref[...]).astype(BF16)
        o_ref[...] = x

    @pl.when((j == 0) & (m == 0))
    def _():
        cg_ref[...] = jnp.zeros_like(cg_ref)
        cv_ref[...] = jnp.zeros_like(cv_ref)

    pairs = ((zg0_ref, zv0_ref), (zg1_ref, zv1_ref))

    acts = (a0_ref, a1_ref)

    def up(slot):
        zg_ref, zv_ref = pairs[slot]
        h = h_ref[...]
        zg_ref[CARRY_ROWS:, :] = _dot(h, wug_ref[...])
        zv_ref[CARRY_ROWS:, :] = _dot(h, wuv_ref[...])

    def conv_gate(slot):
        zg_ref, zv_ref = pairs[slot]
        zg_ref[:CARRY_ROWS, :] = cg_ref[jp]
        zv_ref[:CARRY_ROWS, :] = cv_ref[jp]
        for c in range(zg_ref.shape[1] // LANE):
            cols = slice(c * LANE, (c + 1) * LANE)
            for r in range(bm // ACT_ROWS):
                ug = _causal_conv(zg_ref, r * ACT_ROWS, cols, cwg_ref, cbg_ref)
                uv = _causal_conv(zv_ref, r * ACT_ROWS, cols, cwv_ref, cbv_ref)
                acts[slot][r * ACT_ROWS:(r + 1) * ACT_ROWS, cols] = (
                    ug * jax.nn.sigmoid(ug) * uv).astype(BF16)
        cg_ref[jp] = zg_ref[bm:, :]
        cv_ref[jp] = zv_ref[bm:, :]

    def down(slot):
        o_ref[...] += _dot(acts[slot][...], wd_ref[...])

    def stages(slot, do_up, do_conv, do_down):
        if do_conv:
            conv_gate(1 - slot)
        if do_up:
            up(slot)
        if do_down:
            down(slot)

    @pl.when(j == 0)
    def _():
        stages(0, True, False, False)

    @pl.when(j == 1)
    def _():
        stages(1, True, True, False)

    for parity in range(2):
        @pl.when((j >= 2) & (j < nj) & (j % 2 == parity))
        def _():
            stages(parity, True, True, True)

    @pl.when(j == nj)
    def _():
        stages(nj % 2, False, True, True)

    @pl.when(j == nj + 1)
    def _():
        down((nj + 1) % 2)


def _ffn(x, ln_w, w_up, conv_w, conv_b, w_down, *, bm, tf):
    s, d = x.shape
    dff = w_down.shape[0]
    nj = dff // tf
    def tile(j, lag):
        return jnp.clip(j - lag, 0, nj - 1)

    return pl.pallas_call(
        functools.partial(_ffn_kernel, nj=nj),
        grid=(s // bm, nj + 2),
        in_specs=[
            pl.BlockSpec((bm, d), lambda m, j: (m, 0), pipeline_mode=pl.Buffered(1)),
            pl.BlockSpec((1, d), lambda m, j: (0, 0)),
            pl.BlockSpec((d, tf), lambda m, j: (0, tile(j, 0))),
            pl.BlockSpec((d, tf), lambda m, j: (0, nj + tile(j, 0))),
            pl.BlockSpec((CONV_WIDTH, tf), lambda m, j: (0, tile(j, 1))),
            pl.BlockSpec((CONV_WIDTH, tf), lambda m, j: (0, nj + tile(j, 1))),
            pl.BlockSpec((1, tf), lambda m, j: (0, tile(j, 1))),
            pl.BlockSpec((1, tf), lambda m, j: (0, nj + tile(j, 1))),
            pl.BlockSpec((tf, d), lambda m, j: (tile(j, 2), 0)),
        ],
        out_specs=pl.BlockSpec((bm, d), lambda m, j: (m, 0)),
        out_shape=jax.ShapeDtypeStruct((s, d), F32),
        scratch_shapes=[
            pltpu.VMEM((bm, d), BF16),
            pltpu.VMEM((CARRY_ROWS + bm, tf), F32),
            pltpu.VMEM((CARRY_ROWS + bm, tf), F32),
            pltpu.VMEM((CARRY_ROWS + bm, tf), F32),
            pltpu.VMEM((CARRY_ROWS + bm, tf), F32),
            pltpu.VMEM((bm, tf), BF16),
            pltpu.VMEM((bm, tf), BF16),
            pltpu.VMEM((nj, CARRY_ROWS, tf), F32),
            pltpu.VMEM((nj, CARRY_ROWS, tf), F32),
        ],
        compiler_params=_params("arbitrary", "arbitrary"),
        name="ffn",
    )(x, ln_w, w_up, w_up, conv_w, conv_w, conv_b, conv_b, w_down)


def _ple_kernel(x_ref, p_ref, lnw_ref, wg_ref, wp_ref, lnf_ref, o_ref, *, final):
    x = x_ref[...]
    h = _rms(x, lnw_ref[...]).astype(BF16)
    gate = jax.nn.sigmoid(_dot(h, wg_ref[...]))
    emb = _dot(p_ref[...].astype(BF16), wp_ref[...])
    y = x + gate * emb
    o_ref[...] = _rms(y, lnf_ref[...]) if final else y


def _ple(x, p, ln_w, w_gate, w_ple, ln_final, *, bm, final):
    s, d = x.shape
    e = p.shape[1]
    return pl.pallas_call(
        functools.partial(_ple_kernel, final=final),
        grid=(s // bm,),
        in_specs=[
            pl.BlockSpec((bm, d), lambda m: (m, 0)),
            pl.BlockSpec((bm, e), lambda m: (m, 0)),
            _resident((1, d)),
            _resident((d, d)),
            _resident((e, d)),
            _resident((1, d)),
        ],
        out_specs=pl.BlockSpec((bm, d), lambda m: (m, 0)),
        out_shape=jax.ShapeDtypeStruct((s, d), F32),
        compiler_params=_params("arbitrary"),
        name="ple_final",
    )(x, p, ln_w, w_gate, w_ple, ln_final)


def _pad_ff(w_up, conv_w, conv_b, w_down, tf):
    dff = w_down.shape[0]
    pad = (-dff) % tf
    if pad == 0:
        return w_up, conv_w, conv_b, w_down

    def pad_halves(a):
        g, v = a[..., :dff], a[..., dff:]
        widths = [(0, 0)] * (a.ndim - 1) + [(0, pad)]
        return jnp.concatenate([jnp.pad(g, widths), jnp.pad(v, widths)], axis=-1)

    return pad_halves(w_up), pad_halves(conv_w), pad_halves(conv_b), jnp.pad(w_down, ((0, pad), (0, 0)))


def kernel(x, p, ln_mix_w, w_in, pool_mix_w, pool_scale, hgrn_lb_logits, hgrn_norm_w, w_branch_a,
           w_branch_b, w_out, ln_ffn_w, w_up, conv_w, conv_b, w_down, ln_ple_w, w_ple_gate, w_ple,
           ln_final_w):
    batch, seq, d = x.shape
    depth = w_in.shape[0]
    c_pool = w_branch_a.shape[1]
    c_hgrn = w_branch_b.shape[1]
    blk = c_hgrn
    assert c_pool == blk and d == 2 * blk, "column blocks of the combined projection must line up"
    assert seq % 1024 == 0
    tf = 512
    outs = []
    for bi in range(batch):
        xb = x[bi]
        for i in range(depth):
            proj, f_logit = _inproj(xb, ln_mix_w[i][None], w_in[i].astype(BF16),
                                    bm=1024, bn=blk, f_block=2)
            ya = _pool_branch(proj, pool_mix_w[i].astype(BF16), pool_scale[i][None],
                              w_branch_a[i].astype(BF16), bm=1024, bn=blk, ga_block0=5)
            og = _hgrn(proj, f_logit, hgrn_lb_logits, hgrn_norm_w[i][None], bt=256, layer=i,
                       q_block=1, i_block=3, g_block=4)
            xb = _merge(og, proj, ya, xb, w_branch_b[i].astype(BF16), w_out[i].astype(BF16),
                        bm=512, gb_block0=7)
            wu, cw, cb, wd = _pad_ff(w_up[i], conv_w[i], conv_b[i][None], w_down[i], tf)
            xb = _ffn(xb, ln_ffn_w[i][None], wu.astype(BF16), cw, cb, wd.astype(BF16), bm=1024, tf=tf)
            xb = _ple(xb, p[i, bi], ln_ple_w[i][None], w_ple_gate[i].astype(BF16),
                      w_ple[i].astype(BF16), ln_final_w[None], bm=512, final=(i == depth - 1))
        outs.append(xb)
    return jnp.stack(outs, axis=0)
```

```python
import functools

import jax
import jax.numpy as jnp
from jax import lax
from jax.experimental import pallas as pl
from jax.experimental.pallas import tpu as pltpu

F32 = jnp.float32
BF16 = jnp.bfloat16

EPS = 1e-6
POOL_WINDOWS = (2, 4, 8, 16)
POOL_HALO = 16
HEAD_DIM = 128
CHUNK = 64
SUB = 16
NSUB = CHUNK // SUB
CONV_WIDTH = 3
CARRY_ROWS = 8
ACT_ROWS = 64
LANE = 128
LOG2_E = 1.4426950408889634
MILD_DECAY_LOG2 = -100.0
V7X_VMEM_BYTES = 64 * 1024 * 1024
VMEM_LIMIT = V7X_VMEM_BYTES - 8 * 1024 * 1024


def _dot(a, b):
    return jnp.dot(a, b, preferred_element_type=F32)


def _dot_nt(a, b):
    return lax.dot_general(a, b, (((1,), (1,)), ((), ())), preferred_element_type=F32)


def _dot_tn(a, b):
    return lax.dot_general(a, b, (((0,), (0,)), ((), ())), preferred_element_type=F32)


def _rms(x, w):
    return x * lax.rsqrt(jnp.mean(x * x, axis=-1, keepdims=True) + EPS) * w


def _params(*sem):
    return pltpu.CompilerParams(dimension_semantics=sem, vmem_limit_bytes=VMEM_LIMIT)


def _resident(shape):
    return pl.BlockSpec(shape, lambda *_: (0,) * len(shape), pipeline_mode=pl.Buffered(1))


def _inproj_kernel(x_ref, lnw_ref, w_ref, o_ref, f_ref, h_ref, *, f_block):
    n = pl.program_id(1)

    @pl.when(n == 0)
    def _():
        h_ref[...] = _rms(x_ref[...], lnw_ref[...]).astype(BF16)

    acc = _dot(h_ref[...], w_ref[...])
    o_ref[...] = acc.astype(o_ref.dtype)

    @pl.when(n == f_block)
    def _():
        f_ref[...] = acc


def _inproj(x, ln_w, w_in, *, bm, bn, f_block):
    s, d = x.shape
    d_in = w_in.shape[1]
    return pl.pallas_call(
        functools.partial(_inproj_kernel, f_block=f_block),
        grid=(s // bm, d_in // bn),
        in_specs=[
            pl.BlockSpec((bm, d), lambda m, n: (m, 0)),
            pl.BlockSpec((1, d), lambda m, n: (0, 0)),
            pl.BlockSpec((d, bn), lambda m, n: (0, n)),
        ],
        out_specs=[
            pl.BlockSpec((bm, bn), lambda m, n: (m, n)),
            pl.BlockSpec((bm, bn), lambda m, n: (m, 0)),
        ],
        out_shape=[
            jax.ShapeDtypeStruct((s, d_in), BF16),
            jax.ShapeDtypeStruct((s, bn), F32),
        ],
        scratch_shapes=[pltpu.VMEM((bm, d), BF16)],
        compiler_params=_params("arbitrary", "arbitrary"),
        name="inproj",
    )(x, ln_w, w_in)


def _pool_kernel(u_ref, halo_ref, ga_ref, mix_ref, scale_ref, wa_ref, o_ref, feat_ref, *, bm):
    m = pl.program_id(0)
    n = pl.program_id(1)

    @pl.when(n == 0)
    def _():
        u = u_ref[...].astype(F32)
        halo = halo_ref[...].astype(F32) * (m > 0).astype(F32)
        ext = jnp.concatenate([halo, u], axis=0)
        gw = u.shape[1] // len(POOL_WINDOWS)
        pos = m * bm + lax.broadcasted_iota(jnp.int32, (bm, 1), 0) + 1
        run = ext
        width = 1
        feats = []
        for g, w in enumerate(POOL_WINDOWS):
            while width < w:
                run = run + pltpu.roll(run, width, axis=0)
                width *= 2
            cnt = jnp.minimum(pos, w).astype(F32)
            win = run[POOL_HALO:, g * gw:(g + 1) * gw]
            d = win / cnt - u[:, g * gw:(g + 1) * gw]
            y = _dot(d.astype(BF16), mix_ref[g])
            feats.append(y * scale_ref[:, g * gw:(g + 1) * gw])
        feat_ref[...] = jnp.concatenate(feats, axis=1).astype(BF16)

    y_a = _dot(feat_ref[...], wa_ref[...])
    o_ref[...] = (jax.nn.sigmoid(ga_ref[...].astype(F32)) * y_a).astype(o_ref.dtype)


def _pool_branch(proj, mix_w, scale, w_a, *, bm, bn, ga_block0):
    s = proj.shape[0]
    c = w_a.shape[0]
    d = w_a.shape[1]
    g, gw, _ = mix_w.shape
    halo_per_tile = bm // POOL_HALO
    return pl.pallas_call(
        functools.partial(_pool_kernel, bm=bm),
        grid=(s // bm, d // bn),
        in_specs=[
            pl.BlockSpec((bm, c), lambda m, n: (m, 0)),
            pl.BlockSpec((POOL_HALO, c), lambda m, n: (jnp.maximum(m * halo_per_tile - 1, 0), 0)),
            pl.BlockSpec((bm, bn), lambda m, n: (m, ga_block0 + n)),
            _resident((g, gw, gw)),
            _resident((1, c)),
            pl.BlockSpec((c, bn), lambda m, n: (0, n)),
        ],
        out_specs=pl.BlockSpec((bm, bn), lambda m, n: (m, n)),
        out_shape=jax.ShapeDtypeStruct((s, d), BF16),
        scratch_shapes=[pltpu.VMEM((bm, c), BF16)],
        compiler_params=_params("arbitrary", "arbitrary"),
        name="pool_branch",
    )(proj, proj, proj, mix_w, scale, w_a)


def _split3(x):
    hi = x.astype(BF16)
    r = x - hi.astype(F32)
    mid = r.astype(BF16)
    lo = (r - mid.astype(F32)).astype(BF16)
    return hi, mid, lo


def _gates(fl, lb):
    one_m_lb = 1.0 - lb
    sg = jax.nn.sigmoid(fl)
    log2_f = jnp.log(lb + one_m_lb * sg) * LOG2_E
    return log2_f, one_m_lb * (1.0 - sg)


def _chunk_cumsum(tri, x):
    hi, mid, lo = _split3(x)
    return _dot(tri, hi) + _dot(tri, mid) + _dot(tri, lo)


def _head_norm_gate(o_heads, norm_w, g):
    o = jnp.concatenate(
        [o_h * lax.rsqrt(jnp.mean(o_h * o_h, axis=-1, keepdims=True) + EPS) for o_h in o_heads], axis=1)
    return o * norm_w * (g * jax.nn.sigmoid(g))


def _hgrn_chunk_any_decay(c, q_ref, f_ref, i_ref, g_ref, tri_ref, wsel_ref, o_ref, st_ref, lb, norm_w):
    r0 = pl.multiple_of(c * CHUNK, CHUNK)
    rows = pl.ds(r0, CHUNK)
    d = q_ref.shape[1]
    heads = d // HEAD_DIM

    q = q_ref[rows, :].astype(F32)
    v = i_ref[rows, :]
    log2_f, kk = _gates(f_ref[rows, :], lb)
    b = _chunk_cumsum(tri_ref[:CHUNK, :CHUNK], log2_f)
    b_last = b[CHUNK - 1:CHUNK, :]

    q_in = (q * jnp.exp2(b)).astype(BF16)
    k_st = (kk * jnp.exp2(b_last - b)).astype(BF16)

    b_end = jnp.concatenate(
        [jnp.broadcast_to(b[(j + 1) * SUB - 1:(j + 1) * SUB, :], (SUB, d)) for j in range(NSUB)], axis=0)
    k_hat = kk * jnp.exp2(b_end - b)
    sub_id = lax.broadcasted_iota(jnp.int32, (CHUNK, 1), 0) // SUB
    q_from = []
    for j in range(NSUB - 1):
        lo_row = (j + 1) * SUB
        bj = b[lo_row - 1:lo_row, :]
        part = q[lo_row:, :] * jnp.exp2(b[lo_row:, :] - bj)
        q_from.append(jnp.concatenate([jnp.zeros((lo_row, d), F32), part], axis=0).astype(BF16))
    k_src = [jnp.where(sub_id == j, k_hat, 0.0).astype(BF16) for j in range(NSUB - 1)]

    per_head = [[] for _ in range(heads)]
    for r in range(NSUB):
        qb = q[r * SUB:(r + 1) * SUB, :]
        bb = b[r * SUB:(r + 1) * SUB, :]
        pieces = []
        for s in range(SUB):
            row = r * SUB + s
            e = jnp.exp2(jnp.minimum(bb - b[row:row + 1, :], 0.0))
            pieces.append((qb * kk[row:row + 1, :] * e).astype(BF16))
        for h in range(heads):
            per_head[h].append(jnp.concatenate(
                [p[:, h * HEAD_DIM:(h + 1) * HEAD_DIM] for p in pieces], axis=1))
    a_big = jnp.concatenate([jnp.concatenate(blocks, axis=0) for blocks in per_head], axis=0)
    diag = _dot(a_big, wsel_ref[...])

    t_id = lax.broadcasted_iota(jnp.int32, (CHUNK, CHUNK), 0)
    s_id = lax.broadcasted_iota(jnp.int32, (CHUNK, CHUNK), 1)
    diag_mask = (t_id // SUB == s_id // SUB) & (s_id <= t_id)

    outs = []
    for h in range(heads):
        hs = slice(h * HEAD_DIM, (h + 1) * HEAD_DIM)
        q_cat = jnp.concatenate([qf[:, hs] for qf in q_from], axis=1)
        k_cat = jnp.concatenate([ks[:, hs] for ks in k_src], axis=1)
        scores = _dot_nt(q_cat, k_cat) + jnp.where(diag_mask, diag[h * CHUNK:(h + 1) * CHUNK, :], 0.0)
        v_h = v[:, hs]
        state_t = st_ref[h]
        o_h = _dot_nt(q_in[:, hs], state_t.astype(BF16)) + _dot(scores.astype(BF16), v_h)
        st_ref[h] = state_t * jnp.exp2(b_last[:, hs]) + _dot_tn(v_h, k_st[:, hs])
        outs.append(o_h)
    o_ref[rows, :] = _head_norm_gate(outs, norm_w, g_ref[rows, :].astype(F32)).astype(o_ref.dtype)


def _hgrn_tile_mild_decay(b, kk, q_ref, i_ref, g_ref, o_ref, st_ref, norm_w):
    bt, d = b.shape
    heads = d // HEAD_DIM
    n_chunks = bt // CHUNK
    q = q_ref[...].astype(F32)
    v = i_ref[...]
    b_last_rows = [b[(c + 1) * CHUNK - 1:(c + 1) * CHUNK, :] for c in range(n_chunks)]
    b_last = jnp.concatenate([jnp.broadcast_to(r, (CHUNK, d)) for r in b_last_rows], axis=0)
    q_in = (q * jnp.exp2(b)).astype(BF16)
    k_out = (kk * jnp.exp2(-b)).astype(BF16)
    k_st = (kk * jnp.exp2(b_last - b)).astype(BF16)
    causal = (lax.broadcasted_iota(jnp.int32, (CHUNK, CHUNK), 1)
              <= lax.broadcasted_iota(jnp.int32, (CHUNK, CHUNK), 0))
    states = [st_ref[h] for h in range(heads)]
    tile_out = []
    for c in range(n_chunks):
        rs = slice(c * CHUNK, (c + 1) * CHUNK)
        carry_decay = jnp.exp2(b_last_rows[c])
        outs = []
        for h in range(heads):
            hs = slice(h * HEAD_DIM, (h + 1) * HEAD_DIM)
            scores = jnp.where(causal, _dot_nt(q_in[rs, hs], k_out[rs, hs]), 0.0).astype(BF16)
            outs.append(_dot(scores, v[rs, hs]) + _dot_nt(q_in[rs, hs], states[h].astype(BF16)))
            states[h] = states[h] * carry_decay[:, hs] + _dot_tn(v[rs, hs], k_st[rs, hs])
        tile_out.append(jnp.concatenate(outs, axis=1))
    for h in range(heads):
        st_ref[h] = states[h]
    heads_out = jnp.concatenate(tile_out, axis=0)
    o_heads = [heads_out[:, h * HEAD_DIM:(h + 1) * HEAD_DIM] for h in range(heads)]
    o_ref[...] = _head_norm_gate(o_heads, norm_w, g_ref[...].astype(F32)).astype(o_ref.dtype)


def _hgrn_kernel(q_ref, f_ref, i_ref, g_ref, lbl_ref, nw_ref, tri_ref, wsel_ref, o_ref, st_ref, *, layer):
    @pl.when(pl.program_id(0) == 0)
    def _():
        st_ref[...] = jnp.zeros_like(st_ref)

    logits = lbl_ref[...]
    e = jnp.exp(logits - jnp.max(logits, axis=0, keepdims=True))
    lb = jnp.sum(e[:layer + 1, :], axis=0, keepdims=True) / jnp.sum(e, axis=0, keepdims=True)
    norm_w = nw_ref[...]

    log2_f, kk = _gates(f_ref[...], lb)
    b = _chunk_cumsum(tri_ref[...], log2_f)
    mild = jnp.min(b) >= MILD_DECAY_LOG2

    @pl.when(mild)
    def _():
        _hgrn_tile_mild_decay(b, kk, q_ref, i_ref, g_ref, o_ref, st_ref, norm_w)

    @pl.when(jnp.logical_not(mild))
    def _():
        def body(c, carry):
            _hgrn_chunk_any_decay(c, q_ref, f_ref, i_ref, g_ref, tri_ref, wsel_ref, o_ref, st_ref, lb, norm_w)
            return carry

        lax.fori_loop(0, q_ref.shape[0] // CHUNK, body, 0)


def _hgrn(proj, f_logit, lb_logits, norm_w, *, bt, layer, q_block, i_block, g_block):
    s = proj.shape[0]
    d = f_logit.shape[1]
    heads = d // HEAD_DIM
    t_id = lax.broadcasted_iota(jnp.int32, (bt, bt), 0)
    s_id = lax.broadcasted_iota(jnp.int32, (bt, bt), 1)
    tri = ((s_id <= t_id) & (s_id // CHUNK == t_id // CHUNK)).astype(BF16)
    src = lax.broadcasted_iota(jnp.int32, (SUB * HEAD_DIM, CHUNK), 0) // HEAD_DIM
    col = lax.broadcasted_iota(jnp.int32, (SUB * HEAD_DIM, CHUNK), 1) % SUB
    wsel = (src == col).astype(BF16)
    nl = lb_logits.shape[0]
    return pl.pallas_call(
        functools.partial(_hgrn_kernel, layer=layer),
        grid=(s // bt,),
        in_specs=[
            pl.BlockSpec((bt, d), lambda t: (t, q_block)),
            pl.BlockSpec((bt, d), lambda t: (t, 0)),
            pl.BlockSpec((bt, d), lambda t: (t, i_block)),
            pl.BlockSpec((bt, d), lambda t: (t, g_block)),
            _resident((nl, d)),
            _resident((1, d)),
            _resident((bt, bt)),
            _resident((SUB * HEAD_DIM, CHUNK)),
        ],
        out_specs=pl.BlockSpec((bt, d), lambda t: (t, 0)),
        out_shape=jax.ShapeDtypeStruct((s, d), BF16),
        scratch_shapes=[pltpu.VMEM((heads, HEAD_DIM, HEAD_DIM), F32)],
        compiler_params=_params("arbitrary"),
        name="hgrn",
    )(proj, f_logit, proj, proj, lb_logits, norm_w, tri, wsel)


def _merge_kernel(og_ref, gb0_ref, gb1_ref, ya_ref, x_ref, wb_ref, wo_ref, o_ref):
    y_b = _dot(og_ref[...], wb_ref[...])
    gate_b = jnp.concatenate([gb0_ref[...], gb1_ref[...]], axis=1).astype(F32)
    merged = ya_ref[...].astype(F32) + jax.nn.sigmoid(gate_b) * y_b
    o_ref[...] = x_ref[...] + _dot(merged.astype(BF16), wo_ref[...])


def _merge(og, proj, ya, x, w_b, w_out, *, bm, gb_block0):
    s, d = x.shape
    c = og.shape[1]
    half = d // 2
    return pl.pallas_call(
        _merge_kernel,
        grid=(s // bm,),
        in_specs=[
            pl.BlockSpec((bm, c), lambda m: (m, 0)),
            pl.BlockSpec((bm, half), lambda m: (m, gb_block0)),
            pl.BlockSpec((bm, half), lambda m: (m, gb_block0 + 1)),
            pl.BlockSpec((bm, d), lambda m: (m, 0)),
            pl.BlockSpec((bm, d), lambda m: (m, 0)),
            _resident((c, d)),
            _resident((d, d)),
        ],
        out_specs=pl.BlockSpec((bm, d), lambda m: (m, 0)),
        out_shape=jax.ShapeDtypeStruct((s, d), F32),
        compiler_params=_params("arbitrary"),
        name="merge_out",
    )(og, proj, proj, ya, x, w_b, w_out)


def _causal_conv(z_ref, row0, cols, w_ref, b_ref):
    ext = z_ref[pl.ds(row0, CARRY_ROWS + ACT_ROWS), cols]
    y = b_ref[:, cols] + w_ref[CONV_WIDTH - 1:CONV_WIDTH, cols] * ext[CARRY_ROWS:, :]
    for back in range(1, CONV_WIDTH):
        tap = CONV_WIDTH - 1 - back
        y = y + w_ref[tap:tap + 1, cols] * pltpu.roll(ext, back, axis=0)[CARRY_ROWS:, :]
    return y


def _ffn_kernel(x_ref, lnw_ref, wug_ref, wuv_ref, cwg_ref, cwv_ref, cbg_ref, cbv_ref, wd_ref,
                o_ref, h_ref, zg0_ref, zv0_ref, zg1_ref, zv1_ref, a0_ref, a1_ref, cg_ref, cv_ref, *, nj):
    m = pl.program_id(0)
    j = pl.program_id(1)
    bm = h_ref.shape[0]
    jp = jnp.clip(j - 1, 0, nj - 1)

    @pl.when(j == 0)
    def _():
        x = x_ref[...]
        h_ref[...] = _rms(x, lnw_ref[...]).astype(BF16)
        o_ref[...] = x

    @pl.when((j == 0) & (m == 0))
    def _():
        cg_ref[...] = jnp.zeros_like(cg_ref)
        cv_ref[...] = jnp.zeros_like(cv_ref)

    pairs = ((zg0_ref, zv0_ref), (zg1_ref, zv1_ref))

    acts = (a0_ref, a1_ref)

    def up(slot):
        zg_ref, zv_ref = pairs[slot]
        h = h_ref[...]
        zg_ref[CARRY_ROWS:, :] = _dot(h, wug_ref[...])
        zv_ref[CARRY_ROWS:, :] = _dot(h, wuv_ref[...])

    def conv_gate(slot):
        zg_ref, zv_ref = pairs[slot]
        zg_ref[:CARRY_ROWS, :] = cg_ref[jp]
        zv_ref[:CARRY_ROWS, :] = cv_ref[jp]
        for c in range(zg_ref.shape[1] // LANE):
            cols = slice(c * LANE, (c + 1) * LANE)
            for r in range(bm // ACT_ROWS):
                ug = _causal_conv(zg_ref, r * ACT_ROWS, cols, cwg_ref, cbg_ref)
                uv = _causal_conv(zv_ref, r * ACT_ROWS, cols, cwv_ref, cbv_ref)
                acts[slot][r * ACT_ROWS:(r + 1) * ACT_ROWS, cols] = (
                    ug * jax.nn.sigmoid(ug) * uv).astype(BF16)
        cg_ref[jp] = zg_ref[bm:, :]
        cv_ref[jp] = zv_ref[bm:, :]

    def down(slot):
        o_ref[...] += _dot(acts[slot][...], wd_ref[...])

    def stages(slot, do_up, do_conv, do_down):
        if do_conv:
            conv_gate(1 - slot)
        if do_up:
            up(slot)
        if do_down:
            down(slot)

    @pl.when(j == 0)
    def _():
        stages(0, True, False, False)

    @pl.when(j == 1)
    def _():
        stages(1, True, True, False)

    for parity in range(2):
        @pl.when((j >= 2) & (j < nj) & (j % 2 == parity))
        def _():
            stages(parity, True, True, True)

    @pl.when(j == nj)
    def _():
        stages(nj % 2, False, True, True)

    @pl.when(j == nj + 1)
    def _():
        down((nj + 1) % 2)


def _ffn(x, ln_w, w_up, conv_w, conv_b, w_down, *, bm, tf):
    s, d = x.shape
    dff = w_down.shape[0]
    nj = dff // tf
    def tile(j, lag):
        return jnp.clip(j - lag, 0, nj - 1)

    return pl.pallas_call(
        functools.partial(_ffn_kernel, nj=nj),
        grid=(s // bm, nj + 2),
        in_specs=[
            pl.BlockSpec((bm, d), lambda m, j: (m, 0), pipeline_mode=pl.Buffered(1)),
            pl.BlockSpec((1, d), lambda m, j: (0, 0)),
            pl.BlockSpec((d, tf), lambda m, j: (0, tile(j, 0))),
            pl.BlockSpec((d, tf), lambda m, j: (0, nj + tile(j, 0))),
            pl.BlockSpec((CONV_WIDTH, tf), lambda m, j: (0, tile(j, 1))),
            pl.BlockSpec((CONV_WIDTH, tf), lambda m, j: (0, nj + tile(j, 1))),
            pl.BlockSpec((1, tf), lambda m, j: (0, tile(j, 1))),
            pl.BlockSpec((1, tf), lambda m, j: (0, nj + tile(j, 1))),
            pl.BlockSpec((tf, d), lambda m, j: (tile(j, 2), 0)),
        ],
        out_specs=pl.BlockSpec((bm, d), lambda m, j: (m, 0)),
        out_shape=jax.ShapeDtypeStruct((s, d), F32),
        scratch_shapes=[
            pltpu.VMEM((bm, d), BF16),
            pltpu.VMEM((CARRY_ROWS + bm, tf), F32),
            pltpu.VMEM((CARRY_ROWS + bm, tf), F32),
            pltpu.VMEM((CARRY_ROWS + bm, tf), F32),
            pltpu.VMEM((CARRY_ROWS + bm, tf), F32),
            pltpu.VMEM((bm, tf), BF16),
            pltpu.VMEM((bm, tf), BF16),
            pltpu.VMEM((nj, CARRY_ROWS, tf), F32),
            pltpu.VMEM((nj, CARRY_ROWS, tf), F32),
        ],
        compiler_params=_params("arbitrary", "arbitrary"),
        name="ffn",
    )(x, ln_w, w_up, w_up, conv_w, conv_w, conv_b, conv_b, w_down)


def _ple_kernel(x_ref, p_ref, lnw_ref, wg_ref, wp_ref, lnf_ref, o_ref, *, final):
    x = x_ref[...]
    h = _rms(x, lnw_ref[...]).astype(BF16)
    gate = jax.nn.sigmoid(_dot(h, wg_ref[...]))
    emb = _dot(p_ref[...].astype(BF16), wp_ref[...])
    y = x + gate * emb
    o_ref[...] = _rms(y, lnf_ref[...]) if final else y


def _ple(x, p, ln_w, w_gate, w_ple, ln_final, *, bm, final):
    s, d = x.shape
    e = p.shape[1]
    return pl.pallas_call(
        functools.partial(_ple_kernel, final=final),
        grid=(s // bm,),
        in_specs=[
            pl.BlockSpec((bm, d), lambda m: (m, 0)),
            pl.BlockSpec((bm, e), lambda m: (m, 0)),
            _resident((1, d)),
            _resident((d, d)),
            _resident((e, d)),
            _resident((1, d)),
        ],
        out_specs=pl.BlockSpec((bm, d), lambda m: (m, 0)),
        out_shape=jax.ShapeDtypeStruct((s, d), F32),
        compiler_params=_params("arbitrary"),
        name="ple_final",
    )(x, p, ln_w, w_gate, w_ple, ln_final)


def _pad_ff(w_up, conv_w, conv_b, w_down, tf):
    dff = w_down.shape[0]
    pad = (-dff) % tf
    if pad == 0:
        return w_up, conv_w, conv_b, w_down

    def pad_halves(a):
        g, v = a[..., :dff], a[..., dff:]
        widths = [(0, 0)] * (a.ndim - 1) + [(0, pad)]
        return jnp.concatenate([jnp.pad(g, widths), jnp.pad(v, widths)], axis=-1)

    return pad_halves(w_up), pad_halves(conv_w), pad_halves(conv_b), jnp.pad(w_down, ((0, pad), (0, 0)))


def kernel(x, p, ln_mix_w, w_in, pool_mix_w, pool_scale, hgrn_lb_logits, hgrn_norm_w, w_branch_a,
           w_branch_b, w_out, ln_ffn_w, w_up, conv_w, conv_b, w_down, ln_ple_w, w_ple_gate, w_ple,
           ln_final_w):
    batch, seq, d = x.shape
    depth = w_in.shape[0]
    c_pool = w_branch_a.shape[1]
    c_hgrn = w_branch_b.shape[1]
    blk = c_hgrn
    assert c_pool == blk and d == 2 * blk, "column blocks of the combined projection must line up"
    assert seq % 1024 == 0
    tf = 512
    outs = []
    for bi in range(batch):
        xb = x[bi]
        for i in range(depth):
            proj, f_logit = _inproj(xb, ln_mix_w[i][None], w_in[i].astype(BF16),
                                    bm=1024, bn=blk, f_block=2)
            ya = _pool_branch(proj, pool_mix_w[i].astype(BF16), pool_scale[i][None],
                              w_branch_a[i].astype(BF16), bm=1024, bn=blk, ga_block0=5)
            og = _hgrn(proj, f_logit, hgrn_lb_logits, hgrn_norm_w[i][None], bt=256, layer=i,
                       q_block=1, i_block=3, g_block=4)
            xb = _merge(og, proj, ya, xb, w_branch_b[i].astype(BF16), w_out[i].astype(BF16),
                        bm=512, gb_block0=7)
            wu, cw, cb, wd = _pad_ff(w_up[i], conv_w[i], conv_b[i][None], w_down[i], tf)
            xb = _ffn(xb, ln_ffn_w[i][None], wu.astype(BF16), cw, cb, wd.astype(BF16), bm=1024, tf=tf)
            xb = _ple(xb, p[i, bi], ln_ple_w[i][None], w_ple_gate[i].astype(BF16),
                      w_ple[i].astype(BF16), ln_final_w[None], bm=512, final=(i == depth - 1))
        outs.append(xb)
    return jnp.stack(outs, axis=0)
```

```python
import functools

import jax
import jax.numpy as jnp
from jax import lax
from jax.experimental import pallas as pl
from jax.experimental.pallas import tpu as pltpu

F32 = jnp.float32
BF16 = jnp.bfloat16

EPS = 1e-6
POOL_WINDOWS = (2, 4, 8, 16)
POOL_HALO = 16
HEAD_DIM = 128
CHUNK = 64
SUB = 16
NSUB = CHUNK // SUB
CONV_WIDTH = 3
CARRY_ROWS = 8
ACT_ROWS = 64
LANE = 128
LOG2_E = 1.4426950408889634
MILD_DECAY_LOG2 = -100.0
V7X_VMEM_BYTES = 64 * 1024 * 1024
VMEM_LIMIT = V7X_VMEM_BYTES - 8 * 1024 * 1024


def _dot(a, b):
    return jnp.dot(a, b, preferred_element_type=F32)


def _dot_nt(a, b):
    return lax.dot_general(a, b, (((1,), (1,)), ((), ())), preferred_element_type=F32)


def _dot_tn(a, b):
    return lax.dot_general(a, b, (((0,), (0,)), ((), ())), preferred_element_type=F32)


def _rms(x, w):
    return x * lax.rsqrt(jnp.mean(x * x, axis=-1, keepdims=True) + EPS) * w


def _params(*sem):
    return pltpu.CompilerParams(dimension_semantics=sem, vmem_limit_bytes=VMEM_LIMIT)


def _resident(shape):
    return pl.BlockSpec(shape, lambda *_: (0,) * len(shape), pipeline_mode=pl.Buffered(1))


def _inproj_kernel(x_ref, lnw_ref, w_ref, o_ref, f_ref, h_ref, *, f_block):
    n = pl.program_id(1)

    @pl.when(n == 0)
    def _():
        h_ref[...] = _rms(x_ref[...], lnw_ref[...]).astype(BF16)

    acc = _dot(h_ref[...], w_ref[...])
    o_ref[...] = acc.astype(o_ref.dtype)

    @pl.when(n == f_block)
    def _():
        f_ref[...] = acc


def _inproj(x, ln_w, w_in, *, bm, bn, f_block):
    s, d = x.shape
    d_in = w_in.shape[1]
    return pl.pallas_call(
        functools.partial(_inproj_kernel, f_block=f_block),
        grid=(s // bm, d_in // bn),
        in_specs=[
            pl.BlockSpec((bm, d), lambda m, n: (m, 0)),
            pl.BlockSpec((1, d), lambda m, n: (0, 0)),
            pl.BlockSpec((d, bn), lambda m, n: (0, n)),
        ],
        out_specs=[
            pl.BlockSpec((bm, bn), lambda m, n: (m, n)),
            pl.BlockSpec((bm, bn), lambda m, n: (m, 0)),
        ],
        out_shape=[
            jax.ShapeDtypeStruct((s, d_in), BF16),
            jax.ShapeDtypeStruct((s, bn), F32),
        ],
        scratch_shapes=[pltpu.VMEM((bm, d), BF16)],
        compiler_params=_params("arbitrary", "arbitrary"),
        name="inproj",
    )(x, ln_w, w_in)


def _pool_kernel(u_ref, halo_ref, ga_ref, mix_ref, scale_ref, wa_ref, o_ref, feat_ref, *, bm):
    m = pl.program_id(0)
    n = pl.program_id(1)

    @pl.when(n == 0)
    def _():
        u = u_ref[...].astype(F32)
        halo = halo_ref[...].astype(F32) * (m > 0).astype(F32)
        ext = jnp.concatenate([halo, u], axis=0)
        gw = u.shape[1] // len(POOL_WINDOWS)
        pos = m * bm + lax.broadcasted_iota(jnp.int32, (bm, 1), 0) + 1
        run = ext
        width = 1
        feats = []
        for g, w in enumerate(POOL_WINDOWS):
            while width < w:
                run = run + pltpu.roll(run, width, axis=0)
                width *= 2
            cnt = jnp.minimum(pos, w).astype(F32)
            win = run[POOL_HALO:, g * gw:(g + 1) * gw]
            d = win / cnt - u[:, g * gw:(g + 1) * gw]
            y = _dot(d.astype(BF16), mix_ref[g])
            feats.append(y * scale_ref[:, g * gw:(g + 1) * gw])
        feat_ref[...] = jnp.concatenate(feats, axis=1).astype(BF16)

    y_a = _dot(feat_ref[...], wa_ref[...])
    o_ref[...] = (jax.nn.sigmoid(ga_ref[...].astype(F32)) * y_a).astype(o_ref.dtype)


def _pool_branch(proj, mix_w, scale, w_a, *, bm, bn, ga_block0):
    s = proj.shape[0]
    c = w_a.shape[0]
    d = w_a.shape[1]
    g, gw, _ = mix_w.shape
    halo_per_tile = bm // POOL_HALO
    return pl.pallas_call(
        functools.partial(_pool_kernel, bm=bm),
        grid=(s // bm, d // bn),
        in_specs=[
            pl.BlockSpec((bm, c), lambda m, n: (m, 0)),
            pl.BlockSpec((POOL_HALO, c), lambda m, n: (jnp.maximum(m * halo_per_tile - 1, 0), 0)),
            pl.BlockSpec((bm, bn), lambda m, n: (m, ga_block0 + n)),
            _resident((g, gw, gw)),
            _resident((1, c)),
            pl.BlockSpec((c, bn), lambda m, n: (0, n)),
        ],
        out_specs=pl.BlockSpec((bm, bn), lambda m, n: (m, n)),
        out_shape=jax.ShapeDtypeStruct((s, d), BF16),
        scratch_shapes=[pltpu.VMEM((bm, c), BF16)],
        compiler_params=_params("arbitrary", "arbitrary"),
        name="pool_branch",
    )(proj, proj, proj, mix_w, scale, w_a)


def _split3(x):
    hi = x.astype(BF16)
    r = x - hi.astype(F32)
    mid = r.astype(BF16)
    lo = (r - mid.astype(F32)).astype(BF16)
    return hi, mid, lo


def _gates(fl, lb):
    one_m_lb = 1.0 - lb
    sg = jax.nn.sigmoid(fl)
    log2_f = jnp.log(lb + one_m_lb * sg) * LOG2_E
    return log2_f, one_m_lb * (1.0 - sg)


def _chunk_cumsum(tri, x):
    hi, mid, lo = _split3(x)
    return _dot(tri, hi) + _dot(tri, mid) + _dot(tri, lo)


def _head_norm_gate(o_heads, norm_w, g):
    o = jnp.concatenate(
        [o_h * lax.rsqrt(jnp.mean(o_h * o_h, axis=-1, keepdims=True) + EPS) for o_h in o_heads], axis=1)
    return o * norm_w * (g * jax.nn.sigmoid(g))


def _hgrn_chunk_any_decay(c, q_ref, f_ref, i_ref, g_ref, tri_ref, wsel_ref, o_ref, st_ref, lb, norm_w):
    r0 = pl.multiple_of(c * CHUNK, CHUNK)
    rows = pl.ds(r0, CHUNK)
    d = q_ref.shape[1]
    heads = d // HEAD_DIM

    q = q_ref[rows, :].astype(F32)
    v = i_ref[rows, :]
    log2_f, kk = _gates(f_ref[rows, :], lb)
    b = _chunk_cumsum(tri_ref[:CHUNK, :CHUNK], log2_f)
    b_last = b[CHUNK - 1:CHUNK, :]

    q_in = (q * jnp.exp2(b)).astype(BF16)
    k_st = (kk * jnp.exp2(b_last - b)).astype(BF16)

    b_end = jnp.concatenate(
        [jnp.broadcast_to(b[(j + 1) * SUB - 1:(j + 1) * SUB, :], (SUB, d)) for j in range(NSUB)], axis=0)
    k_hat = kk * jnp.exp2(b_end - b)
    sub_id = lax.broadcasted_iota(jnp.int32, (CHUNK, 1), 0) // SUB
    q_from = []
    for j in range(NSUB - 1):
        lo_row = (j + 1) * SUB
        bj = b[lo_row - 1:lo_row, :]
        part = q[lo_row:, :] * jnp.exp2(b[lo_row:, :] - bj)
        q_from.append(jnp.concatenate([jnp.zeros((lo_row, d), F32), part], axis=0).astype(BF16))
    k_src = [jnp.where(sub_id == j, k_hat, 0.0).astype(BF16) for j in range(NSUB - 1)]

    per_head = [[] for _ in range(heads)]
    for r in range(NSUB):
        qb = q[r * SUB:(r + 1) * SUB, :]
        bb = b[r * SUB:(r + 1) * SUB, :]
        pieces = []
        for s in range(SUB):
            row = r * SUB + s
            e = jnp.exp2(jnp.minimum(bb - b[row:row + 1, :], 0.0))
            pieces.append((qb * kk[row:row + 1, :] * e).astype(BF16))
        for h in range(heads):
            per_head[h].append(jnp.concatenate(
                [p[:, h * HEAD_DIM:(h + 1) * HEAD_DIM] for p in pieces], axis=1))
    a_big = jnp.concatenate([jnp.concatenate(blocks, axis=0) for blocks in per_head], axis=0)
    diag = _dot(a_big, wsel_ref[...])

    t_id = lax.broadcasted_iota(jnp.int32, (CHUNK, CHUNK), 0)
    s_id = lax.broadcasted_iota(jnp.int32, (CHUNK, CHUNK), 1)
    diag_mask = (t_id // SUB == s_id // SUB) & (s_id <= t_id)

    outs = []
    for h in range(heads):
        hs = slice(h * HEAD_DIM, (h + 1) * HEAD_DIM)
        q_cat = jnp.concatenate([qf[:, hs] for qf in q_from], axis=1)
        k_cat = jnp.concatenate([ks[:, hs] for ks in k_src], axis=1)
        scores = _dot_nt(q_cat, k_cat) + jnp.where(diag_mask, diag[h * CHUNK:(h + 1) * CHUNK, :], 0.0)
        v_h = v[:, hs]
        state_t = st_ref[h]
        o_h = _dot_nt(q_in[:, hs], state_t.astype(BF16)) + _dot(scores.astype(BF16), v_h)
        st_ref[h] = state_t * jnp.exp2(b_last[:, hs]) + _dot_tn(v_h, k_st[:, hs])
        outs.append(o_h)
    o_ref[rows, :] = _head_norm_gate(outs, norm_w, g_ref[rows, :].astype(F32)).astype(o_ref.dtype)


def _hgrn_tile_mild_decay(b, kk, q_ref, i_ref, g_ref, o_ref, st_ref, norm_w):
    bt, d = b.shape
    heads = d // HEAD_DIM
    n_chunks = bt // CHUNK
    q = q_ref[...].astype(F32)
    v = i_ref[...]
    b_last_rows = [b[(c + 1) * CHUNK - 1:(c + 1) * CHUNK, :] for c in range(n_chunks)]
    b_last = jnp.concatenate([jnp.broadcast_to(r, (CHUNK, d)) for r in b_last_rows], axis=0)
    q_in = (q * jnp.exp2(b)).astype(BF16)
    k_out = (kk * jnp.exp2(-b)).astype(BF16)
    k_st = (kk * jnp.exp2(b_last - b)).astype(BF16)
    causal = (lax.broadcasted_iota(jnp.int32, (CHUNK, CHUNK), 1)
              <= lax.broadcasted_iota(jnp.int32, (CHUNK, CHUNK), 0))
    states = [st_ref[h] for h in range(heads)]
    tile_out = []
    for c in range(n_chunks):
        rs = slice(c * CHUNK, (c + 1) * CHUNK)
        carry_decay = jnp.exp2(b_last_rows[c])
        outs = []
        for h in range(heads):
            hs = slice(h * HEAD_DIM, (h + 1) * HEAD_DIM)
            scores = jnp.where(causal, _dot_nt(q_in[rs, hs], k_out[rs, hs]), 0.0).astype(BF16)
            outs.append(_dot(scores, v[rs, hs]) + _dot_nt(q_in[rs, hs], states[h].astype(BF16)))
            states[h] = states[h] * carry_decay[:, hs] + _dot_tn(v[rs, hs], k_st[rs, hs])
        tile_out.append(jnp.concatenate(outs, axis=1))
    for h in range(heads):
        st_ref[h] = states[h]
    heads_out = jnp.concatenate(tile_out, axis=0)
    o_heads = [heads_out[:, h * HEAD_DIM:(h + 1) * HEAD_DIM] for h in range(heads)]
    o_ref[...] = _head_norm_gate(o_heads, norm_w, g_ref[...].astype(F32)).astype(o_ref.dtype)


def _hgrn_kernel(q_ref, f_ref, i_ref, g_ref, lbl_ref, nw_ref, tri_ref, wsel_ref, o_ref, st_ref, *, layer):
    @pl.when(pl.program_id(0) == 0)
    def _():
        st_ref[...] = jnp.zeros_like(st_ref)

    logits = lbl_ref[...]
    e = jnp.exp(logits - jnp.max(logits, axis=0, keepdims=True))
    lb = jnp.sum(e[:layer + 1, :], axis=0, keepdims=True) / jnp.sum(e, axis=0, keepdims=True)
    norm_w = nw_ref[...]

    log2_f, kk = _gates(f_ref[...], lb)
    b = _chunk_cumsum(tri_ref[...], log2_f)
    mild = jnp.min(b) >= MILD_DECAY_LOG2

    @pl.when(mild)
    def _():
        _hgrn_tile_mild_decay(b, kk, q_ref, i_ref, g_ref, o_ref, st_ref, norm_w)

    @pl.when(jnp.logical_not(mild))
    def _():
        def body(c, carry):
            _hgrn_chunk_any_decay(c, q_ref, f_ref, i_ref, g_ref, tri_ref, wsel_ref, o_ref, st_ref, lb, norm_w)
            return carry

        lax.fori_loop(0, q_ref.shape[0] // CHUNK, body, 0)


def _hgrn(proj, f_logit, lb_logits, norm_w, *, bt, layer, q_block, i_block, g_block):
    s = proj.shape[0]
    d = f_logit.shape[1]
    heads = d // HEAD_DIM
    t_id = lax.broadcasted_iota(jnp.int32, (bt, bt), 0)
    s_id = lax.broadcasted_iota(jnp.int32, (bt, bt), 1)
    tri = ((s_id <= t_id) & (s_id // CHUNK == t_id // CHUNK)).astype(BF16)
    src = lax.broadcasted_iota(jnp.int32, (SUB * HEAD_DIM, CHUNK), 0) // HEAD_DIM
    col = lax.broadcasted_iota(jnp.int32, (SUB * HEAD_DIM, CHUNK), 1) % SUB
    wsel = (src == col).astype(BF16)
    nl = lb_logits.shape[0]
    return pl.pallas_call(
        functools.partial(_hgrn_kernel, layer=layer),
        grid=(s // bt,),
        in_specs=[
            pl.BlockSpec((bt, d), lambda t: (t, q_block)),
            pl.BlockSpec((bt, d), lambda t: (t, 0)),
            pl.BlockSpec((bt, d), lambda t: (t, i_block)),
            pl.BlockSpec((bt, d), lambda t: (t, g_block)),
            _resident((nl, d)),
            _resident((1, d)),
            _resident((bt, bt)),
            _resident((SUB * HEAD_DIM, CHUNK)),
        ],
        out_specs=pl.BlockSpec((bt, d), lambda t: (t, 0)),
        out_shape=jax.ShapeDtypeStruct((s, d), BF16),
        scratch_shapes=[pltpu.VMEM((heads, HEAD_DIM, HEAD_DIM), F32)],
        compiler_params=_params("arbitrary"),
        name="hgrn",
    )(proj, f_logit, proj, proj, lb_logits, norm_w, tri, wsel)


def _merge_kernel(og_ref, gb0_ref, gb1_ref, ya_ref, x_ref, wb_ref, wo_ref, o_ref):
    y_b = _dot(og_ref[...], wb_ref[...])
    gate_b = jnp.concatenate([gb0_ref[...], gb1_ref[...]], axis=1).astype(F32)
    merged = ya_ref[...].astype(F32) + jax.nn.sigmoid(gate_b) * y_b
    o_ref[...] = x_ref[...] + _dot(merged.astype(BF16), wo_ref[...])


def _merge(og, proj, ya, x, w_b, w_out, *, bm, gb_block0):
    s, d = x.shape
    c = og.shape[1]
    half = d // 2
    return pl.pallas_call(
        _merge_kernel,
        grid=(s // bm,),
        in_specs=[
            pl.BlockSpec((bm, c), lambda m: (m, 0)),
            pl.BlockSpec((bm, half), lambda m: (m, gb_block0)),
            pl.BlockSpec((bm, half), lambda m: (m, gb_block0 + 1)),
            pl.BlockSpec((bm, d), lambda m: (m, 0)),
            pl.BlockSpec((bm, d), lambda m: (m, 0)),
            _resident((c, d)),
            _resident((d, d)),
        ],
        out_specs=pl.BlockSpec((bm, d), lambda m: (m, 0)),
        out_shape=jax.ShapeDtypeStruct((s, d), F32),
        compiler_params=_params("arbitrary"),
        name="merge_out",
    )(og, proj, proj, ya, x, w_b, w_out)


def _causal_conv(z_ref, row0, cols, w_ref, b_ref):
    ext = z_ref[pl.ds(row0, CARRY_ROWS + ACT_ROWS), cols]
    y = b_ref[:, cols] + w_ref[CONV_WIDTH - 1:CONV_WIDTH, cols] * ext[CARRY_ROWS:, :]
    for back in range(1, CONV_WIDTH):
        tap = CONV_WIDTH - 1 - back
        y = y + w_ref[tap:tap + 1, cols] * pltpu.roll(ext, back, axis=0)[CARRY_ROWS:, :]
    return y


def _ffn_kernel(x_ref, lnw_ref, wug_ref, wuv_ref, cwg_ref, cwv_ref, cbg_ref, cbv_ref, wd_ref,
                o_ref, h_ref, zg_new_ref, zv_new_ref, zg_ref, zv_ref, a_new_ref, a_ref, cg_ref, cv_ref,
                *, nj, n_tiles):
    t = pl.program_id(0)
    bm = h_ref.shape[0]
    col_up = jnp.minimum(t, n_tiles - 1) % nj
    has_conv = (t >= 1) & (t <= n_tiles)
    col_conv = jnp.clip(t - 1, 0, n_tiles - 1) % nj
    col_down = jnp.clip(t - 2, 0, n_tiles - 1) % nj
    has_down = t >= 2

    @pl.when(t == 0)
    def _():
        zg_ref[...] = jnp.zeros_like(zg_ref)
        zv_ref[...] = jnp.zeros_like(zv_ref)
        a_ref[...] = jnp.zeros_like(a_ref)
        cg_ref[...] = jnp.zeros_like(cg_ref)
        cv_ref[...] = jnp.zeros_like(cv_ref)
        o_ref[...] = jnp.zeros_like(o_ref)

    @pl.when((t < n_tiles) & (col_up == 0))
    def _():
        h_ref[...] = _rms(x_ref[...], lnw_ref[...]).astype(BF16)

    @pl.when(has_down & (col_down == 0))
    def _():
        o_ref[...] = x_ref[...]

    keep = has_conv.astype(F32)
    carry_to = jnp.where(has_conv, col_conv, nj)
    zg_ref[:CARRY_ROWS, :] = cg_ref[col_conv]
    zv_ref[:CARRY_ROWS, :] = cv_ref[col_conv]
    for c in range(zg_ref.shape[1] // LANE):
        cols = slice(c * LANE, (c + 1) * LANE)
        for r in range(bm // ACT_ROWS):
            ug = _causal_conv(zg_ref, r * ACT_ROWS, cols, cwg_ref, cbg_ref)
            uv = _causal_conv(zv_ref, r * ACT_ROWS, cols, cwv_ref, cbv_ref)
            a_new_ref[r * ACT_ROWS:(r + 1) * ACT_ROWS, cols] = (
                ug * jax.nn.sigmoid(ug) * (uv * keep)).astype(BF16)
    cg_ref[carry_to] = zg_ref[bm:, :]
    cv_ref[carry_to] = zv_ref[bm:, :]

    h = h_ref[...]
    zg_new_ref[...] = _dot(h, wug_ref[...])
    zv_new_ref[...] = _dot(h, wuv_ref[...])

    o_ref[...] += _dot(a_ref[...], wd_ref[...])

    @pl.when(t < n_tiles + 1)
    def _():
        zg_ref[CARRY_ROWS:, :] = zg_new_ref[...]
        zv_ref[CARRY_ROWS:, :] = zv_new_ref[...]
        a_ref[...] = a_new_ref[...]


def _ffn(x, ln_w, w_up, conv_w, conv_b, w_down, *, bm, tf):
    s, d = x.shape
    dff = w_down.shape[0]
    nj = dff // tf
    n_tiles = (s // bm) * nj

    def row(t, lag):
        return jnp.clip(t - lag, 0, n_tiles - 1) // nj

    def col(t, lag):
        return jnp.clip(t - lag, 0, n_tiles - 1) % nj

    return pl.pallas_call(
        functools.partial(_ffn_kernel, nj=nj, n_tiles=n_tiles),
        grid=(n_tiles + 2,),
        in_specs=[
            pl.BlockSpec((bm, d), lambda t: (row(t, 0), 0), pipeline_mode=pl.Buffered(1)),
            pl.BlockSpec((1, d), lambda t: (0, 0)),
            pl.BlockSpec((d, tf), lambda t: (0, col(t, 0))),
            pl.BlockSpec((d, tf), lambda t: (0, nj + col(t, 0))),
            pl.BlockSpec((CONV_WIDTH, tf), lambda t: (0, col(t, 1))),
            pl.BlockSpec((CONV_WIDTH, tf), lambda t: (0, nj + col(t, 1))),
            pl.BlockSpec((1, tf), lambda t: (0, col(t, 1))),
            pl.BlockSpec((1, tf), lambda t: (0, nj + col(t, 1))),
            pl.BlockSpec((tf, d), lambda t: (col(t, 2), 0)),
        ],
        out_specs=pl.BlockSpec((bm, d), lambda t: (row(t, 2), 0)),
        out_shape=jax.ShapeDtypeStruct((s, d), F32),
        scratch_shapes=[
            pltpu.VMEM((bm, d), BF16),
            pltpu.VMEM((bm, tf), F32),
            pltpu.VMEM((bm, tf), F32),
            pltpu.VMEM((CARRY_ROWS + bm, tf), F32),
            pltpu.VMEM((CARRY_ROWS + bm, tf), F32),
            pltpu.VMEM((bm, tf), BF16),
            pltpu.VMEM((bm, tf), BF16),
            pltpu.VMEM((nj + 1, CARRY_ROWS, tf), F32),
            pltpu.VMEM((nj + 1, CARRY_ROWS, tf), F32),
        ],
        compiler_params=_params("arbitrary"),
        name="ffn",
    )(x, ln_w, w_up, w_up, conv_w, conv_w, conv_b, conv_b, w_down)


def _ple_kernel(x_ref, p_ref, lnw_ref, wg_ref, wp_ref, lnf_ref, o_ref, *, final):
    x = x_ref[...]
    h = _rms(x, lnw_ref[...]).astype(BF16)
    gate = jax.nn.sigmoid(_dot(h, wg_ref[...]))
    emb = _dot(p_ref[...].astype(BF16), wp_ref[...])
    y = x + gate * emb
    o_ref[...] = _rms(y, lnf_ref[...]) if final else y


def _ple(x, p, ln_w, w_gate, w_ple, ln_final, *, bm, final):
    s, d = x.shape
    e = p.shape[1]
    return pl.pallas_call(
        functools.partial(_ple_kernel, final=final),
        grid=(s // bm,),
        in_specs=[
            pl.BlockSpec((bm, d), lambda m: (m, 0)),
            pl.BlockSpec((bm, e), lambda m: (m, 0)),
            _resident((1, d)),
            _resident((d, d)),
            _resident((e, d)),
            _resident((1, d)),
        ],
        out_specs=pl.BlockSpec((bm, d), lambda m: (m, 0)),
        out_shape=jax.ShapeDtypeStruct((s, d), F32),
        compiler_params=_params("arbitrary"),
        name="ple_final",
    )(x, p, ln_w, w_gate, w_ple, ln_final)


def _pad_ff(w_up, conv_w, conv_b, w_down, tf):
    dff = w_down.shape[0]
    pad = (-dff) % tf
    if pad == 0:
        return w_up, conv_w, conv_b, w_down

    def pad_halves(a):
        g, v = a[..., :dff], a[..., dff:]
        widths = [(0, 0)] * (a.ndim - 1) + [(0, pad)]
        return jnp.concatenate([jnp.pad(g, widths), jnp.pad(v, widths)], axis=-1)

    return pad_halves(w_up), pad_halves(conv_w), pad_halves(conv_b), jnp.pad(w_down, ((0, pad), (0, 0)))


def kernel(x, p, ln_mix_w, w_in, pool_mix_w, pool_scale, hgrn_lb_logits, hgrn_norm_w, w_branch_a,
           w_branch_b, w_out, ln_ffn_w, w_up, conv_w, conv_b, w_down, ln_ple_w, w_ple_gate, w_ple,
           ln_final_w):
    batch, seq, d = x.shape
    depth = w_in.shape[0]
    c_pool = w_branch_a.shape[1]
    c_hgrn = w_branch_b.shape[1]
    blk = c_hgrn
    assert c_pool == blk and d == 2 * blk, "column blocks of the combined projection must line up"
    assert seq % 1024 == 0
    tf = 512
    outs = []
    for bi in range(batch):
        xb = x[bi]
        for i in range(depth):
            proj, f_logit = _inproj(xb, ln_mix_w[i][None], w_in[i].astype(BF16),
                                    bm=1024, bn=blk, f_block=2)
            ya = _pool_branch(proj, pool_mix_w[i].astype(BF16), pool_scale[i][None],
                              w_branch_a[i].astype(BF16), bm=1024, bn=blk, ga_block0=5)
            og = _hgrn(proj, f_logit, hgrn_lb_logits, hgrn_norm_w[i][None], bt=256, layer=i,
                       q_block=1, i_block=3, g_block=4)
            xb = _merge(og, proj, ya, xb, w_branch_b[i].astype(BF16), w_out[i].astype(BF16),
                        bm=512, gb_block0=7)
            wu, cw, cb, wd = _pad_ff(w_up[i], conv_w[i], conv_b[i][None], w_down[i], tf)
            xb = _ffn(xb, ln_ffn_w[i][None], wu.astype(BF16), cw, cb, wd.astype(BF16), bm=1024, tf=tf)
            xb = _ple(xb, p[i, bi], ln_ple_w[i][None], w_ple_gate[i].astype(BF16),
                      w_ple[i].astype(BF16), ln_final_w[None], bm=512, final=(i == depth - 1))
        outs.append(xb)
    return jnp.stack(outs, axis=0)
```

```python
import functools

import jax
import jax.numpy as jnp
from jax import lax
from jax.experimental import pallas as pl
from jax.experimental.pallas import tpu as pltpu

F32 = jnp.float32
BF16 = jnp.bfloat16

EPS = 1e-6
POOL_WINDOWS = (2, 4, 8, 16)
POOL_HALO = 16
HEAD_DIM = 128
CHUNK = 64
SUB = 16
NSUB = CHUNK // SUB
CONV_WIDTH = 3
CARRY_ROWS = 8
LANE = 128
LOG2_E = 1.4426950408889634
MILD_DECAY_LOG2 = -100.0
V7X_VMEM_BYTES = 64 * 1024 * 1024
VMEM_LIMIT = V7X_VMEM_BYTES - 8 * 1024 * 1024


def _dot(a, b):
    return jnp.dot(a, b, preferred_element_type=F32)


def _dot_nt(a, b):
    return lax.dot_general(a, b, (((1,), (1,)), ((), ())), preferred_element_type=F32)


def _dot_tn(a, b):
    return lax.dot_general(a, b, (((0,), (0,)), ((), ())), preferred_element_type=F32)


def _rms(x, w):
    return x * lax.rsqrt(jnp.mean(x * x, axis=-1, keepdims=True) + EPS) * w


def _params(*sem):
    return pltpu.CompilerParams(dimension_semantics=sem, vmem_limit_bytes=VMEM_LIMIT)


def _resident(shape):
    return pl.BlockSpec(shape, lambda *_: (0,) * len(shape), pipeline_mode=pl.Buffered(1))


def _inproj_kernel(x_ref, lnw_ref, w_ref, o_ref, f_ref, h_ref, *, f_block):
    n = pl.program_id(1)

    @pl.when(n == 0)
    def _():
        h_ref[...] = _rms(x_ref[...], lnw_ref[...]).astype(BF16)

    acc = _dot(h_ref[...], w_ref[...])
    o_ref[...] = acc.astype(o_ref.dtype)

    @pl.when(n == f_block)
    def _():
        f_ref[...] = acc


def _inproj(x, ln_w, w_in, *, bm, bn, f_block):
    s, d = x.shape
    d_in = w_in.shape[1]
    return pl.pallas_call(
        functools.partial(_inproj_kernel, f_block=f_block),
        grid=(s // bm, d_in // bn),
        in_specs=[
            pl.BlockSpec((bm, d), lambda m, n: (m, 0)),
            pl.BlockSpec((1, d), lambda m, n: (0, 0)),
            pl.BlockSpec((d, bn), lambda m, n: (0, n)),
        ],
        out_specs=[
            pl.BlockSpec((bm, bn), lambda m, n: (m, n)),
            pl.BlockSpec((bm, bn), lambda m, n: (m, 0)),
        ],
        out_shape=[
            jax.ShapeDtypeStruct((s, d_in), BF16),
            jax.ShapeDtypeStruct((s, bn), F32),
        ],
        scratch_shapes=[pltpu.VMEM((bm, d), BF16)],
        compiler_params=_params("arbitrary", "arbitrary"),
        name="inproj",
    )(x, ln_w, w_in)


def _pool_kernel(u_ref, halo_ref, ga_ref, mix_ref, scale_ref, wa_ref, o_ref, feat_ref, *, bm):
    m = pl.program_id(0)
    n = pl.program_id(1)

    @pl.when(n == 0)
    def _():
        u = u_ref[...].astype(F32)
        halo = halo_ref[...].astype(F32) * (m > 0).astype(F32)
        ext = jnp.concatenate([halo, u], axis=0)
        gw = u.shape[1] // len(POOL_WINDOWS)
        pos = m * bm + lax.broadcasted_iota(jnp.int32, (bm, 1), 0) + 1
        run = ext
        width = 1
        feats = []
        for g, w in enumerate(POOL_WINDOWS):
            while width < w:
                run = run + pltpu.roll(run, width, axis=0)
                width *= 2
            cnt = jnp.minimum(pos, w).astype(F32)
            win = run[POOL_HALO:, g * gw:(g + 1) * gw]
            d = win / cnt - u[:, g * gw:(g + 1) * gw]
            y = _dot(d.astype(BF16), mix_ref[g])
            feats.append(y * scale_ref[:, g * gw:(g + 1) * gw])
        feat_ref[...] = jnp.concatenate(feats, axis=1).astype(BF16)

    y_a = _dot(feat_ref[...], wa_ref[...])
    o_ref[...] = (jax.nn.sigmoid(ga_ref[...].astype(F32)) * y_a).astype(o_ref.dtype)


def _pool_branch(proj, mix_w, scale, w_a, *, bm, bn, ga_block0):
    s = proj.shape[0]
    c = w_a.shape[0]
    d = w_a.shape[1]
    g, gw, _ = mix_w.shape
    halo_per_tile = bm // POOL_HALO
    return pl.pallas_call(
        functools.partial(_pool_kernel, bm=bm),
        grid=(s // bm, d // bn),
        in_specs=[
            pl.BlockSpec((bm, c), lambda m, n: (m, 0)),
            pl.BlockSpec((POOL_HALO, c), lambda m, n: (jnp.maximum(m * halo_per_tile - 1, 0), 0)),
            pl.BlockSpec((bm, bn), lambda m, n: (m, ga_block0 + n)),
            _resident((g, gw, gw)),
            _resident((1, c)),
            pl.BlockSpec((c, bn), lambda m, n: (0, n)),
        ],
        out_specs=pl.BlockSpec((bm, bn), lambda m, n: (m, n)),
        out_shape=jax.ShapeDtypeStruct((s, d), BF16),
        scratch_shapes=[pltpu.VMEM((bm, c), BF16)],
        compiler_params=_params("arbitrary", "arbitrary"),
        name="pool_branch",
    )(proj, proj, proj, mix_w, scale, w_a)


def _split3(x):
    hi = x.astype(BF16)
    r = x - hi.astype(F32)
    mid = r.astype(BF16)
    lo = (r - mid.astype(F32)).astype(BF16)
    return hi, mid, lo


def _gates(fl, lb):
    one_m_lb = 1.0 - lb
    sg = jax.nn.sigmoid(fl)
    log2_f = jnp.log(lb + one_m_lb * sg) * LOG2_E
    return log2_f, one_m_lb * (1.0 - sg)


def _chunk_cumsum(tri, x):
    hi, mid, lo = _split3(x)
    return _dot(tri, hi) + _dot(tri, mid) + _dot(tri, lo)


def _head_norm_gate(o_heads, norm_w, g):
    o = jnp.concatenate(
        [o_h * lax.rsqrt(jnp.mean(o_h * o_h, axis=-1, keepdims=True) + EPS) for o_h in o_heads], axis=1)
    return o * norm_w * (g * jax.nn.sigmoid(g))


def _hgrn_chunk_any_decay(c, q_ref, f_ref, i_ref, g_ref, tri_ref, wsel_ref, o_ref, st_ref, lb, norm_w):
    r0 = pl.multiple_of(c * CHUNK, CHUNK)
    rows = pl.ds(r0, CHUNK)
    d = q_ref.shape[1]
    heads = d // HEAD_DIM

    q = q_ref[rows, :].astype(F32)
    v = i_ref[rows, :]
    log2_f, kk = _gates(f_ref[rows, :], lb)
    b = _chunk_cumsum(tri_ref[:CHUNK, :CHUNK], log2_f)
    b_last = b[CHUNK - 1:CHUNK, :]

    q_in = (q * jnp.exp2(b)).astype(BF16)
    k_st = (kk * jnp.exp2(b_last - b)).astype(BF16)

    b_end = jnp.concatenate(
        [jnp.broadcast_to(b[(j + 1) * SUB - 1:(j + 1) * SUB, :], (SUB, d)) for j in range(NSUB)], axis=0)
    k_hat = kk * jnp.exp2(b_end - b)
    sub_id = lax.broadcasted_iota(jnp.int32, (CHUNK, 1), 0) // SUB
    q_from = []
    for j in range(NSUB - 1):
        lo_row = (j + 1) * SUB
        bj = b[lo_row - 1:lo_row, :]
        part = q[lo_row:, :] * jnp.exp2(b[lo_row:, :] - bj)
        q_from.append(jnp.concatenate([jnp.zeros((lo_row, d), F32), part], axis=0).astype(BF16))
    k_src = [jnp.where(sub_id == j, k_hat, 0.0).astype(BF16) for j in range(NSUB - 1)]

    per_head = [[] for _ in range(heads)]
    for r in range(NSUB):
        qb = q[r * SUB:(r + 1) * SUB, :]
        bb = b[r * SUB:(r + 1) * SUB, :]
        pieces = []
        for s in range(SUB):
            row = r * SUB + s
            e = jnp.exp2(jnp.minimum(bb - b[row:row + 1, :], 0.0))
            pieces.append((qb * kk[row:row + 1, :] * e).astype(BF16))
        for h in range(heads):
            per_head[h].append(jnp.concatenate(
                [p[:, h * HEAD_DIM:(h + 1) * HEAD_DIM] for p in pieces], axis=1))
    a_big = jnp.concatenate([jnp.concatenate(blocks, axis=0) for blocks in per_head], axis=0)
    diag = _dot(a_big, wsel_ref[...])

    t_id = lax.broadcasted_iota(jnp.int32, (CHUNK, CHUNK), 0)
    s_id = lax.broadcasted_iota(jnp.int32, (CHUNK, CHUNK), 1)
    diag_mask = (t_id // SUB == s_id // SUB) & (s_id <= t_id)

    outs = []
    for h in range(heads):
        hs = slice(h * HEAD_DIM, (h + 1) * HEAD_DIM)
        q_cat = jnp.concatenate([qf[:, hs] for qf in q_from], axis=1)
        k_cat = jnp.concatenate([ks[:, hs] for ks in k_src], axis=1)
        scores = _dot_nt(q_cat, k_cat) + jnp.where(diag_mask, diag[h * CHUNK:(h + 1) * CHUNK, :], 0.0)
        v_h = v[:, hs]
        state_t = st_ref[h]
        o_h = _dot_nt(q_in[:, hs], state_t.astype(BF16)) + _dot(scores.astype(BF16), v_h)
        st_ref[h] = state_t * jnp.exp2(b_last[:, hs]) + _dot_tn(v_h, k_st[:, hs])
        outs.append(o_h)
    o_ref[rows, :] = _head_norm_gate(outs, norm_w, g_ref[rows, :].astype(F32)).astype(o_ref.dtype)


def _hgrn_tile_mild_decay(b, kk, q_ref, i_ref, g_ref, o_ref, st_ref, norm_w):
    bt, d = b.shape
    heads = d // HEAD_DIM
    n_chunks = bt // CHUNK
    q = q_ref[...].astype(F32)
    v = i_ref[...]
    b_last_rows = [b[(c + 1) * CHUNK - 1:(c + 1) * CHUNK, :] for c in range(n_chunks)]
    b_last = jnp.concatenate([jnp.broadcast_to(r, (CHUNK, d)) for r in b_last_rows], axis=0)
    q_in = (q * jnp.exp2(b)).astype(BF16)
    k_out = (kk * jnp.exp2(-b)).astype(BF16)
    k_st = (kk * jnp.exp2(b_last - b)).astype(BF16)
    causal = (lax.broadcasted_iota(jnp.int32, (CHUNK, CHUNK), 1)
              <= lax.broadcasted_iota(jnp.int32, (CHUNK, CHUNK), 0))
    states = [st_ref[h] for h in range(heads)]
    tile_out = []
    for c in range(n_chunks):
        rs = slice(c * CHUNK, (c + 1) * CHUNK)
        carry_decay = jnp.exp2(b_last_rows[c])
        outs = []
        for h in range(heads):
            hs = slice(h * HEAD_DIM, (h + 1) * HEAD_DIM)
            scores = jnp.where(causal, _dot_nt(q_in[rs, hs], k_out[rs, hs]), 0.0).astype(BF16)
            outs.append(_dot(scores, v[rs, hs]) + _dot_nt(q_in[rs, hs], states[h].astype(BF16)))
            states[h] = states[h] * carry_decay[:, hs] + _dot_tn(v[rs, hs], k_st[rs, hs])
        tile_out.append(jnp.concatenate(outs, axis=1))
    for h in range(heads):
        st_ref[h] = states[h]
    heads_out = jnp.concatenate(tile_out, axis=0)
    o_heads = [heads_out[:, h * HEAD_DIM:(h + 1) * HEAD_DIM] for h in range(heads)]
    o_ref[...] = _head_norm_gate(o_heads, norm_w, g_ref[...].astype(F32)).astype(o_ref.dtype)


def _hgrn_kernel(q_ref, f_ref, i_ref, g_ref, lbl_ref, nw_ref, tri_ref, wsel_ref, o_ref, st_ref, *, layer):
    @pl.when(pl.program_id(0) == 0)
    def _():
        st_ref[...] = jnp.zeros_like(st_ref)

    logits = lbl_ref[...]
    e = jnp.exp(logits - jnp.max(logits, axis=0, keepdims=True))
    lb = jnp.sum(e[:layer + 1, :], axis=0, keepdims=True) / jnp.sum(e, axis=0, keepdims=True)
    norm_w = nw_ref[...]

    log2_f, kk = _gates(f_ref[...], lb)
    b = _chunk_cumsum(tri_ref[...], log2_f)
    mild = jnp.min(b) >= MILD_DECAY_LOG2

    @pl.when(mild)
    def _():
        _hgrn_tile_mild_decay(b, kk, q_ref, i_ref, g_ref, o_ref, st_ref, norm_w)

    @pl.when(jnp.logical_not(mild))
    def _():
        def body(c, carry):
            _hgrn_chunk_any_decay(c, q_ref, f_ref, i_ref, g_ref, tri_ref, wsel_ref, o_ref, st_ref, lb, norm_w)
            return carry

        lax.fori_loop(0, q_ref.shape[0] // CHUNK, body, 0)


def _hgrn(proj, f_logit, lb_logits, norm_w, *, bt, layer, q_block, i_block, g_block):
    s = proj.shape[0]
    d = f_logit.shape[1]
    heads = d // HEAD_DIM
    t_id = lax.broadcasted_iota(jnp.int32, (bt, bt), 0)
    s_id = lax.broadcasted_iota(jnp.int32, (bt, bt), 1)
    tri = ((s_id <= t_id) & (s_id // CHUNK == t_id // CHUNK)).astype(BF16)
    src = lax.broadcasted_iota(jnp.int32, (SUB * HEAD_DIM, CHUNK), 0) // HEAD_DIM
    col = lax.broadcasted_iota(jnp.int32, (SUB * HEAD_DIM, CHUNK), 1) % SUB
    wsel = (src == col).astype(BF16)
    nl = lb_logits.shape[0]
    return pl.pallas_call(
        functools.partial(_hgrn_kernel, layer=layer),
        grid=(s // bt,),
        in_specs=[
            pl.BlockSpec((bt, d), lambda t: (t, q_block)),
            pl.BlockSpec((bt, d), lambda t: (t, 0)),
            pl.BlockSpec((bt, d), lambda t: (t, i_block)),
            pl.BlockSpec((bt, d), lambda t: (t, g_block)),
            _resident((nl, d)),
            _resident((1, d)),
            _resident((bt, bt)),
            _resident((SUB * HEAD_DIM, CHUNK)),
        ],
        out_specs=pl.BlockSpec((bt, d), lambda t: (t, 0)),
        out_shape=jax.ShapeDtypeStruct((s, d), BF16),
        scratch_shapes=[pltpu.VMEM((heads, HEAD_DIM, HEAD_DIM), F32)],
        compiler_params=_params("arbitrary"),
        name="hgrn",
    )(proj, f_logit, proj, proj, lb_logits, norm_w, tri, wsel)


def _merge_kernel(og_ref, gb0_ref, gb1_ref, ya_ref, x_ref, wb_ref, wo_ref, o_ref):
    y_b = _dot(og_ref[...], wb_ref[...])
    gate_b = jnp.concatenate([gb0_ref[...], gb1_ref[...]], axis=1).astype(F32)
    merged = ya_ref[...].astype(F32) + jax.nn.sigmoid(gate_b) * y_b
    o_ref[...] = x_ref[...] + _dot(merged.astype(BF16), wo_ref[...])


def _merge(og, proj, ya, x, w_b, w_out, *, bm, gb_block0):
    s, d = x.shape
    c = og.shape[1]
    half = d // 2
    return pl.pallas_call(
        _merge_kernel,
        grid=(s // bm,),
        in_specs=[
            pl.BlockSpec((bm, c), lambda m: (m, 0)),
            pl.BlockSpec((bm, half), lambda m: (m, gb_block0)),
            pl.BlockSpec((bm, half), lambda m: (m, gb_block0 + 1)),
            pl.BlockSpec((bm, d), lambda m: (m, 0)),
            pl.BlockSpec((bm, d), lambda m: (m, 0)),
            _resident((c, d)),
            _resident((d, d)),
        ],
        out_specs=pl.BlockSpec((bm, d), lambda m: (m, 0)),
        out_shape=jax.ShapeDtypeStruct((s, d), F32),
        compiler_params=_params("arbitrary"),
        name="merge_out",
    )(og, proj, proj, ya, x, w_b, w_out)


def _causal_conv(z, prev, w_ref, b_ref):
    ext = jnp.concatenate([prev, z], axis=0)
    y = b_ref[...] + w_ref[CONV_WIDTH - 1:CONV_WIDTH, :] * z
    for back in range(1, CONV_WIDTH):
        shifted = pltpu.roll(ext, back, axis=0)[CARRY_ROWS:, :]
        y = y + w_ref[CONV_WIDTH - 1 - back:CONV_WIDTH - back, :] * shifted
    return y


def _ffn_kernel(x_ref, lnw_ref, wug_ref, wuv_ref, cwg_ref, cwv_ref, cbg_ref, cbv_ref, wd_ref,
                o_ref, h_ref, cg_ref, cv_ref):
    m = pl.program_id(0)
    j = pl.program_id(1)

    @pl.when(j == 0)
    def _():
        x = x_ref[...]
        h_ref[...] = _rms(x, lnw_ref[...]).astype(BF16)
        o_ref[...] = x

    @pl.when(m == 0)
    def _():
        cg_ref[j] = jnp.zeros(cg_ref.shape[1:], F32)
        cv_ref[j] = jnp.zeros(cv_ref.shape[1:], F32)

    h = h_ref[...]
    zg = _dot(h, wug_ref[...])
    zv = _dot(h, wuv_ref[...])
    ug = _causal_conv(zg, cg_ref[j], cwg_ref, cbg_ref)
    uv = _causal_conv(zv, cv_ref[j], cwv_ref, cbv_ref)
    bm = zg.shape[0]
    cg_ref[j] = zg[bm - CARRY_ROWS:, :]
    cv_ref[j] = zv[bm - CARRY_ROWS:, :]
    act = (ug * jax.nn.sigmoid(ug) * uv).astype(BF16)
    o_ref[...] += _dot(act, wd_ref[...])


def _ffn(x, ln_w, w_gate, w_val, cw_gate, cw_val, cb_gate, cb_val, w_down, *, bm, tf):
    s, d = x.shape
    dff = w_down.shape[0]
    nj = dff // tf
    return pl.pallas_call(
        _ffn_kernel,
        grid=(s // bm, nj),
        in_specs=[
            pl.BlockSpec((bm, d), lambda m, j: (m, 0)),
            pl.BlockSpec((1, d), lambda m, j: (0, 0)),
            pl.BlockSpec((d, tf), lambda m, j: (0, j)),
            pl.BlockSpec((d, tf), lambda m, j: (0, j)),
            pl.BlockSpec((CONV_WIDTH, tf), lambda m, j: (0, j)),
            pl.BlockSpec((CONV_WIDTH, tf), lambda m, j: (0, j)),
            pl.BlockSpec((1, tf), lambda m, j: (0, j)),
            pl.BlockSpec((1, tf), lambda m, j: (0, j)),
            pl.BlockSpec((tf, d), lambda m, j: (j, 0)),
        ],
        out_specs=pl.BlockSpec((bm, d), lambda m, j: (m, 0)),
        out_shape=jax.ShapeDtypeStruct((s, d), F32),
        scratch_shapes=[
            pltpu.VMEM((bm, d), BF16),
            pltpu.VMEM((nj, CARRY_ROWS, tf), F32),
            pltpu.VMEM((nj, CARRY_ROWS, tf), F32),
        ],
        compiler_params=_params("arbitrary", "arbitrary"),
        name="ffn",
    )(x, ln_w, w_gate, w_val, cw_gate, cw_val, cb_gate, cb_val, w_down)


def _ple_kernel(x_ref, p_ref, lnw_ref, wg_ref, wp_ref, lnf_ref, o_ref, *, final):
    x = x_ref[...]
    h = _rms(x, lnw_ref[...]).astype(BF16)
    gate = jax.nn.sigmoid(_dot(h, wg_ref[...]))
    emb = _dot(p_ref[...].astype(BF16), wp_ref[...])
    y = x + gate * emb
    o_ref[...] = _rms(y, lnf_ref[...]) if final else y


def _ple(x, p, ln_w, w_gate, w_ple, ln_final, *, bm, final):
    s, d = x.shape
    e = p.shape[1]
    return pl.pallas_call(
        functools.partial(_ple_kernel, final=final),
        grid=(s // bm,),
        in_specs=[
            pl.BlockSpec((bm, d), lambda m: (m, 0)),
            pl.BlockSpec((bm, e), lambda m: (m, 0)),
            _resident((1, d)),
            _resident((d, d)),
            _resident((e, d)),
            _resident((1, d)),
        ],
        out_specs=pl.BlockSpec((bm, d), lambda m: (m, 0)),
        out_shape=jax.ShapeDtypeStruct((s, d), F32),
        compiler_params=_params("arbitrary"),
        name="ple_final",
    )(x, p, ln_w, w_gate, w_ple, ln_final)


def _ffn_operands(w_up, conv_w, conv_b, w_down, tf):
    dff = w_down.shape[0]
    pad = (-dff) % tf

    def halves(a, dtype):
        widths = [(0, 0)] * (a.ndim - 1) + [(0, pad)]
        return [jnp.pad(part.astype(dtype), widths) for part in (a[..., :dff], a[..., dff:])]

    return (*halves(w_up, BF16), *halves(conv_w, F32), *halves(conv_b, F32),
            jnp.pad(w_down.astype(BF16), ((0, pad), (0, 0))))


def kernel(x, p, ln_mix_w, w_in, pool_mix_w, pool_scale, hgrn_lb_logits, hgrn_norm_w, w_branch_a,
           w_branch_b, w_out, ln_ffn_w, w_up, conv_w, conv_b, w_down, ln_ple_w, w_ple_gate, w_ple,
           ln_final_w):
    batch, seq, d = x.shape
    depth = w_in.shape[0]
    c_pool = w_branch_a.shape[1]
    c_hgrn = w_branch_b.shape[1]
    blk = c_hgrn
    assert c_pool == blk and d == 2 * blk, "column blocks of the combined projection must line up"
    assert seq % 1024 == 0
    tf = 512
    outs = []
    for bi in range(batch):
        xb = x[bi]
        for i in range(depth):
            proj, f_logit = _inproj(xb, ln_mix_w[i][None], w_in[i].astype(BF16),
                                    bm=1024, bn=blk, f_block=2)
            ya = _pool_branch(proj, pool_mix_w[i].astype(BF16), pool_scale[i][None],
                              w_branch_a[i].astype(BF16), bm=1024, bn=blk, ga_block0=5)
            og = _hgrn(proj, f_logit, hgrn_lb_logits, hgrn_norm_w[i][None], bt=256, layer=i,
                       q_block=1, i_block=3, g_block=4)
            xb = _merge(og, proj, ya, xb, w_branch_b[i].astype(BF16), w_out[i].astype(BF16),
                        bm=512, gb_block0=7)
            xb = _ffn(xb, ln_ffn_w[i][None],
                      *_ffn_operands(w_up[i], conv_w[i], conv_b[i][None], w_down[i], tf), bm=512, tf=tf)
            xb = _ple(xb, p[i, bi], ln_ple_w[i][None], w_ple_gate[i].astype(BF16),
                      w_ple[i].astype(BF16), ln_final_w[None], bm=512, final=(i == depth - 1))
        outs.append(xb)
    return jnp.stack(outs, axis=0)
```

```python
import functools

import jax
import jax.numpy as jnp
from jax import lax
from jax.experimental import pallas as pl
from jax.experimental.pallas import tpu as pltpu

F32 = jnp.float32
BF16 = jnp.bfloat16

EPS = 1e-6
POOL_WINDOWS = (2, 4, 8, 16)
POOL_HALO = 16
HEAD_DIM = 128
CHUNK = 64
SUB = 16
NSUB = CHUNK // SUB
CONV_WIDTH = 3
CARRY_ROWS = 8
LANE = 128
LOG2_E = 1.4426950408889634
MILD_DECAY_LOG2 = -100.0
V7X_VMEM_BYTES = 64 * 1024 * 1024
VMEM_LIMIT = V7X_VMEM_BYTES - 8 * 1024 * 1024


def _dot(a, b):
    return jnp.dot(a, b, preferred_element_type=F32)


def _dot_nt(a, b):
    return lax.dot_general(a, b, (((1,), (1,)), ((), ())), preferred_element_type=F32)


def _dot_tn(a, b):
    return lax.dot_general(a, b, (((0,), (0,)), ((), ())), preferred_element_type=F32)


def _rms(x, w):
    return x * lax.rsqrt(jnp.mean(x * x, axis=-1, keepdims=True) + EPS) * w


def _params(*sem):
    return pltpu.CompilerParams(dimension_semantics=sem, vmem_limit_bytes=VMEM_LIMIT)


def _resident(shape):
    return pl.BlockSpec(shape, lambda *_: (0,) * len(shape), pipeline_mode=pl.Buffered(1))


def _inproj_kernel(x_ref, lnw_ref, w_ref, o_ref, f_ref, h_ref, *, f_block):
    n = pl.program_id(1)

    @pl.when(n == 0)
    def _():
        h_ref[...] = _rms(x_ref[...], lnw_ref[...]).astype(BF16)

    acc = _dot(h_ref[...], w_ref[...].astype(BF16))
    o_ref[...] = acc.astype(o_ref.dtype)

    @pl.when(n == f_block)
    def _():
        f_ref[...] = acc


def _inproj(x, ln_w, w_in, *, bm, bn, f_block):
    s, d = x.shape
    d_in = w_in.shape[1]
    return pl.pallas_call(
        functools.partial(_inproj_kernel, f_block=f_block),
        grid=(s // bm, d_in // bn),
        in_specs=[
            pl.BlockSpec((bm, d), lambda m, n: (m, 0)),
            pl.BlockSpec((1, d), lambda m, n: (0, 0)),
            pl.BlockSpec((d, bn), lambda m, n: (0, n)),
        ],
        out_specs=[
            pl.BlockSpec((bm, bn), lambda m, n: (m, n)),
            pl.BlockSpec((bm, bn), lambda m, n: (m, 0)),
        ],
        out_shape=[
            jax.ShapeDtypeStruct((s, d_in), BF16),
            jax.ShapeDtypeStruct((s, bn), F32),
        ],
        scratch_shapes=[pltpu.VMEM((bm, d), BF16)],
        compiler_params=_params("arbitrary", "arbitrary"),
        name="inproj",
    )(x, ln_w, w_in)


def _pool_kernel(u_ref, halo_ref, ga_ref, mix_ref, scale_ref, wa_ref, o_ref, feat_ref, *, bm):
    m = pl.program_id(0)
    n = pl.program_id(1)

    @pl.when(n == 0)
    def _():
        u = u_ref[...].astype(F32)
        halo = halo_ref[...].astype(F32) * (m > 0).astype(F32)
        ext = jnp.concatenate([halo, u], axis=0)
        gw = u.shape[1] // len(POOL_WINDOWS)
        pos = m * bm + lax.broadcasted_iota(jnp.int32, (bm, 1), 0) + 1
        run = ext
        width = 1
        feats = []
        for g, w in enumerate(POOL_WINDOWS):
            while width < w:
                run = run + pltpu.roll(run, width, axis=0)
                width *= 2
            cnt = jnp.minimum(pos, w).astype(F32)
            win = run[POOL_HALO:, g * gw:(g + 1) * gw]
            d = win / cnt - u[:, g * gw:(g + 1) * gw]
            y = _dot(d.astype(BF16), mix_ref[g])
            feats.append(y * scale_ref[:, g * gw:(g + 1) * gw])
        feat_ref[...] = jnp.concatenate(feats, axis=1).astype(BF16)

    y_a = _dot(feat_ref[...], wa_ref[...])
    o_ref[...] = (jax.nn.sigmoid(ga_ref[...].astype(F32)) * y_a).astype(o_ref.dtype)


def _pool_branch(proj, mix_w, scale, w_a, *, bm, bn, ga_block0):
    s = proj.shape[0]
    c = w_a.shape[0]
    d = w_a.shape[1]
    g, gw, _ = mix_w.shape
    halo_per_tile = bm // POOL_HALO
    return pl.pallas_call(
        functools.partial(_pool_kernel, bm=bm),
        grid=(s // bm, d // bn),
        in_specs=[
            pl.BlockSpec((bm, c), lambda m, n: (m, 0)),
            pl.BlockSpec((POOL_HALO, c), lambda m, n: (jnp.maximum(m * halo_per_tile - 1, 0), 0)),
            pl.BlockSpec((bm, bn), lambda m, n: (m, ga_block0 + n)),
            _resident((g, gw, gw)),
            _resident((1, c)),
            pl.BlockSpec((c, bn), lambda m, n: (0, n)),
        ],
        out_specs=pl.BlockSpec((bm, bn), lambda m, n: (m, n)),
        out_shape=jax.ShapeDtypeStruct((s, d), BF16),
        scratch_shapes=[pltpu.VMEM((bm, c), BF16)],
        compiler_params=_params("arbitrary", "arbitrary"),
        name="pool_branch",
    )(proj, proj, proj, mix_w, scale, w_a)


def _split3(x):
    hi = x.astype(BF16)
    r = x - hi.astype(F32)
    mid = r.astype(BF16)
    lo = (r - mid.astype(F32)).astype(BF16)
    return hi, mid, lo


def _gates(fl, lb):
    one_m_lb = 1.0 - lb
    sg = jax.nn.sigmoid(fl)
    log2_f = jnp.log(lb + one_m_lb * sg) * LOG2_E
    return log2_f, one_m_lb * (1.0 - sg)


def _chunk_cumsum(tri, x):
    hi, mid, lo = _split3(x)
    return _dot(tri, hi) + _dot(tri, mid) + _dot(tri, lo)


def _head_norm_gate(o_heads, norm_w, g):
    o = jnp.concatenate(
        [o_h * lax.rsqrt(jnp.mean(o_h * o_h, axis=-1, keepdims=True) + EPS) for o_h in o_heads], axis=1)
    return o * norm_w * (g * jax.nn.sigmoid(g))


def _hgrn_chunk_any_decay(c, q_ref, f_ref, i_ref, g_ref, tri_ref, wsel_ref, o_ref, st_ref, lb, norm_w):
    r0 = pl.multiple_of(c * CHUNK, CHUNK)
    rows = pl.ds(r0, CHUNK)
    d = q_ref.shape[1]
    heads = d // HEAD_DIM

    q = q_ref[rows, :].astype(F32)
    v = i_ref[rows, :]
    log2_f, kk = _gates(f_ref[rows, :], lb)
    b = _chunk_cumsum(tri_ref[:CHUNK, :CHUNK], log2_f)
    b_last = b[CHUNK - 1:CHUNK, :]

    q_in = (q * jnp.exp2(b)).astype(BF16)
    k_st = (kk * jnp.exp2(b_last - b)).astype(BF16)

    b_end = jnp.concatenate(
        [jnp.broadcast_to(b[(j + 1) * SUB - 1:(j + 1) * SUB, :], (SUB, d)) for j in range(NSUB)], axis=0)
    k_hat = kk * jnp.exp2(b_end - b)
    sub_id = lax.broadcasted_iota(jnp.int32, (CHUNK, 1), 0) // SUB
    q_from = []
    for j in range(NSUB - 1):
        lo_row = (j + 1) * SUB
        bj = b[lo_row - 1:lo_row, :]
        part = q[lo_row:, :] * jnp.exp2(b[lo_row:, :] - bj)
        q_from.append(jnp.concatenate([jnp.zeros((lo_row, d), F32), part], axis=0).astype(BF16))
    k_src = [jnp.where(sub_id == j, k_hat, 0.0).astype(BF16) for j in range(NSUB - 1)]

    per_head = [[] for _ in range(heads)]
    for r in range(NSUB):
        qb = q[r * SUB:(r + 1) * SUB, :]
        bb = b[r * SUB:(r + 1) * SUB, :]
        pieces = []
        for s in range(SUB):
            row = r * SUB + s
            e = jnp.exp2(jnp.minimum(bb - b[row:row + 1, :], 0.0))
            pieces.append((qb * kk[row:row + 1, :] * e).astype(BF16))
        for h in range(heads):
            per_head[h].append(jnp.concatenate(
                [p[:, h * HEAD_DIM:(h + 1) * HEAD_DIM] for p in pieces], axis=1))
    a_big = jnp.concatenate([jnp.concatenate(blocks, axis=0) for blocks in per_head], axis=0)
    diag = _dot(a_big, wsel_ref[...])

    t_id = lax.broadcasted_iota(jnp.int32, (CHUNK, CHUNK), 0)
    s_id = lax.broadcasted_iota(jnp.int32, (CHUNK, CHUNK), 1)
    diag_mask = (t_id // SUB == s_id // SUB) & (s_id <= t_id)

    outs = []
    for h in range(heads):
        hs = slice(h * HEAD_DIM, (h + 1) * HEAD_DIM)
        q_cat = jnp.concatenate([qf[:, hs] for qf in q_from], axis=1)
        k_cat = jnp.concatenate([ks[:, hs] for ks in k_src], axis=1)
        scores = _dot_nt(q_cat, k_cat) + jnp.where(diag_mask, diag[h * CHUNK:(h + 1) * CHUNK, :], 0.0)
        v_h = v[:, hs]
        state_t = st_ref[h]
        o_h = _dot_nt(q_in[:, hs], state_t.astype(BF16)) + _dot(scores.astype(BF16), v_h)
        st_ref[h] = state_t * jnp.exp2(b_last[:, hs]) + _dot_tn(v_h, k_st[:, hs])
        outs.append(o_h)
    o_ref[rows, :] = _head_norm_gate(outs, norm_w, g_ref[rows, :].astype(F32)).astype(o_ref.dtype)


def _hgrn_tile_mild_decay(b, kk, q_ref, i_ref, g_ref, o_ref, st_ref, norm_w):
    bt, d = b.shape
    heads = d // HEAD_DIM
    n_chunks = bt // CHUNK
    q = q_ref[...].astype(F32)
    v = i_ref[...]
    b_last_rows = [b[(c + 1) * CHUNK - 1:(c + 1) * CHUNK, :] for c in range(n_chunks)]
    b_last = jnp.concatenate([jnp.broadcast_to(r, (CHUNK, d)) for r in b_last_rows], axis=0)
    q_in = (q * jnp.exp2(b)).astype(BF16)
    k_out = (kk * jnp.exp2(-b)).astype(BF16)
    k_st = (kk * jnp.exp2(b_last - b)).astype(BF16)
    causal = (lax.broadcasted_iota(jnp.int32, (CHUNK, CHUNK), 1)
              <= lax.broadcasted_iota(jnp.int32, (CHUNK, CHUNK), 0))
    states = [st_ref[h] for h in range(heads)]
    tile_out = []
    for c in range(n_chunks):
        rs = slice(c * CHUNK, (c + 1) * CHUNK)
        carry_decay = jnp.exp2(b_last_rows[c])
        outs = []
        for h in range(heads):
            hs = slice(h * HEAD_DIM, (h + 1) * HEAD_DIM)
            scores = jnp.where(causal, _dot_nt(q_in[rs, hs], k_out[rs, hs]), 0.0).astype(BF16)
            outs.append(_dot(scores, v[rs, hs]) + _dot_nt(q_in[rs, hs], states[h].astype(BF16)))
            states[h] = states[h] * carry_decay[:, hs] + _dot_tn(v[rs, hs], k_st[rs, hs])
        tile_out.append(jnp.concatenate(outs, axis=1))
    for h in range(heads):
        st_ref[h] = states[h]
    heads_out = jnp.concatenate(tile_out, axis=0)
    o_heads = [heads_out[:, h * HEAD_DIM:(h + 1) * HEAD_DIM] for h in range(heads)]
    o_ref[...] = _head_norm_gate(o_heads, norm_w, g_ref[...].astype(F32)).astype(o_ref.dtype)


def _hgrn_kernel(q_ref, f_ref, i_ref, g_ref, lbl_ref, nw_ref, tri_ref, wsel_ref, o_ref, st_ref, *, layer):
    @pl.when(pl.program_id(0) == 0)
    def _():
        st_ref[...] = jnp.zeros_like(st_ref)

    logits = lbl_ref[...]
    e = jnp.exp(logits - jnp.max(logits, axis=0, keepdims=True))
    lb = jnp.sum(e[:layer + 1, :], axis=0, keepdims=True) / jnp.sum(e, axis=0, keepdims=True)
    norm_w = nw_ref[...]

    log2_f, kk = _gates(f_ref[...], lb)
    b = _chunk_cumsum(tri_ref[...], log2_f)
    mild = jnp.min(b) >= MILD_DECAY_LOG2

    @pl.when(mild)
    def _():
        _hgrn_tile_mild_decay(b, kk, q_ref, i_ref, g_ref, o_ref, st_ref, norm_w)

    @pl.when(jnp.logical_not(mild))
    def _():
        def body(c, carry):
            _hgrn_chunk_any_decay(c, q_ref, f_ref, i_ref, g_ref, tri_ref, wsel_ref, o_ref, st_ref, lb, norm_w)
            return carry

        lax.fori_loop(0, q_ref.shape[0] // CHUNK, body, 0)


def _hgrn(proj, f_logit, lb_logits, norm_w, *, bt, layer, q_block, i_block, g_block):
    s = proj.shape[0]
    d = f_logit.shape[1]
    heads = d // HEAD_DIM
    t_id = lax.broadcasted_iota(jnp.int32, (bt, bt), 0)
    s_id = lax.broadcasted_iota(jnp.int32, (bt, bt), 1)
    tri = ((s_id <= t_id) & (s_id // CHUNK == t_id // CHUNK)).astype(BF16)
    src = lax.broadcasted_iota(jnp.int32, (SUB * HEAD_DIM, CHUNK), 0) // HEAD_DIM
    col = lax.broadcasted_iota(jnp.int32, (SUB * HEAD_DIM, CHUNK), 1) % SUB
    wsel = (src == col).astype(BF16)
    nl = lb_logits.shape[0]
    return pl.pallas_call(
        functools.partial(_hgrn_kernel, layer=layer),
        grid=(s // bt,),
        in_specs=[
            pl.BlockSpec((bt, d), lambda t: (t, q_block)),
            pl.BlockSpec((bt, d), lambda t: (t, 0)),
            pl.BlockSpec((bt, d), lambda t: (t, i_block)),
            pl.BlockSpec((bt, d), lambda t: (t, g_block)),
            _resident((nl, d)),
            _resident((1, d)),
            _resident((bt, bt)),
            _resident((SUB * HEAD_DIM, CHUNK)),
        ],
        out_specs=pl.BlockSpec((bt, d), lambda t: (t, 0)),
        out_shape=jax.ShapeDtypeStruct((s, d), BF16),
        scratch_shapes=[pltpu.VMEM((heads, HEAD_DIM, HEAD_DIM), F32)],
        compiler_params=_params("arbitrary"),
        name="hgrn",
    )(proj, f_logit, proj, proj, lb_logits, norm_w, tri, wsel)


def _merge_kernel(og_ref, gb0_ref, gb1_ref, ya_ref, x_ref, wb_ref, wo_ref, o_ref):
    y_b = _dot(og_ref[...], wb_ref[...])
    gate_b = jnp.concatenate([gb0_ref[...], gb1_ref[...]], axis=1).astype(F32)
    merged = ya_ref[...].astype(F32) + jax.nn.sigmoid(gate_b) * y_b
    o_ref[...] = x_ref[...] + _dot(merged.astype(BF16), wo_ref[...])


def _merge(og, proj, ya, x, w_b, w_out, *, bm, gb_block0):
    s, d = x.shape
    c = og.shape[1]
    half = d // 2
    return pl.pallas_call(
        _merge_kernel,
        grid=(s // bm,),
        in_specs=[
            pl.BlockSpec((bm, c), lambda m: (m, 0)),
            pl.BlockSpec((bm, half), lambda m: (m, gb_block0)),
            pl.BlockSpec((bm, half), lambda m: (m, gb_block0 + 1)),
            pl.BlockSpec((bm, d), lambda m: (m, 0)),
            pl.BlockSpec((bm, d), lambda m: (m, 0)),
            _resident((c, d)),
            _resident((d, d)),
        ],
        out_specs=pl.BlockSpec((bm, d), lambda m: (m, 0)),
        out_shape=jax.ShapeDtypeStruct((s, d), F32),
        compiler_params=_params("arbitrary"),
        name="merge_out",
    )(og, proj, proj, ya, x, w_b, w_out)


def _causal_conv(z, prev, w_ref, b_ref):
    ext = jnp.concatenate([prev, z], axis=0)
    y = b_ref[...] + w_ref[CONV_WIDTH - 1:CONV_WIDTH, :] * z
    for back in range(1, CONV_WIDTH):
        shifted = pltpu.roll(ext, back, axis=0)[CARRY_ROWS:, :]
        y = y + w_ref[CONV_WIDTH - 1 - back:CONV_WIDTH - back, :] * shifted
    return y


def _ffn_kernel(x_ref, lnw_ref, wug_ref, wuv_ref, cwg_ref, cwv_ref, cbg_ref, cbv_ref, wd_ref,
                o_ref, h_ref, cg_ref, cv_ref):
    m = pl.program_id(0)
    j = pl.program_id(1)

    @pl.when(j == 0)
    def _():
        x = x_ref[...]
        h_ref[...] = _rms(x, lnw_ref[...]).astype(BF16)
        o_ref[...] = x

    @pl.when(m == 0)
    def _():
        cg_ref[j] = jnp.zeros(cg_ref.shape[1:], F32)
        cv_ref[j] = jnp.zeros(cv_ref.shape[1:], F32)

    h = h_ref[...]
    zg = _dot(h, wug_ref[...])
    zv = _dot(h, wuv_ref[...])
    ug = _causal_conv(zg, cg_ref[j], cwg_ref, cbg_ref)
    uv = _causal_conv(zv, cv_ref[j], cwv_ref, cbv_ref)
    bm = zg.shape[0]
    cg_ref[j] = zg[bm - CARRY_ROWS:, :]
    cv_ref[j] = zv[bm - CARRY_ROWS:, :]
    act = (ug * jax.nn.sigmoid(ug) * uv).astype(BF16)
    o_ref[...] += _dot(act, wd_ref[...])


def _ffn(x, ln_w, w_gate, w_val, cw_gate, cw_val, cb_gate, cb_val, w_down, *, bm, tf):
    s, d = x.shape
    dff = w_down.shape[0]
    nj = dff // tf
    return pl.pallas_call(
        _ffn_kernel,
        grid=(s // bm, nj),
        in_specs=[
            pl.BlockSpec((bm, d), lambda m, j: (m, 0)),
            pl.BlockSpec((1, d), lambda m, j: (0, 0)),
            pl.BlockSpec((d, tf), lambda m, j: (0, j)),
            pl.BlockSpec((d, tf), lambda m, j: (0, j)),
            pl.BlockSpec((CONV_WIDTH, tf), lambda m, j: (0, j)),
            pl.BlockSpec((CONV_WIDTH, tf), lambda m, j: (0, j)),
            pl.BlockSpec((1, tf), lambda m, j: (0, j)),
            pl.BlockSpec((1, tf), lambda m, j: (0, j)),
            pl.BlockSpec((tf, d), lambda m, j: (j, 0)),
        ],
        out_specs=pl.BlockSpec((bm, d), lambda m, j: (m, 0)),
        out_shape=jax.ShapeDtypeStruct((s, d), F32),
        scratch_shapes=[
            pltpu.VMEM((bm, d), BF16),
            pltpu.VMEM((nj, CARRY_ROWS, tf), F32),
            pltpu.VMEM((nj, CARRY_ROWS, tf), F32),
        ],
        compiler_params=_params("arbitrary", "arbitrary"),
        name="ffn",
    )(x, ln_w, w_gate, w_val, cw_gate, cw_val, cb_gate, cb_val, w_down)


def _ple_kernel(x_ref, p_ref, lnw_ref, wg_ref, wp_ref, lnf_ref, o_ref, *, final):
    x = x_ref[...]
    h = _rms(x, lnw_ref[...]).astype(BF16)
    gate = jax.nn.sigmoid(_dot(h, wg_ref[...]))
    emb = _dot(p_ref[...].astype(BF16), wp_ref[...])
    y = x + gate * emb
    o_ref[...] = _rms(y, lnf_ref[...]) if final else y


def _ple(x, p, ln_w, w_gate, w_ple, ln_final, *, bm, final):
    s, d = x.shape
    e = p.shape[1]
    return pl.pallas_call(
        functools.partial(_ple_kernel, final=final),
        grid=(s // bm,),
        in_specs=[
            pl.BlockSpec((bm, d), lambda m: (m, 0)),
            pl.BlockSpec((bm, e), lambda m: (m, 0)),
            _resident((1, d)),
            _resident((d, d)),
            _resident((e, d)),
            _resident((1, d)),
        ],
        out_specs=pl.BlockSpec((bm, d), lambda m: (m, 0)),
        out_shape=jax.ShapeDtypeStruct((s, d), F32),
        compiler_params=_params("arbitrary"),
        name="ple_final",
    )(x, p, ln_w, w_gate, w_ple, ln_final)


def _ffn_operands(w_up, conv_w, conv_b, w_down, tf):
    dff = w_down.shape[0]
    pad = (-dff) % tf

    def halves(a, dtype):
        widths = [(0, 0)] * (a.ndim - 1) + [(0, pad)]
        return [jnp.pad(part.astype(dtype), widths) for part in (a[..., :dff], a[..., dff:])]

    return (*halves(w_up, BF16), *halves(conv_w, F32), *halves(conv_b, F32),
            jnp.pad(w_down.astype(BF16), ((0, pad), (0, 0))))


def kernel(x, p, ln_mix_w, w_in, pool_mix_w, pool_scale, hgrn_lb_logits, hgrn_norm_w, w_branch_a,
           w_branch_b, w_out, ln_ffn_w, w_up, conv_w, conv_b, w_down, ln_ple_w, w_ple_gate, w_ple,
           ln_final_w):
    batch, seq, d = x.shape
    depth = w_in.shape[0]
    c_pool = w_branch_a.shape[1]
    c_hgrn = w_branch_b.shape[1]
    blk = c_hgrn
    assert c_pool == blk and d == 2 * blk, "column blocks of the combined projection must line up"
    assert seq % 1024 == 0
    tf = 512
    outs = []
    for bi in range(batch):
        xb = x[bi]
        for i in range(depth):
            proj, f_logit = _inproj(xb, ln_mix_w[i][None], w_in[i], bm=1024, bn=blk, f_block=2)
            ya = _pool_branch(proj, pool_mix_w[i].astype(BF16), pool_scale[i][None],
                              w_branch_a[i].astype(BF16), bm=1024, bn=blk, ga_block0=5)
            og = _hgrn(proj, f_logit, hgrn_lb_logits, hgrn_norm_w[i][None], bt=256, layer=i,
                       q_block=1, i_block=3, g_block=4)
            xb = _merge(og, proj, ya, xb, w_branch_b[i].astype(BF16), w_out[i].astype(BF16),
                        bm=512, gb_block0=7)
            xb = _ffn(xb, ln_ffn_w[i][None],
                      *_ffn_operands(w_up[i], conv_w[i], conv_b[i][None], w_down[i], tf), bm=512, tf=tf)
            xb = _ple(xb, p[i, bi], ln_ple_w[i][None], w_ple_gate[i].astype(BF16),
                      w_ple[i].astype(BF16), ln_final_w[None], bm=512, final=(i == depth - 1))
        outs.append(xb)
    return jnp.stack(outs, axis=0)
```

```python
import functools

import jax
import jax.numpy as jnp
from jax import lax
from jax.experimental import pallas as pl
from jax.experimental.pallas import tpu as pltpu

F32 = jnp.float32
BF16 = jnp.bfloat16

EPS = 1e-6
POOL_WINDOWS = (2, 4, 8, 16)
POOL_HALO = 16
HEAD_DIM = 128
CHUNK = 64
SUB = 16
NSUB = CHUNK // SUB
CONV_WIDTH = 3
CARRY_ROWS = 8
LANE = 128
LOG2_E = 1.4426950408889634
MILD_DECAY_LOG2 = -100.0
V7X_VMEM_BYTES = 64 * 1024 * 1024
VMEM_LIMIT = V7X_VMEM_BYTES - 8 * 1024 * 1024


def _dot(a, b):
    return jnp.dot(a, b, preferred_element_type=F32)


def _dot_nt(a, b):
    return lax.dot_general(a, b, (((1,), (1,)), ((), ())), preferred_element_type=F32)


def _dot_tn(a, b):
    return lax.dot_general(a, b, (((0,), (0,)), ((), ())), preferred_element_type=F32)


def _rms(x, w):
    return x * lax.rsqrt(jnp.mean(x * x, axis=-1, keepdims=True) + EPS) * w


def _params(*sem):
    return pltpu.CompilerParams(dimension_semantics=sem, vmem_limit_bytes=VMEM_LIMIT)


def _resident(shape):
    return pl.BlockSpec(shape, lambda *_: (0,) * len(shape), pipeline_mode=pl.Buffered(1))


def _inproj_kernel(x_ref, lnw_ref, w_ref, o_ref, f_ref, h_ref, *, f_block):
    n = pl.program_id(1)

    @pl.when(n == 0)
    def _():
        h_ref[...] = _rms(x_ref[...], lnw_ref[...]).astype(BF16)

    acc = _dot(h_ref[...], w_ref[...].astype(BF16))
    o_ref[...] = acc.astype(o_ref.dtype)

    @pl.when(n == f_block)
    def _():
        f_ref[...] = acc


def _inproj(x, ln_w, w_in, *, bm, bn, f_block):
    s, d = x.shape
    d_in = w_in.shape[1]
    return pl.pallas_call(
        functools.partial(_inproj_kernel, f_block=f_block),
        grid=(s // bm, d_in // bn),
        in_specs=[
            pl.BlockSpec((bm, d), lambda m, n: (m, 0)),
            pl.BlockSpec((1, d), lambda m, n: (0, 0)),
            pl.BlockSpec((d, bn), lambda m, n: (0, n)),
        ],
        out_specs=[
            pl.BlockSpec((bm, bn), lambda m, n: (m, n)),
            pl.BlockSpec((bm, bn), lambda m, n: (m, 0)),
        ],
        out_shape=[
            jax.ShapeDtypeStruct((s, d_in), BF16),
            jax.ShapeDtypeStruct((s, bn), F32),
        ],
        scratch_shapes=[pltpu.VMEM((bm, d), BF16)],
        compiler_params=_params("arbitrary", "arbitrary"),
        name="inproj",
    )(x, ln_w, w_in)


def _pool_kernel(u_ref, halo_ref, ga_ref, mix_ref, scale_ref, wa_ref, o_ref, feat_ref, *, bm):
    m = pl.program_id(0)
    n = pl.program_id(1)

    @pl.when(n == 0)
    def _():
        u = u_ref[...].astype(F32)
        halo = halo_ref[...].astype(F32) * (m > 0).astype(F32)
        ext = jnp.concatenate([halo, u], axis=0)
        gw = u.shape[1] // len(POOL_WINDOWS)
        pos = m * bm + lax.broadcasted_iota(jnp.int32, (bm, 1), 0) + 1
        run = ext
        width = 1
        feats = []
        for g, w in enumerate(POOL_WINDOWS):
            while width < w:
                run = run + pltpu.roll(run, width, axis=0)
                width *= 2
            cnt = jnp.minimum(pos, w).astype(F32)
            win = run[POOL_HALO:, g * gw:(g + 1) * gw]
            d = win / cnt - u[:, g * gw:(g + 1) * gw]
            y = _dot(d.astype(BF16), mix_ref[g])
            feats.append(y * scale_ref[:, g * gw:(g + 1) * gw])
        feat_ref[...] = jnp.concatenate(feats, axis=1).astype(BF16)

    y_a = _dot(feat_ref[...], wa_ref[...])
    o_ref[...] = (jax.nn.sigmoid(ga_ref[...].astype(F32)) * y_a).astype(o_ref.dtype)


def _pool_branch(proj, mix_w, scale, w_a, *, bm, bn, ga_block0):
    s = proj.shape[0]
    c = w_a.shape[0]
    d = w_a.shape[1]
    g, gw, _ = mix_w.shape
    halo_per_tile = bm // POOL_HALO
    return pl.pallas_call(
        functools.partial(_pool_kernel, bm=bm),
        grid=(s // bm, d // bn),
        in_specs=[
            pl.BlockSpec((bm, c), lambda m, n: (m, 0)),
            pl.BlockSpec((POOL_HALO, c), lambda m, n: (jnp.maximum(m * halo_per_tile - 1, 0), 0)),
            pl.BlockSpec((bm, bn), lambda m, n: (m, ga_block0 + n)),
            _resident((g, gw, gw)),
            _resident((1, c)),
            pl.BlockSpec((c, bn), lambda m, n: (0, n)),
        ],
        out_specs=pl.BlockSpec((bm, bn), lambda m, n: (m, n)),
        out_shape=jax.ShapeDtypeStruct((s, d), BF16),
        scratch_shapes=[pltpu.VMEM((bm, c), BF16)],
        compiler_params=_params("arbitrary", "arbitrary"),
        name="pool_branch",
    )(proj, proj, proj, mix_w, scale, w_a)


def _split3(x):
    hi = x.astype(BF16)
    r = x - hi.astype(F32)
    mid = r.astype(BF16)
    lo = (r - mid.astype(F32)).astype(BF16)
    return hi, mid, lo


def _gates(fl, lb):
    one_m_lb = 1.0 - lb
    sg = jax.nn.sigmoid(fl)
    log2_f = jnp.log(lb + one_m_lb * sg) * LOG2_E
    return log2_f, one_m_lb * (1.0 - sg)


def _chunk_cumsum(tri, x):
    hi, mid, lo = _split3(x)
    return _dot(tri, hi) + _dot(tri, mid) + _dot(tri, lo)


def _head_norm_gate(o_heads, norm_w, g):
    o = jnp.concatenate(
        [o_h * lax.rsqrt(jnp.mean(o_h * o_h, axis=-1, keepdims=True) + EPS) for o_h in o_heads], axis=1)
    return o * norm_w * (g * jax.nn.sigmoid(g))


def _hgrn_chunk_any_decay(c, q_ref, f_ref, i_ref, g_ref, tri_ref, wsel_ref, o_ref, st_ref, lb, norm_w):
    r0 = pl.multiple_of(c * CHUNK, CHUNK)
    rows = pl.ds(r0, CHUNK)
    d = q_ref.shape[1]
    heads = d // HEAD_DIM

    q = q_ref[rows, :].astype(F32)
    v = i_ref[rows, :]
    log2_f, kk = _gates(f_ref[rows, :], lb)
    b = _chunk_cumsum(tri_ref[:CHUNK, :CHUNK], log2_f)
    b_last = b[CHUNK - 1:CHUNK, :]

    q_in = (q * jnp.exp2(b)).astype(BF16)
    k_st = (kk * jnp.exp2(b_last - b)).astype(BF16)

    b_end = jnp.concatenate(
        [jnp.broadcast_to(b[(j + 1) * SUB - 1:(j + 1) * SUB, :], (SUB, d)) for j in range(NSUB)], axis=0)
    k_hat = kk * jnp.exp2(b_end - b)
    sub_id = lax.broadcasted_iota(jnp.int32, (CHUNK, 1), 0) // SUB
    q_from = []
    for j in range(NSUB - 1):
        lo_row = (j + 1) * SUB
        bj = b[lo_row - 1:lo_row, :]
        part = q[lo_row:, :] * jnp.exp2(b[lo_row:, :] - bj)
        q_from.append(jnp.concatenate([jnp.zeros((lo_row, d), F32), part], axis=0).astype(BF16))
    k_src = [jnp.where(sub_id == j, k_hat, 0.0).astype(BF16) for j in range(NSUB - 1)]

    per_head = [[] for _ in range(heads)]
    for r in range(NSUB):
        qb = q[r * SUB:(r + 1) * SUB, :]
        bb = b[r * SUB:(r + 1) * SUB, :]
        pieces = []
        for s in range(SUB):
            row = r * SUB + s
            e = jnp.exp2(jnp.minimum(bb - b[row:row + 1, :], 0.0))
            pieces.append((qb * kk[row:row + 1, :] * e).astype(BF16))
        for h in range(heads):
            per_head[h].append(jnp.concatenate(
                [p[:, h * HEAD_DIM:(h + 1) * HEAD_DIM] for p in pieces], axis=1))
    a_big = jnp.concatenate([jnp.concatenate(blocks, axis=0) for blocks in per_head], axis=0)
    diag = _dot(a_big, wsel_ref[...])

    t_id = lax.broadcasted_iota(jnp.int32, (CHUNK, CHUNK), 0)
    s_id = lax.broadcasted_iota(jnp.int32, (CHUNK, CHUNK), 1)
    diag_mask = (t_id // SUB == s_id // SUB) & (s_id <= t_id)

    outs = []
    for h in range(heads):
        hs = slice(h * HEAD_DIM, (h + 1) * HEAD_DIM)
        q_cat = jnp.concatenate([qf[:, hs] for qf in q_from], axis=1)
        k_cat = jnp.concatenate([ks[:, hs] for ks in k_src], axis=1)
        scores = _dot_nt(q_cat, k_cat) + jnp.where(diag_mask, diag[h * CHUNK:(h + 1) * CHUNK, :], 0.0)
        v_h = v[:, hs]
        state_t = st_ref[h]
        o_h = _dot_nt(q_in[:, hs], state_t.astype(BF16)) + _dot(scores.astype(BF16), v_h)
        st_ref[h] = state_t * jnp.exp2(b_last[:, hs]) + _dot_tn(v_h, k_st[:, hs])
        outs.append(o_h)
    o_ref[rows, :] = _head_norm_gate(outs, norm_w, g_ref[rows, :].astype(F32)).astype(o_ref.dtype)


def _hgrn_tile_mild_decay(b, kk, q_ref, i_ref, g_ref, o_ref, st_ref, norm_w):
    bt, d = b.shape
    heads = d // HEAD_DIM
    n_chunks = bt // CHUNK
    q = q_ref[...].astype(F32)
    v = i_ref[...]
    b_last_rows = [b[(c + 1) * CHUNK - 1:(c + 1) * CHUNK, :] for c in range(n_chunks)]
    b_last = jnp.concatenate([jnp.broadcast_to(r, (CHUNK, d)) for r in b_last_rows], axis=0)
    q_in = (q * jnp.exp2(b)).astype(BF16)
    k_out = (kk * jnp.exp2(-b)).astype(BF16)
    k_st = (kk * jnp.exp2(b_last - b)).astype(BF16)
    causal = (lax.broadcasted_iota(jnp.int32, (CHUNK, CHUNK), 1)
              <= lax.broadcasted_iota(jnp.int32, (CHUNK, CHUNK), 0))
    states = [st_ref[h] for h in range(heads)]
    tile_out = []
    for c in range(n_chunks):
        rs = slice(c * CHUNK, (c + 1) * CHUNK)
        carry_decay = jnp.exp2(b_last_rows[c])
        outs = []
        for h in range(heads):
            hs = slice(h * HEAD_DIM, (h + 1) * HEAD_DIM)
            scores = jnp.where(causal, _dot_nt(q_in[rs, hs], k_out[rs, hs]), 0.0).astype(BF16)
            outs.append(_dot(scores, v[rs, hs]) + _dot_nt(q_in[rs, hs], states[h].astype(BF16)))
            states[h] = states[h] * carry_decay[:, hs] + _dot_tn(v[rs, hs], k_st[rs, hs])
        tile_out.append(jnp.concatenate(outs, axis=1))
    for h in range(heads):
        st_ref[h] = states[h]
    heads_out = jnp.concatenate(tile_out, axis=0)
    o_heads = [heads_out[:, h * HEAD_DIM:(h + 1) * HEAD_DIM] for h in range(heads)]
    o_ref[...] = _head_norm_gate(o_heads, norm_w, g_ref[...].astype(F32)).astype(o_ref.dtype)


def _hgrn_kernel(q_ref, f_ref, i_ref, g_ref, lbl_ref, nw_ref, tri_ref, wsel_ref, o_ref, st_ref, *, layer):
    @pl.when(pl.program_id(0) == 0)
    def _():
        st_ref[...] = jnp.zeros_like(st_ref)

    logits = lbl_ref[...]
    e = jnp.exp(logits - jnp.max(logits, axis=0, keepdims=True))
    lb = jnp.sum(e[:layer + 1, :], axis=0, keepdims=True) / jnp.sum(e, axis=0, keepdims=True)
    norm_w = nw_ref[...]

    log2_f, kk = _gates(f_ref[...], lb)
    b = _chunk_cumsum(tri_ref[...], log2_f)
    mild = jnp.min(b) >= MILD_DECAY_LOG2

    @pl.when(mild)
    def _():
        _hgrn_tile_mild_decay(b, kk, q_ref, i_ref, g_ref, o_ref, st_ref, norm_w)

    @pl.when(jnp.logical_not(mild))
    def _():
        def body(c, carry):
            _hgrn_chunk_any_decay(c, q_ref, f_ref, i_ref, g_ref, tri_ref, wsel_ref, o_ref, st_ref, lb, norm_w)
            return carry

        lax.fori_loop(0, q_ref.shape[0] // CHUNK, body, 0)


def _hgrn(proj, f_logit, lb_logits, norm_w, *, bt, layer, q_block, i_block, g_block):
    s = proj.shape[0]
    d = f_logit.shape[1]
    heads = d // HEAD_DIM
    t_id = lax.broadcasted_iota(jnp.int32, (bt, bt), 0)
    s_id = lax.broadcasted_iota(jnp.int32, (bt, bt), 1)
    tri = ((s_id <= t_id) & (s_id // CHUNK == t_id // CHUNK)).astype(BF16)
    src = lax.broadcasted_iota(jnp.int32, (SUB * HEAD_DIM, CHUNK), 0) // HEAD_DIM
    col = lax.broadcasted_iota(jnp.int32, (SUB * HEAD_DIM, CHUNK), 1) % SUB
    wsel = (src == col).astype(BF16)
    nl = lb_logits.shape[0]
    return pl.pallas_call(
        functools.partial(_hgrn_kernel, layer=layer),
        grid=(s // bt,),
        in_specs=[
            pl.BlockSpec((bt, d), lambda t: (t, q_block)),
            pl.BlockSpec((bt, d), lambda t: (t, 0)),
            pl.BlockSpec((bt, d), lambda t: (t, i_block)),
            pl.BlockSpec((bt, d), lambda t: (t, g_block)),
            _resident((nl, d)),
            _resident((1, d)),
            _resident((bt, bt)),
            _resident((SUB * HEAD_DIM, CHUNK)),
        ],
        out_specs=pl.BlockSpec((bt, d), lambda t: (t, 0)),
        out_shape=jax.ShapeDtypeStruct((s, d), BF16),
        scratch_shapes=[pltpu.VMEM((heads, HEAD_DIM, HEAD_DIM), F32)],
        compiler_params=_params("arbitrary"),
        name="hgrn",
    )(proj, f_logit, proj, proj, lb_logits, norm_w, tri, wsel)


def _merge_kernel(og_ref, gb0_ref, gb1_ref, ya_ref, x_ref, wb_ref, wo_ref, o_ref):
    y_b = _dot(og_ref[...], wb_ref[...])
    gate_b = jnp.concatenate([gb0_ref[...], gb1_ref[...]], axis=1).astype(F32)
    merged = ya_ref[...].astype(F32) + jax.nn.sigmoid(gate_b) * y_b
    o_ref[...] = x_ref[...] + _dot(merged.astype(BF16), wo_ref[...])


def _merge(og, proj, ya, x, w_b, w_out, *, bm, gb_block0):
    s, d = x.shape
    c = og.shape[1]
    half = d // 2
    return pl.pallas_call(
        _merge_kernel,
        grid=(s // bm,),
        in_specs=[
            pl.BlockSpec((bm, c), lambda m: (m, 0)),
            pl.BlockSpec((bm, half), lambda m: (m, gb_block0)),
            pl.BlockSpec((bm, half), lambda m: (m, gb_block0 + 1)),
            pl.BlockSpec((bm, d), lambda m: (m, 0)),
            pl.BlockSpec((bm, d), lambda m: (m, 0)),
            _resident((c, d)),
            _resident((d, d)),
        ],
        out_specs=pl.BlockSpec((bm, d), lambda m: (m, 0)),
        out_shape=jax.ShapeDtypeStruct((s, d), F32),
        compiler_params=_params("arbitrary"),
        name="merge_out",
    )(og, proj, proj, ya, x, w_b, w_out)


def _causal_conv(z, prev, w_ref, b_ref):
    ext = jnp.concatenate([prev, z], axis=0)
    y = b_ref[...] + w_ref[CONV_WIDTH - 1:CONV_WIDTH, :] * z
    for back in range(1, CONV_WIDTH):
        shifted = pltpu.roll(ext, back, axis=0)[CARRY_ROWS:, :]
        y = y + w_ref[CONV_WIDTH - 1 - back:CONV_WIDTH - back, :] * shifted
    return y


def _ffn_kernel(x_ref, lnw_ref, wug_ref, wuv_ref, cwg_ref, cwv_ref, cbg_ref, cbv_ref, wd_ref,
                o_ref, h_ref, cg_ref, cv_ref, *, dff):
    m = pl.program_id(0)
    j = pl.program_id(1)
    tf = wd_ref.shape[0]
    shared = j * tf - jnp.minimum(j * tf, dff - tf)

    @pl.when(j == 0)
    def _():
        x = x_ref[...]
        h_ref[...] = _rms(x, lnw_ref[...]).astype(BF16)
        o_ref[...] = x

    @pl.when(m == 0)
    def _():
        cg_ref[j] = jnp.zeros(cg_ref.shape[1:], F32)
        cv_ref[j] = jnp.zeros(cv_ref.shape[1:], F32)

    h = h_ref[...]
    zg = _dot(h, wug_ref[...])
    zv = _dot(h, wuv_ref[...])
    ug = _causal_conv(zg, cg_ref[j], cwg_ref, cbg_ref)
    uv = _causal_conv(zv, cv_ref[j], cwv_ref, cbv_ref)
    bm = zg.shape[0]
    cg_ref[j] = zg[bm - CARRY_ROWS:, :]
    cv_ref[j] = zv[bm - CARRY_ROWS:, :]
    fresh = lax.broadcasted_iota(jnp.int32, (1, tf), 1) >= shared
    act = jnp.where(fresh, ug * jax.nn.sigmoid(ug) * uv, 0.0).astype(BF16)
    o_ref[...] += _dot(act, wd_ref[...])


def _ffn(x, ln_w, w_up, conv_w, conv_b, w_down, *, bm, tf):
    s, d = x.shape
    dff = w_down.shape[0]
    nj = pl.cdiv(dff, tf)
    assert dff >= tf and dff % LANE == 0

    def start(j, base=0):
        return (base // LANE + jnp.minimum(j * (tf // LANE), (dff - tf) // LANE)) * LANE

    def window(rows, offset):
        return pl.BlockSpec((pl.Element(rows), pl.Element(tf)), lambda m, j: (0, offset(j)))

    return pl.pallas_call(
        functools.partial(_ffn_kernel, dff=dff),
        grid=(s // bm, nj),
        in_specs=[
            pl.BlockSpec((bm, d), lambda m, j: (m, 0)),
            pl.BlockSpec((1, d), lambda m, j: (0, 0)),
            window(d, start),
            window(d, lambda j: start(j, dff)),
            window(CONV_WIDTH, start),
            window(CONV_WIDTH, lambda j: start(j, dff)),
            window(1, start),
            window(1, lambda j: start(j, dff)),
            pl.BlockSpec((pl.Element(tf), pl.Element(d)), lambda m, j: (start(j), 0)),
        ],
        out_specs=pl.BlockSpec((bm, d), lambda m, j: (m, 0)),
        out_shape=jax.ShapeDtypeStruct((s, d), F32),
        scratch_shapes=[
            pltpu.VMEM((bm, d), BF16),
            pltpu.VMEM((nj, CARRY_ROWS, tf), F32),
            pltpu.VMEM((nj, CARRY_ROWS, tf), F32),
        ],
        compiler_params=_params("arbitrary", "arbitrary"),
        name="ffn",
    )(x, ln_w, w_up, w_up, conv_w, conv_w, conv_b, conv_b, w_down)


def _ple_kernel(x_ref, p_ref, lnw_ref, wg_ref, wp_ref, lnf_ref, o_ref, *, final):
    x = x_ref[...]
    h = _rms(x, lnw_ref[...]).astype(BF16)
    gate = jax.nn.sigmoid(_dot(h, wg_ref[...]))
    emb = _dot(p_ref[...].astype(BF16), wp_ref[...])
    y = x + gate * emb
    o_ref[...] = _rms(y, lnf_ref[...]) if final else y


def _ple(x, p, ln_w, w_gate, w_ple, ln_final, *, bm, final):
    s, d = x.shape
    e = p.shape[1]
    return pl.pallas_call(
        functools.partial(_ple_kernel, final=final),
        grid=(s // bm,),
        in_specs=[
            pl.BlockSpec((bm, d), lambda m: (m, 0)),
            pl.BlockSpec((bm, e), lambda m: (m, 0)),
            _resident((1, d)),
            _resident((d, d)),
            _resident((e, d)),
            _resident((1, d)),
        ],
        out_specs=pl.BlockSpec((bm, d), lambda m: (m, 0)),
        out_shape=jax.ShapeDtypeStruct((s, d), F32),
        compiler_params=_params("arbitrary"),
        name="ple_final",
    )(x, p, ln_w, w_gate, w_ple, ln_final)


def kernel(x, p, ln_mix_w, w_in, pool_mix_w, pool_scale, hgrn_lb_logits, hgrn_norm_w, w_branch_a,
           w_branch_b, w_out, ln_ffn_w, w_up, conv_w, conv_b, w_down, ln_ple_w, w_ple_gate, w_ple,
           ln_final_w):
    batch, seq, d = x.shape
    depth = w_in.shape[0]
    c_pool = w_branch_a.shape[1]
    c_hgrn = w_branch_b.shape[1]
    blk = c_hgrn
    assert c_pool == blk and d == 2 * blk, "column blocks of the combined projection must line up"
    assert seq % 1024 == 0
    tf = 512
    outs = []
    for bi in range(batch):
        xb = x[bi]
        for i in range(depth):
            proj, f_logit = _inproj(xb, ln_mix_w[i][None], w_in[i], bm=1024, bn=blk, f_block=2)
            ya = _pool_branch(proj, pool_mix_w[i].astype(BF16), pool_scale[i][None],
                              w_branch_a[i].astype(BF16), bm=1024, bn=blk, ga_block0=5)
            og = _hgrn(proj, f_logit, hgrn_lb_logits, hgrn_norm_w[i][None], bt=256, layer=i,
                       q_block=1, i_block=3, g_block=4)
            xb = _merge(og, proj, ya, xb, w_branch_b[i].astype(BF16), w_out[i].astype(BF16),
                        bm=512, gb_block0=7)
            xb = _ffn(xb, ln_ffn_w[i][None], w_up[i].astype(BF16), conv_w[i], conv_b[i][None],
                      w_down[i].astype(BF16), bm=512, tf=512)
            xb = _ple(xb, p[i, bi], ln_ple_w[i][None], w_ple_gate[i].astype(BF16),
                      w_ple[i].astype(BF16), ln_final_w[None], bm=512, final=(i == depth - 1))
        outs.append(xb)
    return jnp.stack(outs, axis=0)
```

```python
import functools

import jax
import jax.numpy as jnp
from jax import lax
from jax.experimental import pallas as pl
from jax.experimental.pallas import tpu as pltpu

F32 = jnp.float32
BF16 = jnp.bfloat16

EPS = 1e-6
POOL_WINDOWS = (2, 4, 8, 16)
POOL_HALO = 16
HEAD_DIM = 128
CHUNK = 64
SUB = 16
NSUB = CHUNK // SUB
CONV_WIDTH = 3
CARRY_ROWS = 8
ACT_ROWS = 64
LANE = 128
LOG2_E = 1.4426950408889634
MILD_DECAY_LOG2 = -100.0
V7X_VMEM_BYTES = 64 * 1024 * 1024
VMEM_LIMIT = V7X_VMEM_BYTES - 8 * 1024 * 1024


def _dot(a, b):
    return jnp.dot(a, b, preferred_element_type=F32)


def _dot_nt(a, b):
    return lax.dot_general(a, b, (((1,), (1,)), ((), ())), preferred_element_type=F32)


def _dot_tn(a, b):
    return lax.dot_general(a, b, (((0,), (0,)), ((), ())), preferred_element_type=F32)


def _rms(x, w):
    return x * lax.rsqrt(jnp.mean(x * x, axis=-1, keepdims=True) + EPS) * w


def _params(*sem):
    return pltpu.CompilerParams(dimension_semantics=sem, vmem_limit_bytes=VMEM_LIMIT)


def _resident(shape):
    return pl.BlockSpec(shape, lambda *_: (0,) * len(shape), pipeline_mode=pl.Buffered(1))


def _inproj_kernel(x_ref, lnw_ref, w_ref, o_ref, f_ref, h_ref, *, f_block):
    n = pl.program_id(1)

    @pl.when(n == 0)
    def _():
        h_ref[...] = _rms(x_ref[...], lnw_ref[...]).astype(BF16)

    acc = _dot(h_ref[...], w_ref[...].astype(BF16))
    o_ref[...] = acc.astype(o_ref.dtype)

    @pl.when(n == f_block)
    def _():
        f_ref[...] = acc


def _inproj(x, ln_w, w_in, *, bm, bn, f_block):
    s, d = x.shape
    d_in = w_in.shape[1]
    return pl.pallas_call(
        functools.partial(_inproj_kernel, f_block=f_block),
        grid=(s // bm, d_in // bn),
        in_specs=[
            pl.BlockSpec((bm, d), lambda m, n: (m, 0)),
            pl.BlockSpec((1, d), lambda m, n: (0, 0)),
            pl.BlockSpec((d, bn), lambda m, n: (0, n)),
        ],
        out_specs=[
            pl.BlockSpec((bm, bn), lambda m, n: (m, n)),
            pl.BlockSpec((bm, bn), lambda m, n: (m, 0)),
        ],
        out_shape=[
            jax.ShapeDtypeStruct((s, d_in), BF16),
            jax.ShapeDtypeStruct((s, bn), F32),
        ],
        scratch_shapes=[pltpu.VMEM((bm, d), BF16)],
        compiler_params=_params("arbitrary", "arbitrary"),
        name="inproj",
    )(x, ln_w, w_in)


def _pool_kernel(u_ref, halo_ref, ga_ref, mix_ref, scale_ref, wa_ref, o_ref, feat_ref, *, bm):
    m = pl.program_id(0)
    n = pl.program_id(1)

    @pl.when(n == 0)
    def _():
        u = u_ref[...].astype(F32)
        halo = halo_ref[...].astype(F32) * (m > 0).astype(F32)
        ext = jnp.concatenate([halo, u], axis=0)
        gw = u.shape[1] // len(POOL_WINDOWS)
        pos = m * bm + lax.broadcasted_iota(jnp.int32, (bm, 1), 0) + 1
        run = ext
        width = 1
        feats = []
        for g, w in enumerate(POOL_WINDOWS):
            while width < w:
                run = run + pltpu.roll(run, width, axis=0)
                width *= 2
            cnt = jnp.minimum(pos, w).astype(F32)
            win = run[POOL_HALO:, g * gw:(g + 1) * gw]
            d = win / cnt - u[:, g * gw:(g + 1) * gw]
            y = _dot(d.astype(BF16), mix_ref[g])
            feats.append(y * scale_ref[:, g * gw:(g + 1) * gw])
        feat_ref[...] = jnp.concatenate(feats, axis=1).astype(BF16)

    y_a = _dot(feat_ref[...], wa_ref[...])
    o_ref[...] = (jax.nn.sigmoid(ga_ref[...].astype(F32)) * y_a).astype(o_ref.dtype)


def _pool_branch(proj, mix_w, scale, w_a, *, bm, bn, ga_block0):
    s = proj.shape[0]
    c = w_a.shape[0]
    d = w_a.shape[1]
    g, gw, _ = mix_w.shape
    halo_per_tile = bm // POOL_HALO
    return pl.pallas_call(
        functools.partial(_pool_kernel, bm=bm),
        grid=(s // bm, d // bn),
        in_specs=[
            pl.BlockSpec((bm, c), lambda m, n: (m, 0)),
            pl.BlockSpec((POOL_HALO, c), lambda m, n: (jnp.maximum(m * halo_per_tile - 1, 0), 0)),
            pl.BlockSpec((bm, bn), lambda m, n: (m, ga_block0 + n)),
            _resident((g, gw, gw)),
            _resident((1, c)),
            pl.BlockSpec((c, bn), lambda m, n: (0, n)),
        ],
        out_specs=pl.BlockSpec((bm, bn), lambda m, n: (m, n)),
        out_shape=jax.ShapeDtypeStruct((s, d), BF16),
        scratch_shapes=[pltpu.VMEM((bm, c), BF16)],
        compiler_params=_params("arbitrary", "arbitrary"),
        name="pool_branch",
    )(proj, proj, proj, mix_w, scale, w_a)


def _split3(x):
    hi = x.astype(BF16)
    r = x - hi.astype(F32)
    mid = r.astype(BF16)
    lo = (r - mid.astype(F32)).astype(BF16)
    return hi, mid, lo


def _gates(fl, lb):
    one_m_lb = 1.0 - lb
    sg = jax.nn.sigmoid(fl)
    log2_f = jnp.log(lb + one_m_lb * sg) * LOG2_E
    return log2_f, one_m_lb * (1.0 - sg)


def _chunk_cumsum(tri, x):
    hi, mid, lo = _split3(x)
    return _dot(tri, hi) + _dot(tri, mid) + _dot(tri, lo)


def _head_norm_gate(o_heads, norm_w, g):
    o = jnp.concatenate(
        [o_h * lax.rsqrt(jnp.mean(o_h * o_h, axis=-1, keepdims=True) + EPS) for o_h in o_heads], axis=1)
    return o * norm_w * (g * jax.nn.sigmoid(g))


def _hgrn_chunk_any_decay(c, q_ref, f_ref, i_ref, g_ref, tri_ref, wsel_ref, o_ref, st_ref, lb, norm_w):
    r0 = pl.multiple_of(c * CHUNK, CHUNK)
    rows = pl.ds(r0, CHUNK)
    d = q_ref.shape[1]
    heads = d // HEAD_DIM

    q = q_ref[rows, :].astype(F32)
    v = i_ref[rows, :]
    log2_f, kk = _gates(f_ref[rows, :], lb)
    b = _chunk_cumsum(tri_ref[:CHUNK, :CHUNK], log2_f)
    b_last = b[CHUNK - 1:CHUNK, :]

    q_in = (q * jnp.exp2(b)).astype(BF16)
    k_st = (kk * jnp.exp2(b_last - b)).astype(BF16)

    b_end = jnp.concatenate(
        [jnp.broadcast_to(b[(j + 1) * SUB - 1:(j + 1) * SUB, :], (SUB, d)) for j in range(NSUB)], axis=0)
    k_hat = kk * jnp.exp2(b_end - b)
    sub_id = lax.broadcasted_iota(jnp.int32, (CHUNK, 1), 0) // SUB
    q_from = []
    for j in range(NSUB - 1):
        lo_row = (j + 1) * SUB
        bj = b[lo_row - 1:lo_row, :]
        part = q[lo_row:, :] * jnp.exp2(b[lo_row:, :] - bj)
        q_from.append(jnp.concatenate([jnp.zeros((lo_row, d), F32), part], axis=0).astype(BF16))
    k_src = [jnp.where(sub_id == j, k_hat, 0.0).astype(BF16) for j in range(NSUB - 1)]

    per_head = [[] for _ in range(heads)]
    for r in range(NSUB):
        qb = q[r * SUB:(r + 1) * SUB, :]
        bb = b[r * SUB:(r + 1) * SUB, :]
        pieces = []
        for s in range(SUB):
            row = r * SUB + s
            e = jnp.exp2(jnp.minimum(bb - b[row:row + 1, :], 0.0))
            pieces.append((qb * kk[row:row + 1, :] * e).astype(BF16))
        for h in range(heads):
            per_head[h].append(jnp.concatenate(
                [p[:, h * HEAD_DIM:(h + 1) * HEAD_DIM] for p in pieces], axis=1))
    a_big = jnp.concatenate([jnp.concatenate(blocks, axis=0) for blocks in per_head], axis=0)
    diag = _dot(a_big, wsel_ref[...])

    t_id = lax.broadcasted_iota(jnp.int32, (CHUNK, CHUNK), 0)
    s_id = lax.broadcasted_iota(jnp.int32, (CHUNK, CHUNK), 1)
    diag_mask = (t_id // SUB == s_id // SUB) & (s_id <= t_id)

    outs = []
    for h in range(heads):
        hs = slice(h * HEAD_DIM, (h + 1) * HEAD_DIM)
        q_cat = jnp.concatenate([qf[:, hs] for qf in q_from], axis=1)
        k_cat = jnp.concatenate([ks[:, hs] for ks in k_src], axis=1)
        scores = _dot_nt(q_cat, k_cat) + jnp.where(diag_mask, diag[h * CHUNK:(h + 1) * CHUNK, :], 0.0)
        v_h = v[:, hs]
        state_t = st_ref[h]
        o_h = _dot_nt(q_in[:, hs], state_t.astype(BF16)) + _dot(scores.astype(BF16), v_h)
        st_ref[h] = state_t * jnp.exp2(b_last[:, hs]) + _dot_tn(v_h, k_st[:, hs])
        outs.append(o_h)
    o_ref[rows, :] = _head_norm_gate(outs, norm_w, g_ref[rows, :].astype(F32)).astype(o_ref.dtype)


def _hgrn_tile_mild_decay(b, kk, q_ref, i_ref, g_ref, o_ref, st_ref, norm_w):
    bt, d = b.shape
    heads = d // HEAD_DIM
    n_chunks = bt // CHUNK
    q = q_ref[...].astype(F32)
    v = i_ref[...]
    b_last_rows = [b[(c + 1) * CHUNK - 1:(c + 1) * CHUNK, :] for c in range(n_chunks)]
    b_last = jnp.concatenate([jnp.broadcast_to(r, (CHUNK, d)) for r in b_last_rows], axis=0)
    q_in = (q * jnp.exp2(b)).astype(BF16)
    k_out = (kk * jnp.exp2(-b)).astype(BF16)
    k_st = (kk * jnp.exp2(b_last - b)).astype(BF16)
    causal = (lax.broadcasted_iota(jnp.int32, (CHUNK, CHUNK), 1)
              <= lax.broadcasted_iota(jnp.int32, (CHUNK, CHUNK), 0))
    states = [st_ref[h] for h in range(heads)]
    tile_out = []
    for c in range(n_chunks):
        rs = slice(c * CHUNK, (c + 1) * CHUNK)
        carry_decay = jnp.exp2(b_last_rows[c])
        outs = []
        for h in range(heads):
            hs = slice(h * HEAD_DIM, (h + 1) * HEAD_DIM)
            scores = jnp.where(causal, _dot_nt(q_in[rs, hs], k_out[rs, hs]), 0.0).astype(BF16)
            outs.append(_dot(scores, v[rs, hs]) + _dot_nt(q_in[rs, hs], states[h].astype(BF16)))
            states[h] = states[h] * carry_decay[:, hs] + _dot_tn(v[rs, hs], k_st[rs, hs])
        tile_out.append(jnp.concatenate(outs, axis=1))
    for h in range(heads):
        st_ref[h] = states[h]
    heads_out = jnp.concatenate(tile_out, axis=0)
    o_heads = [heads_out[:, h * HEAD_DIM:(h + 1) * HEAD_DIM] for h in range(heads)]
    o_ref[...] = _head_norm_gate(o_heads, norm_w, g_ref[...].astype(F32)).astype(o_ref.dtype)


def _hgrn_kernel(q_ref, f_ref, i_ref, g_ref, lbl_ref, nw_ref, tri_ref, wsel_ref, o_ref, st_ref, *, layer):
    @pl.when(pl.program_id(0) == 0)
    def _():
        st_ref[...] = jnp.zeros_like(st_ref)

    logits = lbl_ref[...]
    e = jnp.exp(logits - jnp.max(logits, axis=0, keepdims=True))
    lb = jnp.sum(e[:layer + 1, :], axis=0, keepdims=True) / jnp.sum(e, axis=0, keepdims=True)
    norm_w = nw_ref[...]

    log2_f, kk = _gates(f_ref[...], lb)
    b = _chunk_cumsum(tri_ref[...], log2_f)
    mild = jnp.min(b) >= MILD_DECAY_LOG2

    @pl.when(mild)
    def _():
        _hgrn_tile_mild_decay(b, kk, q_ref, i_ref, g_ref, o_ref, st_ref, norm_w)

    @pl.when(jnp.logical_not(mild))
    def _():
        def body(c, carry):
            _hgrn_chunk_any_decay(c, q_ref, f_ref, i_ref, g_ref, tri_ref, wsel_ref, o_ref, st_ref, lb, norm_w)
            return carry

        lax.fori_loop(0, q_ref.shape[0] // CHUNK, body, 0)


def _hgrn(proj, f_logit, lb_logits, norm_w, *, bt, layer, q_block, i_block, g_block):
    s = proj.shape[0]
    d = f_logit.shape[1]
    heads = d // HEAD_DIM
    t_id = lax.broadcasted_iota(jnp.int32, (bt, bt), 0)
    s_id = lax.broadcasted_iota(jnp.int32, (bt, bt), 1)
    tri = ((s_id <= t_id) & (s_id // CHUNK == t_id // CHUNK)).astype(BF16)
    src = lax.broadcasted_iota(jnp.int32, (SUB * HEAD_DIM, CHUNK), 0) // HEAD_DIM
    col = lax.broadcasted_iota(jnp.int32, (SUB * HEAD_DIM, CHUNK), 1) % SUB
    wsel = (src == col).astype(BF16)
    nl = lb_logits.shape[0]
    return pl.pallas_call(
        functools.partial(_hgrn_kernel, layer=layer),
        grid=(s // bt,),
        in_specs=[
            pl.BlockSpec((bt, d), lambda t: (t, q_block)),
            pl.BlockSpec((bt, d), lambda t: (t, 0)),
            pl.BlockSpec((bt, d), lambda t: (t, i_block)),
            pl.BlockSpec((bt, d), lambda t: (t, g_block)),
            _resident((nl, d)),
            _resident((1, d)),
            _resident((bt, bt)),
            _resident((SUB * HEAD_DIM, CHUNK)),
        ],
        out_specs=pl.BlockSpec((bt, d), lambda t: (t, 0)),
        out_shape=jax.ShapeDtypeStruct((s, d), BF16),
        scratch_shapes=[pltpu.VMEM((heads, HEAD_DIM, HEAD_DIM), F32)],
        compiler_params=_params("arbitrary"),
        name="hgrn",
    )(proj, f_logit, proj, proj, lb_logits, norm_w, tri, wsel)


def _merge_kernel(og_ref, gb0_ref, gb1_ref, ya_ref, x_ref, wb_ref, wo_ref, o_ref):
    y_b = _dot(og_ref[...], wb_ref[...])
    gate_b = jnp.concatenate([gb0_ref[...], gb1_ref[...]], axis=1).astype(F32)
    merged = ya_ref[...].astype(F32) + jax.nn.sigmoid(gate_b) * y_b
    o_ref[...] = x_ref[...] + _dot(merged.astype(BF16), wo_ref[...])


def _merge(og, proj, ya, x, w_b, w_out, *, bm, gb_block0):
    s, d = x.shape
    c = og.shape[1]
    half = d // 2
    return pl.pallas_call(
        _merge_kernel,
        grid=(s // bm,),
        in_specs=[
            pl.BlockSpec((bm, c), lambda m: (m, 0)),
            pl.BlockSpec((bm, half), lambda m: (m, gb_block0)),
            pl.BlockSpec((bm, half), lambda m: (m, gb_block0 + 1)),
            pl.BlockSpec((bm, d), lambda m: (m, 0)),
            pl.BlockSpec((bm, d), lambda m: (m, 0)),
            _resident((c, d)),
            _resident((d, d)),
        ],
        out_specs=pl.BlockSpec((bm, d), lambda m: (m, 0)),
        out_shape=jax.ShapeDtypeStruct((s, d), F32),
        compiler_params=_params("arbitrary"),
        name="merge_out",
    )(og, proj, proj, ya, x, w_b, w_out)


def _causal_conv(z_ref, row0, cols, w_ref, b_ref):
    ext = z_ref[pl.ds(row0, CARRY_ROWS + ACT_ROWS), cols]
    y = b_ref[:, cols] + w_ref[CONV_WIDTH - 1:CONV_WIDTH, cols] * ext[CARRY_ROWS:, :]
    for back in range(1, CONV_WIDTH):
        tap = CONV_WIDTH - 1 - back
        y = y + w_ref[tap:tap + 1, cols] * pltpu.roll(ext, back, axis=0)[CARRY_ROWS:, :]
    return y


def _ffn_kernel(x_ref, lnw_ref, wug_ref, wuv_ref, cwg_ref, cwv_ref, cbg_ref, cbv_ref, wd_ref,
                o_ref, h_ref, zg_ref, zv_ref, a_ref, cg_ref, cv_ref, *, dff):
    m = pl.program_id(0)
    j = pl.program_id(1)
    bm = h_ref.shape[0]
    tf = wd_ref.shape[0]
    shared = j * tf - jnp.minimum(j * tf, dff - tf)

    @pl.when(j == 0)
    def _():
        x = x_ref[...]
        h_ref[...] = _rms(x, lnw_ref[...]).astype(BF16)
        o_ref[...] = x

    @pl.when(m == 0)
    def _():
        cg_ref[j] = jnp.zeros(cg_ref.shape[1:], F32)
        cv_ref[j] = jnp.zeros(cv_ref.shape[1:], F32)

    h = h_ref[...]
    zg_ref[:CARRY_ROWS, :] = cg_ref[j]
    zv_ref[:CARRY_ROWS, :] = cv_ref[j]
    zg_ref[CARRY_ROWS:, :] = _dot(h, wug_ref[...])
    zv_ref[CARRY_ROWS:, :] = _dot(h, wuv_ref[...])
    cg_ref[j] = zg_ref[bm:, :]
    cv_ref[j] = zv_ref[bm:, :]

    for c in range(tf // LANE):
        cols = slice(c * LANE, (c + 1) * LANE)
        fresh = (lax.broadcasted_iota(jnp.int32, (1, LANE), 1) + c * LANE) >= shared
        for r in range(bm // ACT_ROWS):
            ug = _causal_conv(zg_ref, r * ACT_ROWS, cols, cwg_ref, cbg_ref)
            uv = _causal_conv(zv_ref, r * ACT_ROWS, cols, cwv_ref, cbv_ref)
            a_ref[r * ACT_ROWS:(r + 1) * ACT_ROWS, cols] = jnp.where(
                fresh, ug * jax.nn.sigmoid(ug) * uv, 0.0).astype(BF16)

    o_ref[...] += _dot(a_ref[...], wd_ref[...])


def _ffn(x, ln_w, w_up, conv_w, conv_b, w_down, *, bm, tf):
    s, d = x.shape
    dff = w_down.shape[0]
    nj = pl.cdiv(dff, tf)
    assert dff >= tf and dff % LANE == 0

    def start(j, base=0):
        return (base // LANE + jnp.minimum(j * (tf // LANE), (dff - tf) // LANE)) * LANE

    def window(rows, offset):
        return pl.BlockSpec((pl.Element(rows), pl.Element(tf)), lambda m, j: (0, offset(j)))

    return pl.pallas_call(
        functools.partial(_ffn_kernel, dff=dff),
        grid=(s // bm, nj),
        in_specs=[
            pl.BlockSpec((bm, d), lambda m, j: (m, 0)),
            pl.BlockSpec((1, d), lambda m, j: (0, 0)),
            window(d, start),
            window(d, lambda j: start(j, dff)),
            window(CONV_WIDTH, start),
            window(CONV_WIDTH, lambda j: start(j, dff)),
            window(1, start),
            window(1, lambda j: start(j, dff)),
            pl.BlockSpec((pl.Element(tf), pl.Element(d)), lambda m, j: (start(j), 0)),
        ],
        out_specs=pl.BlockSpec((bm, d), lambda m, j: (m, 0)),
        out_shape=jax.ShapeDtypeStruct((s, d), F32),
        scratch_shapes=[
            pltpu.VMEM((bm, d), BF16),
            pltpu.VMEM((CARRY_ROWS + bm, tf), F32),
            pltpu.VMEM((CARRY_ROWS + bm, tf), F32),
            pltpu.VMEM((bm, tf), BF16),
            pltpu.VMEM((nj, CARRY_ROWS, tf), F32),
            pltpu.VMEM((nj, CARRY_ROWS, tf), F32),
        ],
        compiler_params=_params("arbitrary", "arbitrary"),
        name="ffn",
    )(x, ln_w, w_up, w_up, conv_w, conv_w, conv_b, conv_b, w_down)


def _ple_kernel(x_ref, p_ref, lnw_ref, wg_ref, wp_ref, lnf_ref, o_ref, *, final):
    x = x_ref[...]
    h = _rms(x, lnw_ref[...]).astype(BF16)
    gate = jax.nn.sigmoid(_dot(h, wg_ref[...]))
    emb = _dot(p_ref[...].astype(BF16), wp_ref[...])
    y = x + gate * emb
    o_ref[...] = _rms(y, lnf_ref[...]) if final else y


def _ple(x, p, ln_w, w_gate, w_ple, ln_final, *, bm, final):
    s, d = x.shape
    e = p.shape[1]
    return pl.pallas_call(
        functools.partial(_ple_kernel, final=final),
        grid=(s // bm,),
        in_specs=[
            pl.BlockSpec((bm, d), lambda m: (m, 0)),
            pl.BlockSpec((bm, e), lambda m: (m, 0)),
            _resident((1, d)),
            _resident((d, d)),
            _resident((e, d)),
            _resident((1, d)),
        ],
        out_specs=pl.BlockSpec((bm, d), lambda m: (m, 0)),
        out_shape=jax.ShapeDtypeStruct((s, d), F32),
        compiler_params=_params("arbitrary"),
        name="ple_final",
    )(x, p, ln_w, w_gate, w_ple, ln_final)


def kernel(x, p, ln_mix_w, w_in, pool_mix_w, pool_scale, hgrn_lb_logits, hgrn_norm_w, w_branch_a,
           w_branch_b, w_out, ln_ffn_w, w_up, conv_w, conv_b, w_down, ln_ple_w, w_ple_gate, w_ple,
           ln_final_w):
    batch, seq, d = x.shape
    depth = w_in.shape[0]
    c_pool = w_branch_a.shape[1]
    c_hgrn = w_branch_b.shape[1]
    blk = c_hgrn
    assert c_pool == blk and d == 2 * blk, "column blocks of the combined projection must line up"
    assert seq % 1024 == 0
    tf = 512
    outs = []
    for bi in range(batch):
        xb = x[bi]
        for i in range(depth):
            proj, f_logit = _inproj(xb, ln_mix_w[i][None], w_in[i], bm=1024, bn=blk, f_block=2)
            ya = _pool_branch(proj, pool_mix_w[i].astype(BF16), pool_scale[i][None],
                              w_branch_a[i].astype(BF16), bm=1024, bn=blk, ga_block0=5)
            og = _hgrn(proj, f_logit, hgrn_lb_logits, hgrn_norm_w[i][None], bt=256, layer=i,
                       q_block=1, i_block=3, g_block=4)
            xb = _merge(og, proj, ya, xb, w_branch_b[i].astype(BF16), w_out[i].astype(BF16),
                        bm=512, gb_block0=7)
            xb = _ffn(xb, ln_ffn_w[i][None], w_up[i].astype(BF16), conv_w[i], conv_b[i][None],
                      w_down[i].astype(BF16), bm=1024, tf=512)
            xb = _ple(xb, p[i, bi], ln_ple_w[i][None], w_ple_gate[i].astype(BF16),
                      w_ple[i].astype(BF16), ln_final_w[None], bm=512, final=(i == depth - 1))
        outs.append(xb)
    return jnp.stack(outs, axis=0)
```

```python
import functools

import jax
import jax.numpy as jnp
from jax import lax
from jax.experimental import pallas as pl
from jax.experimental.pallas import tpu as pltpu

F32 = jnp.float32
BF16 = jnp.bfloat16

EPS = 1e-6
POOL_WINDOWS = (2, 4, 8, 16)
POOL_HALO = 16
HEAD_DIM = 128
CHUNK = 64
SUB = 16
NSUB = CHUNK // SUB
CONV_WIDTH = 3
CARRY_ROWS = 8
ACT_ROWS = 64
LANE = 128
LOG2_E = 1.4426950408889634
MILD_DECAY_LOG2 = -100.0
V7X_VMEM_BYTES = 64 * 1024 * 1024
VMEM_LIMIT = V7X_VMEM_BYTES - 8 * 1024 * 1024


def _dot(a, b):
    return jnp.dot(a, b, preferred_element_type=F32)


def _dot_nt(a, b):
    return lax.dot_general(a, b, (((1,), (1,)), ((), ())), preferred_element_type=F32)


def _dot_tn(a, b):
    return lax.dot_general(a, b, (((0,), (0,)), ((), ())), preferred_element_type=F32)


def _rms(x, w):
    return x * lax.rsqrt(jnp.mean(x * x, axis=-1, keepdims=True) + EPS) * w


def _params(*sem):
    return pltpu.CompilerParams(dimension_semantics=sem, vmem_limit_bytes=VMEM_LIMIT)


def _resident(shape):
    return pl.BlockSpec(shape, lambda *_: (0,) * len(shape), pipeline_mode=pl.Buffered(1))


def _inproj_kernel(x_ref, lnw_ref, w_ref, o_ref, h_ref):
    @pl.when(pl.program_id(1) == 0)
    def _():
        h_ref[...] = _rms(x_ref[...], lnw_ref[...]).astype(BF16)

    o_ref[...] = _dot(h_ref[...], w_ref[...].astype(BF16)).astype(o_ref.dtype)


def _inproj(x, ln_w, w_in, *, bm, bn):
    s, d = x.shape
    d_in = w_in.shape[1]
    return pl.pallas_call(
        _inproj_kernel,
        grid=(s // bm, d_in // bn),
        in_specs=[
            pl.BlockSpec((bm, d), lambda m, n: (m, 0)),
            pl.BlockSpec((1, d), lambda m, n: (0, 0)),
            pl.BlockSpec((d, bn), lambda m, n: (0, n)),
        ],
        out_specs=pl.BlockSpec((bm, bn), lambda m, n: (m, n)),
        out_shape=jax.ShapeDtypeStruct((s, d_in), BF16),
        scratch_shapes=[pltpu.VMEM((bm, d), BF16)],
        compiler_params=_params("arbitrary", "arbitrary"),
        name="inproj",
    )(x, ln_w, w_in)


def _pool_kernel(u_ref, halo_ref, ga_ref, mix_ref, scale_ref, wa_ref, o_ref, feat_ref, *, bm):
    m = pl.program_id(0)
    n = pl.program_id(1)

    @pl.when(n == 0)
    def _():
        u = u_ref[...].astype(F32)
        halo = halo_ref[...].astype(F32) * (m > 0).astype(F32)
        ext = jnp.concatenate([halo, u], axis=0)
        gw = u.shape[1] // len(POOL_WINDOWS)
        pos = m * bm + lax.broadcasted_iota(jnp.int32, (bm, 1), 0) + 1
        run = ext
        width = 1
        feats = []
        for g, w in enumerate(POOL_WINDOWS):
            while width < w:
                run = run + pltpu.roll(run, width, axis=0)
                width *= 2
            cnt = jnp.minimum(pos, w).astype(F32)
            win = run[POOL_HALO:, :gw]
            if g + 1 < len(POOL_WINDOWS):
                run = run[:, gw:]
            d = win / cnt - u[:, g * gw:(g + 1) * gw]
            y = _dot(d.astype(BF16), mix_ref[g])
            feats.append(y * scale_ref[:, g * gw:(g + 1) * gw])
        feat_ref[...] = jnp.concatenate(feats, axis=1).astype(BF16)

    y_a = _dot(feat_ref[...], wa_ref[...])
    o_ref[...] = (jax.nn.sigmoid(ga_ref[...].astype(F32)) * y_a).astype(o_ref.dtype)


def _pool_branch(proj, mix_w, scale, w_a, *, bm, bn, ga_block0):
    s = proj.shape[0]
    c = w_a.shape[0]
    d = w_a.shape[1]
    g, gw, _ = mix_w.shape
    halo_per_tile = bm // POOL_HALO
    return pl.pallas_call(
        functools.partial(_pool_kernel, bm=bm),
        grid=(s // bm, d // bn),
        in_specs=[
            pl.BlockSpec((bm, c), lambda m, n: (m, 0)),
            pl.BlockSpec((POOL_HALO, c), lambda m, n: (jnp.maximum(m * halo_per_tile - 1, 0), 0)),
            pl.BlockSpec((bm, bn), lambda m, n: (m, ga_block0 + n)),
            _resident((g, gw, gw)),
            _resident((1, c)),
            pl.BlockSpec((c, bn), lambda m, n: (0, n)),
        ],
        out_specs=pl.BlockSpec((bm, bn), lambda m, n: (m, n)),
        out_shape=jax.ShapeDtypeStruct((s, d), BF16),
        scratch_shapes=[pltpu.VMEM((bm, c), BF16)],
        compiler_params=_params("arbitrary", "arbitrary"),
        name="pool_branch",
    )(proj, proj, proj, mix_w, scale, w_a)


def _split3(x):
    hi = x.astype(BF16)
    r = x - hi.astype(F32)
    mid = r.astype(BF16)
    lo = (r - mid.astype(F32)).astype(BF16)
    return hi, mid, lo


def _gates(fl, lb):
    one_m_lb = 1.0 - lb
    sg = jax.nn.sigmoid(fl)
    log2_f = jnp.log(lb + one_m_lb * sg) * LOG2_E
    return log2_f, one_m_lb * (1.0 - sg)


def _chunk_cumsum(tri, x):
    hi, mid, lo = _split3(x)
    return _dot(tri, hi) + _dot(tri, mid) + _dot(tri, lo)


def _head_norm_gate(o_heads, norm_w, g):
    o = jnp.concatenate(
        [o_h * lax.rsqrt(jnp.mean(o_h * o_h, axis=-1, keepdims=True) + EPS) for o_h in o_heads], axis=1)
    return o * norm_w * (g * jax.nn.sigmoid(g))


def _hgrn_chunk_any_decay(c, q_ref, f_ref, i_ref, g_ref, tri_ref, wsel_ref, o_ref, st_ref, lb, norm_w):
    r0 = pl.multiple_of(c * CHUNK, CHUNK)
    rows = pl.ds(r0, CHUNK)
    d = q_ref.shape[1]
    heads = d // HEAD_DIM

    q = q_ref[rows, :].astype(F32)
    v = i_ref[rows, :]
    log2_f, kk = _gates(f_ref[rows, :].astype(F32), lb)
    b = _chunk_cumsum(tri_ref[:CHUNK, :CHUNK], log2_f)
    b_last = b[CHUNK - 1:CHUNK, :]

    q_in = (q * jnp.exp2(b)).astype(BF16)
    k_st = (kk * jnp.exp2(b_last - b)).astype(BF16)

    b_end = jnp.concatenate(
        [jnp.broadcast_to(b[(j + 1) * SUB - 1:(j + 1) * SUB, :], (SUB, d)) for j in range(NSUB)], axis=0)
    k_hat = kk * jnp.exp2(b_end - b)
    sub_id = lax.broadcasted_iota(jnp.int32, (CHUNK, 1), 0) // SUB
    q_from = []
    for j in range(NSUB - 1):
        lo_row = (j + 1) * SUB
        bj = b[lo_row - 1:lo_row, :]
        part = q[lo_row:, :] * jnp.exp2(b[lo_row:, :] - bj)
        q_from.append(jnp.concatenate([jnp.zeros((lo_row, d), F32), part], axis=0).astype(BF16))
    k_src = [jnp.where(sub_id == j, k_hat, 0.0).astype(BF16) for j in range(NSUB - 1)]

    per_head = [[] for _ in range(heads)]
    for r in range(NSUB):
        qb = q[r * SUB:(r + 1) * SUB, :]
        bb = b[r * SUB:(r + 1) * SUB, :]
        pieces = []
        for s in range(SUB):
            row = r * SUB + s
            e = jnp.exp2(jnp.minimum(bb - b[row:row + 1, :], 0.0))
            pieces.append((qb * kk[row:row + 1, :] * e).astype(BF16))
        for h in range(heads):
            per_head[h].append(jnp.concatenate(
                [p[:, h * HEAD_DIM:(h + 1) * HEAD_DIM] for p in pieces], axis=1))
    a_big = jnp.concatenate([jnp.concatenate(blocks, axis=0) for blocks in per_head], axis=0)
    diag = _dot(a_big, wsel_ref[...])

    t_id = lax.broadcasted_iota(jnp.int32, (CHUNK, CHUNK), 0)
    s_id = lax.broadcasted_iota(jnp.int32, (CHUNK, CHUNK), 1)
    diag_mask = (t_id // SUB == s_id // SUB) & (s_id <= t_id)

    outs = []
    for h in range(heads):
        hs = slice(h * HEAD_DIM, (h + 1) * HEAD_DIM)
        q_cat = jnp.concatenate([qf[:, hs] for qf in q_from], axis=1)
        k_cat = jnp.concatenate([ks[:, hs] for ks in k_src], axis=1)
        scores = _dot_nt(q_cat, k_cat) + jnp.where(diag_mask, diag[h * CHUNK:(h + 1) * CHUNK, :], 0.0)
        v_h = v[:, hs]
        state_t = st_ref[h]
        o_h = _dot_nt(q_in[:, hs], state_t.astype(BF16)) + _dot(scores.astype(BF16), v_h)
        st_ref[h] = state_t * jnp.exp2(b_last[:, hs]) + _dot_tn(v_h, k_st[:, hs])
        outs.append(o_h)
    o_ref[rows, :] = _head_norm_gate(outs, norm_w, g_ref[rows, :].astype(F32)).astype(o_ref.dtype)


def _hgrn_tile_mild_decay(b, kk, q_ref, i_ref, g_ref, o_ref, st_ref, norm_w):
    bt, d = b.shape
    heads = d // HEAD_DIM
    n_chunks = bt // CHUNK
    q = q_ref[...].astype(F32)
    v = i_ref[...]
    chunk_decay = [jnp.exp2(b[(c + 1) * CHUNK - 1:(c + 1) * CHUNK, :]) for c in range(n_chunks)]
    q_in = (q * jnp.exp2(b)).astype(BF16)
    k_undecayed = kk * jnp.exp2(-b)
    k_out = k_undecayed.astype(BF16)
    k_st = (k_undecayed * jnp.concatenate(
        [jnp.broadcast_to(r, (CHUNK, d)) for r in chunk_decay], axis=0)).astype(BF16)
    causal = (lax.broadcasted_iota(jnp.int32, (CHUNK, CHUNK), 1)
              <= lax.broadcasted_iota(jnp.int32, (CHUNK, CHUNK), 0))
    states = [st_ref[h] for h in range(heads)]
    tile_out = []
    for c in range(n_chunks):
        rs = slice(c * CHUNK, (c + 1) * CHUNK)
        carry_decay = chunk_decay[c]
        outs = []
        for h in range(heads):
            hs = slice(h * HEAD_DIM, (h + 1) * HEAD_DIM)
            scores = jnp.where(causal, _dot_nt(q_in[rs, hs], k_out[rs, hs]), 0.0).astype(BF16)
            outs.append(_dot(scores, v[rs, hs]) + _dot_nt(q_in[rs, hs], states[h].astype(BF16)))
            states[h] = states[h] * carry_decay[:, hs] + _dot_tn(v[rs, hs], k_st[rs, hs])
        tile_out.append(jnp.concatenate(outs, axis=1))
    for h in range(heads):
        st_ref[h] = states[h]
    heads_out = jnp.concatenate(tile_out, axis=0)
    o_heads = [heads_out[:, h * HEAD_DIM:(h + 1) * HEAD_DIM] for h in range(heads)]
    o_ref[...] = _head_norm_gate(o_heads, norm_w, g_ref[...].astype(F32)).astype(o_ref.dtype)


def _hgrn_kernel(q_ref, f_ref, i_ref, g_ref, lbl_ref, nw_ref, tri_ref, wsel_ref, o_ref, st_ref, *, layer):
    @pl.when(pl.program_id(0) == 0)
    def _():
        st_ref[...] = jnp.zeros_like(st_ref)

    logits = lbl_ref[...]
    e = jnp.exp(logits - jnp.max(logits, axis=0, keepdims=True))
    lb = jnp.sum(e[:layer + 1, :], axis=0, keepdims=True) / jnp.sum(e, axis=0, keepdims=True)
    norm_w = nw_ref[...]

    log2_f, kk = _gates(f_ref[...].astype(F32), lb)
    b = _chunk_cumsum(tri_ref[...], log2_f)
    mild = jnp.min(b) >= MILD_DECAY_LOG2

    @pl.when(mild)
    def _():
        _hgrn_tile_mild_decay(b, kk, q_ref, i_ref, g_ref, o_ref, st_ref, norm_w)

    @pl.when(jnp.logical_not(mild))
    def _():
        def body(c, carry):
            _hgrn_chunk_any_decay(c, q_ref, f_ref, i_ref, g_ref, tri_ref, wsel_ref, o_ref, st_ref, lb, norm_w)
            return carry

        lax.fori_loop(0, q_ref.shape[0] // CHUNK, body, 0)


def _hgrn(proj, lb_logits, norm_w, *, bt, layer, q_block, f_block, i_block, g_block):
    s = proj.shape[0]
    d = norm_w.shape[1]
    heads = d // HEAD_DIM
    t_id = lax.broadcasted_iota(jnp.int32, (bt, bt), 0)
    s_id = lax.broadcasted_iota(jnp.int32, (bt, bt), 1)
    tri = ((s_id <= t_id) & (s_id // CHUNK == t_id // CHUNK)).astype(BF16)
    src = lax.broadcasted_iota(jnp.int32, (SUB * HEAD_DIM, CHUNK), 0) // HEAD_DIM
    col = lax.broadcasted_iota(jnp.int32, (SUB * HEAD_DIM, CHUNK), 1) % SUB
    wsel = (src == col).astype(BF16)
    nl = lb_logits.shape[0]
    return pl.pallas_call(
        functools.partial(_hgrn_kernel, layer=layer),
        grid=(s // bt,),
        in_specs=[
            pl.BlockSpec((bt, d), lambda t: (t, q_block)),
            pl.BlockSpec((bt, d), lambda t: (t, f_block)),
            pl.BlockSpec((bt, d), lambda t: (t, i_block)),
            pl.BlockSpec((bt, d), lambda t: (t, g_block)),
            _resident((nl, d)),
            _resident((1, d)),
            _resident((bt, bt)),
            _resident((SUB * HEAD_DIM, CHUNK)),
        ],
        out_specs=pl.BlockSpec((bt, d), lambda t: (t, 0)),
        out_shape=jax.ShapeDtypeStruct((s, d), BF16),
        scratch_shapes=[pltpu.VMEM((heads, HEAD_DIM, HEAD_DIM), F32)],
        compiler_params=_params("arbitrary"),
        name="hgrn",
    )(proj, proj, proj, proj, lb_logits, norm_w, tri, wsel)


def _merge_kernel(og_ref, gb0_ref, gb1_ref, ya_ref, x_ref, wb_ref, wo_ref, o_ref):
    y_b = _dot(og_ref[...], wb_ref[...])
    gate_b = jnp.concatenate([gb0_ref[...], gb1_ref[...]], axis=1).astype(F32)
    merged = ya_ref[...].astype(F32) + jax.nn.sigmoid(gate_b) * y_b
    o_ref[...] = x_ref[...] + _dot(merged.astype(BF16), wo_ref[...])


def _merge(og, proj, ya, x, w_b, w_out, *, bm, gb_block0):
    s, d = x.shape
    c = og.shape[1]
    half = d // 2
    return pl.pallas_call(
        _merge_kernel,
        grid=(s // bm,),
        in_specs=[
            pl.BlockSpec((bm, c), lambda m: (m, 0)),
            pl.BlockSpec((bm, half), lambda m: (m, gb_block0)),
            pl.BlockSpec((bm, half), lambda m: (m, gb_block0 + 1)),
            pl.BlockSpec((bm, d), lambda m: (m, 0)),
            pl.BlockSpec((bm, d), lambda m: (m, 0)),
            _resident((c, d)),
            _resident((d, d)),
        ],
        out_specs=pl.BlockSpec((bm, d), lambda m: (m, 0)),
        out_shape=jax.ShapeDtypeStruct((s, d), F32),
        compiler_params=_params("arbitrary"),
        name="merge_out",
    )(og, proj, proj, ya, x, w_b, w_out)


def _causal_conv(z_ref, row0, cols, w_ref, b_ref):
    ext = z_ref[pl.ds(row0, CARRY_ROWS + ACT_ROWS), cols]
    y = b_ref[:, cols] + w_ref[CONV_WIDTH - 1:CONV_WIDTH, cols] * ext[CARRY_ROWS:, :]
    for back in range(1, CONV_WIDTH):
        tap = CONV_WIDTH - 1 - back
        y = y + w_ref[tap:tap + 1, cols] * pltpu.roll(ext, back, axis=0)[CARRY_ROWS:, :]
    return y


def _ffn_kernel(x_ref, lnw_ref, wug_ref, wuv_ref, cwg_ref, cwv_ref, cbg_ref, cbv_ref, wd_ref,
                o_ref, h_ref, zg_ref, zv_ref, a_ref, cg_ref, cv_ref, *, dff):
    m = pl.program_id(0)
    j = pl.program_id(1)
    bm = h_ref.shape[0]
    tf = wd_ref.shape[0]
    shared = j * tf - jnp.minimum(j * tf, dff - tf)

    @pl.when(j == 0)
    def _():
        x = x_ref[...]
        h_ref[...] = _rms(x, lnw_ref[...]).astype(BF16)
        o_ref[...] = x

    @pl.when(m == 0)
    def _():
        cg_ref[j] = jnp.zeros(cg_ref.shape[1:], F32)
        cv_ref[j] = jnp.zeros(cv_ref.shape[1:], F32)

    h = h_ref[...]
    zg_ref[:CARRY_ROWS, :] = cg_ref[j]
    zv_ref[:CARRY_ROWS, :] = cv_ref[j]
    zg_ref[CARRY_ROWS:, :] = _dot(h, wug_ref[...])
    zv_ref[CARRY_ROWS:, :] = _dot(h, wuv_ref[...])
    cg_ref[j] = zg_ref[bm:, :]
    cv_ref[j] = zv_ref[bm:, :]

    for c in range(tf // LANE):
        cols = slice(c * LANE, (c + 1) * LANE)
        fresh = (lax.broadcasted_iota(jnp.int32, (1, LANE), 1) + c * LANE) >= shared
        for r in range(bm // ACT_ROWS):
            ug = _causal_conv(zg_ref, r * ACT_ROWS, cols, cwg_ref, cbg_ref)
            uv = _causal_conv(zv_ref, r * ACT_ROWS, cols, cwv_ref, cbv_ref)
            a_ref[r * ACT_ROWS:(r + 1) * ACT_ROWS, cols] = jnp.where(
                fresh, ug * jax.nn.sigmoid(ug) * uv, 0.0).astype(BF16)

    o_ref[...] += _dot(a_ref[...], wd_ref[...])


def _ffn(x, ln_w, w_up, conv_w, conv_b, w_down, *, bm, tf):
    s, d = x.shape
    dff = w_down.shape[0]
    nj = pl.cdiv(dff, tf)
    assert dff >= tf and dff % LANE == 0

    def start(j, base=0):
        return (base // LANE + jnp.minimum(j * (tf // LANE), (dff - tf) // LANE)) * LANE

    def window(rows, offset):
        return pl.BlockSpec((pl.Element(rows), pl.Element(tf)), lambda m, j: (0, offset(j)))

    return pl.pallas_call(
        functools.partial(_ffn_kernel, dff=dff),
        grid=(s // bm, nj),
        in_specs=[
            pl.BlockSpec((bm, d), lambda m, j: (m, 0)),
            pl.BlockSpec((1, d), lambda m, j: (0, 0)),
            window(d, start),
            window(d, lambda j: start(j, dff)),
            window(CONV_WIDTH, start),
            window(CONV_WIDTH, lambda j: start(j, dff)),
            window(1, start),
            window(1, lambda j: start(j, dff)),
            pl.BlockSpec((pl.Element(tf), pl.Element(d)), lambda m, j: (start(j), 0)),
        ],
        out_specs=pl.BlockSpec((bm, d), lambda m, j: (m, 0)),
        out_shape=jax.ShapeDtypeStruct((s, d), F32),
        scratch_shapes=[
            pltpu.VMEM((bm, d), BF16),
            pltpu.VMEM((CARRY_ROWS + bm, tf), F32),
            pltpu.VMEM((CARRY_ROWS + bm, tf), F32),
            pltpu.VMEM((bm, tf), BF16),
            pltpu.VMEM((nj, CARRY_ROWS, tf), F32),
            pltpu.VMEM((nj, CARRY_ROWS, tf), F32),
        ],
        compiler_params=_params("arbitrary", "arbitrary"),
        name="ffn",
    )(x, ln_w, w_up, w_up, conv_w, conv_w, conv_b, conv_b, w_down)


def _ple_kernel(x_ref, p_ref, lnw_ref, wg_ref, wp_ref, lnf_ref, o_ref, *, final):
    x = x_ref[...]
    h = _rms(x, lnw_ref[...]).astype(BF16)
    gate = jax.nn.sigmoid(_dot(h, wg_ref[...]))
    emb = _dot(p_ref[...].astype(BF16), wp_ref[...])
    y = x + gate * emb
    o_ref[...] = _rms(y, lnf_ref[...]) if final else y


def _ple(x, p, ln_w, w_gate, w_ple, ln_final, *, bm, final):
    s, d = x.shape
    e = p.shape[1]
    return pl.pallas_call(
        functools.partial(_ple_kernel, final=final),
        grid=(s // bm,),
        in_specs=[
            pl.BlockSpec((bm, d), lambda m: (m, 0)),
            pl.BlockSpec((bm, e), lambda m: (m, 0)),
            _resident((1, d)),
            _resident((d, d)),
            _resident((e, d)),
            _resident((1, d)),
        ],
        out_specs=pl.BlockSpec((bm, d), lambda m: (m, 0)),
        out_shape=jax.ShapeDtypeStruct((s, d), F32),
        compiler_params=_params("arbitrary"),
        name="ple_final",
    )(x, p, ln_w, w_gate, w_ple, ln_final)


def kernel(x, p, ln_mix_w, w_in, pool_mix_w, pool_scale, hgrn_lb_logits, hgrn_norm_w, w_branch_a,
           w_branch_b, w_out, ln_ffn_w, w_up, conv_w, conv_b, w_down, ln_ple_w, w_ple_gate, w_ple,
           ln_final_w):
    batch, seq, d = x.shape
    depth = w_in.shape[0]
    c_pool = w_branch_a.shape[1]
    c_hgrn = w_branch_b.shape[1]
    blk = c_hgrn
    assert c_pool == blk and d == 2 * blk, "column blocks of the combined projection must line up"
    assert seq % 1024 == 0
    tf = 512
    outs = []
    for bi in range(batch):
        xb = x[bi]
        for i in range(depth):
            proj = _inproj(xb, ln_mix_w[i][None], w_in[i], bm=1024, bn=blk)
            ya = _pool_branch(proj, pool_mix_w[i].astype(BF16), pool_scale[i][None],
                              w_branch_a[i].astype(BF16), bm=1024, bn=blk, ga_block0=5)
            og = _hgrn(proj, hgrn_lb_logits, hgrn_norm_w[i][None], bt=256, layer=i,
                       q_block=1, f_block=2, i_block=3, g_block=4)
            xb = _merge(og, proj, ya, xb, w_branch_b[i].astype(BF16), w_out[i].astype(BF16),
                        bm=512, gb_block0=7)
            xb = _ffn(xb, ln_ffn_w[i][None], w_up[i].astype(BF16), conv_w[i], conv_b[i][None],
                      w_down[i].astype(BF16), bm=1024, tf=512)
            xb = _ple(xb, p[i, bi], ln_ple_w[i][None], w_ple_gate[i].astype(BF16),
                      w_ple[i].astype(BF16), ln_final_w[None], bm=1024, final=(i == depth - 1))
        outs.append(xb)
    return jnp.stack(outs, axis=0)
```

```python
import functools

import jax
import jax.numpy as jnp
from jax import lax
from jax.experimental import pallas as pl
from jax.experimental.pallas import tpu as pltpu

F32 = jnp.float32
BF16 = jnp.bfloat16

EPS = 1e-6
POOL_WINDOWS = (2, 4, 8, 16)
POOL_HALO = 16
HEAD_DIM = 128
CHUNK = 64
SUB = 16
NSUB = CHUNK // SUB
CONV_WIDTH = 3
CARRY_ROWS = 8
ACT_ROWS = 64
LANE = 128
BF16_TILE_ROWS = 16
LOG2_E = 1.4426950408889634
MILD_DECAY_LOG2 = -100.0
V7X_VMEM_BYTES = 64 * 1024 * 1024
VMEM_LIMIT = V7X_VMEM_BYTES - 8 * 1024 * 1024


def _dot(a, b):
    return jnp.dot(a, b, preferred_element_type=F32)


def _dot_nt(a, b):
    return lax.dot_general(a, b, (((1,), (1,)), ((), ())), preferred_element_type=F32)


def _dot_tn(a, b):
    return lax.dot_general(a, b, (((0,), (0,)), ((), ())), preferred_element_type=F32)


def _rms(x, w):
    return x * lax.rsqrt(jnp.mean(x * x, axis=-1, keepdims=True) + EPS) * w


def _params(*sem):
    return pltpu.CompilerParams(dimension_semantics=sem, vmem_limit_bytes=VMEM_LIMIT)


def _resident(shape):
    return pl.BlockSpec(shape, lambda *_: (0,) * len(shape), pipeline_mode=pl.Buffered(1))


def _inproj_kernel(x_ref, lnw_ref, w_ref, o_ref, h_ref):
    @pl.when(pl.program_id(1) == 0)
    def _():
        h_ref[...] = _rms(x_ref[...], lnw_ref[...]).astype(BF16)

    o_ref[...] = _dot(h_ref[...], w_ref[...].astype(BF16)).astype(o_ref.dtype)


def _inproj(x, ln_w, w_in, *, bm, bn):
    s, d = x.shape
    d_in = w_in.shape[1]
    return pl.pallas_call(
        _inproj_kernel,
        grid=(s // bm, d_in // bn),
        in_specs=[
            pl.BlockSpec((bm, d), lambda m, n: (m, 0)),
            pl.BlockSpec((1, d), lambda m, n: (0, 0)),
            pl.BlockSpec((d, bn), lambda m, n: (0, n)),
        ],
        out_specs=pl.BlockSpec((bm, bn), lambda m, n: (m, n)),
        out_shape=jax.ShapeDtypeStruct((s, d_in), BF16),
        scratch_shapes=[pltpu.VMEM((bm, d), BF16)],
        compiler_params=_params("arbitrary", "arbitrary"),
        name="inproj",
    )(x, ln_w, w_in)


def _pool_kernel(u_ref, halo_ref, ga_ref, mix_ref, scale_ref, wa_ref, o_ref, feat_ref, *, bm):
    m = pl.program_id(0)
    n = pl.program_id(1)

    @pl.when(n == 0)
    def _():
        u = u_ref[...].astype(F32)
        halo = halo_ref[...].astype(F32) * (m > 0).astype(F32)
        ext = jnp.concatenate([halo, u], axis=0)
        gw = u.shape[1] // len(POOL_WINDOWS)
        pos = m * bm + lax.broadcasted_iota(jnp.int32, (bm, 1), 0) + 1
        run = ext
        width = 1
        feats = []
        for g, w in enumerate(POOL_WINDOWS):
            while width < w:
                run = run + pltpu.roll(run, width, axis=0)
                width *= 2
            cnt = jnp.minimum(pos, w).astype(F32)
            win = run[POOL_HALO:, :gw]
            if g + 1 < len(POOL_WINDOWS):
                run = run[:, gw:]
            d = win / cnt - u[:, g * gw:(g + 1) * gw]
            y = _dot(d.astype(BF16), mix_ref[g])
            feats.append(y * scale_ref[:, g * gw:(g + 1) * gw])
        feat_ref[...] = jnp.concatenate(feats, axis=1).astype(BF16)

    y_a = _dot(feat_ref[...], wa_ref[...])
    o_ref[...] = (jax.nn.sigmoid(ga_ref[...].astype(F32)) * y_a).astype(o_ref.dtype)


def _pool_branch(proj, mix_w, scale, w_a, *, bm, bn, ga_block0):
    s = proj.shape[0]
    c = w_a.shape[0]
    d = w_a.shape[1]
    g, gw, _ = mix_w.shape
    halo_per_tile = bm // POOL_HALO
    return pl.pallas_call(
        functools.partial(_pool_kernel, bm=bm),
        grid=(s // bm, d // bn),
        in_specs=[
            pl.BlockSpec((bm, c), lambda m, n: (m, 0)),
            pl.BlockSpec((POOL_HALO, c), lambda m, n: (jnp.maximum(m * halo_per_tile - 1, 0), 0)),
            pl.BlockSpec((bm, bn), lambda m, n: (m, ga_block0 + n)),
            _resident((g, gw, gw)),
            _resident((1, c)),
            pl.BlockSpec((c, bn), lambda m, n: (0, n)),
        ],
        out_specs=pl.BlockSpec((bm, bn), lambda m, n: (m, n)),
        out_shape=jax.ShapeDtypeStruct((s, d), BF16),
        scratch_shapes=[pltpu.VMEM((bm, c), BF16)],
        compiler_params=_params("arbitrary", "arbitrary"),
        name="pool_branch",
    )(proj, proj, proj, mix_w, scale, w_a)


def _split3(x):
    hi = x.astype(BF16)
    r = x - hi.astype(F32)
    mid = r.astype(BF16)
    lo = (r - mid.astype(F32)).astype(BF16)
    return hi, mid, lo


def _gates(fl, lb):
    one_m_lb = 1.0 - lb
    sg = jax.nn.sigmoid(fl)
    log2_f = jnp.log(lb + one_m_lb * sg) * LOG2_E
    return log2_f, one_m_lb * (1.0 - sg)


def _chunk_cumsum(tri, x):
    hi, mid, lo = _split3(x)
    return _dot(tri, hi) + _dot(tri, mid) + _dot(tri, lo)


def _head_norm_gate(o_heads, norm_w, g):
    o = jnp.concatenate(
        [o_h * lax.rsqrt(jnp.mean(o_h * o_h, axis=-1, keepdims=True) + EPS) for o_h in o_heads], axis=1)
    return o * norm_w * (g * jax.nn.sigmoid(g))


def _hgrn_chunk_any_decay(c, q_ref, f_ref, i_ref, g_ref, tri_ref, wsel_ref, o_ref, st_ref, lb, norm_w):
    r0 = pl.multiple_of(c * CHUNK, CHUNK)
    rows = pl.ds(r0, CHUNK)
    d = q_ref.shape[1]
    heads = d // HEAD_DIM

    q = q_ref[rows, :].astype(F32)
    v = i_ref[rows, :]
    log2_f, kk = _gates(f_ref[rows, :].astype(F32), lb)
    b = _chunk_cumsum(tri_ref[:CHUNK, :CHUNK], log2_f)
    b_last = b[CHUNK - 1:CHUNK, :]

    q_in = (q * jnp.exp2(b)).astype(BF16)
    k_st = (kk * jnp.exp2(b_last - b)).astype(BF16)

    b_end = jnp.concatenate(
        [jnp.broadcast_to(b[(j + 1) * SUB - 1:(j + 1) * SUB, :], (SUB, d)) for j in range(NSUB)], axis=0)
    k_hat = kk * jnp.exp2(b_end - b)
    sub_id = lax.broadcasted_iota(jnp.int32, (CHUNK, 1), 0) // SUB
    q_from = []
    for j in range(NSUB - 1):
        lo_row = (j + 1) * SUB
        bj = b[lo_row - 1:lo_row, :]
        part = q[lo_row:, :] * jnp.exp2(b[lo_row:, :] - bj)
        q_from.append(jnp.concatenate([jnp.zeros((lo_row, d), F32), part], axis=0).astype(BF16))
    k_src = [jnp.where(sub_id == j, k_hat, 0.0).astype(BF16) for j in range(NSUB - 1)]

    per_head = [[] for _ in range(heads)]
    for r in range(NSUB):
        qb = q[r * SUB:(r + 1) * SUB, :]
        bb = b[r * SUB:(r + 1) * SUB, :]
        pieces = []
        for s in range(SUB):
            row = r * SUB + s
            e = jnp.exp2(jnp.minimum(bb - b[row:row + 1, :], 0.0))
            pieces.append((qb * kk[row:row + 1, :] * e).astype(BF16))
        for h in range(heads):
            per_head[h].append(jnp.concatenate(
                [p[:, h * HEAD_DIM:(h + 1) * HEAD_DIM] for p in pieces], axis=1))
    a_big = jnp.concatenate([jnp.concatenate(blocks, axis=0) for blocks in per_head], axis=0)
    diag = _dot(a_big, wsel_ref[...])

    t_id = lax.broadcasted_iota(jnp.int32, (CHUNK, CHUNK), 0)
    s_id = lax.broadcasted_iota(jnp.int32, (CHUNK, CHUNK), 1)
    diag_mask = (t_id // SUB == s_id // SUB) & (s_id <= t_id)

    outs = []
    for h in range(heads):
        hs = slice(h * HEAD_DIM, (h + 1) * HEAD_DIM)
        q_cat = jnp.concatenate([qf[:, hs] for qf in q_from], axis=1)
        k_cat = jnp.concatenate([ks[:, hs] for ks in k_src], axis=1)
        scores = _dot_nt(q_cat, k_cat) + jnp.where(diag_mask, diag[h * CHUNK:(h + 1) * CHUNK, :], 0.0)
        v_h = v[:, hs]
        state_t = st_ref[h]
        o_h = _dot_nt(q_in[:, hs], state_t.astype(BF16)) + _dot(scores.astype(BF16), v_h)
        st_ref[h] = state_t * jnp.exp2(b_last[:, hs]) + _dot_tn(v_h, k_st[:, hs])
        outs.append(o_h)
    o_ref[rows, :] = _head_norm_gate(outs, norm_w, g_ref[rows, :].astype(F32)).astype(o_ref.dtype)


def _hgrn_tile_mild_decay(b, kk, q_ref, i_ref, g_ref, o_ref, st_ref, norm_w):
    bt, d = b.shape
    heads = d // HEAD_DIM
    n_chunks = bt // CHUNK
    q = q_ref[...].astype(F32)
    v = i_ref[...]
    chunk_decay = [jnp.exp2(b[(c + 1) * CHUNK - 1:(c + 1) * CHUNK, :]) for c in range(n_chunks)]
    q_in = (q * jnp.exp2(b)).astype(BF16)
    k_undecayed = kk * jnp.exp2(-b)
    k_out = k_undecayed.astype(BF16)
    k_st = (k_undecayed * jnp.concatenate(
        [jnp.broadcast_to(r, (CHUNK, d)) for r in chunk_decay], axis=0)).astype(BF16)
    causal = (lax.broadcasted_iota(jnp.int32, (CHUNK, CHUNK), 1)
              <= lax.broadcasted_iota(jnp.int32, (CHUNK, CHUNK), 0))
    states = [st_ref[h] for h in range(heads)]
    tile_out = []
    for c in range(n_chunks):
        rs = slice(c * CHUNK, (c + 1) * CHUNK)
        carry_decay = chunk_decay[c]
        outs = []
        for h in range(heads):
            hs = slice(h * HEAD_DIM, (h + 1) * HEAD_DIM)
            scores = jnp.where(causal, _dot_nt(q_in[rs, hs], k_out[rs, hs]), 0.0).astype(BF16)
            outs.append(_dot(scores, v[rs, hs]) + _dot_nt(q_in[rs, hs], states[h].astype(BF16)))
            states[h] = states[h] * carry_decay[:, hs] + _dot_tn(v[rs, hs], k_st[rs, hs])
        tile_out.append(jnp.concatenate(outs, axis=1))
    for h in range(heads):
        st_ref[h] = states[h]
    heads_out = jnp.concatenate(tile_out, axis=0)
    o_heads = [heads_out[:, h * HEAD_DIM:(h + 1) * HEAD_DIM] for h in range(heads)]
    o_ref[...] = _head_norm_gate(o_heads, norm_w, g_ref[...].astype(F32)).astype(o_ref.dtype)


def _hgrn_kernel(q_ref, f_ref, i_ref, g_ref, lbl_ref, nw_ref, tri_ref, wsel_ref, *rest, layer,
                 cast_blocks):
    n_cast = len(cast_blocks)
    cast_in, (o_ref, *cast_out, st_ref) = rest[:n_cast], rest[n_cast:]
    step = pl.program_id(0)

    @pl.when(step == 0)
    def _():
        st_ref[...] = jnp.zeros_like(st_ref)

    for src, dst, n_blocks in zip(cast_in, cast_out, cast_blocks):
        @pl.when(step < n_blocks)
        def _():
            dst[...] = src[...].astype(dst.dtype)

    logits = lbl_ref[...]
    e = jnp.exp(logits - jnp.max(logits, axis=0, keepdims=True))
    lb = jnp.sum(e[:layer + 1, :], axis=0, keepdims=True) / jnp.sum(e, axis=0, keepdims=True)
    norm_w = nw_ref[...]

    log2_f, kk = _gates(f_ref[...].astype(F32), lb)
    b = _chunk_cumsum(tri_ref[...], log2_f)
    mild = jnp.min(b) >= MILD_DECAY_LOG2

    @pl.when(mild)
    def _():
        _hgrn_tile_mild_decay(b, kk, q_ref, i_ref, g_ref, o_ref, st_ref, norm_w)

    @pl.when(jnp.logical_not(mild))
    def _():
        def body(c, carry):
            _hgrn_chunk_any_decay(c, q_ref, f_ref, i_ref, g_ref, tri_ref, wsel_ref, o_ref, st_ref, lb, norm_w)
            return carry

        lax.fori_loop(0, q_ref.shape[0] // CHUNK, body, 0)


def _split_over_steps(shape, steps):
    rows, cols = shape
    if rows % (steps * BF16_TILE_ROWS) == 0:
        return pl.BlockSpec((rows // steps, cols), lambda t: (t, 0)), steps
    n = max(k for k in range(1, steps + 1) if (cols // LANE) % k == 0)
    assert cols % LANE == 0
    return pl.BlockSpec((rows, cols // n), lambda t: (0, jnp.minimum(t, n - 1))), n


def _hgrn(proj, lb_logits, norm_w, to_bf16, *, bt, layer, q_block, f_block, i_block, g_block):
    s = proj.shape[0]
    d = norm_w.shape[1]
    heads = d // HEAD_DIM
    steps = s // bt
    cast_specs, cast_blocks = zip(*[_split_over_steps(a.shape, steps) for a in to_bf16])
    t_id = lax.broadcasted_iota(jnp.int32, (bt, bt), 0)
    s_id = lax.broadcasted_iota(jnp.int32, (bt, bt), 1)
    tri = ((s_id <= t_id) & (s_id // CHUNK == t_id // CHUNK)).astype(BF16)
    src = lax.broadcasted_iota(jnp.int32, (SUB * HEAD_DIM, CHUNK), 0) // HEAD_DIM
    col = lax.broadcasted_iota(jnp.int32, (SUB * HEAD_DIM, CHUNK), 1) % SUB
    wsel = (src == col).astype(BF16)
    nl = lb_logits.shape[0]
    og, *as_bf16 = pl.pallas_call(
        functools.partial(_hgrn_kernel, layer=layer, cast_blocks=cast_blocks),
        grid=(steps,),
        in_specs=[
            pl.BlockSpec((bt, d), lambda t: (t, q_block)),
            pl.BlockSpec((bt, d), lambda t: (t, f_block)),
            pl.BlockSpec((bt, d), lambda t: (t, i_block)),
            pl.BlockSpec((bt, d), lambda t: (t, g_block)),
            _resident((nl, d)),
            _resident((1, d)),
            _resident((bt, bt)),
            _resident((SUB * HEAD_DIM, CHUNK)),
            *cast_specs,
        ],
        out_specs=[pl.BlockSpec((bt, d), lambda t: (t, 0)), *cast_specs],
        out_shape=[jax.ShapeDtypeStruct((s, d), BF16),
                   *[jax.ShapeDtypeStruct(a.shape, BF16) for a in to_bf16]],
        scratch_shapes=[pltpu.VMEM((heads, HEAD_DIM, HEAD_DIM), F32)],
        compiler_params=_params("arbitrary"),
        name="hgrn",
    )(proj, proj, proj, proj, lb_logits, norm_w, tri, wsel, *to_bf16)
    return og, as_bf16


def _merge_kernel(og_ref, gb0_ref, gb1_ref, ya_ref, x_ref, wb_ref, wo_ref, o_ref):
    y_b = _dot(og_ref[...], wb_ref[...])
    gate_b = jnp.concatenate([gb0_ref[...], gb1_ref[...]], axis=1).astype(F32)
    merged = ya_ref[...].astype(F32) + jax.nn.sigmoid(gate_b) * y_b
    o_ref[...] = x_ref[...] + _dot(merged.astype(BF16), wo_ref[...])


def _merge(og, proj, ya, x, w_b, w_out, *, bm, gb_block0):
    s, d = x.shape
    c = og.shape[1]
    half = d // 2
    return pl.pallas_call(
        _merge_kernel,
        grid=(s // bm,),
        in_specs=[
            pl.BlockSpec((bm, c), lambda m: (m, 0)),
            pl.BlockSpec((bm, half), lambda m: (m, gb_block0)),
            pl.BlockSpec((bm, half), lambda m: (m, gb_block0 + 1)),
            pl.BlockSpec((bm, d), lambda m: (m, 0)),
            pl.BlockSpec((bm, d), lambda m: (m, 0)),
            _resident((c, d)),
            _resident((d, d)),
        ],
        out_specs=pl.BlockSpec((bm, d), lambda m: (m, 0)),
        out_shape=jax.ShapeDtypeStruct((s, d), F32),
        compiler_params=_params("arbitrary"),
        name="merge_out",
    )(og, proj, proj, ya, x, w_b, w_out)


def _causal_conv(z_ref, row0, cols, w_ref, b_ref):
    ext = z_ref[pl.ds(row0, CARRY_ROWS + ACT_ROWS), cols]
    y = b_ref[:, cols] + w_ref[CONV_WIDTH - 1:CONV_WIDTH, cols] * ext[CARRY_ROWS:, :]
    for back in range(1, CONV_WIDTH):
        tap = CONV_WIDTH - 1 - back
        y = y + w_ref[tap:tap + 1, cols] * pltpu.roll(ext, back, axis=0)[CARRY_ROWS:, :]
    return y


def _ffn_kernel(x_ref, lnw_ref, wug_ref, wuv_ref, cwg_ref, cwv_ref, cbg_ref, cbv_ref, wd_ref,
                o_ref, h_ref, zg_ref, zv_ref, a_ref, cg_ref, cv_ref, *, dff):
    m = pl.program_id(0)
    j = pl.program_id(1)
    bm = h_ref.shape[0]
    tf = wd_ref.shape[0]
    shared = j * tf - jnp.minimum(j * tf, dff - tf)

    @pl.when(j == 0)
    def _():
        x = x_ref[...]
        h_ref[...] = _rms(x, lnw_ref[...]).astype(BF16)
        o_ref[...] = x

    @pl.when(m == 0)
    def _():
        cg_ref[j] = jnp.zeros(cg_ref.shape[1:], F32)
        cv_ref[j] = jnp.zeros(cv_ref.shape[1:], F32)

    h = h_ref[...]
    zg_ref[:CARRY_ROWS, :] = cg_ref[j]
    zv_ref[:CARRY_ROWS, :] = cv_ref[j]
    zg_ref[CARRY_ROWS:, :] = _dot(h, wug_ref[...])
    zv_ref[CARRY_ROWS:, :] = _dot(h, wuv_ref[...])
    cg_ref[j] = zg_ref[bm:, :]
    cv_ref[j] = zv_ref[bm:, :]

    for c in range(tf // LANE):
        cols = slice(c * LANE, (c + 1) * LANE)
        fresh = (lax.broadcasted_iota(jnp.int32, (1, LANE), 1) + c * LANE) >= shared
        for r in range(bm // ACT_ROWS):
            ug = _causal_conv(zg_ref, r * ACT_ROWS, cols, cwg_ref, cbg_ref)
            uv = _causal_conv(zv_ref, r * ACT_ROWS, cols, cwv_ref, cbv_ref)
            a_ref[r * ACT_ROWS:(r + 1) * ACT_ROWS, cols] = jnp.where(
                fresh, ug * jax.nn.sigmoid(ug) * uv, 0.0).astype(BF16)

    o_ref[...] += _dot(a_ref[...], wd_ref[...])


def _ffn(x, ln_w, w_up, conv_w, conv_b, w_down, *, bm, tf):
    s, d = x.shape
    dff = w_down.shape[0]
    nj = pl.cdiv(dff, tf)
    assert dff >= tf and dff % LANE == 0

    def start(j, base=0):
        return (base // LANE + jnp.minimum(j * (tf // LANE), (dff - tf) // LANE)) * LANE

    def window(rows, offset):
        return pl.BlockSpec((pl.Element(rows), pl.Element(tf)), lambda m, j: (0, offset(j)))

    return pl.pallas_call(
        functools.partial(_ffn_kernel, dff=dff),
        grid=(s // bm, nj),
        in_specs=[
            pl.BlockSpec((bm, d), lambda m, j: (m, 0)),
            pl.BlockSpec((1, d), lambda m, j: (0, 0)),
            window(d, start),
            window(d, lambda j: start(j, dff)),
            window(CONV_WIDTH, start),
            window(CONV_WIDTH, lambda j: start(j, dff)),
            window(1, start),
            window(1, lambda j: start(j, dff)),
            pl.BlockSpec((pl.Element(tf), pl.Element(d)), lambda m, j: (start(j), 0)),
        ],
        out_specs=pl.BlockSpec((bm, d), lambda m, j: (m, 0)),
        out_shape=jax.ShapeDtypeStruct((s, d), F32),
        scratch_shapes=[
            pltpu.VMEM((bm, d), BF16),
            pltpu.VMEM((CARRY_ROWS + bm, tf), F32),
            pltpu.VMEM((CARRY_ROWS + bm, tf), F32),
            pltpu.VMEM((bm, tf), BF16),
            pltpu.VMEM((nj, CARRY_ROWS, tf), F32),
            pltpu.VMEM((nj, CARRY_ROWS, tf), F32),
        ],
        compiler_params=_params("arbitrary", "arbitrary"),
        name="ffn",
    )(x, ln_w, w_up, w_up, conv_w, conv_w, conv_b, conv_b, w_down)


def _ple_kernel(x_ref, p_ref, lnw_ref, wg_ref, wp_ref, lnf_ref, o_ref, *, final):
    x = x_ref[...]
    h = _rms(x, lnw_ref[...]).astype(BF16)
    gate = jax.nn.sigmoid(_dot(h, wg_ref[...]))
    emb = _dot(p_ref[...].astype(BF16), wp_ref[...])
    y = x + gate * emb
    o_ref[...] = _rms(y, lnf_ref[...]) if final else y


def _ple(x, p, ln_w, w_gate, w_ple, ln_final, *, bm, final):
    s, d = x.shape
    e = p.shape[1]
    return pl.pallas_call(
        functools.partial(_ple_kernel, final=final),
        grid=(s // bm,),
        in_specs=[
            pl.BlockSpec((bm, d), lambda m: (m, 0)),
            pl.BlockSpec((bm, e), lambda m: (m, 0)),
            _resident((1, d)),
            _resident((d, d)),
            _resident((e, d)),
            _resident((1, d)),
        ],
        out_specs=pl.BlockSpec((bm, d), lambda m: (m, 0)),
        out_shape=jax.ShapeDtypeStruct((s, d), F32),
        compiler_params=_params("arbitrary"),
        name="ple_final",
    )(x, p, ln_w, w_gate, w_ple, ln_final)


def kernel(x, p, ln_mix_w, w_in, pool_mix_w, pool_scale, hgrn_lb_logits, hgrn_norm_w, w_branch_a,
           w_branch_b, w_out, ln_ffn_w, w_up, conv_w, conv_b, w_down, ln_ple_w, w_ple_gate, w_ple,
           ln_final_w):
    batch, seq, d = x.shape
    depth = w_in.shape[0]
    c_pool = w_branch_a.shape[1]
    c_hgrn = w_branch_b.shape[1]
    blk = c_hgrn
    assert c_pool == blk and d == 2 * blk, "column blocks of the combined projection must line up"
    assert seq % 1024 == 0
    tf = 512
    outs = []
    for bi in range(batch):
        xb = x[bi]
        for i in range(depth):
            proj = _inproj(xb, ln_mix_w[i][None], w_in[i], bm=1024, bn=blk)
            ya = _pool_branch(proj, pool_mix_w[i].astype(BF16), pool_scale[i][None],
                              w_branch_a[i].astype(BF16), bm=1024, bn=blk, ga_block0=5)
            og, (w_up_bf16, w_down_bf16) = _hgrn(
                proj, hgrn_lb_logits, hgrn_norm_w[i][None], [w_up[i], w_down[i]], bt=256, layer=i,
                q_block=1, f_block=2, i_block=3, g_block=4)
            xb = _merge(og, proj, ya, xb, w_branch_b[i].astype(BF16), w_out[i].astype(BF16),
                        bm=512, gb_block0=7)
            xb = _ffn(xb, ln_ffn_w[i][None], w_up_bf16, conv_w[i], conv_b[i][None], w_down_bf16,
                      bm=1024, tf=512)
            xb = _ple(xb, p[i, bi], ln_ple_w[i][None], w_ple_gate[i].astype(BF16),
                      w_ple[i].astype(BF16), ln_final_w[None], bm=1024, final=(i == depth - 1))
        outs.append(xb)
    return jnp.stack(outs, axis=0)
```

```python
import functools

import jax
import jax.numpy as jnp
from jax import lax
from jax.experimental import pallas as pl
from jax.experimental.pallas import tpu as pltpu

F32 = jnp.float32
BF16 = jnp.bfloat16

EPS = 1e-6
POOL_WINDOWS = (2, 4, 8, 16)
POOL_HALO = 16
HEAD_DIM = 128
CHUNK = 64
SUB = 16
NSUB = CHUNK // SUB
CONV_WIDTH = 3
CARRY_ROWS = 8
ACT_ROWS = 64
LANE = 128
BF16_TILE_ROWS = 16
LOG2_E = 1.4426950408889634
MILD_DECAY_LOG2 = -100.0
V7X_VMEM_BYTES = 64 * 1024 * 1024
VMEM_LIMIT = V7X_VMEM_BYTES - 8 * 1024 * 1024


def _dot(a, b):
    return jnp.dot(a, b, preferred_element_type=F32)


def _dot_nt(a, b):
    return lax.dot_general(a, b, (((1,), (1,)), ((), ())), preferred_element_type=F32)


def _dot_tn(a, b):
    return lax.dot_general(a, b, (((0,), (0,)), ((), ())), preferred_element_type=F32)


def _rms(x, w):
    return x * lax.rsqrt(jnp.mean(x * x, axis=-1, keepdims=True) + EPS) * w


def _params(*sem):
    return pltpu.CompilerParams(dimension_semantics=sem, vmem_limit_bytes=VMEM_LIMIT)


def _resident(shape):
    return pl.BlockSpec(shape, lambda *_: (0,) * len(shape), pipeline_mode=pl.Buffered(1))


def _inproj_kernel(x_ref, lnw_ref, w_ref, o_ref, h_ref):
    @pl.when(pl.program_id(1) == 0)
    def _():
        h_ref[...] = _rms(x_ref[...], lnw_ref[...]).astype(BF16)

    o_ref[...] = _dot(h_ref[...], w_ref[...].astype(BF16)).astype(o_ref.dtype)


def _inproj(x, ln_w, w_in, *, bm, bn):
    s, d = x.shape
    d_in = w_in.shape[1]
    return pl.pallas_call(
        _inproj_kernel,
        grid=(s // bm, d_in // bn),
        in_specs=[
            pl.BlockSpec((bm, d), lambda m, n: (m, 0)),
            pl.BlockSpec((1, d), lambda m, n: (0, 0)),
            pl.BlockSpec((d, bn), lambda m, n: (0, n)),
        ],
        out_specs=pl.BlockSpec((bm, bn), lambda m, n: (m, n)),
        out_shape=jax.ShapeDtypeStruct((s, d_in), BF16),
        scratch_shapes=[pltpu.VMEM((bm, d), BF16)],
        compiler_params=_params("arbitrary", "arbitrary"),
        name="inproj",
    )(x, ln_w, w_in)


def _pool_kernel(u_ref, halo_ref, ga_ref, mix_ref, scale_ref, wa_ref, o_ref, feat_ref, *, bm):
    m = pl.program_id(0)
    n = pl.program_id(1)

    @pl.when(n == 0)
    def _():
        u = u_ref[...].astype(F32)
        halo = halo_ref[...].astype(F32) * (m > 0).astype(F32)
        ext = jnp.concatenate([halo, u], axis=0)
        gw = u.shape[1] // len(POOL_WINDOWS)
        pos = m * bm + lax.broadcasted_iota(jnp.int32, (bm, 1), 0) + 1
        run = ext
        width = 1
        feats = []
        for g, w in enumerate(POOL_WINDOWS):
            while width < w:
                run = run + pltpu.roll(run, width, axis=0)
                width *= 2
            cnt = jnp.minimum(pos, w).astype(F32)
            win = run[POOL_HALO:, :gw]
            if g + 1 < len(POOL_WINDOWS):
                run = run[:, gw:]
            d = win / cnt - u[:, g * gw:(g + 1) * gw]
            y = _dot(d.astype(BF16), mix_ref[g])
            feats.append(y * scale_ref[:, g * gw:(g + 1) * gw])
        feat_ref[...] = jnp.concatenate(feats, axis=1).astype(BF16)

    y_a = _dot(feat_ref[...], wa_ref[...])
    o_ref[...] = (jax.nn.sigmoid(ga_ref[...].astype(F32)) * y_a).astype(o_ref.dtype)


def _pool_branch(proj, mix_w, scale, w_a, *, bm, bn, ga_block0):
    s = proj.shape[0]
    c = w_a.shape[0]
    d = w_a.shape[1]
    g, gw, _ = mix_w.shape
    halo_per_tile = bm // POOL_HALO
    return pl.pallas_call(
        functools.partial(_pool_kernel, bm=bm),
        grid=(s // bm, d // bn),
        in_specs=[
            pl.BlockSpec((bm, c), lambda m, n: (m, 0)),
            pl.BlockSpec((POOL_HALO, c), lambda m, n: (jnp.maximum(m * halo_per_tile - 1, 0), 0)),
            pl.BlockSpec((bm, bn), lambda m, n: (m, ga_block0 + n)),
            _resident((g, gw, gw)),
            _resident((1, c)),
            pl.BlockSpec((c, bn), lambda m, n: (0, n)),
        ],
        out_specs=pl.BlockSpec((bm, bn), lambda m, n: (m, n)),
        out_shape=jax.ShapeDtypeStruct((s, d), BF16),
        scratch_shapes=[pltpu.VMEM((bm, c), BF16)],
        compiler_params=_params("arbitrary", "arbitrary"),
        name="pool_branch",
    )(proj, proj, proj, mix_w, scale, w_a)


def _split3(x):
    hi = x.astype(BF16)
    r = x - hi.astype(F32)
    mid = r.astype(BF16)
    lo = (r - mid.astype(F32)).astype(BF16)
    return hi, mid, lo


def _gates(fl, lb):
    one_m_lb = 1.0 - lb
    sg = jax.nn.sigmoid(fl)
    log2_f = jnp.log(lb + one_m_lb * sg) * LOG2_E
    return log2_f, one_m_lb * (1.0 - sg)


def _chunk_cumsum(tri, x):
    hi, mid, lo = _split3(x)
    return _dot(tri, hi) + _dot(tri, mid) + _dot(tri, lo)


def _head_norm_gate(o_heads, norm_w, g):
    o = jnp.concatenate(
        [o_h * lax.rsqrt(jnp.mean(o_h * o_h, axis=-1, keepdims=True) + EPS) for o_h in o_heads], axis=1)
    return o * norm_w * (g * jax.nn.sigmoid(g))


def _hgrn_chunk_any_decay(c, q_ref, f_ref, i_ref, g_ref, tri_ref, wsel_ref, o_ref, st_ref, lb, norm_w):
    r0 = pl.multiple_of(c * CHUNK, CHUNK)
    rows = pl.ds(r0, CHUNK)
    d = q_ref.shape[1]
    heads = d // HEAD_DIM

    q = q_ref[rows, :].astype(F32)
    v = i_ref[rows, :]
    log2_f, kk = _gates(f_ref[rows, :].astype(F32), lb)
    b = _chunk_cumsum(tri_ref[:CHUNK, :CHUNK], log2_f)
    b_last = b[CHUNK - 1:CHUNK, :]

    q_in = (q * jnp.exp2(b)).astype(BF16)
    k_st = (kk * jnp.exp2(b_last - b)).astype(BF16)

    b_end = jnp.concatenate(
        [jnp.broadcast_to(b[(j + 1) * SUB - 1:(j + 1) * SUB, :], (SUB, d)) for j in range(NSUB)], axis=0)
    k_hat = kk * jnp.exp2(b_end - b)
    sub_id = lax.broadcasted_iota(jnp.int32, (CHUNK, 1), 0) // SUB
    q_from = []
    for j in range(NSUB - 1):
        lo_row = (j + 1) * SUB
        bj = b[lo_row - 1:lo_row, :]
        part = q[lo_row:, :] * jnp.exp2(b[lo_row:, :] - bj)
        q_from.append(jnp.concatenate([jnp.zeros((lo_row, d), F32), part], axis=0).astype(BF16))
    k_src = [jnp.where(sub_id == j, k_hat, 0.0).astype(BF16) for j in range(NSUB - 1)]

    per_head = [[] for _ in range(heads)]
    for r in range(NSUB):
        qb = q[r * SUB:(r + 1) * SUB, :]
        bb = b[r * SUB:(r + 1) * SUB, :]
        pieces = []
        for s in range(SUB):
            row = r * SUB + s
            e = jnp.exp2(jnp.minimum(bb - b[row:row + 1, :], 0.0))
            pieces.append((qb * kk[row:row + 1, :] * e).astype(BF16))
        for h in range(heads):
            per_head[h].append(jnp.concatenate(
                [p[:, h * HEAD_DIM:(h + 1) * HEAD_DIM] for p in pieces], axis=1))
    a_big = jnp.concatenate([jnp.concatenate(blocks, axis=0) for blocks in per_head], axis=0)
    diag = _dot(a_big, wsel_ref[...])

    t_id = lax.broadcasted_iota(jnp.int32, (CHUNK, CHUNK), 0)
    s_id = lax.broadcasted_iota(jnp.int32, (CHUNK, CHUNK), 1)
    diag_mask = (t_id // SUB == s_id // SUB) & (s_id <= t_id)

    outs = []
    for h in range(heads):
        hs = slice(h * HEAD_DIM, (h + 1) * HEAD_DIM)
        q_cat = jnp.concatenate([qf[:, hs] for qf in q_from], axis=1)
        k_cat = jnp.concatenate([ks[:, hs] for ks in k_src], axis=1)
        scores = _dot_nt(q_cat, k_cat) + jnp.where(diag_mask, diag[h * CHUNK:(h + 1) * CHUNK, :], 0.0)
        v_h = v[:, hs]
        state_t = st_ref[h]
        o_h = _dot_nt(q_in[:, hs], state_t.astype(BF16)) + _dot(scores.astype(BF16), v_h)
        st_ref[h] = state_t * jnp.exp2(b_last[:, hs]) + _dot_tn(v_h, k_st[:, hs])
        outs.append(o_h)
    o_ref[rows, :] = _head_norm_gate(outs, norm_w, g_ref[rows, :].astype(F32)).astype(o_ref.dtype)


def _hgrn_tile_mild_decay(b, kk, q_ref, i_ref, g_ref, o_ref, st_ref, norm_w):
    bt, d = b.shape
    heads = d // HEAD_DIM
    n_chunks = bt // CHUNK
    q = q_ref[...].astype(F32)
    v = i_ref[...]
    chunk_decay = [jnp.exp2(b[(c + 1) * CHUNK - 1:(c + 1) * CHUNK, :]) for c in range(n_chunks)]
    q_in = (q * jnp.exp2(b)).astype(BF16)
    k_undecayed = kk * jnp.exp2(-b)
    k_out = k_undecayed.astype(BF16)
    k_st = (k_undecayed * jnp.concatenate(
        [jnp.broadcast_to(r, (CHUNK, d)) for r in chunk_decay], axis=0)).astype(BF16)
    causal = (lax.broadcasted_iota(jnp.int32, (CHUNK, CHUNK), 1)
              <= lax.broadcasted_iota(jnp.int32, (CHUNK, CHUNK), 0))
    states = [st_ref[h] for h in range(heads)]
    tile_out = []
    for c in range(n_chunks):
        rs = slice(c * CHUNK, (c + 1) * CHUNK)
        carry_decay = chunk_decay[c]
        outs = []
        for h in range(heads):
            hs = slice(h * HEAD_DIM, (h + 1) * HEAD_DIM)
            scores = jnp.where(causal, _dot_nt(q_in[rs, hs], k_out[rs, hs]), 0.0).astype(BF16)
            outs.append(_dot(scores, v[rs, hs]) + _dot_nt(q_in[rs, hs], states[h].astype(BF16)))
            states[h] = states[h] * carry_decay[:, hs] + _dot_tn(v[rs, hs], k_st[rs, hs])
        tile_out.append(jnp.concatenate(outs, axis=1))
    for h in range(heads):
        st_ref[h] = states[h]
    heads_out = jnp.concatenate(tile_out, axis=0)
    o_heads = [heads_out[:, h * HEAD_DIM:(h + 1) * HEAD_DIM] for h in range(heads)]
    o_ref[...] = _head_norm_gate(o_heads, norm_w, g_ref[...].astype(F32)).astype(o_ref.dtype)


def _hgrn_kernel(q_ref, f_ref, i_ref, g_ref, lbl_ref, nw_ref, tri_ref, wsel_ref, *rest, layer,
                 cast_blocks):
    n_cast = len(cast_blocks)
    cast_in, (o_ref, *cast_out, st_ref) = rest[:n_cast], rest[n_cast:]
    step = pl.program_id(0)

    @pl.when(step == 0)
    def _():
        st_ref[...] = jnp.zeros_like(st_ref)

    for src, dst, n_blocks in zip(cast_in, cast_out, cast_blocks):
        @pl.when(step < n_blocks)
        def _():
            dst[...] = src[...].astype(dst.dtype)

    logits = lbl_ref[...]
    e = jnp.exp(logits - jnp.max(logits, axis=0, keepdims=True))
    lb = jnp.sum(e[:layer + 1, :], axis=0, keepdims=True) / jnp.sum(e, axis=0, keepdims=True)
    norm_w = nw_ref[...]

    log2_f, kk = _gates(f_ref[...].astype(F32), lb)
    b = _chunk_cumsum(tri_ref[...], log2_f)
    mild = jnp.min(b) >= MILD_DECAY_LOG2

    @pl.when(mild)
    def _():
        _hgrn_tile_mild_decay(b, kk, q_ref, i_ref, g_ref, o_ref, st_ref, norm_w)

    @pl.when(jnp.logical_not(mild))
    def _():
        def body(c, carry):
            _hgrn_chunk_any_decay(c, q_ref, f_ref, i_ref, g_ref, tri_ref, wsel_ref, o_ref, st_ref, lb, norm_w)
            return carry

        lax.fori_loop(0, q_ref.shape[0] // CHUNK, body, 0)


def _split_over_steps(shape, steps):
    rows, cols = shape
    if rows % (steps * BF16_TILE_ROWS) == 0:
        return pl.BlockSpec((rows // steps, cols), lambda t: (t, 0)), steps
    n = max(k for k in range(1, steps + 1) if (cols // LANE) % k == 0)
    assert cols % LANE == 0
    return pl.BlockSpec((rows, cols // n), lambda t: (0, jnp.minimum(t, n - 1))), n


def _hgrn(proj, lb_logits, norm_w, to_bf16, *, bt, layer, q_block, f_block, i_block, g_block):
    s = proj.shape[0]
    d = norm_w.shape[1]
    heads = d // HEAD_DIM
    steps = s // bt
    cast_specs, cast_blocks = zip(*[_split_over_steps(a.shape, steps) for a in to_bf16])
    t_id = lax.broadcasted_iota(jnp.int32, (bt, bt), 0)
    s_id = lax.broadcasted_iota(jnp.int32, (bt, bt), 1)
    tri = ((s_id <= t_id) & (s_id // CHUNK == t_id // CHUNK)).astype(BF16)
    src = lax.broadcasted_iota(jnp.int32, (SUB * HEAD_DIM, CHUNK), 0) // HEAD_DIM
    col = lax.broadcasted_iota(jnp.int32, (SUB * HEAD_DIM, CHUNK), 1) % SUB
    wsel = (src == col).astype(BF16)
    nl = lb_logits.shape[0]
    og, *as_bf16 = pl.pallas_call(
        functools.partial(_hgrn_kernel, layer=layer, cast_blocks=cast_blocks),
        grid=(steps,),
        in_specs=[
            pl.BlockSpec((bt, d), lambda t: (t, q_block)),
            pl.BlockSpec((bt, d), lambda t: (t, f_block)),
            pl.BlockSpec((bt, d), lambda t: (t, i_block)),
            pl.BlockSpec((bt, d), lambda t: (t, g_block)),
            _resident((nl, d)),
            _resident((1, d)),
            _resident((bt, bt)),
            _resident((SUB * HEAD_DIM, CHUNK)),
            *cast_specs,
        ],
        out_specs=[pl.BlockSpec((bt, d), lambda t: (t, 0)), *cast_specs],
        out_shape=[jax.ShapeDtypeStruct((s, d), BF16),
                   *[jax.ShapeDtypeStruct(a.shape, BF16) for a in to_bf16]],
        scratch_shapes=[pltpu.VMEM((heads, HEAD_DIM, HEAD_DIM), F32)],
        compiler_params=_params("arbitrary"),
        name="hgrn",
    )(proj, proj, proj, proj, lb_logits, norm_w, tri, wsel, *to_bf16)
    return og, as_bf16


def _merge_kernel(og_ref, gb0_ref, gb1_ref, ya_ref, x_ref, wb_ref, wo_ref, o_ref):
    y_b = _dot(og_ref[...], wb_ref[...])
    gate_b = jnp.concatenate([gb0_ref[...], gb1_ref[...]], axis=1).astype(F32)
    merged = ya_ref[...].astype(F32) + jax.nn.sigmoid(gate_b) * y_b
    o_ref[...] = x_ref[...] + _dot(merged.astype(BF16), wo_ref[...])


def _merge(og, proj, ya, x, w_b, w_out, *, bm, gb_block0):
    s, d = x.shape
    c = og.shape[1]
    half = d // 2
    return pl.pallas_call(
        _merge_kernel,
        grid=(s // bm,),
        in_specs=[
            pl.BlockSpec((bm, c), lambda m: (m, 0)),
            pl.BlockSpec((bm, half), lambda m: (m, gb_block0)),
            pl.BlockSpec((bm, half), lambda m: (m, gb_block0 + 1)),
            pl.BlockSpec((bm, d), lambda m: (m, 0)),
            pl.BlockSpec((bm, d), lambda m: (m, 0)),
            _resident((c, d)),
            _resident((d, d)),
        ],
        out_specs=pl.BlockSpec((bm, d), lambda m: (m, 0)),
        out_shape=jax.ShapeDtypeStruct((s, d), F32),
        compiler_params=_params("arbitrary"),
        name="merge_out",
    )(og, proj, proj, ya, x, w_b, w_out)


def _causal_conv(z_ref, row0, cols, w_ref, b_ref):
    ext = z_ref[pl.ds(row0, CARRY_ROWS + ACT_ROWS), cols]
    y = b_ref[:, cols] + w_ref[CONV_WIDTH - 1:CONV_WIDTH, cols] * ext[CARRY_ROWS:, :]
    for back in range(1, CONV_WIDTH):
        tap = CONV_WIDTH - 1 - back
        y = y + w_ref[tap:tap + 1, cols] * pltpu.roll(ext, back, axis=0)[CARRY_ROWS:, :]
    return y


def _ffn_kernel(x_ref, lnw_ref, wug_ref, wuv_ref, cwg_ref, cwv_ref, cbg_ref, cbv_ref, wd_ref,
                o_ref, h_ref, zg_ref, zv_ref, a_ref, cg_ref, cv_ref, *, dff):
    m = pl.program_id(0)
    j = pl.program_id(1)
    bm = h_ref.shape[0]
    tf = wd_ref.shape[0]
    shared = j * tf - jnp.minimum(j * tf, dff - tf)

    @pl.when(j == 0)
    def _():
        x = x_ref[...]
        h_ref[...] = _rms(x, lnw_ref[...]).astype(BF16)
        o_ref[...] = x

    @pl.when(m == 0)
    def _():
        cg_ref[j] = jnp.zeros(cg_ref.shape[1:], F32)
        cv_ref[j] = jnp.zeros(cv_ref.shape[1:], F32)

    h = h_ref[...]
    zg_ref[:CARRY_ROWS, :] = cg_ref[j]
    zv_ref[:CARRY_ROWS, :] = cv_ref[j]
    zg_ref[CARRY_ROWS:, :] = _dot(h, wug_ref[...])
    zv_ref[CARRY_ROWS:, :] = _dot(h, wuv_ref[...])
    cg_ref[j] = zg_ref[bm:, :]
    cv_ref[j] = zv_ref[bm:, :]

    for c in range(tf // LANE):
        cols = slice(c * LANE, (c + 1) * LANE)
        fresh = (lax.broadcasted_iota(jnp.int32, (1, LANE), 1) + c * LANE) >= shared
        for r in range(bm // ACT_ROWS):
            ug = _causal_conv(zg_ref, r * ACT_ROWS, cols, cwg_ref, cbg_ref)
            uv = _causal_conv(zv_ref, r * ACT_ROWS, cols, cwv_ref, cbv_ref)
            a_ref[r * ACT_ROWS:(r + 1) * ACT_ROWS, cols] = jnp.where(
                fresh, ug * jax.nn.sigmoid(ug) * uv, 0.0).astype(BF16)

    o_ref[...] += _dot(a_ref[...], wd_ref[...])


def _ffn(x, ln_w, w_up, conv_w, conv_b, w_down, *, bm, tf):
    s, d = x.shape
    dff = w_down.shape[0]
    nj = pl.cdiv(dff, tf)
    assert dff >= tf and dff % LANE == 0

    def start(j, base=0):
        return (base // LANE + jnp.minimum(j * (tf // LANE), (dff - tf) // LANE)) * LANE

    def window(rows, offset):
        return pl.BlockSpec((pl.Element(rows), pl.Element(tf)), lambda m, j: (0, offset(j)))

    return pl.pallas_call(
        functools.partial(_ffn_kernel, dff=dff),
        grid=(s // bm, nj),
        in_specs=[
            pl.BlockSpec((bm, d), lambda m, j: (m, 0)),
            pl.BlockSpec((1, d), lambda m, j: (0, 0)),
            window(d, start),
            window(d, lambda j: start(j, dff)),
            window(CONV_WIDTH, start),
            window(CONV_WIDTH, lambda j: start(j, dff)),
            window(1, start),
            window(1, lambda j: start(j, dff)),
            pl.BlockSpec((pl.Element(tf), pl.Element(d)), lambda m, j: (start(j), 0)),
        ],
        out_specs=pl.BlockSpec((bm, d), lambda m, j: (m, 0)),
        out_shape=jax.ShapeDtypeStruct((s, d), F32),
        scratch_shapes=[
            pltpu.VMEM((bm, d), BF16),
            pltpu.VMEM((CARRY_ROWS + bm, tf), F32),
            pltpu.VMEM((CARRY_ROWS + bm, tf), F32),
            pltpu.VMEM((bm, tf), BF16),
            pltpu.VMEM((nj, CARRY_ROWS, tf), F32),
            pltpu.VMEM((nj, CARRY_ROWS, tf), F32),
        ],
        compiler_params=_params("arbitrary", "arbitrary"),
        name="ffn",
    )(x, ln_w, w_up, w_up, conv_w, conv_w, conv_b, conv_b, w_down)


def _ple_kernel(x_ref, p_ref, lnw_ref, wg_ref, wp_ref, lnf_ref, o_ref, *, final):
    x = x_ref[...]
    h = _rms(x, lnw_ref[...]).astype(BF16)
    gate = jax.nn.sigmoid(_dot(h, wg_ref[...]))
    emb = _dot(p_ref[...].astype(BF16), wp_ref[...])
    y = x + gate * emb
    o_ref[...] = _rms(y, lnf_ref[...]) if final else y


def _ple(x, p, ln_w, w_gate, w_ple, ln_final, *, bm, final):
    s, d = x.shape
    e = p.shape[1]
    return pl.pallas_call(
        functools.partial(_ple_kernel, final=final),
        grid=(s // bm,),
        in_specs=[
            pl.BlockSpec((bm, d), lambda m: (m, 0)),
            pl.BlockSpec((bm, e), lambda m: (m, 0)),
            _resident((1, d)),
            _resident((d, d)),
            _resident((e, d)),
            _resident((1, d)),
        ],
        out_specs=pl.BlockSpec((bm, d), lambda m: (m, 0)),
        out_shape=jax.ShapeDtypeStruct((s, d), F32),
        compiler_params=_params("arbitrary"),
        name="ple_final",
    )(x, p, ln_w, w_gate, w_ple, ln_final)


def kernel(x, p, ln_mix_w, w_in, pool_mix_w, pool_scale, hgrn_lb_logits, hgrn_norm_w, w_branch_a,
           w_branch_b, w_out, ln_ffn_w, w_up, conv_w, conv_b, w_down, ln_ple_w, w_ple_gate, w_ple,
           ln_final_w):
    batch, seq, d = x.shape
    depth = w_in.shape[0]
    c_pool = w_branch_a.shape[1]
    c_hgrn = w_branch_b.shape[1]
    blk = c_hgrn
    assert c_pool == blk and d == 2 * blk, "column blocks of the combined projection must line up"
    assert seq % 1024 == 0
    tf = 512
    outs = []
    for bi in range(batch):
        xb = x[bi]
        for i in range(depth):
            proj = _inproj(xb, ln_mix_w[i][None], w_in[i], bm=1024, bn=blk)
            mix = pool_mix_w[i]
            og, (mix_b, wa_b, wb_b, wo_b, wu_b, wd_b, wg_b, wp_b) = _hgrn(
                proj, hgrn_lb_logits, hgrn_norm_w[i][None],
                [mix.reshape(-1, mix.shape[-1]), w_branch_a[i], w_branch_b[i], w_out[i], w_up[i], w_down[i],
                 w_ple_gate[i], w_ple[i]],
                bt=256, layer=i, q_block=1, f_block=2, i_block=3, g_block=4)
            ya = _pool_branch(proj, mix_b.reshape(mix.shape), pool_scale[i][None], wa_b,
                              bm=1024, bn=blk, ga_block0=5)
            xb = _merge(og, proj, ya, xb, wb_b, wo_b, bm=512, gb_block0=7)
            xb = _ffn(xb, ln_ffn_w[i][None], wu_b, conv_w[i], conv_b[i][None], wd_b, bm=1024, tf=512)
            xb = _ple(xb, p[i, bi], ln_ple_w[i][None], wg_b, wp_b, ln_final_w[None], bm=1024,
                      final=(i == depth - 1))
        outs.append(xb)
    return jnp.stack(outs, axis=0)
```

```python
import functools

import jax
import jax.numpy as jnp
from jax import lax
from jax.experimental import pallas as pl
from jax.experimental.pallas import tpu as pltpu

F32 = jnp.float32
BF16 = jnp.bfloat16

EPS = 1e-6
POOL_WINDOWS = (2, 4, 8, 16)
POOL_HALO = 16
HEAD_DIM = 128
CHUNK = 64
SUB = 16
NSUB = CHUNK // SUB
CONV_WIDTH = 3
CARRY_ROWS = 8
ACT_ROWS = 64
LANE = 128
BF16_TILE_ROWS = 16
LOG2_E = 1.4426950408889634
MILD_DECAY_LOG2 = -100.0
V7X_VMEM_BYTES = 64 * 1024 * 1024
VMEM_LIMIT = V7X_VMEM_BYTES - 8 * 1024 * 1024


def _dot(a, b):
    return jnp.dot(a, b, preferred_element_type=F32)


def _dot_nt(a, b):
    return lax.dot_general(a, b, (((1,), (1,)), ((), ())), preferred_element_type=F32)


def _dot_tn(a, b):
    return lax.dot_general(a, b, (((0,), (0,)), ((), ())), preferred_element_type=F32)


def _rms(x, w):
    return x * lax.rsqrt(jnp.mean(x * x, axis=-1, keepdims=True) + EPS) * w


def _params(*sem):
    return pltpu.CompilerParams(dimension_semantics=sem, vmem_limit_bytes=VMEM_LIMIT)


def _resident(shape):
    return pl.BlockSpec(shape, lambda *_: (0,) * len(shape), pipeline_mode=pl.Buffered(1))


def _inproj_kernel(x_ref, lnw_ref, w_ref, *rest, first):
    if first:
        o_ref, wb_ref, h_ref = rest
    else:
        _, o_ref, h_ref = rest

    @pl.when(pl.program_id(1) == 0)
    def _():
        h_ref[...] = _rms(x_ref[...], lnw_ref[...]).astype(BF16)

    w = w_ref[...].astype(BF16)
    if first:
        wb_ref[...] = w
    o_ref[...] = _dot(h_ref[...], w).astype(o_ref.dtype)


def _inproj(x, ln_w, w_in, *, bm, bn):
    s, d = x.shape
    d_in = w_in.shape[1]
    proj_shape = jax.ShapeDtypeStruct((s, d_in), BF16)
    common = dict(scratch_shapes=[pltpu.VMEM((bm, d), BF16)],
                  compiler_params=_params("arbitrary", "arbitrary"))

    def specs(row0):
        return [pl.BlockSpec((bm, d), lambda m, n: (m + row0, 0)),
                pl.BlockSpec((1, d), lambda m, n: (0, 0)),
                pl.BlockSpec((d, bn), lambda m, n: (0, n))]

    def out_spec(row0):
        return pl.BlockSpec((bm, bn), lambda m, n: (m + row0, n))

    proj, w_bf16 = pl.pallas_call(
        functools.partial(_inproj_kernel, first=True),
        grid=(1, d_in // bn),
        in_specs=specs(0),
        out_specs=[out_spec(0), pl.BlockSpec((d, bn), lambda m, n: (0, n))],
        out_shape=[proj_shape, jax.ShapeDtypeStruct(w_in.shape, BF16)],
        name="inproj_first", **common,
    )(x, ln_w, w_in)
    return pl.pallas_call(
        functools.partial(_inproj_kernel, first=False),
        grid=(s // bm - 1, d_in // bn),
        in_specs=[*specs(1), pl.BlockSpec(memory_space=pl.ANY)],
        out_specs=out_spec(1),
        out_shape=proj_shape,
        input_output_aliases={3: 0},
        name="inproj_rest", **common,
    )(x, ln_w, w_bf16, proj)


def _pool_kernel(u_ref, halo_ref, ga_ref, mix_ref, scale_ref, wa_ref, o_ref, feat_ref, *, bm):
    m = pl.program_id(0)
    n = pl.program_id(1)

    @pl.when(n == 0)
    def _():
        u = u_ref[...].astype(F32)
        halo = halo_ref[...].astype(F32) * (m > 0).astype(F32)
        ext = jnp.concatenate([halo, u], axis=0)
        gw = u.shape[1] // len(POOL_WINDOWS)
        pos = m * bm + lax.broadcasted_iota(jnp.int32, (bm, 1), 0) + 1
        run = ext
        width = 1
        feats = []
        for g, w in enumerate(POOL_WINDOWS):
            while width < w:
                run = run + pltpu.roll(run, width, axis=0)
                width *= 2
            cnt = jnp.minimum(pos, w).astype(F32)
            win = run[POOL_HALO:, :gw]
            if g + 1 < len(POOL_WINDOWS):
                run = run[:, gw:]
            d = win / cnt - u[:, g * gw:(g + 1) * gw]
            y = _dot(d.astype(BF16), mix_ref[g])
            feats.append(y * scale_ref[:, g * gw:(g + 1) * gw])
        feat_ref[...] = jnp.concatenate(feats, axis=1).astype(BF16)

    y_a = _dot(feat_ref[...], wa_ref[...])
    o_ref[...] = (jax.nn.sigmoid(ga_ref[...].astype(F32)) * y_a).astype(o_ref.dtype)


def _pool_branch(proj, mix_w, scale, w_a, *, bm, bn, ga_block0):
    s = proj.shape[0]
    c = w_a.shape[0]
    d = w_a.shape[1]
    g, gw, _ = mix_w.shape
    halo_per_tile = bm // POOL_HALO
    return pl.pallas_call(
        functools.partial(_pool_kernel, bm=bm),
        grid=(s // bm, d // bn),
        in_specs=[
            pl.BlockSpec((bm, c), lambda m, n: (m, 0)),
            pl.BlockSpec((POOL_HALO, c), lambda m, n: (jnp.maximum(m * halo_per_tile - 1, 0), 0)),
            pl.BlockSpec((bm, bn), lambda m, n: (m, ga_block0 + n)),
            _resident((g, gw, gw)),
            _resident((1, c)),
            pl.BlockSpec((c, bn), lambda m, n: (0, n)),
        ],
        out_specs=pl.BlockSpec((bm, bn), lambda m, n: (m, n)),
        out_shape=jax.ShapeDtypeStruct((s, d), BF16),
        scratch_shapes=[pltpu.VMEM((bm, c), BF16)],
        compiler_params=_params("arbitrary", "arbitrary"),
        name="pool_branch",
    )(proj, proj, proj, mix_w, scale, w_a)


def _split3(x):
    hi = x.astype(BF16)
    r = x - hi.astype(F32)
    mid = r.astype(BF16)
    lo = (r - mid.astype(F32)).astype(BF16)
    return hi, mid, lo


def _gates(fl, lb):
    one_m_lb = 1.0 - lb
    sg = jax.nn.sigmoid(fl)
    log2_f = jnp.log(lb + one_m_lb * sg) * LOG2_E
    return log2_f, one_m_lb * (1.0 - sg)


def _chunk_cumsum(tri, x):
    hi, mid, lo = _split3(x)
    return _dot(tri, hi) + _dot(tri, mid) + _dot(tri, lo)


def _head_norm_gate(o_heads, norm_w, g):
    o = jnp.concatenate(
        [o_h * lax.rsqrt(jnp.mean(o_h * o_h, axis=-1, keepdims=True) + EPS) for o_h in o_heads], axis=1)
    return o * norm_w * (g * jax.nn.sigmoid(g))


def _hgrn_chunk_any_decay(c, q_ref, f_ref, i_ref, g_ref, tri_ref, wsel_ref, o_ref, st_ref, lb, norm_w):
    r0 = pl.multiple_of(c * CHUNK, CHUNK)
    rows = pl.ds(r0, CHUNK)
    d = q_ref.shape[1]
    heads = d // HEAD_DIM

    q = q_ref[rows, :].astype(F32)
    v = i_ref[rows, :]
    log2_f, kk = _gates(f_ref[rows, :].astype(F32), lb)
    b = _chunk_cumsum(tri_ref[:CHUNK, :CHUNK], log2_f)
    b_last = b[CHUNK - 1:CHUNK, :]

    q_in = (q * jnp.exp2(b)).astype(BF16)
    k_st = (kk * jnp.exp2(b_last - b)).astype(BF16)

    b_end = jnp.concatenate(
        [jnp.broadcast_to(b[(j + 1) * SUB - 1:(j + 1) * SUB, :], (SUB, d)) for j in range(NSUB)], axis=0)
    k_hat = kk * jnp.exp2(b_end - b)
    sub_id = lax.broadcasted_iota(jnp.int32, (CHUNK, 1), 0) // SUB
    q_from = []
    for j in range(NSUB - 1):
        lo_row = (j + 1) * SUB
        bj = b[lo_row - 1:lo_row, :]
        part = q[lo_row:, :] * jnp.exp2(b[lo_row:, :] - bj)
        q_from.append(jnp.concatenate([jnp.zeros((lo_row, d), F32), part], axis=0).astype(BF16))
    k_src = [jnp.where(sub_id == j, k_hat, 0.0).astype(BF16) for j in range(NSUB - 1)]

    per_head = [[] for _ in range(heads)]
    for r in range(NSUB):
        qb = q[r * SUB:(r + 1) * SUB, :]
        bb = b[r * SUB:(r + 1) * SUB, :]
        pieces = []
        for s in range(SUB):
            row = r * SUB + s
            e = jnp.exp2(jnp.minimum(bb - b[row:row + 1, :], 0.0))
            pieces.append((qb * kk[row:row + 1, :] * e).astype(BF16))
        for h in range(heads):
            per_head[h].append(jnp.concatenate(
                [p[:, h * HEAD_DIM:(h + 1) * HEAD_DIM] for p in pieces], axis=1))
    a_big = jnp.concatenate([jnp.concatenate(blocks, axis=0) for blocks in per_head], axis=0)
    diag = _dot(a_big, wsel_ref[...])

    t_id = lax.broadcasted_iota(jnp.int32, (CHUNK, CHUNK), 0)
    s_id = lax.broadcasted_iota(jnp.int32, (CHUNK, CHUNK), 1)
    diag_mask = (t_id // SUB == s_id // SUB) & (s_id <= t_id)

    outs = []
    for h in range(heads):
        hs = slice(h * HEAD_DIM, (h + 1) * HEAD_DIM)
        q_cat = jnp.concatenate([qf[:, hs] for qf in q_from], axis=1)
        k_cat = jnp.concatenate([ks[:, hs] for ks in k_src], axis=1)
        scores = _dot_nt(q_cat, k_cat) + jnp.where(diag_mask, diag[h * CHUNK:(h + 1) * CHUNK, :], 0.0)
        v_h = v[:, hs]
        state_t = st_ref[h]
        o_h = _dot_nt(q_in[:, hs], state_t.astype(BF16)) + _dot(scores.astype(BF16), v_h)
        st_ref[h] = state_t * jnp.exp2(b_last[:, hs]) + _dot_tn(v_h, k_st[:, hs])
        outs.append(o_h)
    o_ref[rows, :] = _head_norm_gate(outs, norm_w, g_ref[rows, :].astype(F32)).astype(o_ref.dtype)


def _hgrn_tile_mild_decay(b, kk, q_ref, i_ref, g_ref, o_ref, st_ref, norm_w):
    bt, d = b.shape
    heads = d // HEAD_DIM
    n_chunks = bt // CHUNK
    q = q_ref[...].astype(F32)
    v = i_ref[...]
    chunk_decay = [jnp.exp2(b[(c + 1) * CHUNK - 1:(c + 1) * CHUNK, :]) for c in range(n_chunks)]
    q_in = (q * jnp.exp2(b)).astype(BF16)
    k_undecayed = kk * jnp.exp2(-b)
    k_out = k_undecayed.astype(BF16)
    k_st = (k_undecayed * jnp.concatenate(
        [jnp.broadcast_to(r, (CHUNK, d)) for r in chunk_decay], axis=0)).astype(BF16)
    causal = (lax.broadcasted_iota(jnp.int32, (CHUNK, CHUNK), 1)
              <= lax.broadcasted_iota(jnp.int32, (CHUNK, CHUNK), 0))
    states = [st_ref[h] for h in range(heads)]
    tile_out = []
    for c in range(n_chunks):
        rs = slice(c * CHUNK, (c + 1) * CHUNK)
        carry_decay = chunk_decay[c]
        outs = []
        for h in range(heads):
            hs = slice(h * HEAD_DIM, (h + 1) * HEAD_DIM)
            scores = jnp.where(causal, _dot_nt(q_in[rs, hs], k_out[rs, hs]), 0.0).astype(BF16)
            outs.append(_dot(scores, v[rs, hs]) + _dot_nt(q_in[rs, hs], states[h].astype(BF16)))
            states[h] = states[h] * carry_decay[:, hs] + _dot_tn(v[rs, hs], k_st[rs, hs])
        tile_out.append(jnp.concatenate(outs, axis=1))
    for h in range(heads):
        st_ref[h] = states[h]
    heads_out = jnp.concatenate(tile_out, axis=0)
    o_heads = [heads_out[:, h * HEAD_DIM:(h + 1) * HEAD_DIM] for h in range(heads)]
    o_ref[...] = _head_norm_gate(o_heads, norm_w, g_ref[...].astype(F32)).astype(o_ref.dtype)


def _hgrn_kernel(q_ref, f_ref, i_ref, g_ref, lbl_ref, nw_ref, tri_ref, wsel_ref, *rest, layer,
                 cast_blocks):
    n_cast = len(cast_blocks)
    cast_in, (o_ref, *cast_out, st_ref) = rest[:n_cast], rest[n_cast:]
    step = pl.program_id(0)

    @pl.when(step == 0)
    def _():
        st_ref[...] = jnp.zeros_like(st_ref)

    for src, dst, n_blocks in zip(cast_in, cast_out, cast_blocks):
        @pl.when(step < n_blocks)
        def _():
            dst[...] = src[...].astype(dst.dtype)

    logits = lbl_ref[...]
    e = jnp.exp(logits - jnp.max(logits, axis=0, keepdims=True))
    lb = jnp.sum(e[:layer + 1, :], axis=0, keepdims=True) / jnp.sum(e, axis=0, keepdims=True)
    norm_w = nw_ref[...]

    log2_f, kk = _gates(f_ref[...].astype(F32), lb)
    b = _chunk_cumsum(tri_ref[...], log2_f)
    mild = jnp.min(b) >= MILD_DECAY_LOG2

    @pl.when(mild)
    def _():
        _hgrn_tile_mild_decay(b, kk, q_ref, i_ref, g_ref, o_ref, st_ref, norm_w)

    @pl.when(jnp.logical_not(mild))
    def _():
        def body(c, carry):
            _hgrn_chunk_any_decay(c, q_ref, f_ref, i_ref, g_ref, tri_ref, wsel_ref, o_ref, st_ref, lb, norm_w)
            return carry

        lax.fori_loop(0, q_ref.shape[0] // CHUNK, body, 0)


def _split_over_steps(shape, steps):
    rows, cols = shape
    if rows % (steps * BF16_TILE_ROWS) == 0:
        return pl.BlockSpec((rows // steps, cols), lambda t: (t, 0)), steps
    n = max(k for k in range(1, steps + 1) if (cols // LANE) % k == 0)
    assert cols % LANE == 0
    return pl.BlockSpec((rows, cols // n), lambda t: (0, jnp.minimum(t, n - 1))), n


def _hgrn(proj, lb_logits, norm_w, to_bf16, *, bt, layer, q_block, f_block, i_block, g_block):
    s = proj.shape[0]
    d = norm_w.shape[1]
    heads = d // HEAD_DIM
    steps = s // bt
    cast_specs, cast_blocks = zip(*[_split_over_steps(a.shape, steps) for a in to_bf16])
    t_id = lax.broadcasted_iota(jnp.int32, (bt, bt), 0)
    s_id = lax.broadcasted_iota(jnp.int32, (bt, bt), 1)
    tri = ((s_id <= t_id) & (s_id // CHUNK == t_id // CHUNK)).astype(BF16)
    src = lax.broadcasted_iota(jnp.int32, (SUB * HEAD_DIM, CHUNK), 0) // HEAD_DIM
    col = lax.broadcasted_iota(jnp.int32, (SUB * HEAD_DIM, CHUNK), 1) % SUB
    wsel = (src == col).astype(BF16)
    nl = lb_logits.shape[0]
    og, *as_bf16 = pl.pallas_call(
        functools.partial(_hgrn_kernel, layer=layer, cast_blocks=cast_blocks),
        grid=(steps,),
        in_specs=[
            pl.BlockSpec((bt, d), lambda t: (t, q_block)),
            pl.BlockSpec((bt, d), lambda t: (t, f_block)),
            pl.BlockSpec((bt, d), lambda t: (t, i_block)),
            pl.BlockSpec((bt, d), lambda t: (t, g_block)),
            _resident((nl, d)),
            _resident((1, d)),
            _resident((bt, bt)),
            _resident((SUB * HEAD_DIM, CHUNK)),
            *cast_specs,
        ],
        out_specs=[pl.BlockSpec((bt, d), lambda t: (t, 0)), *cast_specs],
        out_shape=[jax.ShapeDtypeStruct((s, d), BF16),
                   *[jax.ShapeDtypeStruct(a.shape, BF16) for a in to_bf16]],
        scratch_shapes=[pltpu.VMEM((heads, HEAD_DIM, HEAD_DIM), F32)],
        compiler_params=_params("arbitrary"),
        name="hgrn",
    )(proj, proj, proj, proj, lb_logits, norm_w, tri, wsel, *to_bf16)
    return og, as_bf16


def _merge_kernel(og_ref, gb0_ref, gb1_ref, ya_ref, x_ref, wb_ref, wo_ref, o_ref):
    y_b = _dot(og_ref[...], wb_ref[...])
    gate_b = jnp.concatenate([gb0_ref[...], gb1_ref[...]], axis=1).astype(F32)
    merged = ya_ref[...].astype(F32) + jax.nn.sigmoid(gate_b) * y_b
    o_ref[...] = x_ref[...] + _dot(merged.astype(BF16), wo_ref[...])


def _merge(og, proj, ya, x, w_b, w_out, *, bm, gb_block0):
    s, d = x.shape
    c = og.shape[1]
    half = d // 2
    return pl.pallas_call(
        _merge_kernel,
        grid=(s // bm,),
        in_specs=[
            pl.BlockSpec((bm, c), lambda m: (m, 0)),
            pl.BlockSpec((bm, half), lambda m: (m, gb_block0)),
            pl.BlockSpec((bm, half), lambda m: (m, gb_block0 + 1)),
            pl.BlockSpec((bm, d), lambda m: (m, 0)),
            pl.BlockSpec((bm, d), lambda m: (m, 0)),
            _resident((c, d)),
            _resident((d, d)),
        ],
        out_specs=pl.BlockSpec((bm, d), lambda m: (m, 0)),
        out_shape=jax.ShapeDtypeStruct((s, d), F32),
        compiler_params=_params("arbitrary"),
        name="merge_out",
    )(og, proj, proj, ya, x, w_b, w_out)


def _causal_conv(z_ref, row0, cols, w_ref, b_ref):
    ext = z_ref[pl.ds(row0, CARRY_ROWS + ACT_ROWS), cols]
    y = b_ref[:, cols] + w_ref[CONV_WIDTH - 1:CONV_WIDTH, cols] * ext[CARRY_ROWS:, :]
    for back in range(1, CONV_WIDTH):
        tap = CONV_WIDTH - 1 - back
        y = y + w_ref[tap:tap + 1, cols] * pltpu.roll(ext, back, axis=0)[CARRY_ROWS:, :]
    return y


def _ffn_kernel(x_ref, lnw_ref, wug_ref, wuv_ref, cwg_ref, cwv_ref, cbg_ref, cbv_ref, wd_ref,
                o_ref, h_ref, zg_ref, zv_ref, a_ref, cg_ref, cv_ref, *, dff):
    m = pl.program_id(0)
    j = pl.program_id(1)
    bm = h_ref.shape[0]
    tf = wd_ref.shape[0]
    shared = j * tf - jnp.minimum(j * tf, dff - tf)

    @pl.when(j == 0)
    def _():
        x = x_ref[...]
        h_ref[...] = _rms(x, lnw_ref[...]).astype(BF16)
        o_ref[...] = x

    @pl.when(m == 0)
    def _():
        cg_ref[j] = jnp.zeros(cg_ref.shape[1:], F32)
        cv_ref[j] = jnp.zeros(cv_ref.shape[1:], F32)

    h = h_ref[...]
    zg_ref[:CARRY_ROWS, :] = cg_ref[j]
    zv_ref[:CARRY_ROWS, :] = cv_ref[j]
    zg_ref[CARRY_ROWS:, :] = _dot(h, wug_ref[...])
    zv_ref[CARRY_ROWS:, :] = _dot(h, wuv_ref[...])
    cg_ref[j] = zg_ref[bm:, :]
    cv_ref[j] = zv_ref[bm:, :]

    for c in range(tf // LANE):
        cols = slice(c * LANE, (c + 1) * LANE)
        fresh = (lax.broadcasted_iota(jnp.int32, (1, LANE), 1) + c * LANE) >= shared
        for r in range(bm // ACT_ROWS):
            ug = _causal_conv(zg_ref, r * ACT_ROWS, cols, cwg_ref, cbg_ref)
            uv = _causal_conv(zv_ref, r * ACT_ROWS, cols, cwv_ref, cbv_ref)
            a_ref[r * ACT_ROWS:(r + 1) * ACT_ROWS, cols] = jnp.where(
                fresh, ug * jax.nn.sigmoid(ug) * uv, 0.0).astype(BF16)

    o_ref[...] += _dot(a_ref[...], wd_ref[...])


def _ffn(x, ln_w, w_up, conv_w, conv_b, w_down, *, bm, tf):
    s, d = x.shape
    dff = w_down.shape[0]
    nj = pl.cdiv(dff, tf)
    assert dff >= tf and dff % LANE == 0

    def start(j, base=0):
        return (base // LANE + jnp.minimum(j * (tf // LANE), (dff - tf) // LANE)) * LANE

    def window(rows, offset):
        return pl.BlockSpec((pl.Element(rows), pl.Element(tf)), lambda m, j: (0, offset(j)))

    return pl.pallas_call(
        functools.partial(_ffn_kernel, dff=dff),
        grid=(s // bm, nj),
        in_specs=[
            pl.BlockSpec((bm, d), lambda m, j: (m, 0)),
            pl.BlockSpec((1, d), lambda m, j: (0, 0)),
            window(d, start),
            window(d, lambda j: start(j, dff)),
            window(CONV_WIDTH, start),
            window(CONV_WIDTH, lambda j: start(j, dff)),
            window(1, start),
            window(1, lambda j: start(j, dff)),
            pl.BlockSpec((pl.Element(tf), pl.Element(d)), lambda m, j: (start(j), 0)),
        ],
        out_specs=pl.BlockSpec((bm, d), lambda m, j: (m, 0)),
        out_shape=jax.ShapeDtypeStruct((s, d), F32),
        scratch_shapes=[
            pltpu.VMEM((bm, d), BF16),
            pltpu.VMEM((CARRY_ROWS + bm, tf), F32),
            pltpu.VMEM((CARRY_ROWS + bm, tf), F32),
            pltpu.VMEM((bm, tf), BF16),
            pltpu.VMEM((nj, CARRY_ROWS, tf), F32),
            pltpu.VMEM((nj, CARRY_ROWS, tf), F32),
        ],
        compiler_params=_params("arbitrary", "arbitrary"),
        name="ffn",
    )(x, ln_w, w_up, w_up, conv_w, conv_w, conv_b, conv_b, w_down)


def _ple_kernel(x_ref, p_ref, lnw_ref, wg_ref, wp_ref, lnf_ref, o_ref, *, final):
    x = x_ref[...]
    h = _rms(x, lnw_ref[...]).astype(BF16)
    gate = jax.nn.sigmoid(_dot(h, wg_ref[...]))
    emb = _dot(p_ref[...].astype(BF16), wp_ref[...])
    y = x + gate * emb
    o_ref[...] = _rms(y, lnf_ref[...]) if final else y


def _ple(x, p, ln_w, w_gate, w_ple, ln_final, *, bm, final):
    s, d = x.shape
    e = p.shape[1]
    return pl.pallas_call(
        functools.partial(_ple_kernel, final=final),
        grid=(s // bm,),
        in_specs=[
            pl.BlockSpec((bm, d), lambda m: (m, 0)),
            pl.BlockSpec((bm, e), lambda m: (m, 0)),
            _resident((1, d)),
            _resident((d, d)),
            _resident((e, d)),
            _resident((1, d)),
        ],
        out_specs=pl.BlockSpec((bm, d), lambda m: (m, 0)),
        out_shape=jax.ShapeDtypeStruct((s, d), F32),
        compiler_params=_params("arbitrary"),
        name="ple_final",
    )(x, p, ln_w, w_gate, w_ple, ln_final)


def kernel(x, p, ln_mix_w, w_in, pool_mix_w, pool_scale, hgrn_lb_logits, hgrn_norm_w, w_branch_a,
           w_branch_b, w_out, ln_ffn_w, w_up, conv_w, conv_b, w_down, ln_ple_w, w_ple_gate, w_ple,
           ln_final_w):
    batch, seq, d = x.shape
    depth = w_in.shape[0]
    c_pool = w_branch_a.shape[1]
    c_hgrn = w_branch_b.shape[1]
    blk = c_hgrn
    assert c_pool == blk and d == 2 * blk, "column blocks of the combined projection must line up"
    assert seq % 1024 == 0
    tf = 512
    outs = []
    for bi in range(batch):
        xb = x[bi]
        for i in range(depth):
            proj = _inproj(xb, ln_mix_w[i][None], w_in[i], bm=1024, bn=blk)
            mix = pool_mix_w[i]
            og, (mix_b, wa_b, wb_b, wo_b, wu_b, wd_b, wg_b, wp_b) = _hgrn(
                proj, hgrn_lb_logits, hgrn_norm_w[i][None],
                [mix.reshape(-1, mix.shape[-1]), w_branch_a[i], w_branch_b[i], w_out[i], w_up[i], w_down[i],
                 w_ple_gate[i], w_ple[i]],
                bt=256, layer=i, q_block=1, f_block=2, i_block=3, g_block=4)
            ya = _pool_branch(proj, mix_b.reshape(mix.shape), pool_scale[i][None], wa_b,
                              bm=1024, bn=blk, ga_block0=5)
            xb = _merge(og, proj, ya, xb, wb_b, wo_b, bm=512, gb_block0=7)
            xb = _ffn(xb, ln_ffn_w[i][None], wu_b, conv_w[i], conv_b[i][None], wd_b, bm=1024, tf=512)
            xb = _ple(xb, p[i, bi], ln_ple_w[i][None], wg_b, wp_b, ln_final_w[None], bm=1024,
                      final=(i == depth - 1))
        outs.append(xb)
    return jnp.stack(outs, axis=0)
```

```python
import functools

import jax
import jax.numpy as jnp
from jax import lax
from jax.experimental import pallas as pl
from jax.experimental.pallas import tpu as pltpu

F32 = jnp.float32
BF16 = jnp.bfloat16

EPS = 1e-6
POOL_WINDOWS = (2, 4, 8, 16)
POOL_HALO = 16
HEAD_DIM = 128
CHUNK = 64
FAST_CHUNK = 128
SUB = 16
NSUB = CHUNK // SUB
CONV_WIDTH = 3
CARRY_ROWS = 8
ACT_ROWS = 64
LANE = 128
BF16_TILE_ROWS = 16
LOG2_E = 1.4426950408889634
MILD_DECAY_LOG2 = -100.0
V7X_VMEM_BYTES = 64 * 1024 * 1024
VMEM_LIMIT = V7X_VMEM_BYTES - 8 * 1024 * 1024


def _dot(a, b):
    return jnp.dot(a, b, preferred_element_type=F32)


def _dot_nt(a, b):
    return lax.dot_general(a, b, (((1,), (1,)), ((), ())), preferred_element_type=F32)


def _dot_tn(a, b):
    return lax.dot_general(a, b, (((0,), (0,)), ((), ())), preferred_element_type=F32)


def _rms(x, w):
    return x * lax.rsqrt(jnp.mean(x * x, axis=-1, keepdims=True) + EPS) * w


def _params(*sem):
    return pltpu.CompilerParams(dimension_semantics=sem, vmem_limit_bytes=VMEM_LIMIT)


def _resident(shape):
    return pl.BlockSpec(shape, lambda *_: (0,) * len(shape), pipeline_mode=pl.Buffered(1))


def _inproj_kernel(x_ref, lnw_ref, w_ref, o_ref, h_ref):
    @pl.when(pl.program_id(1) == 0)
    def _():
        h_ref[...] = _rms(x_ref[...], lnw_ref[...]).astype(BF16)

    o_ref[...] = _dot(h_ref[...], w_ref[...].astype(BF16)).astype(o_ref.dtype)


def _inproj(x, ln_w, w_in, *, bm, bn):
    s, d = x.shape
    d_in = w_in.shape[1]
    return pl.pallas_call(
        _inproj_kernel,
        grid=(s // bm, d_in // bn),
        in_specs=[
            pl.BlockSpec((bm, d), lambda m, n: (m, 0)),
            pl.BlockSpec((1, d), lambda m, n: (0, 0)),
            pl.BlockSpec((d, bn), lambda m, n: (0, n)),
        ],
        out_specs=pl.BlockSpec((bm, bn), lambda m, n: (m, n)),
        out_shape=jax.ShapeDtypeStruct((s, d_in), BF16),
        scratch_shapes=[pltpu.VMEM((bm, d), BF16)],
        compiler_params=_params("arbitrary", "arbitrary"),
        name="inproj",
    )(x, ln_w, w_in)


def _pool_kernel(u_ref, halo_ref, ga_ref, mix_ref, scale_ref, wa_ref, o_ref, feat_ref, *, bm):
    m = pl.program_id(0)
    n = pl.program_id(1)

    @pl.when(n == 0)
    def _():
        u = u_ref[...].astype(F32)
        halo = halo_ref[...].astype(F32) * (m > 0).astype(F32)
        ext = jnp.concatenate([halo, u], axis=0)
        gw = u.shape[1] // len(POOL_WINDOWS)
        pos = m * bm + lax.broadcasted_iota(jnp.int32, (bm, 1), 0) + 1
        run = ext
        width = 1
        feats = []
        for g, w in enumerate(POOL_WINDOWS):
            while width < w:
                run = run + pltpu.roll(run, width, axis=0)
                width *= 2
            cnt = jnp.minimum(pos, w).astype(F32)
            win = run[POOL_HALO:, :gw]
            if g + 1 < len(POOL_WINDOWS):
                run = run[:, gw:]
            d = win / cnt - u[:, g * gw:(g + 1) * gw]
            y = _dot(d.astype(BF16), mix_ref[g])
            feats.append(y * scale_ref[:, g * gw:(g + 1) * gw])
        feat_ref[...] = jnp.concatenate(feats, axis=1).astype(BF16)

    y_a = _dot(feat_ref[...], wa_ref[...])
    o_ref[...] = (jax.nn.sigmoid(ga_ref[...].astype(F32)) * y_a).astype(o_ref.dtype)


def _pool_branch(proj, mix_w, scale, w_a, *, bm, bn, ga_block0):
    s = proj.shape[0]
    c = w_a.shape[0]
    d = w_a.shape[1]
    g, gw, _ = mix_w.shape
    halo_per_tile = bm // POOL_HALO
    return pl.pallas_call(
        functools.partial(_pool_kernel, bm=bm),
        grid=(s // bm, d // bn),
        in_specs=[
            pl.BlockSpec((bm, c), lambda m, n: (m, 0)),
            pl.BlockSpec((POOL_HALO, c), lambda m, n: (jnp.maximum(m * halo_per_tile - 1, 0), 0)),
            pl.BlockSpec((bm, bn), lambda m, n: (m, ga_block0 + n)),
            _resident((g, gw, gw)),
            _resident((1, c)),
            pl.BlockSpec((c, bn), lambda m, n: (0, n)),
        ],
        out_specs=pl.BlockSpec((bm, bn), lambda m, n: (m, n)),
        out_shape=jax.ShapeDtypeStruct((s, d), BF16),
        scratch_shapes=[pltpu.VMEM((bm, c), BF16)],
        compiler_params=_params("arbitrary", "arbitrary"),
        name="pool_branch",
    )(proj, proj, proj, mix_w, scale, w_a)


def _split3(x):
    hi = x.astype(BF16)
    r = x - hi.astype(F32)
    mid = r.astype(BF16)
    lo = (r - mid.astype(F32)).astype(BF16)
    return hi, mid, lo


def _gates(fl, lb):
    one_m_lb = 1.0 - lb
    sg = jax.nn.sigmoid(fl)
    log2_f = jnp.log(lb + one_m_lb * sg) * LOG2_E
    return log2_f, one_m_lb * (1.0 - sg)


def _chunk_cumsum(tri, x):
    hi, mid, lo = _split3(x)
    return _dot(tri, hi) + _dot(tri, mid) + _dot(tri, lo)


def _head_norm_gate(o_heads, norm_w, g):
    o = jnp.concatenate(
        [o_h * lax.rsqrt(jnp.mean(o_h * o_h, axis=-1, keepdims=True) + EPS) for o_h in o_heads], axis=1)
    return o * norm_w * (g * jax.nn.sigmoid(g))


def _hgrn_chunk_any_decay(c, q_ref, f_ref, i_ref, g_ref, tri_ref, wsel_ref, o_ref, st_ref, lb, norm_w):
    r0 = pl.multiple_of(c * CHUNK, CHUNK)
    rows = pl.ds(r0, CHUNK)
    d = q_ref.shape[1]
    heads = d // HEAD_DIM

    q = q_ref[rows, :].astype(F32)
    v = i_ref[rows, :]
    log2_f, kk = _gates(f_ref[rows, :].astype(F32), lb)
    b = _chunk_cumsum(tri_ref[:CHUNK, :CHUNK], log2_f)
    b_last = b[CHUNK - 1:CHUNK, :]

    q_in = (q * jnp.exp2(b)).astype(BF16)
    k_st = (kk * jnp.exp2(b_last - b)).astype(BF16)

    b_end = jnp.concatenate(
        [jnp.broadcast_to(b[(j + 1) * SUB - 1:(j + 1) * SUB, :], (SUB, d)) for j in range(NSUB)], axis=0)
    k_hat = kk * jnp.exp2(b_end - b)
    sub_id = lax.broadcasted_iota(jnp.int32, (CHUNK, 1), 0) // SUB
    q_from = []
    for j in range(NSUB - 1):
        lo_row = (j + 1) * SUB
        bj = b[lo_row - 1:lo_row, :]
        part = q[lo_row:, :] * jnp.exp2(b[lo_row:, :] - bj)
        q_from.append(jnp.concatenate([jnp.zeros((lo_row, d), F32), part], axis=0).astype(BF16))
    k_src = [jnp.where(sub_id == j, k_hat, 0.0).astype(BF16) for j in range(NSUB - 1)]

    per_head = [[] for _ in range(heads)]
    for r in range(NSUB):
        qb = q[r * SUB:(r + 1) * SUB, :]
        bb = b[r * SUB:(r + 1) * SUB, :]
        pieces = []
        for s in range(SUB):
            row = r * SUB + s
            e = jnp.exp2(jnp.minimum(bb - b[row:row + 1, :], 0.0))
            pieces.append((qb * kk[row:row + 1, :] * e).astype(BF16))
        for h in range(heads):
            per_head[h].append(jnp.concatenate(
                [p[:, h * HEAD_DIM:(h + 1) * HEAD_DIM] for p in pieces], axis=1))
    a_big = jnp.concatenate([jnp.concatenate(blocks, axis=0) for blocks in per_head], axis=0)
    diag = _dot(a_big, wsel_ref[...])

    t_id = lax.broadcasted_iota(jnp.int32, (CHUNK, CHUNK), 0)
    s_id = lax.broadcasted_iota(jnp.int32, (CHUNK, CHUNK), 1)
    diag_mask = (t_id // SUB == s_id // SUB) & (s_id <= t_id)

    outs = []
    for h in range(heads):
        hs = slice(h * HEAD_DIM, (h + 1) * HEAD_DIM)
        q_cat = jnp.concatenate([qf[:, hs] for qf in q_from], axis=1)
        k_cat = jnp.concatenate([ks[:, hs] for ks in k_src], axis=1)
        scores = _dot_nt(q_cat, k_cat) + jnp.where(diag_mask, diag[h * CHUNK:(h + 1) * CHUNK, :], 0.0)
        v_h = v[:, hs]
        state_t = st_ref[h]
        o_h = _dot_nt(q_in[:, hs], state_t.astype(BF16)) + _dot(scores.astype(BF16), v_h)
        st_ref[h] = state_t * jnp.exp2(b_last[:, hs]) + _dot_tn(v_h, k_st[:, hs])
        outs.append(o_h)
    o_ref[rows, :] = _head_norm_gate(outs, norm_w, g_ref[rows, :].astype(F32)).astype(o_ref.dtype)


def _hgrn_tile_mild_decay(b, kk, q_ref, i_ref, g_ref, o_ref, st_ref, norm_w):
    bt, d = b.shape
    heads = d // HEAD_DIM
    ck = FAST_CHUNK
    n_chunks = bt // ck
    q = q_ref[...].astype(F32)
    v = i_ref[...]
    chunk_decay = [jnp.exp2(b[(c + 1) * ck - 1:(c + 1) * ck, :]) for c in range(n_chunks)]
    q_in = (q * jnp.exp2(b)).astype(BF16)
    k_undecayed = kk * jnp.exp2(-b)
    k_out = k_undecayed.astype(BF16)
    k_st = (k_undecayed * jnp.concatenate(
        [jnp.broadcast_to(r, (ck, d)) for r in chunk_decay], axis=0)).astype(BF16)
    causal = (lax.broadcasted_iota(jnp.int32, (ck, ck), 1) <= lax.broadcasted_iota(jnp.int32, (ck, ck), 0))
    states = [st_ref[h] for h in range(heads)]
    tile_out = []
    for c in range(n_chunks):
        rs = slice(c * ck, (c + 1) * ck)
        carry_decay = chunk_decay[c]
        outs = []
        for h in range(heads):
            hs = slice(h * HEAD_DIM, (h + 1) * HEAD_DIM)
            scores = jnp.where(causal, _dot_nt(q_in[rs, hs], k_out[rs, hs]), 0.0).astype(BF16)
            outs.append(_dot(scores, v[rs, hs]) + _dot_nt(q_in[rs, hs], states[h].astype(BF16)))
            states[h] = states[h] * carry_decay[:, hs] + _dot_tn(v[rs, hs], k_st[rs, hs])
        tile_out.append(jnp.concatenate(outs, axis=1))
    for h in range(heads):
        st_ref[h] = states[h]
    heads_out = jnp.concatenate(tile_out, axis=0)
    o_heads = [heads_out[:, h * HEAD_DIM:(h + 1) * HEAD_DIM] for h in range(heads)]
    o_ref[...] = _head_norm_gate(o_heads, norm_w, g_ref[...].astype(F32)).astype(o_ref.dtype)


def _hgrn_kernel(q_ref, f_ref, i_ref, g_ref, lbl_ref, nw_ref, tri_ref, wsel_ref, *rest, layer,
                 cast_blocks):
    n_cast = len(cast_blocks)
    cast_in, (o_ref, *cast_out, st_ref) = rest[:n_cast], rest[n_cast:]
    step = pl.program_id(0)

    @pl.when(step == 0)
    def _():
        st_ref[...] = jnp.zeros_like(st_ref)

    for src, dst, n_blocks in zip(cast_in, cast_out, cast_blocks):
        @pl.when(step < n_blocks)
        def _():
            dst[...] = src[...].astype(dst.dtype)

    logits = lbl_ref[...]
    e = jnp.exp(logits - jnp.max(logits, axis=0, keepdims=True))
    lb = jnp.sum(e[:layer + 1, :], axis=0, keepdims=True) / jnp.sum(e, axis=0, keepdims=True)
    norm_w = nw_ref[...]

    log2_f, kk = _gates(f_ref[...].astype(F32), lb)
    b = _chunk_cumsum(tri_ref[...], log2_f)
    mild = jnp.min(b) >= MILD_DECAY_LOG2

    @pl.when(mild)
    def _():
        _hgrn_tile_mild_decay(b, kk, q_ref, i_ref, g_ref, o_ref, st_ref, norm_w)

    @pl.when(jnp.logical_not(mild))
    def _():
        def body(c, carry):
            _hgrn_chunk_any_decay(c, q_ref, f_ref, i_ref, g_ref, tri_ref, wsel_ref, o_ref, st_ref, lb, norm_w)
            return carry

        lax.fori_loop(0, q_ref.shape[0] // CHUNK, body, 0)


def _split_over_steps(shape, steps):
    rows, cols = shape
    if rows % (steps * BF16_TILE_ROWS) == 0:
        return pl.BlockSpec((rows // steps, cols), lambda t: (t, 0)), steps
    n = max(k for k in range(1, steps + 1) if (cols // LANE) % k == 0)
    assert cols % LANE == 0
    return pl.BlockSpec((rows, cols // n), lambda t: (0, jnp.minimum(t, n - 1))), n


def _hgrn(proj, lb_logits, norm_w, to_bf16, *, bt, layer, q_block, f_block, i_block, g_block):
    s = proj.shape[0]
    d = norm_w.shape[1]
    heads = d // HEAD_DIM
    steps = s // bt
    cast_specs, cast_blocks = zip(*[_split_over_steps(a.shape, steps) for a in to_bf16])
    t_id = lax.broadcasted_iota(jnp.int32, (bt, bt), 0)
    s_id = lax.broadcasted_iota(jnp.int32, (bt, bt), 1)
    tri = ((s_id <= t_id) & (s_id // FAST_CHUNK == t_id // FAST_CHUNK)).astype(BF16)
    src = lax.broadcasted_iota(jnp.int32, (SUB * HEAD_DIM, CHUNK), 0) // HEAD_DIM
    col = lax.broadcasted_iota(jnp.int32, (SUB * HEAD_DIM, CHUNK), 1) % SUB
    wsel = (src == col).astype(BF16)
    nl = lb_logits.shape[0]
    og, *as_bf16 = pl.pallas_call(
        functools.partial(_hgrn_kernel, layer=layer, cast_blocks=cast_blocks),
        grid=(steps,),
        in_specs=[
            pl.BlockSpec((bt, d), lambda t: (t, q_block)),
            pl.BlockSpec((bt, d), lambda t: (t, f_block)),
            pl.BlockSpec((bt, d), lambda t: (t, i_block)),
            pl.BlockSpec((bt, d), lambda t: (t, g_block)),
            _resident((nl, d)),
            _resident((1, d)),
            _resident((bt, bt)),
            _resident((SUB * HEAD_DIM, CHUNK)),
            *cast_specs,
        ],
        out_specs=[pl.BlockSpec((bt, d), lambda t: (t, 0)), *cast_specs],
        out_shape=[jax.ShapeDtypeStruct((s, d), BF16),
                   *[jax.ShapeDtypeStruct(a.shape, BF16) for a in to_bf16]],
        scratch_shapes=[pltpu.VMEM((heads, HEAD_DIM, HEAD_DIM), F32)],
        compiler_params=_params("arbitrary"),
        name="hgrn",
    )(proj, proj, proj, proj, lb_logits, norm_w, tri, wsel, *to_bf16)
    return og, as_bf16


def _merge_kernel(og_ref, gb0_ref, gb1_ref, ya_ref, x_ref, wb_ref, wo_ref, o_ref):
    y_b = _dot(og_ref[...], wb_ref[...])
    gate_b = jnp.concatenate([gb0_ref[...], gb1_ref[...]], axis=1).astype(F32)
    merged = ya_ref[...].astype(F32) + jax.nn.sigmoid(gate_b) * y_b
    o_ref[...] = x_ref[...] + _dot(merged.astype(BF16), wo_ref[...])


def _merge(og, proj, ya, x, w_b, w_out, *, bm, gb_block0):
    s, d = x.shape
    c = og.shape[1]
    half = d // 2
    return pl.pallas_call(
        _merge_kernel,
        grid=(s // bm,),
        in_specs=[
            pl.BlockSpec((bm, c), lambda m: (m, 0)),
            pl.BlockSpec((bm, half), lambda m: (m, gb_block0)),
            pl.BlockSpec((bm, half), lambda m: (m, gb_block0 + 1)),
            pl.BlockSpec((bm, d), lambda m: (m, 0)),
            pl.BlockSpec((bm, d), lambda m: (m, 0)),
            _resident((c, d)),
            _resident((d, d)),
        ],
        out_specs=pl.BlockSpec((bm, d), lambda m: (m, 0)),
        out_shape=jax.ShapeDtypeStruct((s, d), F32),
        compiler_params=_params("arbitrary"),
        name="merge_out",
    )(og, proj, proj, ya, x, w_b, w_out)


def _causal_conv(z_ref, row0, cols, w_ref, b_ref):
    ext = z_ref[pl.ds(row0, CARRY_ROWS + ACT_ROWS), cols]
    y = b_ref[:, cols] + w_ref[CONV_WIDTH - 1:CONV_WIDTH, cols] * ext[CARRY_ROWS:, :]
    for back in range(1, CONV_WIDTH):
        tap = CONV_WIDTH - 1 - back
        y = y + w_ref[tap:tap + 1, cols] * pltpu.roll(ext, back, axis=0)[CARRY_ROWS:, :]
    return y


def _ffn_kernel(x_ref, lnw_ref, wug_ref, wuv_ref, cwg_ref, cwv_ref, cbg_ref, cbv_ref, wd_ref,
                o_ref, h_ref, zg_ref, zv_ref, a_ref, cg_ref, cv_ref, *, dff):
    m = pl.program_id(0)
    j = pl.program_id(1)
    bm = h_ref.shape[0]
    tf = wd_ref.shape[0]
    shared = j * tf - jnp.minimum(j * tf, dff - tf)

    @pl.when(j == 0)
    def _():
        x = x_ref[...]
        h_ref[...] = _rms(x, lnw_ref[...]).astype(BF16)
        o_ref[...] = x

    @pl.when(m == 0)
    def _():
        cg_ref[j] = jnp.zeros(cg_ref.shape[1:], F32)
        cv_ref[j] = jnp.zeros(cv_ref.shape[1:], F32)

    h = h_ref[...]
    zg_ref[:CARRY_ROWS, :] = cg_ref[j]
    zv_ref[:CARRY_ROWS, :] = cv_ref[j]
    zg_ref[CARRY_ROWS:, :] = _dot(h, wug_ref[...])
    zv_ref[CARRY_ROWS:, :] = _dot(h, wuv_ref[...])
    cg_ref[j] = zg_ref[bm:, :]
    cv_ref[j] = zv_ref[bm:, :]

    for c in range(tf // LANE):
        cols = slice(c * LANE, (c + 1) * LANE)
        fresh = (lax.broadcasted_iota(jnp.int32, (1, LANE), 1) + c * LANE) >= shared
        for r in range(bm // ACT_ROWS):
            ug = _causal_conv(zg_ref, r * ACT_ROWS, cols, cwg_ref, cbg_ref)
            uv = _causal_conv(zv_ref, r * ACT_ROWS, cols, cwv_ref, cbv_ref)
            a_ref[r * ACT_ROWS:(r + 1) * ACT_ROWS, cols] = jnp.where(
                fresh, ug * jax.nn.sigmoid(ug) * uv, 0.0).astype(BF16)

    o_ref[...] += _dot(a_ref[...], wd_ref[...])


def _ffn(x, ln_w, w_up, conv_w, conv_b, w_down, *, bm, tf):
    s, d = x.shape
    dff = w_down.shape[0]
    nj = pl.cdiv(dff, tf)
    assert dff >= tf and dff % LANE == 0

    def start(j, base=0):
        return (base // LANE + jnp.minimum(j * (tf // LANE), (dff - tf) // LANE)) * LANE

    def window(rows, offset):
        return pl.BlockSpec((pl.Element(rows), pl.Element(tf)), lambda m, j: (0, offset(j)))

    return pl.pallas_call(
        functools.partial(_ffn_kernel, dff=dff),
        grid=(s // bm, nj),
        in_specs=[
            pl.BlockSpec((bm, d), lambda m, j: (m, 0)),
            pl.BlockSpec((1, d), lambda m, j: (0, 0)),
            window(d, start),
            window(d, lambda j: start(j, dff)),
            window(CONV_WIDTH, start),
            window(CONV_WIDTH, lambda j: start(j, dff)),
            window(1, start),
            window(1, lambda j: start(j, dff)),
            pl.BlockSpec((pl.Element(tf), pl.Element(d)), lambda m, j: (start(j), 0)),
        ],
        out_specs=pl.BlockSpec((bm, d), lambda m, j: (m, 0)),
        out_shape=jax.ShapeDtypeStruct((s, d), F32),
        scratch_shapes=[
            pltpu.VMEM((bm, d), BF16),
            pltpu.VMEM((CARRY_ROWS + bm, tf), F32),
            pltpu.VMEM((CARRY_ROWS + bm, tf), F32),
            pltpu.VMEM((bm, tf), BF16),
            pltpu.VMEM((nj, CARRY_ROWS, tf), F32),
            pltpu.VMEM((nj, CARRY_ROWS, tf), F32),
        ],
        compiler_params=_params("arbitrary", "arbitrary"),
        name="ffn",
    )(x, ln_w, w_up, w_up, conv_w, conv_w, conv_b, conv_b, w_down)


def _ple_kernel(x_ref, p_ref, lnw_ref, wg_ref, wp_ref, lnf_ref, o_ref, *, final):
    x = x_ref[...]
    h = _rms(x, lnw_ref[...]).astype(BF16)
    gate = jax.nn.sigmoid(_dot(h, wg_ref[...]))
    emb = _dot(p_ref[...].astype(BF16), wp_ref[...])
    y = x + gate * emb
    o_ref[...] = _rms(y, lnf_ref[...]) if final else y


def _ple(x, p, ln_w, w_gate, w_ple, ln_final, *, bm, final):
    s, d = x.shape
    e = p.shape[1]
    return pl.pallas_call(
        functools.partial(_ple_kernel, final=final),
        grid=(s // bm,),
        in_specs=[
            pl.BlockSpec((bm, d), lambda m: (m, 0)),
            pl.BlockSpec((bm, e), lambda m: (m, 0)),
            _resident((1, d)),
            _resident((d, d)),
            _resident((e, d)),
            _resident((1, d)),
        ],
        out_specs=pl.BlockSpec((bm, d), lambda m: (m, 0)),
        out_shape=jax.ShapeDtypeStruct((s, d), F32),
        compiler_params=_params("arbitrary"),
        name="ple_final",
    )(x, p, ln_w, w_gate, w_ple, ln_final)


def kernel(x, p, ln_mix_w, w_in, pool_mix_w, pool_scale, hgrn_lb_logits, hgrn_norm_w, w_branch_a,
           w_branch_b, w_out, ln_ffn_w, w_up, conv_w, conv_b, w_down, ln_ple_w, w_ple_gate, w_ple,
           ln_final_w):
    batch, seq, d = x.shape
    depth = w_in.shape[0]
    c_pool = w_branch_a.shape[1]
    c_hgrn = w_branch_b.shape[1]
    blk = c_hgrn
    assert c_pool == blk and d == 2 * blk, "column blocks of the combined projection must line up"
    assert seq % 1024 == 0
    tf = 512
    outs = []
    for bi in range(batch):
        xb = x[bi]
        for i in range(depth):
            proj = _inproj(xb, ln_mix_w[i][None], w_in[i], bm=1024, bn=blk)
            mix = pool_mix_w[i]
            og, (mix_b, wa_b, wb_b, wo_b, wu_b, wd_b, wg_b, wp_b) = _hgrn(
                proj, hgrn_lb_logits, hgrn_norm_w[i][None],
                [mix.reshape(-1, mix.shape[-1]), w_branch_a[i], w_branch_b[i], w_out[i], w_up[i], w_down[i],
                 w_ple_gate[i], w_ple[i]],
                bt=256, layer=i, q_block=1, f_block=2, i_block=3, g_block=4)
            ya = _pool_branch(proj, mix_b.reshape(mix.shape), pool_scale[i][None], wa_b,
                              bm=1024, bn=blk, ga_block0=5)
            xb = _merge(og, proj, ya, xb, wb_b, wo_b, bm=512, gb_block0=7)
            xb = _ffn(xb, ln_ffn_w[i][None], wu_b, conv_w[i], conv_b[i][None], wd_b, bm=1024, tf=512)
            xb = _ple(xb, p[i, bi], ln_ple_w[i][None], wg_b, wp_b, ln_final_w[None], bm=1024,
                      final=(i == depth - 1))
        outs.append(xb)
    return jnp.stack(outs, axis=0)
```

```python
import functools

import jax
import jax.numpy as jnp
from jax import lax
from jax.experimental import pallas as pl
from jax.experimental.pallas import tpu as pltpu

F32 = jnp.float32
BF16 = jnp.bfloat16

EPS = 1e-6
POOL_WINDOWS = (2, 4, 8, 16)
POOL_HALO = 16
HEAD_DIM = 128
CHUNK = 64
FAST_CHUNK = 128
SUB = 16
NSUB = CHUNK // SUB
CONV_WIDTH = 3
CARRY_ROWS = 8
ACT_ROWS = 64
LANE = 128
BF16_TILE_ROWS = 16
LOG2_E = 1.4426950408889634
MILD_DECAY_LOG2 = -100.0
V7X_VMEM_BYTES = 64 * 1024 * 1024
VMEM_LIMIT = V7X_VMEM_BYTES - 8 * 1024 * 1024


def _dot(a, b):
    return jnp.dot(a, b, preferred_element_type=F32)


def _dot_nt(a, b):
    return lax.dot_general(a, b, (((1,), (1,)), ((), ())), preferred_element_type=F32)


def _dot_tn(a, b):
    return lax.dot_general(a, b, (((0,), (0,)), ((), ())), preferred_element_type=F32)


def _rms(x, w):
    return x * lax.rsqrt(jnp.mean(x * x, axis=-1, keepdims=True) + EPS) * w


def _params(*sem):
    return pltpu.CompilerParams(dimension_semantics=sem, vmem_limit_bytes=VMEM_LIMIT)


def _resident(shape):
    return pl.BlockSpec(shape, lambda *_: (0,) * len(shape), pipeline_mode=pl.Buffered(1))


def _inproj_kernel(x_ref, lnw_ref, w_ref, o_ref, h_ref):
    @pl.when(pl.program_id(1) == 0)
    def _():
        h_ref[...] = _rms(x_ref[...], lnw_ref[...]).astype(BF16)

    o_ref[...] = _dot(h_ref[...], w_ref[...].astype(BF16)).astype(o_ref.dtype)


def _inproj(x, ln_w, w_in, *, bm, bn):
    s, d = x.shape
    d_in = w_in.shape[1]
    return pl.pallas_call(
        _inproj_kernel,
        grid=(s // bm, d_in // bn),
        in_specs=[
            pl.BlockSpec((bm, d), lambda m, n: (m, 0)),
            pl.BlockSpec((1, d), lambda m, n: (0, 0)),
            pl.BlockSpec((d, bn), lambda m, n: (0, n)),
        ],
        out_specs=pl.BlockSpec((bm, bn), lambda m, n: (m, n)),
        out_shape=jax.ShapeDtypeStruct((s, d_in), BF16),
        scratch_shapes=[pltpu.VMEM((bm, d), BF16)],
        compiler_params=_params("arbitrary", "arbitrary"),
        name="inproj",
    )(x, ln_w, w_in)


def _pool_kernel(u_ref, halo_ref, ga_ref, mix_ref, scale_ref, wa_ref, o_ref, feat_ref, *, bm):
    m = pl.program_id(0)
    n = pl.program_id(1)

    @pl.when(n == 0)
    def _():
        u = u_ref[...].astype(F32)
        halo = halo_ref[...].astype(F32) * (m > 0).astype(F32)
        ext = jnp.concatenate([halo, u], axis=0)
        gw = u.shape[1] // len(POOL_WINDOWS)
        pos = m * bm + lax.broadcasted_iota(jnp.int32, (bm, 1), 0) + 1
        run = ext
        width = 1
        feats = []
        for g, w in enumerate(POOL_WINDOWS):
            while width < w:
                run = run + pltpu.roll(run, width, axis=0)
                width *= 2
            cnt = jnp.minimum(pos, w).astype(F32)
            win = run[POOL_HALO:, :gw]
            if g + 1 < len(POOL_WINDOWS):
                run = run[:, gw:]
            d = win / cnt - u[:, g * gw:(g + 1) * gw]
            y = _dot(d.astype(BF16), mix_ref[g])
            feats.append(y * scale_ref[:, g * gw:(g + 1) * gw])
        feat_ref[...] = jnp.concatenate(feats, axis=1).astype(BF16)

    y_a = _dot(feat_ref[...], wa_ref[...])
    o_ref[...] = (jax.nn.sigmoid(ga_ref[...].astype(F32)) * y_a).astype(o_ref.dtype)


def _pool_branch(proj, mix_w, scale, w_a, *, bm, bn, ga_block0):
    s = proj.shape[0]
    c = w_a.shape[0]
    d = w_a.shape[1]
    g, gw, _ = mix_w.shape
    halo_per_tile = bm // POOL_HALO
    return pl.pallas_call(
        functools.partial(_pool_kernel, bm=bm),
        grid=(s // bm, d // bn),
        in_specs=[
            pl.BlockSpec((bm, c), lambda m, n: (m, 0)),
            pl.BlockSpec((POOL_HALO, c), lambda m, n: (jnp.maximum(m * halo_per_tile - 1, 0), 0)),
            pl.BlockSpec((bm, bn), lambda m, n: (m, ga_block0 + n)),
            _resident((g, gw, gw)),
            _resident((1, c)),
            pl.BlockSpec((c, bn), lambda m, n: (0, n)),
        ],
        out_specs=pl.BlockSpec((bm, bn), lambda m, n: (m, n)),
        out_shape=jax.ShapeDtypeStruct((s, d), BF16),
        scratch_shapes=[pltpu.VMEM((bm, c), BF16)],
        compiler_params=_params("arbitrary", "arbitrary"),
        name="pool_branch",
    )(proj, proj, proj, mix_w, scale, w_a)


def _split2(x):
    hi = x.astype(BF16)
    return hi, (x - hi.astype(F32)).astype(BF16)


def _gates(fl, lb):
    one_m_lb = 1.0 - lb
    sg = jax.nn.sigmoid(fl)
    log2_f = jnp.log(lb + one_m_lb * sg) * LOG2_E
    return log2_f, one_m_lb * (1.0 - sg)


def _chunk_cumsum(tri, x):
    hi, lo = _split2(x)
    return _dot(tri, hi) + _dot(tri, lo)


def _head_norm_gate(o_heads, norm_w, g):
    o = jnp.concatenate(
        [o_h * lax.rsqrt(jnp.mean(o_h * o_h, axis=-1, keepdims=True) + EPS) for o_h in o_heads], axis=1)
    return o * norm_w * (g * jax.nn.sigmoid(g))


def _hgrn_chunk_any_decay(c, q_ref, f_ref, i_ref, g_ref, tri_ref, wsel_ref, o_ref, st_ref, lb, norm_w):
    r0 = pl.multiple_of(c * CHUNK, CHUNK)
    rows = pl.ds(r0, CHUNK)
    d = q_ref.shape[1]
    heads = d // HEAD_DIM

    q = q_ref[rows, :].astype(F32)
    v = i_ref[rows, :]
    log2_f, kk = _gates(f_ref[rows, :].astype(F32), lb)
    b = _chunk_cumsum(tri_ref[:CHUNK, :CHUNK], log2_f)
    b_last = b[CHUNK - 1:CHUNK, :]

    q_in = (q * jnp.exp2(b)).astype(BF16)
    k_st = (kk * jnp.exp2(b_last - b)).astype(BF16)

    b_end = jnp.concatenate(
        [jnp.broadcast_to(b[(j + 1) * SUB - 1:(j + 1) * SUB, :], (SUB, d)) for j in range(NSUB)], axis=0)
    k_hat = kk * jnp.exp2(b_end - b)
    sub_id = lax.broadcasted_iota(jnp.int32, (CHUNK, 1), 0) // SUB
    q_from = []
    for j in range(NSUB - 1):
        lo_row = (j + 1) * SUB
        bj = b[lo_row - 1:lo_row, :]
        part = q[lo_row:, :] * jnp.exp2(b[lo_row:, :] - bj)
        q_from.append(jnp.concatenate([jnp.zeros((lo_row, d), F32), part], axis=0).astype(BF16))
    k_src = [jnp.where(sub_id == j, k_hat, 0.0).astype(BF16) for j in range(NSUB - 1)]

    per_head = [[] for _ in range(heads)]
    for r in range(NSUB):
        qb = q[r * SUB:(r + 1) * SUB, :]
        bb = b[r * SUB:(r + 1) * SUB, :]
        pieces = []
        for s in range(SUB):
            row = r * SUB + s
            e = jnp.exp2(jnp.minimum(bb - b[row:row + 1, :], 0.0))
            pieces.append((qb * kk[row:row + 1, :] * e).astype(BF16))
        for h in range(heads):
            per_head[h].append(jnp.concatenate(
                [p[:, h * HEAD_DIM:(h + 1) * HEAD_DIM] for p in pieces], axis=1))
    a_big = jnp.concatenate([jnp.concatenate(blocks, axis=0) for blocks in per_head], axis=0)
    diag = _dot(a_big, wsel_ref[...])

    t_id = lax.broadcasted_iota(jnp.int32, (CHUNK, CHUNK), 0)
    s_id = lax.broadcasted_iota(jnp.int32, (CHUNK, CHUNK), 1)
    diag_mask = (t_id // SUB == s_id // SUB) & (s_id <= t_id)

    outs = []
    for h in range(heads):
        hs = slice(h * HEAD_DIM, (h + 1) * HEAD_DIM)
        q_cat = jnp.concatenate([qf[:, hs] for qf in q_from], axis=1)
        k_cat = jnp.concatenate([ks[:, hs] for ks in k_src], axis=1)
        scores = _dot_nt(q_cat, k_cat) + jnp.where(diag_mask, diag[h * CHUNK:(h + 1) * CHUNK, :], 0.0)
        v_h = v[:, hs]
        state_t = st_ref[h]
        o_h = _dot_nt(q_in[:, hs], state_t.astype(BF16)) + _dot(scores.astype(BF16), v_h)
        st_ref[h] = state_t * jnp.exp2(b_last[:, hs]) + _dot_tn(v_h, k_st[:, hs])
        outs.append(o_h)
    o_ref[rows, :] = _head_norm_gate(outs, norm_w, g_ref[rows, :].astype(F32)).astype(o_ref.dtype)


def _hgrn_tile_mild_decay(b, kk, q_ref, i_ref, g_ref, o_ref, st_ref, norm_w):
    bt, d = b.shape
    heads = d // HEAD_DIM
    ck = FAST_CHUNK
    n_chunks = bt // ck
    q = q_ref[...].astype(F32)
    v = i_ref[...]
    chunk_decay = [jnp.exp2(b[(c + 1) * ck - 1:(c + 1) * ck, :]) for c in range(n_chunks)]
    q_in = (q * jnp.exp2(b)).astype(BF16)
    k_undecayed = kk * jnp.exp2(-b)
    k_out = k_undecayed.astype(BF16)
    k_st = (k_undecayed * jnp.concatenate(
        [jnp.broadcast_to(r, (ck, d)) for r in chunk_decay], axis=0)).astype(BF16)
    causal = (lax.broadcasted_iota(jnp.int32, (ck, ck), 1) <= lax.broadcasted_iota(jnp.int32, (ck, ck), 0))
    states = [st_ref[h] for h in range(heads)]
    tile_out = []
    for c in range(n_chunks):
        rs = slice(c * ck, (c + 1) * ck)
        carry_decay = chunk_decay[c]
        outs = []
        for h in range(heads):
            hs = slice(h * HEAD_DIM, (h + 1) * HEAD_DIM)
            scores = jnp.where(causal, _dot_nt(q_in[rs, hs], k_out[rs, hs]), 0.0).astype(BF16)
            outs.append(_dot(scores, v[rs, hs]) + _dot_nt(q_in[rs, hs], states[h].astype(BF16)))
            states[h] = states[h] * carry_decay[:, hs] + _dot_tn(v[rs, hs], k_st[rs, hs])
        tile_out.append(jnp.concatenate(outs, axis=1))
    for h in range(heads):
        st_ref[h] = states[h]
    heads_out = jnp.concatenate(tile_out, axis=0)
    o_heads = [heads_out[:, h * HEAD_DIM:(h + 1) * HEAD_DIM] for h in range(heads)]
    o_ref[...] = _head_norm_gate(o_heads, norm_w, g_ref[...].astype(F32)).astype(o_ref.dtype)


def _hgrn_kernel(q_ref, f_ref, i_ref, g_ref, lbl_ref, nw_ref, tri_ref, wsel_ref, *rest, layer,
                 cast_blocks):
    n_cast = len(cast_blocks)
    cast_in, (o_ref, *cast_out, st_ref) = rest[:n_cast], rest[n_cast:]
    step = pl.program_id(0)

    @pl.when(step == 0)
    def _():
        st_ref[...] = jnp.zeros_like(st_ref)

    for src, dst, n_blocks in zip(cast_in, cast_out, cast_blocks):
        @pl.when(step < n_blocks)
        def _():
            dst[...] = src[...].astype(dst.dtype)

    logits = lbl_ref[...]
    e = jnp.exp(logits - jnp.max(logits, axis=0, keepdims=True))
    lb = jnp.sum(e[:layer + 1, :], axis=0, keepdims=True) / jnp.sum(e, axis=0, keepdims=True)
    norm_w = nw_ref[...]

    log2_f, kk = _gates(f_ref[...].astype(F32), lb)
    b = _chunk_cumsum(tri_ref[...], log2_f)
    mild = jnp.min(b) >= MILD_DECAY_LOG2

    @pl.when(mild)
    def _():
        _hgrn_tile_mild_decay(b, kk, q_ref, i_ref, g_ref, o_ref, st_ref, norm_w)

    @pl.when(jnp.logical_not(mild))
    def _():
        def body(c, carry):
            _hgrn_chunk_any_decay(c, q_ref, f_ref, i_ref, g_ref, tri_ref, wsel_ref, o_ref, st_ref, lb, norm_w)
            return carry

        lax.fori_loop(0, q_ref.shape[0] // CHUNK, body, 0)


def _split_over_steps(shape, steps):
    rows, cols = shape
    if rows % (steps * BF16_TILE_ROWS) == 0:
        return pl.BlockSpec((rows // steps, cols), lambda t: (t, 0)), steps
    n = max(k for k in range(1, steps + 1) if (cols // LANE) % k == 0)
    assert cols % LANE == 0
    return pl.BlockSpec((rows, cols // n), lambda t: (0, jnp.minimum(t, n - 1))), n


def _hgrn(proj, lb_logits, norm_w, to_bf16, *, bt, layer, q_block, f_block, i_block, g_block):
    s = proj.shape[0]
    d = norm_w.shape[1]
    heads = d // HEAD_DIM
    steps = s // bt
    cast_specs, cast_blocks = zip(*[_split_over_steps(a.shape, steps) for a in to_bf16])
    t_id = lax.broadcasted_iota(jnp.int32, (bt, bt), 0)
    s_id = lax.broadcasted_iota(jnp.int32, (bt, bt), 1)
    tri = ((s_id <= t_id) & (s_id // FAST_CHUNK == t_id // FAST_CHUNK)).astype(BF16)
    src = lax.broadcasted_iota(jnp.int32, (SUB * HEAD_DIM, CHUNK), 0) // HEAD_DIM
    col = lax.broadcasted_iota(jnp.int32, (SUB * HEAD_DIM, CHUNK), 1) % SUB
    wsel = (src == col).astype(BF16)
    nl = lb_logits.shape[0]
    og, *as_bf16 = pl.pallas_call(
        functools.partial(_hgrn_kernel, layer=layer, cast_blocks=cast_blocks),
        grid=(steps,),
        in_specs=[
            pl.BlockSpec((bt, d), lambda t: (t, q_block)),
            pl.BlockSpec((bt, d), lambda t: (t, f_block)),
            pl.BlockSpec((bt, d), lambda t: (t, i_block)),
            pl.BlockSpec((bt, d), lambda t: (t, g_block)),
            _resident((nl, d)),
            _resident((1, d)),
            _resident((bt, bt)),
            _resident((SUB * HEAD_DIM, CHUNK)),
            *cast_specs,
        ],
        out_specs=[pl.BlockSpec((bt, d), lambda t: (t, 0)), *cast_specs],
        out_shape=[jax.ShapeDtypeStruct((s, d), BF16),
                   *[jax.ShapeDtypeStruct(a.shape, BF16) for a in to_bf16]],
        scratch_shapes=[pltpu.VMEM((heads, HEAD_DIM, HEAD_DIM), F32)],
        compiler_params=_params("arbitrary"),
        name="hgrn",
    )(proj, proj, proj, proj, lb_logits, norm_w, tri, wsel, *to_bf16)
    return og, as_bf16


def _merge_kernel(og_ref, gb0_ref, gb1_ref, ya_ref, x_ref, wb_ref, wo_ref, o_ref):
    y_b = _dot(og_ref[...], wb_ref[...])
    gate_b = jnp.concatenate([gb0_ref[...], gb1_ref[...]], axis=1).astype(F32)
    merged = ya_ref[...].astype(F32) + jax.nn.sigmoid(gate_b) * y_b
    o_ref[...] = x_ref[...] + _dot(merged.astype(BF16), wo_ref[...])


def _merge(og, proj, ya, x, w_b, w_out, *, bm, gb_block0):
    s, d = x.shape
    c = og.shape[1]
    half = d // 2
    return pl.pallas_call(
        _merge_kernel,
        grid=(s // bm,),
        in_specs=[
            pl.BlockSpec((bm, c), lambda m: (m, 0)),
            pl.BlockSpec((bm, half), lambda m: (m, gb_block0)),
            pl.BlockSpec((bm, half), lambda m: (m, gb_block0 + 1)),
            pl.BlockSpec((bm, d), lambda m: (m, 0)),
            pl.BlockSpec((bm, d), lambda m: (m, 0)),
            _resident((c, d)),
            _resident((d, d)),
        ],
        out_specs=pl.BlockSpec((bm, d), lambda m: (m, 0)),
        out_shape=jax.ShapeDtypeStruct((s, d), F32),
        compiler_params=_params("arbitrary"),
        name="merge_out",
    )(og, proj, proj, ya, x, w_b, w_out)


def _causal_conv(z_ref, row0, cols, w_ref, b_ref):
    ext = z_ref[pl.ds(row0, CARRY_ROWS + ACT_ROWS), cols]
    y = b_ref[:, cols] + w_ref[CONV_WIDTH - 1:CONV_WIDTH, cols] * ext[CARRY_ROWS:, :]
    for back in range(1, CONV_WIDTH):
        tap = CONV_WIDTH - 1 - back
        y = y + w_ref[tap:tap + 1, cols] * pltpu.roll(ext, back, axis=0)[CARRY_ROWS:, :]
    return y


def _ffn_kernel(x_ref, lnw_ref, wug_ref, wuv_ref, cwg_ref, cwv_ref, cbg_ref, cbv_ref, wd_ref,
                o_ref, h_ref, zg_ref, zv_ref, a_ref, cg_ref, cv_ref, *, dff):
    m = pl.program_id(0)
    j = pl.program_id(1)
    bm = h_ref.shape[0]
    tf = wd_ref.shape[0]
    shared = j * tf - jnp.minimum(j * tf, dff - tf)

    @pl.when(j == 0)
    def _():
        x = x_ref[...]
        h_ref[...] = _rms(x, lnw_ref[...]).astype(BF16)
        o_ref[...] = x

    @pl.when(m == 0)
    def _():
        cg_ref[j] = jnp.zeros(cg_ref.shape[1:], F32)
        cv_ref[j] = jnp.zeros(cv_ref.shape[1:], F32)

    h = h_ref[...]
    zg_ref[:CARRY_ROWS, :] = cg_ref[j]
    zv_ref[:CARRY_ROWS, :] = cv_ref[j]
    zg_ref[CARRY_ROWS:, :] = _dot(h, wug_ref[...])
    zv_ref[CARRY_ROWS:, :] = _dot(h, wuv_ref[...])
    cg_ref[j] = zg_ref[bm:, :]
    cv_ref[j] = zv_ref[bm:, :]

    for c in range(tf // LANE):
        cols = slice(c * LANE, (c + 1) * LANE)
        fresh = (lax.broadcasted_iota(jnp.int32, (1, LANE), 1) + c * LANE) >= shared
        for r in range(bm // ACT_ROWS):
            ug = _causal_conv(zg_ref, r * ACT_ROWS, cols, cwg_ref, cbg_ref)
            uv = _causal_conv(zv_ref, r * ACT_ROWS, cols, cwv_ref, cbv_ref)
            a_ref[r * ACT_ROWS:(r + 1) * ACT_ROWS, cols] = jnp.where(
                fresh, ug * jax.nn.sigmoid(ug) * uv, 0.0).astype(BF16)

    o_ref[...] += _dot(a_ref[...], wd_ref[...])


def _ffn(x, ln_w, w_up, conv_w, conv_b, w_down, *, bm, tf):
    s, d = x.shape
    dff = w_down.shape[0]
    nj = pl.cdiv(dff, tf)
    assert dff >= tf and dff % LANE == 0

    def start(j, base=0):
        return (base // LANE + jnp.minimum(j * (tf // LANE), (dff - tf) // LANE)) * LANE

    def window(rows, offset):
        return pl.BlockSpec((pl.Element(rows), pl.Element(tf)), lambda m, j: (0, offset(j)))

    return pl.pallas_call(
        functools.partial(_ffn_kernel, dff=dff),
        grid=(s // bm, nj),
        in_specs=[
            pl.BlockSpec((bm, d), lambda m, j: (m, 0)),
            pl.BlockSpec((1, d), lambda m, j: (0, 0)),
            window(d, start),
            window(d, lambda j: start(j, dff)),
            window(CONV_WIDTH, start),
            window(CONV_WIDTH, lambda j: start(j, dff)),
            window(1, start),
            window(1, lambda j: start(j, dff)),
            pl.BlockSpec((pl.Element(tf), pl.Element(d)), lambda m, j: (start(j), 0)),
        ],
        out_specs=pl.BlockSpec((bm, d), lambda m, j: (m, 0)),
        out_shape=jax.ShapeDtypeStruct((s, d), F32),
        scratch_shapes=[
            pltpu.VMEM((bm, d), BF16),
            pltpu.VMEM((CARRY_ROWS + bm, tf), F32),
            pltpu.VMEM((CARRY_ROWS + bm, tf), F32),
            pltpu.VMEM((bm, tf), BF16),
            pltpu.VMEM((nj, CARRY_ROWS, tf), F32),
            pltpu.VMEM((nj, CARRY_ROWS, tf), F32),
        ],
        compiler_params=_params("arbitrary", "arbitrary"),
        name="ffn",
    )(x, ln_w, w_up, w_up, conv_w, conv_w, conv_b, conv_b, w_down)


def _ple_kernel(x_ref, p_ref, lnw_ref, wg_ref, wp_ref, lnf_ref, o_ref, *, final):
    x = x_ref[...]
    h = _rms(x, lnw_ref[...]).astype(BF16)
    gate = jax.nn.sigmoid(_dot(h, wg_ref[...]))
    emb = _dot(p_ref[...].astype(BF16), wp_ref[...])
    y = x + gate * emb
    o_ref[...] = _rms(y, lnf_ref[...]) if final else y


def _ple(x, p, ln_w, w_gate, w_ple, ln_final, *, bm, final):
    s, d = x.shape
    e = p.shape[1]
    return pl.pallas_call(
        functools.partial(_ple_kernel, final=final),
        grid=(s // bm,),
        in_specs=[
            pl.BlockSpec((bm, d), lambda m: (m, 0)),
            pl.BlockSpec((bm, e), lambda m: (m, 0)),
            _resident((1, d)),
            _resident((d, d)),
            _resident((e, d)),
            _resident((1, d)),
        ],
        out_specs=pl.BlockSpec((bm, d), lambda m: (m, 0)),
        out_shape=jax.ShapeDtypeStruct((s, d), F32),
        compiler_params=_params("arbitrary"),
        name="ple_final",
    )(x, p, ln_w, w_gate, w_ple, ln_final)


def kernel(x, p, ln_mix_w, w_in, pool_mix_w, pool_scale, hgrn_lb_logits, hgrn_norm_w, w_branch_a,
           w_branch_b, w_out, ln_ffn_w, w_up, conv_w, conv_b, w_down, ln_ple_w, w_ple_gate, w_ple,
           ln_final_w):
    batch, seq, d = x.shape
    depth = w_in.shape[0]
    c_pool = w_branch_a.shape[1]
    c_hgrn = w_branch_b.shape[1]
    blk = c_hgrn
    assert c_pool == blk and d == 2 * blk, "column blocks of the combined projection must line up"
    assert seq % 1024 == 0
    tf = 512
    outs = []
    for bi in range(batch):
        xb = x[bi]
        for i in range(depth):
            proj = _inproj(xb, ln_mix_w[i][None], w_in[i], bm=1024, bn=blk)
            mix = pool_mix_w[i]
            og, (mix_b, wa_b, wb_b, wo_b, wu_b, wd_b, wg_b, wp_b) = _hgrn(
                proj, hgrn_lb_logits, hgrn_norm_w[i][None],
                [mix.reshape(-1, mix.shape[-1]), w_branch_a[i], w_branch_b[i], w_out[i], w_up[i], w_down[i],
                 w_ple_gate[i], w_ple[i]],
                bt=512, layer=i, q_block=1, f_block=2, i_block=3, g_block=4)
            ya = _pool_branch(proj, mix_b.reshape(mix.shape), pool_scale[i][None], wa_b,
                              bm=1024, bn=blk, ga_block0=5)
            xb = _merge(og, proj, ya, xb, wb_b, wo_b, bm=512, gb_block0=7)
            xb = _ffn(xb, ln_ffn_w[i][None], wu_b, conv_w[i], conv_b[i][None], wd_b, bm=1024, tf=512)
            xb = _ple(xb, p[i, bi], ln_ple_w[i][None], wg_b, wp_b, ln_final_w[None], bm=1024,
                      final=(i == depth - 1))
        outs.append(xb)
    return jnp.stack(outs, axis=0)
```

```python
import functools

import jax
import jax.numpy as jnp
from jax import lax
from jax.experimental import pallas as pl
from jax.experimental.pallas import tpu as pltpu

F32 = jnp.float32
BF16 = jnp.bfloat16

EPS = 1e-6
POOL_WINDOWS = (2, 4, 8, 16)
POOL_HALO = 16
HEAD_DIM = 128
CHUNK = 64
FAST_CHUNK = 128
SUB = 16
NSUB = CHUNK // SUB
CONV_WIDTH = 3
CARRY_ROWS = 8
ACT_ROWS = 64
LANE = 128
BF16_TILE_ROWS = 16
LOG2_E = 1.4426950408889634
MILD_DECAY_LOG2 = -100.0
V7X_VMEM_BYTES = 64 * 1024 * 1024
VMEM_LIMIT = V7X_VMEM_BYTES - 8 * 1024 * 1024


def _dot(a, b):
    return jnp.dot(a, b, preferred_element_type=F32)


def _dot_nt(a, b):
    return lax.dot_general(a, b, (((1,), (1,)), ((), ())), preferred_element_type=F32)


def _dot_tn(a, b):
    return lax.dot_general(a, b, (((0,), (0,)), ((), ())), preferred_element_type=F32)


def _rms(x, w):
    return x * lax.rsqrt(jnp.mean(x * x, axis=-1, keepdims=True) + EPS) * w


def _params(*sem):
    return pltpu.CompilerParams(dimension_semantics=sem, vmem_limit_bytes=VMEM_LIMIT)


class _Bf16Copies:
    def __init__(self, arrays, steps, step_of):
        self.arrays = list(arrays)
        self.specs, self.counts = [], []
        for a in self.arrays:
            rows, cols = a.shape
            if rows % (steps * BF16_TILE_ROWS) == 0:
                n, block = steps, (rows // steps, cols)
                index = lambda *g: (step_of(*g), 0)
            else:
                assert cols % LANE == 0
                n = max(k for k in range(1, steps + 1) if (cols // LANE) % k == 0)
                block = (rows, cols // n)
                index = lambda *g, n=n: (0, jnp.minimum(step_of(*g), n - 1))
            self.specs.append(pl.BlockSpec(block, index))
            self.counts.append(n)
        self.out_shapes = [jax.ShapeDtypeStruct(a.shape, BF16) for a in self.arrays]

    def run(self, step, srcs, dsts):
        for src, dst, n in zip(srcs, dsts, self.counts):
            @pl.when(step < n)
            def _():
                dst[...] = src[...].astype(dst.dtype)


def _resident(shape):
    return pl.BlockSpec(shape, lambda *_: (0,) * len(shape), pipeline_mode=pl.Buffered(1))


def _inproj_kernel(x_ref, lnw_ref, w_ref, o_ref, h_ref):
    @pl.when(pl.program_id(1) == 0)
    def _():
        h_ref[...] = _rms(x_ref[...], lnw_ref[...]).astype(BF16)

    o_ref[...] = _dot(h_ref[...], w_ref[...].astype(BF16)).astype(o_ref.dtype)


def _inproj(x, ln_w, w_in, *, bm, bn):
    s, d = x.shape
    d_in = w_in.shape[1]
    return pl.pallas_call(
        _inproj_kernel,
        grid=(s // bm, d_in // bn),
        in_specs=[
            pl.BlockSpec((bm, d), lambda m, n: (m, 0)),
            pl.BlockSpec((1, d), lambda m, n: (0, 0)),
            pl.BlockSpec((d, bn), lambda m, n: (0, n)),
        ],
        out_specs=pl.BlockSpec((bm, bn), lambda m, n: (m, n)),
        out_shape=jax.ShapeDtypeStruct((s, d_in), BF16),
        scratch_shapes=[pltpu.VMEM((bm, d), BF16)],
        compiler_params=_params("arbitrary", "arbitrary"),
        name="inproj",
    )(x, ln_w, w_in)


def _pool_kernel(u_ref, halo_ref, ga_ref, mix_ref, scale_ref, wa_ref, *rest, bm, copies):
    n_copy = len(copies.arrays)
    copy_in, (o_ref, *copy_out, feat_ref) = rest[:n_copy], rest[n_copy:]
    m = pl.program_id(0)
    n = pl.program_id(1)
    copies.run(m * pl.num_programs(1) + n, copy_in, copy_out)

    @pl.when(n == 0)
    def _():
        u = u_ref[...].astype(F32)
        halo = halo_ref[...].astype(F32) * (m > 0).astype(F32)
        ext = jnp.concatenate([halo, u], axis=0)
        gw = u.shape[1] // len(POOL_WINDOWS)
        pos = m * bm + lax.broadcasted_iota(jnp.int32, (bm, 1), 0) + 1
        run = ext
        width = 1
        feats = []
        for g, w in enumerate(POOL_WINDOWS):
            while width < w:
                run = run + pltpu.roll(run, width, axis=0)
                width *= 2
            cnt = jnp.minimum(pos, w).astype(F32)
            win = run[POOL_HALO:, :gw]
            if g + 1 < len(POOL_WINDOWS):
                run = run[:, gw:]
            d = win / cnt - u[:, g * gw:(g + 1) * gw]
            y = _dot(d.astype(BF16), mix_ref[g])
            feats.append(y * scale_ref[:, g * gw:(g + 1) * gw])
        feat_ref[...] = jnp.concatenate(feats, axis=1).astype(BF16)

    y_a = _dot(feat_ref[...], wa_ref[...])
    o_ref[...] = (jax.nn.sigmoid(ga_ref[...].astype(F32)) * y_a).astype(o_ref.dtype)


def _pool_branch(proj, mix_w, scale, w_a, to_bf16, *, bm, bn, ga_block0):
    s = proj.shape[0]
    c = w_a.shape[0]
    d = w_a.shape[1]
    g, gw, _ = mix_w.shape
    halo_per_tile = bm // POOL_HALO
    n_col = d // bn
    copies = _Bf16Copies(to_bf16, (s // bm) * n_col, lambda m, n: m * n_col + n)
    ya, *as_bf16 = pl.pallas_call(
        functools.partial(_pool_kernel, bm=bm, copies=copies),
        grid=(s // bm, n_col),
        in_specs=[
            pl.BlockSpec((bm, c), lambda m, n: (m, 0)),
            pl.BlockSpec((POOL_HALO, c), lambda m, n: (jnp.maximum(m * halo_per_tile - 1, 0), 0)),
            pl.BlockSpec((bm, bn), lambda m, n: (m, ga_block0 + n)),
            _resident((g, gw, gw)),
            _resident((1, c)),
            pl.BlockSpec((c, bn), lambda m, n: (0, n)),
            *copies.specs,
        ],
        out_specs=[pl.BlockSpec((bm, bn), lambda m, n: (m, n)), *copies.specs],
        out_shape=[jax.ShapeDtypeStruct((s, d), BF16), *copies.out_shapes],
        scratch_shapes=[pltpu.VMEM((bm, c), BF16)],
        compiler_params=_params("arbitrary", "arbitrary"),
        name="pool_branch",
    )(proj, proj, proj, mix_w, scale, w_a, *to_bf16)
    return ya, as_bf16


def _split2(x):
    hi = x.astype(BF16)
    return hi, (x - hi.astype(F32)).astype(BF16)


def _gates(fl, lb):
    one_m_lb = 1.0 - lb
    sg = jax.nn.sigmoid(fl)
    log2_f = jnp.log(lb + one_m_lb * sg) * LOG2_E
    return log2_f, one_m_lb * (1.0 - sg)


def _chunk_cumsum(tri, x):
    hi, lo = _split2(x)
    return _dot(tri, hi) + _dot(tri, lo)


def _head_norm_gate(o_heads, norm_w, g):
    o = jnp.concatenate(
        [o_h * lax.rsqrt(jnp.mean(o_h * o_h, axis=-1, keepdims=True) + EPS) for o_h in o_heads], axis=1)
    return o * norm_w * (g * jax.nn.sigmoid(g))


def _hgrn_chunk_any_decay(c, q_ref, f_ref, i_ref, g_ref, tri_ref, wsel_ref, o_ref, st_ref, lb, norm_w):
    r0 = pl.multiple_of(c * CHUNK, CHUNK)
    rows = pl.ds(r0, CHUNK)
    d = q_ref.shape[1]
    heads = d // HEAD_DIM

    q = q_ref[rows, :].astype(F32)
    v = i_ref[rows, :]
    log2_f, kk = _gates(f_ref[rows, :].astype(F32), lb)
    b = _chunk_cumsum(tri_ref[:CHUNK, :CHUNK], log2_f)
    b_last = b[CHUNK - 1:CHUNK, :]

    q_in = (q * jnp.exp2(b)).astype(BF16)
    k_st = (kk * jnp.exp2(b_last - b)).astype(BF16)

    b_end = jnp.concatenate(
        [jnp.broadcast_to(b[(j + 1) * SUB - 1:(j + 1) * SUB, :], (SUB, d)) for j in range(NSUB)], axis=0)
    k_hat = kk * jnp.exp2(b_end - b)
    sub_id = lax.broadcasted_iota(jnp.int32, (CHUNK, 1), 0) // SUB
    q_from = []
    for j in range(NSUB - 1):
        lo_row = (j + 1) * SUB
        bj = b[lo_row - 1:lo_row, :]
        part = q[lo_row:, :] * jnp.exp2(b[lo_row:, :] - bj)
        q_from.append(jnp.concatenate([jnp.zeros((lo_row, d), F32), part], axis=0).astype(BF16))
    k_src = [jnp.where(sub_id == j, k_hat, 0.0).astype(BF16) for j in range(NSUB - 1)]

    per_head = [[] for _ in range(heads)]
    for r in range(NSUB):
        qb = q[r * SUB:(r + 1) * SUB, :]
        bb = b[r * SUB:(r + 1) * SUB, :]
        pieces = []
        for s in range(SUB):
            row = r * SUB + s
            e = jnp.exp2(jnp.minimum(bb - b[row:row + 1, :], 0.0))
            pieces.append((qb * kk[row:row + 1, :] * e).astype(BF16))
        for h in range(heads):
            per_head[h].append(jnp.concatenate(
                [p[:, h * HEAD_DIM:(h + 1) * HEAD_DIM] for p in pieces], axis=1))
    a_big = jnp.concatenate([jnp.concatenate(blocks, axis=0) for blocks in per_head], axis=0)
    diag = _dot(a_big, wsel_ref[...])

    t_id = lax.broadcasted_iota(jnp.int32, (CHUNK, CHUNK), 0)
    s_id = lax.broadcasted_iota(jnp.int32, (CHUNK, CHUNK), 1)
    diag_mask = (t_id // SUB == s_id // SUB) & (s_id <= t_id)

    outs = []
    for h in range(heads):
        hs = slice(h * HEAD_DIM, (h + 1) * HEAD_DIM)
        q_cat = jnp.concatenate([qf[:, hs] for qf in q_from], axis=1)
        k_cat = jnp.concatenate([ks[:, hs] for ks in k_src], axis=1)
        scores = _dot_nt(q_cat, k_cat) + jnp.where(diag_mask, diag[h * CHUNK:(h + 1) * CHUNK, :], 0.0)
        v_h = v[:, hs]
        state_t = st_ref[h]
        o_h = _dot_nt(q_in[:, hs], state_t.astype(BF16)) + _dot(scores.astype(BF16), v_h)
        st_ref[h] = state_t * jnp.exp2(b_last[:, hs]) + _dot_tn(v_h, k_st[:, hs])
        outs.append(o_h)
    o_ref[rows, :] = _head_norm_gate(outs, norm_w, g_ref[rows, :].astype(F32)).astype(o_ref.dtype)


def _hgrn_tile_mild_decay(b, kk, q_ref, i_ref, g_ref, o_ref, st_ref, norm_w):
    bt, d = b.shape
    heads = d // HEAD_DIM
    ck = FAST_CHUNK
    n_chunks = bt // ck
    q = q_ref[...].astype(F32)
    v = i_ref[...]
    chunk_decay = [jnp.exp2(b[(c + 1) * ck - 1:(c + 1) * ck, :]) for c in range(n_chunks)]
    q_in = (q * jnp.exp2(b)).astype(BF16)
    k_undecayed = kk * jnp.exp2(-b)
    k_out = k_undecayed.astype(BF16)
    k_st = (k_undecayed * jnp.concatenate(
        [jnp.broadcast_to(r, (ck, d)) for r in chunk_decay], axis=0)).astype(BF16)
    causal = (lax.broadcasted_iota(jnp.int32, (ck, ck), 1) <= lax.broadcasted_iota(jnp.int32, (ck, ck), 0))
    states = [st_ref[h] for h in range(heads)]
    tile_out = []
    for c in range(n_chunks):
        rs = slice(c * ck, (c + 1) * ck)
        carry_decay = chunk_decay[c]
        outs = []
        for h in range(heads):
            hs = slice(h * HEAD_DIM, (h + 1) * HEAD_DIM)
            scores = jnp.where(causal, _dot_nt(q_in[rs, hs], k_out[rs, hs]), 0.0).astype(BF16)
            outs.append(_dot(scores, v[rs, hs]) + _dot_nt(q_in[rs, hs], states[h].astype(BF16)))
            states[h] = states[h] * carry_decay[:, hs] + _dot_tn(v[rs, hs], k_st[rs, hs])
        tile_out.append(jnp.concatenate(outs, axis=1))
    for h in range(heads):
        st_ref[h] = states[h]
    heads_out = jnp.concatenate(tile_out, axis=0)
    o_heads = [heads_out[:, h * HEAD_DIM:(h + 1) * HEAD_DIM] for h in range(heads)]
    o_ref[...] = _head_norm_gate(o_heads, norm_w, g_ref[...].astype(F32)).astype(o_ref.dtype)


def _hgrn_kernel(q_ref, f_ref, i_ref, g_ref, lbl_ref, nw_ref, tri_ref, wsel_ref, *rest, layer, copies):
    n_copy = len(copies.arrays)
    copy_in, (o_ref, *copy_out, st_ref) = rest[:n_copy], rest[n_copy:]
    step = pl.program_id(0)

    @pl.when(step == 0)
    def _():
        st_ref[...] = jnp.zeros_like(st_ref)

    copies.run(step, copy_in, copy_out)

    logits = lbl_ref[...]
    e = jnp.exp(logits - jnp.max(logits, axis=0, keepdims=True))
    lb = jnp.sum(e[:layer + 1, :], axis=0, keepdims=True) / jnp.sum(e, axis=0, keepdims=True)
    norm_w = nw_ref[...]

    log2_f, kk = _gates(f_ref[...].astype(F32), lb)
    b = _chunk_cumsum(tri_ref[...], log2_f)
    mild = jnp.min(b) >= MILD_DECAY_LOG2

    @pl.when(mild)
    def _():
        _hgrn_tile_mild_decay(b, kk, q_ref, i_ref, g_ref, o_ref, st_ref, norm_w)

    @pl.when(jnp.logical_not(mild))
    def _():
        def body(c, carry):
            _hgrn_chunk_any_decay(c, q_ref, f_ref, i_ref, g_ref, tri_ref, wsel_ref, o_ref, st_ref, lb, norm_w)
            return carry

        lax.fori_loop(0, q_ref.shape[0] // CHUNK, body, 0)


def _hgrn(proj, lb_logits, norm_w, to_bf16, *, bt, layer, q_block, f_block, i_block, g_block):
    s = proj.shape[0]
    d = norm_w.shape[1]
    heads = d // HEAD_DIM
    steps = s // bt
    copies = _Bf16Copies(to_bf16, steps, lambda t: t)
    t_id = lax.broadcasted_iota(jnp.int32, (bt, bt), 0)
    s_id = lax.broadcasted_iota(jnp.int32, (bt, bt), 1)
    tri = ((s_id <= t_id) & (s_id // FAST_CHUNK == t_id // FAST_CHUNK)).astype(BF16)
    src = lax.broadcasted_iota(jnp.int32, (SUB * HEAD_DIM, CHUNK), 0) // HEAD_DIM
    col = lax.broadcasted_iota(jnp.int32, (SUB * HEAD_DIM, CHUNK), 1) % SUB
    wsel = (src == col).astype(BF16)
    nl = lb_logits.shape[0]
    og, *as_bf16 = pl.pallas_call(
        functools.partial(_hgrn_kernel, layer=layer, copies=copies),
        grid=(steps,),
        in_specs=[
            pl.BlockSpec((bt, d), lambda t: (t, q_block)),
            pl.BlockSpec((bt, d), lambda t: (t, f_block)),
            pl.BlockSpec((bt, d), lambda t: (t, i_block)),
            pl.BlockSpec((bt, d), lambda t: (t, g_block)),
            _resident((nl, d)),
            _resident((1, d)),
            _resident((bt, bt)),
            _resident((SUB * HEAD_DIM, CHUNK)),
            *copies.specs,
        ],
        out_specs=[pl.BlockSpec((bt, d), lambda t: (t, 0)), *copies.specs],
        out_shape=[jax.ShapeDtypeStruct((s, d), BF16), *copies.out_shapes],
        scratch_shapes=[pltpu.VMEM((heads, HEAD_DIM, HEAD_DIM), F32)],
        compiler_params=_params("arbitrary"),
        name="hgrn",
    )(proj, proj, proj, proj, lb_logits, norm_w, tri, wsel, *to_bf16)
    return og, as_bf16


def _merge_kernel(og_ref, gb0_ref, gb1_ref, ya_ref, x_ref, wb_ref, wo_ref, *rest, copies):
    n_copy = len(copies.arrays)
    copy_in, (o_ref, *copy_out) = rest[:n_copy], rest[n_copy:]
    copies.run(pl.program_id(0), copy_in, copy_out)
    y_b = _dot(og_ref[...], wb_ref[...])
    gate_b = jnp.concatenate([gb0_ref[...], gb1_ref[...]], axis=1).astype(F32)
    merged = ya_ref[...].astype(F32) + jax.nn.sigmoid(gate_b) * y_b
    o_ref[...] = x_ref[...] + _dot(merged.astype(BF16), wo_ref[...])


def _merge(og, proj, ya, x, w_b, w_out, to_bf16, *, bm, gb_block0):
    s, d = x.shape
    c = og.shape[1]
    half = d // 2
    copies = _Bf16Copies(to_bf16, s // bm, lambda m: m)
    x_new, *as_bf16 = pl.pallas_call(
        functools.partial(_merge_kernel, copies=copies),
        grid=(s // bm,),
        in_specs=[
            pl.BlockSpec((bm, c), lambda m: (m, 0)),
            pl.BlockSpec((bm, half), lambda m: (m, gb_block0)),
            pl.BlockSpec((bm, half), lambda m: (m, gb_block0 + 1)),
            pl.BlockSpec((bm, d), lambda m: (m, 0)),
            pl.BlockSpec((bm, d), lambda m: (m, 0)),
            _resident((c, d)),
            _resident((d, d)),
            *copies.specs,
        ],
        out_specs=[pl.BlockSpec((bm, d), lambda m: (m, 0)), *copies.specs],
        out_shape=[jax.ShapeDtypeStruct((s, d), F32), *copies.out_shapes],
        compiler_params=_params("arbitrary"),
        name="merge_out",
    )(og, proj, proj, ya, x, w_b, w_out, *to_bf16)
    return x_new, as_bf16


def _causal_conv(z_ref, row0, cols, w_ref, b_ref):
    ext = z_ref[pl.ds(row0, CARRY_ROWS + ACT_ROWS), cols]
    y = b_ref[:, cols] + w_ref[CONV_WIDTH - 1:CONV_WIDTH, cols] * ext[CARRY_ROWS:, :]
    for back in range(1, CONV_WIDTH):
        tap = CONV_WIDTH - 1 - back
        y = y + w_ref[tap:tap + 1, cols] * pltpu.roll(ext, back, axis=0)[CARRY_ROWS:, :]
    return y


def _ffn_kernel(x_ref, lnw_ref, wug_ref, wuv_ref, cwg_ref, cwv_ref, cbg_ref, cbv_ref, wd_ref,
                o_ref, h_ref, zg_ref, zv_ref, a_ref, cg_ref, cv_ref, *, dff):
    m = pl.program_id(0)
    j = pl.program_id(1)
    bm = h_ref.shape[0]
    tf = wd_ref.shape[0]
    shared = j * tf - jnp.minimum(j * tf, dff - tf)

    @pl.when(j == 0)
    def _():
        x = x_ref[...]
        h_ref[...] = _rms(x, lnw_ref[...]).astype(BF16)
        o_ref[...] = x

    @pl.when(m == 0)
    def _():
        cg_ref[j] = jnp.zeros(cg_ref.shape[1:], F32)
        cv_ref[j] = jnp.zeros(cv_ref.shape[1:], F32)

    h = h_ref[...]
    zg_ref[:CARRY_ROWS, :] = cg_ref[j]
    zv_ref[:CARRY_ROWS, :] = cv_ref[j]
    zg_ref[CARRY_ROWS:, :] = _dot(h, wug_ref[...])
    zv_ref[CARRY_ROWS:, :] = _dot(h, wuv_ref[...])
    cg_ref[j] = zg_ref[bm:, :]
    cv_ref[j] = zv_ref[bm:, :]

    for c in range(tf // LANE):
        cols = slice(c * LANE, (c + 1) * LANE)
        fresh = (lax.broadcasted_iota(jnp.int32, (1, LANE), 1) + c * LANE) >= shared
        for r in range(bm // ACT_ROWS):
            ug = _causal_conv(zg_ref, r * ACT_ROWS, cols, cwg_ref, cbg_ref)
            uv = _causal_conv(zv_ref, r * ACT_ROWS, cols, cwv_ref, cbv_ref)
            a_ref[r * ACT_ROWS:(r + 1) * ACT_ROWS, cols] = jnp.where(
                fresh, ug * jax.nn.sigmoid(ug) * uv, 0.0).astype(BF16)

    o_ref[...] += _dot(a_ref[...], wd_ref[...])


def _ffn(x, ln_w, w_up, conv_w, conv_b, w_down, *, bm, tf):
    s, d = x.shape
    dff = w_down.shape[0]
    nj = pl.cdiv(dff, tf)
    assert dff >= tf and dff % LANE == 0

    def start(j, base=0):
        return (base // LANE + jnp.minimum(j * (tf // LANE), (dff - tf) // LANE)) * LANE

    def window(rows, offset):
        return pl.BlockSpec((pl.Element(rows), pl.Element(tf)), lambda m, j: (0, offset(j)))

    return pl.pallas_call(
        functools.partial(_ffn_kernel, dff=dff),
        grid=(s // bm, nj),
        in_specs=[
            pl.BlockSpec((bm, d), lambda m, j: (m, 0)),
            pl.BlockSpec((1, d), lambda m, j: (0, 0)),
            window(d, start),
            window(d, lambda j: start(j, dff)),
            window(CONV_WIDTH, start),
            window(CONV_WIDTH, lambda j: start(j, dff)),
            window(1, start),
            window(1, lambda j: start(j, dff)),
            pl.BlockSpec((pl.Element(tf), pl.Element(d)), lambda m, j: (start(j), 0)),
        ],
        out_specs=pl.BlockSpec((bm, d), lambda m, j: (m, 0)),
        out_shape=jax.ShapeDtypeStruct((s, d), F32),
        scratch_shapes=[
            pltpu.VMEM((bm, d), BF16),
            pltpu.VMEM((CARRY_ROWS + bm, tf), F32),
            pltpu.VMEM((CARRY_ROWS + bm, tf), F32),
            pltpu.VMEM((bm, tf), BF16),
            pltpu.VMEM((nj, CARRY_ROWS, tf), F32),
            pltpu.VMEM((nj, CARRY_ROWS, tf), F32),
        ],
        compiler_params=_params("arbitrary", "arbitrary"),
        name="ffn",
    )(x, ln_w, w_up, w_up, conv_w, conv_w, conv_b, conv_b, w_down)


def _ple_kernel(x_ref, p_ref, lnw_ref, wg_ref, wp_ref, lnf_ref, o_ref, *, final):
    x = x_ref[...]
    h = _rms(x, lnw_ref[...]).astype(BF16)
    gate = jax.nn.sigmoid(_dot(h, wg_ref[...]))
    emb = _dot(p_ref[...].astype(BF16), wp_ref[...])
    y = x + gate * emb
    o_ref[...] = _rms(y, lnf_ref[...]) if final else y


def _ple(x, p, ln_w, w_gate, w_ple, ln_final, *, bm, final):
    s, d = x.shape
    e = p.shape[1]
    return pl.pallas_call(
        functools.partial(_ple_kernel, final=final),
        grid=(s // bm,),
        in_specs=[
            pl.BlockSpec((bm, d), lambda m: (m, 0)),
            pl.BlockSpec((bm, e), lambda m: (m, 0)),
            _resident((1, d)),
            _resident((d, d)),
            _resident((e, d)),
            _resident((1, d)),
        ],
        out_specs=pl.BlockSpec((bm, d), lambda m: (m, 0)),
        out_shape=jax.ShapeDtypeStruct((s, d), F32),
        compiler_params=_params("arbitrary"),
        name="ple_final",
    )(x, p, ln_w, w_gate, w_ple, ln_final)


def kernel(x, p, ln_mix_w, w_in, pool_mix_w, pool_scale, hgrn_lb_logits, hgrn_norm_w, w_branch_a,
           w_branch_b, w_out, ln_ffn_w, w_up, conv_w, conv_b, w_down, ln_ple_w, w_ple_gate, w_ple,
           ln_final_w):
    batch, seq, d = x.shape
    depth = w_in.shape[0]
    c_pool = w_branch_a.shape[1]
    c_hgrn = w_branch_b.shape[1]
    blk = c_hgrn
    assert c_pool == blk and d == 2 * blk, "column blocks of the combined projection must line up"
    assert seq % 1024 == 0
    tf = 512
    outs = []
    for bi in range(batch):
        xb = x[bi]
        for i in range(depth):
            proj = _inproj(xb, ln_mix_w[i][None], w_in[i], bm=1024, bn=blk)
            mix = pool_mix_w[i]
            og, (mix_b, wa_b, wu_b) = _hgrn(
                proj, hgrn_lb_logits, hgrn_norm_w[i][None],
                [mix.reshape(-1, mix.shape[-1]), w_branch_a[i], w_up[i]],
                bt=512, layer=i, q_block=1, f_block=2, i_block=3, g_block=4)
            ya, (wb_b, wo_b, wg_b, wp_b) = _pool_branch(
                proj, mix_b.reshape(mix.shape), pool_scale[i][None], wa_b,
                [w_branch_b[i], w_out[i], w_ple_gate[i], w_ple[i]], bm=1024, bn=blk, ga_block0=5)
            xb, (wd_b,) = _merge(og, proj, ya, xb, wb_b, wo_b, [w_down[i]], bm=512, gb_block0=7)
            xb = _ffn(xb, ln_ffn_w[i][None], wu_b, conv_w[i], conv_b[i][None], wd_b, bm=1024, tf=512)
            xb = _ple(xb, p[i, bi], ln_ple_w[i][None], wg_b, wp_b, ln_final_w[None], bm=1024,
                      final=(i == depth - 1))
        outs.append(xb)
    return jnp.stack(outs, axis=0)
```

```python
import functools

import jax
import jax.numpy as jnp
from jax import lax
from jax.experimental import pallas as pl
from jax.experimental.pallas import tpu as pltpu

F32 = jnp.float32
BF16 = jnp.bfloat16

EPS = 1e-6
POOL_WINDOWS = (2, 4, 8, 16)
POOL_HALO = 16
HEAD_DIM = 128
CHUNK = 64
FAST_CHUNK = 128
SUB = 16
NSUB = CHUNK // SUB
CONV_WIDTH = 3
CARRY_ROWS = 8
ACT_ROWS = 64
LANE = 128
BF16_TILE_ROWS = 16
LOG2_E = 1.4426950408889634
MILD_DECAY_LOG2 = -100.0
V7X_VMEM_BYTES = 64 * 1024 * 1024
VMEM_LIMIT = V7X_VMEM_BYTES - 8 * 1024 * 1024


def _dot(a, b):
    return jnp.dot(a, b, preferred_element_type=F32)


def _dot_nt(a, b):
    return lax.dot_general(a, b, (((1,), (1,)), ((), ())), preferred_element_type=F32)


def _dot_tn(a, b):
    return lax.dot_general(a, b, (((0,), (0,)), ((), ())), preferred_element_type=F32)


def _rms(x, w):
    return x * lax.rsqrt(jnp.mean(x * x, axis=-1, keepdims=True) + EPS) * w


def _params(*sem):
    return pltpu.CompilerParams(dimension_semantics=sem, vmem_limit_bytes=VMEM_LIMIT)


class _Bf16Copies:
    def __init__(self, arrays, steps, step_of):
        self.arrays = list(arrays)
        self.specs, self.counts = [], []
        for a in self.arrays:
            rows, cols = a.shape
            if rows % (steps * BF16_TILE_ROWS) == 0:
                n, block = steps, (rows // steps, cols)
                index = lambda *g: (step_of(*g), 0)
            else:
                assert cols % LANE == 0
                n = max(k for k in range(1, steps + 1) if (cols // LANE) % k == 0)
                block = (rows, cols // n)
                index = lambda *g, n=n: (0, jnp.minimum(step_of(*g), n - 1))
            self.specs.append(pl.BlockSpec(block, index))
            self.counts.append(n)
        self.out_shapes = [jax.ShapeDtypeStruct(a.shape, BF16) for a in self.arrays]

    def run(self, step, srcs, dsts):
        for src, dst, n in zip(srcs, dsts, self.counts):
            @pl.when(step < n)
            def _():
                dst[...] = src[...].astype(dst.dtype)


def _resident(shape):
    return pl.BlockSpec(shape, lambda *_: (0,) * len(shape), pipeline_mode=pl.Buffered(1))


def _inproj_kernel(x_ref, lnw_ref, w_ref, o_ref, h_ref):
    @pl.when(pl.program_id(1) == 0)
    def _():
        h_ref[...] = _rms(x_ref[...], lnw_ref[...]).astype(BF16)

    o_ref[...] = _dot(h_ref[...], w_ref[...].astype(BF16)).astype(o_ref.dtype)


def _inproj(x, ln_w, w_in, *, bm, bn):
    s, d = x.shape
    d_in = w_in.shape[1]
    return pl.pallas_call(
        _inproj_kernel,
        grid=(s // bm, d_in // bn),
        in_specs=[
            pl.BlockSpec((bm, d), lambda m, n: (m, 0)),
            pl.BlockSpec((1, d), lambda m, n: (0, 0)),
            pl.BlockSpec((d, bn), lambda m, n: (0, n)),
        ],
        out_specs=pl.BlockSpec((bm, bn), lambda m, n: (m, n)),
        out_shape=jax.ShapeDtypeStruct((s, d_in), BF16),
        scratch_shapes=[pltpu.VMEM((bm, d), BF16)],
        compiler_params=_params("arbitrary", "arbitrary"),
        name="inproj",
    )(x, ln_w, w_in)


def _pool_kernel(u_ref, halo_ref, ga_ref, mix_ref, scale_ref, wa_ref, *rest, bm, copies):
    n_copy = len(copies.arrays)
    copy_in, (o_ref, *copy_out, feat_ref) = rest[:n_copy], rest[n_copy:]
    m = pl.program_id(0)
    n = pl.program_id(1)
    copies.run(m * pl.num_programs(1) + n, copy_in, copy_out)

    @pl.when(n == 0)
    def _():
        u = u_ref[...].astype(F32)
        halo = halo_ref[...].astype(F32) * (m > 0).astype(F32)
        ext = jnp.concatenate([halo, u], axis=0)
        gw = u.shape[1] // len(POOL_WINDOWS)
        pos = m * bm + lax.broadcasted_iota(jnp.int32, (bm, 1), 0) + 1
        run = ext
        width = 1
        feats = []
        for g, w in enumerate(POOL_WINDOWS):
            while width < w:
                run = run + pltpu.roll(run, width, axis=0)
                width *= 2
            cnt = jnp.minimum(pos, w).astype(F32)
            win = run[POOL_HALO:, :gw]
            if g + 1 < len(POOL_WINDOWS):
                run = run[:, gw:]
            d = win / cnt - u[:, g * gw:(g + 1) * gw]
            y = _dot(d.astype(BF16), mix_ref[g])
            feats.append(y * scale_ref[:, g * gw:(g + 1) * gw])
        feat_ref[...] = jnp.concatenate(feats, axis=1).astype(BF16)

    y_a = _dot(feat_ref[...], wa_ref[...])
    o_ref[...] = (jax.nn.sigmoid(ga_ref[...].astype(F32)) * y_a).astype(o_ref.dtype)


def _pool_branch(proj, mix_w, scale, w_a, to_bf16, *, bm, bn, ga_block0):
    s = proj.shape[0]
    c = w_a.shape[0]
    d = w_a.shape[1]
    g, gw, _ = mix_w.shape
    halo_per_tile = bm // POOL_HALO
    n_col = d // bn
    copies = _Bf16Copies(to_bf16, (s // bm) * n_col, lambda m, n: m * n_col + n)
    ya, *as_bf16 = pl.pallas_call(
        functools.partial(_pool_kernel, bm=bm, copies=copies),
        grid=(s // bm, n_col),
        in_specs=[
            pl.BlockSpec((bm, c), lambda m, n: (m, 0)),
            pl.BlockSpec((POOL_HALO, c), lambda m, n: (jnp.maximum(m * halo_per_tile - 1, 0), 0)),
            pl.BlockSpec((bm, bn), lambda m, n: (m, ga_block0 + n)),
            _resident((g, gw, gw)),
            _resident((1, c)),
            pl.BlockSpec((c, bn), lambda m, n: (0, n)),
            *copies.specs,
        ],
        out_specs=[pl.BlockSpec((bm, bn), lambda m, n: (m, n)), *copies.specs],
        out_shape=[jax.ShapeDtypeStruct((s, d), BF16), *copies.out_shapes],
        scratch_shapes=[pltpu.VMEM((bm, c), BF16)],
        compiler_params=_params("arbitrary", "arbitrary"),
        name="pool_branch",
    )(proj, proj, proj, mix_w, scale, w_a, *to_bf16)
    return ya, as_bf16


def _split2(x):
    hi = x.astype(BF16)
    return hi, (x - hi.astype(F32)).astype(BF16)


def _gates(fl, lb):
    one_m_lb = 1.0 - lb
    sg = jax.nn.sigmoid(fl)
    log2_f = jnp.log(lb + one_m_lb * sg) * LOG2_E
    return log2_f, one_m_lb * (1.0 - sg)


def _chunk_cumsum(tri, x):
    hi, lo = _split2(x)
    return _dot(tri, hi) + _dot(tri, lo)


def _head_norm_gate(o_heads, norm_w, g):
    o = jnp.concatenate(
        [o_h * lax.rsqrt(jnp.mean(o_h * o_h, axis=-1, keepdims=True) + EPS) for o_h in o_heads], axis=1)
    return o * norm_w * (g * jax.nn.sigmoid(g))


def _hgrn_chunk_any_decay(c, q_ref, f_ref, i_ref, g_ref, tri_ref, wsel_ref, o_ref, st_ref, lb, norm_w):
    r0 = pl.multiple_of(c * CHUNK, CHUNK)
    rows = pl.ds(r0, CHUNK)
    d = q_ref.shape[1]
    heads = d // HEAD_DIM

    q = q_ref[rows, :].astype(F32)
    v = i_ref[rows, :]
    log2_f, kk = _gates(f_ref[rows, :].astype(F32), lb)
    b = _chunk_cumsum(tri_ref[:CHUNK, :CHUNK], log2_f)
    b_last = b[CHUNK - 1:CHUNK, :]

    q_in = (q * jnp.exp2(b)).astype(BF16)
    k_st = (kk * jnp.exp2(b_last - b)).astype(BF16)

    b_end = jnp.concatenate(
        [jnp.broadcast_to(b[(j + 1) * SUB - 1:(j + 1) * SUB, :], (SUB, d)) for j in range(NSUB)], axis=0)
    k_hat = kk * jnp.exp2(b_end - b)
    sub_id = lax.broadcasted_iota(jnp.int32, (CHUNK, 1), 0) // SUB
    q_from = []
    for j in range(NSUB - 1):
        lo_row = (j + 1) * SUB
        bj = b[lo_row - 1:lo_row, :]
        part = q[lo_row:, :] * jnp.exp2(b[lo_row:, :] - bj)
        q_from.append(jnp.concatenate([jnp.zeros((lo_row, d), F32), part], axis=0).astype(BF16))
    k_src = [jnp.where(sub_id == j, k_hat, 0.0).astype(BF16) for j in range(NSUB - 1)]

    per_head = [[] for _ in range(heads)]
    for r in range(NSUB):
        qb = q[r * SUB:(r + 1) * SUB, :]
        bb = b[r * SUB:(r + 1) * SUB, :]
        pieces = []
        for s in range(SUB):
            row = r * SUB + s
            e = jnp.exp2(jnp.minimum(bb - b[row:row + 1, :], 0.0))
            pieces.append((qb * kk[row:row + 1, :] * e).astype(BF16))
        for h in range(heads):
            per_head[h].append(jnp.concatenate(
                [p[:, h * HEAD_DIM:(h + 1) * HEAD_DIM] for p in pieces], axis=1))
    a_big = jnp.concatenate([jnp.concatenate(blocks, axis=0) for blocks in per_head], axis=0)
    diag = _dot(a_big, wsel_ref[...])

    t_id = lax.broadcasted_iota(jnp.int32, (CHUNK, CHUNK), 0)
    s_id = lax.broadcasted_iota(jnp.int32, (CHUNK, CHUNK), 1)
    diag_mask = (t_id // SUB == s_id // SUB) & (s_id <= t_id)

    outs = []
    for h in range(heads):
        hs = slice(h * HEAD_DIM, (h + 1) * HEAD_DIM)
        q_cat = jnp.concatenate([qf[:, hs] for qf in q_from], axis=1)
        k_cat = jnp.concatenate([ks[:, hs] for ks in k_src], axis=1)
        scores = _dot_nt(q_cat, k_cat) + jnp.where(diag_mask, diag[h * CHUNK:(h + 1) * CHUNK, :], 0.0)
        v_h = v[:, hs]
        state_t = st_ref[h]
        o_h = _dot_nt(q_in[:, hs], state_t.astype(BF16)) + _dot(scores.astype(BF16), v_h)
        st_ref[h] = state_t * jnp.exp2(b_last[:, hs]) + _dot_tn(v_h, k_st[:, hs])
        outs.append(o_h)
    o_ref[rows, :] = _head_norm_gate(outs, norm_w, g_ref[rows, :].astype(F32)).astype(o_ref.dtype)


def _hgrn_tile_mild_decay(b, kk, q_ref, i_ref, g_ref, o_ref, st_ref, norm_w):
    bt, d = b.shape
    heads = d // HEAD_DIM
    ck = FAST_CHUNK
    n_chunks = bt // ck
    q = q_ref[...].astype(F32)
    v = i_ref[...]
    chunk_decay = [jnp.exp2(b[(c + 1) * ck - 1:(c + 1) * ck, :]) for c in range(n_chunks)]
    q_in = (q * jnp.exp2(b)).astype(BF16)
    k_undecayed = kk * jnp.exp2(-b)
    k_out = k_undecayed.astype(BF16)
    k_st = (k_undecayed * jnp.concatenate(
        [jnp.broadcast_to(r, (ck, d)) for r in chunk_decay], axis=0)).astype(BF16)
    causal = (lax.broadcasted_iota(jnp.int32, (ck, ck), 1) <= lax.broadcasted_iota(jnp.int32, (ck, ck), 0))
    states = [st_ref[h] for h in range(heads)]
    tile_out = []
    for c in range(n_chunks):
        rs = slice(c * ck, (c + 1) * ck)
        carry_decay = chunk_decay[c]
        outs = []
        for h in range(heads):
            hs = slice(h * HEAD_DIM, (h + 1) * HEAD_DIM)
            scores = jnp.where(causal, _dot_nt(q_in[rs, hs], k_out[rs, hs]), 0.0).astype(BF16)
            outs.append(_dot(scores, v[rs, hs]) + _dot_nt(q_in[rs, hs], states[h].astype(BF16)))
            states[h] = states[h] * carry_decay[:, hs] + _dot_tn(v[rs, hs], k_st[rs, hs])
        tile_out.append(jnp.concatenate(outs, axis=1))
    for h in range(heads):
        st_ref[h] = states[h]
    heads_out = jnp.concatenate(tile_out, axis=0)
    o_heads = [heads_out[:, h * HEAD_DIM:(h + 1) * HEAD_DIM] for h in range(heads)]
    o_ref[...] = _head_norm_gate(o_heads, norm_w, g_ref[...].astype(F32)).astype(o_ref.dtype)


def _hgrn_kernel(q_ref, f_ref, i_ref, g_ref, lbl_ref, nw_ref, tri_ref, wsel_ref, *rest, layer, copies):
    n_copy = len(copies.arrays)
    copy_in, (o_ref, *copy_out, st_ref) = rest[:n_copy], rest[n_copy:]
    step = pl.program_id(0)

    @pl.when(step == 0)
    def _():
        st_ref[...] = jnp.zeros_like(st_ref)

    copies.run(step, copy_in, copy_out)

    logits = lbl_ref[...]
    e = jnp.exp(logits - jnp.max(logits, axis=0, keepdims=True))
    lb = jnp.sum(e[:layer + 1, :], axis=0, keepdims=True) / jnp.sum(e, axis=0, keepdims=True)
    norm_w = nw_ref[...]

    log2_f, kk = _gates(f_ref[...].astype(F32), lb)
    b = _chunk_cumsum(tri_ref[...], log2_f)
    mild = jnp.min(b) >= MILD_DECAY_LOG2

    @pl.when(mild)
    def _():
        _hgrn_tile_mild_decay(b, kk, q_ref, i_ref, g_ref, o_ref, st_ref, norm_w)

    @pl.when(jnp.logical_not(mild))
    def _():
        def body(c, carry):
            _hgrn_chunk_any_decay(c, q_ref, f_ref, i_ref, g_ref, tri_ref, wsel_ref, o_ref, st_ref, lb, norm_w)
            return carry

        lax.fori_loop(0, q_ref.shape[0] // CHUNK, body, 0)


def _hgrn(proj, lb_logits, norm_w, to_bf16, *, bt, layer, q_block, f_block, i_block, g_block):
    s = proj.shape[0]
    d = norm_w.shape[1]
    heads = d // HEAD_DIM
    steps = s // bt
    copies = _Bf16Copies(to_bf16, steps, lambda t: t)
    t_id = lax.broadcasted_iota(jnp.int32, (bt, bt), 0)
    s_id = lax.broadcasted_iota(jnp.int32, (bt, bt), 1)
    tri = ((s_id <= t_id) & (s_id // FAST_CHUNK == t_id // FAST_CHUNK)).astype(BF16)
    src = lax.broadcasted_iota(jnp.int32, (SUB * HEAD_DIM, CHUNK), 0) // HEAD_DIM
    col = lax.broadcasted_iota(jnp.int32, (SUB * HEAD_DIM, CHUNK), 1) % SUB
    wsel = (src == col).astype(BF16)
    nl = lb_logits.shape[0]
    og, *as_bf16 = pl.pallas_call(
        functools.partial(_hgrn_kernel, layer=layer, copies=copies),
        grid=(steps,),
        in_specs=[
            pl.BlockSpec((bt, d), lambda t: (t, q_block)),
            pl.BlockSpec((bt, d), lambda t: (t, f_block)),
            pl.BlockSpec((bt, d), lambda t: (t, i_block)),
            pl.BlockSpec((bt, d), lambda t: (t, g_block)),
            _resident((nl, d)),
            _resident((1, d)),
            _resident((bt, bt)),
            _resident((SUB * HEAD_DIM, CHUNK)),
            *copies.specs,
        ],
        out_specs=[pl.BlockSpec((bt, d), lambda t: (t, 0)), *copies.specs],
        out_shape=[jax.ShapeDtypeStruct((s, d), BF16), *copies.out_shapes],
        scratch_shapes=[pltpu.VMEM((heads, HEAD_DIM, HEAD_DIM), F32)],
        compiler_params=_params("arbitrary"),
        name="hgrn",
    )(proj, proj, proj, proj, lb_logits, norm_w, tri, wsel, *to_bf16)
    return og, as_bf16


def _merge_kernel(og_ref, gb0_ref, gb1_ref, ya_ref, x_ref, wb_ref, wo_ref, *rest, copies):
    n_copy = len(copies.arrays)
    copy_in, (o_ref, *copy_out) = rest[:n_copy], rest[n_copy:]
    copies.run(pl.program_id(0), copy_in, copy_out)
    y_b = _dot(og_ref[...], wb_ref[...])
    gate_b = jnp.concatenate([gb0_ref[...], gb1_ref[...]], axis=1).astype(F32)
    merged = ya_ref[...].astype(F32) + jax.nn.sigmoid(gate_b) * y_b
    o_ref[...] = x_ref[...] + _dot(merged.astype(BF16), wo_ref[...])


def _merge(og, proj, ya, x, w_b, w_out, to_bf16, *, bm, gb_block0):
    s, d = x.shape
    c = og.shape[1]
    half = d // 2
    copies = _Bf16Copies(to_bf16, s // bm, lambda m: m)
    x_new, *as_bf16 = pl.pallas_call(
        functools.partial(_merge_kernel, copies=copies),
        grid=(s // bm,),
        in_specs=[
            pl.BlockSpec((bm, c), lambda m: (m, 0)),
            pl.BlockSpec((bm, half), lambda m: (m, gb_block0)),
            pl.BlockSpec((bm, half), lambda m: (m, gb_block0 + 1)),
            pl.BlockSpec((bm, d), lambda m: (m, 0)),
            pl.BlockSpec((bm, d), lambda m: (m, 0)),
            _resident((c, d)),
            _resident((d, d)),
            *copies.specs,
        ],
        out_specs=[pl.BlockSpec((bm, d), lambda m: (m, 0)), *copies.specs],
        out_shape=[jax.ShapeDtypeStruct((s, d), F32), *copies.out_shapes],
        compiler_params=_params("arbitrary"),
        name="merge_out",
    )(og, proj, proj, ya, x, w_b, w_out, *to_bf16)
    return x_new, as_bf16


def _causal_conv(z_ref, row0, cols, w_ref, b_ref):
    ext = z_ref[pl.ds(row0, CARRY_ROWS + ACT_ROWS), cols]
    y = b_ref[:, cols] + w_ref[CONV_WIDTH - 1:CONV_WIDTH, cols] * ext[CARRY_ROWS:, :]
    for back in range(1, CONV_WIDTH):
        tap = CONV_WIDTH - 1 - back
        y = y + w_ref[tap:tap + 1, cols] * pltpu.roll(ext, back, axis=0)[CARRY_ROWS:, :]
    return y


def _ffn_kernel(x_ref, lnw_ref, wug_ref, wuv_ref, cwg_ref, cwv_ref, cbg_ref, cbv_ref, wd_ref,
                o_ref, h_ref, zg_ref, zv_ref, a_ref, cg_ref, cv_ref, *, dff):
    m = pl.program_id(0)
    j = pl.program_id(1)
    bm = h_ref.shape[0]
    tf = wd_ref.shape[0]
    shared = j * tf - jnp.minimum(j * tf, dff - tf)

    @pl.when(j == 0)
    def _():
        x = x_ref[...]
        h_ref[...] = _rms(x, lnw_ref[...]).astype(BF16)
        o_ref[...] = x

    @pl.when(m == 0)
    def _():
        cg_ref[j] = jnp.zeros(cg_ref.shape[1:], F32)
        cv_ref[j] = jnp.zeros(cv_ref.shape[1:], F32)

    h = h_ref[...]
    zg_ref[:CARRY_ROWS, :] = cg_ref[j]
    zv_ref[:CARRY_ROWS, :] = cv_ref[j]
    zg_ref[CARRY_ROWS:, :] = _dot(h, wug_ref[...])
    zv_ref[CARRY_ROWS:, :] = _dot(h, wuv_ref[...])
    cg_ref[j] = zg_ref[bm:, :]
    cv_ref[j] = zv_ref[bm:, :]

    for c in range(tf // LANE):
        cols = slice(c * LANE, (c + 1) * LANE)
        fresh = (lax.broadcasted_iota(jnp.int32, (1, LANE), 1) + c * LANE) >= shared
        for r in range(bm // ACT_ROWS):
            ug = _causal_conv(zg_ref, r * ACT_ROWS, cols, cwg_ref, cbg_ref)
            uv = _causal_conv(zv_ref, r * ACT_ROWS, cols, cwv_ref, cbv_ref)
            a_ref[r * ACT_ROWS:(r + 1) * ACT_ROWS, cols] = jnp.where(
                fresh, ug * jax.nn.sigmoid(ug) * uv, 0.0).astype(BF16)

    o_ref[...] += _dot(a_ref[...], wd_ref[...])


def _ffn(x, ln_w, w_up, conv_w, conv_b, w_down, *, bm, tf):
    s, d = x.shape
    dff = w_down.shape[0]
    nj = pl.cdiv(dff, tf)
    assert dff >= tf and dff % LANE == 0

    def start(j, base=0):
        return (base // LANE + jnp.minimum(j * (tf // LANE), (dff - tf) // LANE)) * LANE

    def window(rows, offset):
        return pl.BlockSpec((pl.Element(rows), pl.Element(tf)), lambda m, j: (0, offset(j)))

    return pl.pallas_call(
        functools.partial(_ffn_kernel, dff=dff),
        grid=(s // bm, nj),
        in_specs=[
            pl.BlockSpec((bm, d), lambda m, j: (m, 0)),
            pl.BlockSpec((1, d), lambda m, j: (0, 0)),
            window(d, start),
            window(d, lambda j: start(j, dff)),
            window(CONV_WIDTH, start),
            window(CONV_WIDTH, lambda j: start(j, dff)),
            window(1, start),
            window(1, lambda j: start(j, dff)),
            pl.BlockSpec((pl.Element(tf), pl.Element(d)), lambda m, j: (start(j), 0)),
        ],
        out_specs=pl.BlockSpec((bm, d), lambda m, j: (m, 0)),
        out_shape=jax.ShapeDtypeStruct((s, d), F32),
        scratch_shapes=[
            pltpu.VMEM((bm, d), BF16),
            pltpu.VMEM((CARRY_ROWS + bm, tf), F32),
            pltpu.VMEM((CARRY_ROWS + bm, tf), F32),
            pltpu.VMEM((bm, tf), BF16),
            pltpu.VMEM((nj, CARRY_ROWS, tf), F32),
            pltpu.VMEM((nj, CARRY_ROWS, tf), F32),
        ],
        compiler_params=_params("arbitrary", "arbitrary"),
        name="ffn",
    )(x, ln_w, w_up, w_up, conv_w, conv_w, conv_b, conv_b, w_down)


def _ple_kernel(x_ref, p_ref, lnw_ref, wg_ref, wp_ref, lnf_ref, o_ref, *, final):
    x = x_ref[...]
    h = _rms(x, lnw_ref[...]).astype(BF16)
    gate = jax.nn.sigmoid(_dot(h, wg_ref[...]))
    emb = _dot(p_ref[...].astype(BF16), wp_ref[...])
    y = x + gate * emb
    o_ref[...] = _rms(y, lnf_ref[...]) if final else y


def _ple(x, p, ln_w, w_gate, w_ple, ln_final, *, bm, final):
    s, d = x.shape
    e = p.shape[1]
    return pl.pallas_call(
        functools.partial(_ple_kernel, final=final),
        grid=(s // bm,),
        in_specs=[
            pl.BlockSpec((bm, d), lambda m: (m, 0)),
            pl.BlockSpec((bm, e), lambda m: (m, 0)),
            _resident((1, d)),
            _resident((d, d)),
            _resident((e, d)),
            _resident((1, d)),
        ],
        out_specs=pl.BlockSpec((bm, d), lambda m: (m, 0)),
        out_shape=jax.ShapeDtypeStruct((s, d), F32),
        compiler_params=_params("arbitrary"),
        name="ple_final",
    )(x, p, ln_w, w_gate, w_ple, ln_final)


def kernel(x, p, ln_mix_w, w_in, pool_mix_w, pool_scale, hgrn_lb_logits, hgrn_norm_w, w_branch_a,
           w_branch_b, w_out, ln_ffn_w, w_up, conv_w, conv_b, w_down, ln_ple_w, w_ple_gate, w_ple,
           ln_final_w):
    batch, seq, d = x.shape
    depth = w_in.shape[0]
    c_pool = w_branch_a.shape[1]
    c_hgrn = w_branch_b.shape[1]
    blk = c_hgrn
    assert c_pool == blk and d == 2 * blk, "column blocks of the combined projection must line up"
    assert seq % 1024 == 0
    tf = 512
    outs = []
    for bi in range(batch):
        xb = x[bi]
        for i in range(depth):
            proj = _inproj(xb, ln_mix_w[i][None], w_in[i], bm=1024, bn=blk)
            mix = pool_mix_w[i]
            og, (mix_b, wa_b, wu_b) = _hgrn(
                proj, hgrn_lb_logits, hgrn_norm_w[i][None],
                [mix.reshape(-1, mix.shape[-1]), w_branch_a[i], w_up[i]],
                bt=512, layer=i, q_block=1, f_block=2, i_block=3, g_block=4)
            ya, (wb_b, wo_b) = _pool_branch(
                proj, mix_b.reshape(mix.shape), pool_scale[i][None], wa_b,
                [w_branch_b[i], w_out[i]], bm=1024, bn=blk, ga_block0=5)
            xb, (wd_b, wg_b, wp_b) = _merge(og, proj, ya, xb, wb_b, wo_b,
                                            [w_down[i], w_ple_gate[i], w_ple[i]], bm=512, gb_block0=7)
            xb = _ffn(xb, ln_ffn_w[i][None], wu_b, conv_w[i], conv_b[i][None], wd_b, bm=1024, tf=512)
            xb = _ple(xb, p[i, bi], ln_ple_w[i][None], wg_b, wp_b, ln_final_w[None], bm=1024,
                      final=(i == depth - 1))
        outs.append(xb)
    return jnp.stack(outs, axis=0)
```

```python
import functools

import jax
import jax.numpy as jnp
from jax import lax
from jax.experimental import pallas as pl
from jax.experimental.pallas import tpu as pltpu

F32 = jnp.float32
BF16 = jnp.bfloat16

EPS = 1e-6
POOL_WINDOWS = (2, 4, 8, 16)
POOL_HALO = 16
HEAD_DIM = 128
CHUNK = 64
FAST_CHUNK = 128
SUB = 16
NSUB = CHUNK // SUB
CONV_WIDTH = 3
CARRY_ROWS = 8
ACT_ROWS = 64
LANE = 128
BF16_TILE_ROWS = 16
LOG2_E = 1.4426950408889634
MILD_DECAY_LOG2 = -100.0
V7X_VMEM_BYTES = 64 * 1024 * 1024
VMEM_LIMIT = V7X_VMEM_BYTES - 8 * 1024 * 1024


def _dot(a, b):
    return jnp.dot(a, b, preferred_element_type=F32)


def _dot_nt(a, b):
    return lax.dot_general(a, b, (((1,), (1,)), ((), ())), preferred_element_type=F32)


def _dot_tn(a, b):
    return lax.dot_general(a, b, (((0,), (0,)), ((), ())), preferred_element_type=F32)


def _rms(x, w):
    return x * lax.rsqrt(jnp.mean(x * x, axis=-1, keepdims=True) + EPS) * w


def _params(*sem):
    return pltpu.CompilerParams(dimension_semantics=sem, vmem_limit_bytes=VMEM_LIMIT)


class _Bf16Copies:
    def __init__(self, arrays, steps, step_of):
        self.arrays = list(arrays)
        self.specs, self.counts = [], []
        for a in self.arrays:
            rows, cols = a.shape
            if rows % (steps * BF16_TILE_ROWS) == 0:
                n, block = steps, (rows // steps, cols)
                index = lambda *g: (step_of(*g), 0)
            else:
                assert cols % LANE == 0
                n = max(k for k in range(1, steps + 1) if (cols // LANE) % k == 0)
                block = (rows, cols // n)
                index = lambda *g, n=n: (0, jnp.minimum(step_of(*g), n - 1))
            self.specs.append(pl.BlockSpec(block, index))
            self.counts.append(n)
        self.out_shapes = [jax.ShapeDtypeStruct(a.shape, BF16) for a in self.arrays]

    def run(self, step, srcs, dsts):
        for src, dst, n in zip(srcs, dsts, self.counts):
            @pl.when(step < n)
            def _():
                dst[...] = src[...].astype(dst.dtype)


def _resident(shape):
    return pl.BlockSpec(shape, lambda *_: (0,) * len(shape), pipeline_mode=pl.Buffered(1))


def _inproj_kernel(x_ref, lnw_ref, w_ref, o_ref, h_ref):
    @pl.when(pl.program_id(1) == 0)
    def _():
        h_ref[...] = _rms(x_ref[...], lnw_ref[...]).astype(BF16)

    o_ref[...] = _dot(h_ref[...], w_ref[...].astype(BF16)).astype(o_ref.dtype)


def _inproj(x, ln_w, w_in, *, bm, bn):
    s, d = x.shape
    d_in = w_in.shape[1]
    return pl.pallas_call(
        _inproj_kernel,
        grid=(s // bm, d_in // bn),
        in_specs=[
            pl.BlockSpec((bm, d), lambda m, n: (m, 0)),
            pl.BlockSpec((1, d), lambda m, n: (0, 0)),
            pl.BlockSpec((d, bn), lambda m, n: (0, n)),
        ],
        out_specs=pl.BlockSpec((bm, bn), lambda m, n: (m, n)),
        out_shape=jax.ShapeDtypeStruct((s, d_in), BF16),
        scratch_shapes=[pltpu.VMEM((bm, d), BF16)],
        compiler_params=_params("arbitrary", "arbitrary"),
        name="inproj",
    )(x, ln_w, w_in)


def _pool_kernel(u_ref, halo_ref, ga_ref, mix_ref, scale_ref, wa_ref, *rest, bm, copies):
    n_copy = len(copies.arrays)
    copy_in, (o_ref, *copy_out, feat_ref) = rest[:n_copy], rest[n_copy:]
    m = pl.program_id(0)
    n = pl.program_id(1)
    copies.run(m * pl.num_programs(1) + n, copy_in, copy_out)

    @pl.when(n == 0)
    def _():
        u = u_ref[...].astype(F32)
        halo = halo_ref[...].astype(F32) * (m > 0).astype(F32)
        ext = jnp.concatenate([halo, u], axis=0)
        gw = u.shape[1] // len(POOL_WINDOWS)
        pos = m * bm + lax.broadcasted_iota(jnp.int32, (bm, 1), 0) + 1
        run = ext
        width = 1
        feats = []
        for g, w in enumerate(POOL_WINDOWS):
            while width < w:
                run = run + pltpu.roll(run, width, axis=0)
                width *= 2
            cnt = jnp.minimum(pos, w).astype(F32)
            win = run[POOL_HALO:, :gw]
            if g + 1 < len(POOL_WINDOWS):
                run = run[:, gw:]
            d = win / cnt - u[:, g * gw:(g + 1) * gw]
            y = _dot(d.astype(BF16), mix_ref[g])
            feats.append(y * scale_ref[:, g * gw:(g + 1) * gw])
        feat_ref[...] = jnp.concatenate(feats, axis=1).astype(BF16)

    y_a = _dot(feat_ref[...], wa_ref[...])
    o_ref[...] = (jax.nn.sigmoid(ga_ref[...].astype(F32)) * y_a).astype(o_ref.dtype)


def _pool_branch(proj, mix_w, scale, w_a, to_bf16, *, bm, bn, ga_block0):
    s = proj.shape[0]
    c = w_a.shape[0]
    d = w_a.shape[1]
    g, gw, _ = mix_w.shape
    halo_per_tile = bm // POOL_HALO
    n_col = d // bn
    copies = _Bf16Copies(to_bf16, (s // bm) * n_col, lambda m, n: m * n_col + n)
    ya, *as_bf16 = pl.pallas_call(
        functools.partial(_pool_kernel, bm=bm, copies=copies),
        grid=(s // bm, n_col),
        in_specs=[
            pl.BlockSpec((bm, c), lambda m, n: (m, 0)),
            pl.BlockSpec((POOL_HALO, c), lambda m, n: (jnp.maximum(m * halo_per_tile - 1, 0), 0)),
            pl.BlockSpec((bm, bn), lambda m, n: (m, ga_block0 + n)),
            _resident((g, gw, gw)),
            _resident((1, c)),
            pl.BlockSpec((c, bn), lambda m, n: (0, n)),
            *copies.specs,
        ],
        out_specs=[pl.BlockSpec((bm, bn), lambda m, n: (m, n)), *copies.specs],
        out_shape=[jax.ShapeDtypeStruct((s, d), BF16), *copies.out_shapes],
        scratch_shapes=[pltpu.VMEM((bm, c), BF16)],
        compiler_params=_params("arbitrary", "arbitrary"),
        name="pool_branch",
    )(proj, proj, proj, mix_w, scale, w_a, *to_bf16)
    return ya, as_bf16


def _split2(x):
    hi = x.astype(BF16)
    return hi, (x - hi.astype(F32)).astype(BF16)


def _gates(fl, lb):
    one_m_lb = 1.0 - lb
    sg = jax.nn.sigmoid(fl)
    log2_f = jnp.log(lb + one_m_lb * sg) * LOG2_E
    return log2_f, one_m_lb * (1.0 - sg)


def _chunk_cumsum(tri, x):
    ck = tri.shape[0]
    hi, lo = _split2(x)
    return jnp.concatenate(
        [_dot(tri, hi[r:r + ck, :]) + _dot(tri, lo[r:r + ck, :]) for r in range(0, x.shape[0], ck)], axis=0)


def _head_norm_gate(o_heads, norm_w, g):
    o = jnp.concatenate(
        [o_h * lax.rsqrt(jnp.mean(o_h * o_h, axis=-1, keepdims=True) + EPS) for o_h in o_heads], axis=1)
    return o * norm_w * (g * jax.nn.sigmoid(g))


def _hgrn_chunk_any_decay(c, q_ref, f_ref, i_ref, g_ref, tri_ref, wsel_ref, o_ref, st_ref, lb, norm_w):
    r0 = pl.multiple_of(c * CHUNK, CHUNK)
    rows = pl.ds(r0, CHUNK)
    d = q_ref.shape[1]
    heads = d // HEAD_DIM

    q = q_ref[rows, :].astype(F32)
    v = i_ref[rows, :]
    log2_f, kk = _gates(f_ref[rows, :].astype(F32), lb)
    b = _chunk_cumsum(tri_ref[:CHUNK, :CHUNK], log2_f)
    b_last = b[CHUNK - 1:CHUNK, :]

    q_in = (q * jnp.exp2(b)).astype(BF16)
    k_st = (kk * jnp.exp2(b_last - b)).astype(BF16)

    b_end = jnp.concatenate(
        [jnp.broadcast_to(b[(j + 1) * SUB - 1:(j + 1) * SUB, :], (SUB, d)) for j in range(NSUB)], axis=0)
    k_hat = kk * jnp.exp2(b_end - b)
    sub_id = lax.broadcasted_iota(jnp.int32, (CHUNK, 1), 0) // SUB
    q_from = []
    for j in range(NSUB - 1):
        lo_row = (j + 1) * SUB
        bj = b[lo_row - 1:lo_row, :]
        part = q[lo_row:, :] * jnp.exp2(b[lo_row:, :] - bj)
        q_from.append(jnp.concatenate([jnp.zeros((lo_row, d), F32), part], axis=0).astype(BF16))
    k_src = [jnp.where(sub_id == j, k_hat, 0.0).astype(BF16) for j in range(NSUB - 1)]

    per_head = [[] for _ in range(heads)]
    for r in range(NSUB):
        qb = q[r * SUB:(r + 1) * SUB, :]
        bb = b[r * SUB:(r + 1) * SUB, :]
        pieces = []
        for s in range(SUB):
            row = r * SUB + s
            e = jnp.exp2(jnp.minimum(bb - b[row:row + 1, :], 0.0))
            pieces.append((qb * kk[row:row + 1, :] * e).astype(BF16))
        for h in range(heads):
            per_head[h].append(jnp.concatenate(
                [p[:, h * HEAD_DIM:(h + 1) * HEAD_DIM] for p in pieces], axis=1))
    a_big = jnp.concatenate([jnp.concatenate(blocks, axis=0) for blocks in per_head], axis=0)
    diag = _dot(a_big, wsel_ref[...])

    t_id = lax.broadcasted_iota(jnp.int32, (CHUNK, CHUNK), 0)
    s_id = lax.broadcasted_iota(jnp.int32, (CHUNK, CHUNK), 1)
    diag_mask = (t_id // SUB == s_id // SUB) & (s_id <= t_id)

    outs = []
    for h in range(heads):
        hs = slice(h * HEAD_DIM, (h + 1) * HEAD_DIM)
        q_cat = jnp.concatenate([qf[:, hs] for qf in q_from], axis=1)
        k_cat = jnp.concatenate([ks[:, hs] for ks in k_src], axis=1)
        scores = _dot_nt(q_cat, k_cat) + jnp.where(diag_mask, diag[h * CHUNK:(h + 1) * CHUNK, :], 0.0)
        v_h = v[:, hs]
        state_t = st_ref[h]
        o_h = _dot_nt(q_in[:, hs], state_t.astype(BF16)) + _dot(scores.astype(BF16), v_h)
        st_ref[h] = state_t * jnp.exp2(b_last[:, hs]) + _dot_tn(v_h, k_st[:, hs])
        outs.append(o_h)
    o_ref[rows, :] = _head_norm_gate(outs, norm_w, g_ref[rows, :].astype(F32)).astype(o_ref.dtype)


def _hgrn_tile_mild_decay(b, kk, q_ref, i_ref, g_ref, o_ref, st_ref, norm_w):
    bt, d = b.shape
    heads = d // HEAD_DIM
    ck = FAST_CHUNK
    n_chunks = bt // ck
    q = q_ref[...].astype(F32)
    v = i_ref[...]
    chunk_decay = [jnp.exp2(b[(c + 1) * ck - 1:(c + 1) * ck, :]) for c in range(n_chunks)]
    q_in = (q * jnp.exp2(b)).astype(BF16)
    k_undecayed = kk * jnp.exp2(-b)
    k_out = k_undecayed.astype(BF16)
    k_st = (k_undecayed * jnp.concatenate(
        [jnp.broadcast_to(r, (ck, d)) for r in chunk_decay], axis=0)).astype(BF16)
    causal = (lax.broadcasted_iota(jnp.int32, (ck, ck), 1) <= lax.broadcasted_iota(jnp.int32, (ck, ck), 0))
    states = [st_ref[h] for h in range(heads)]
    tile_out = []
    for c in range(n_chunks):
        rs = slice(c * ck, (c + 1) * ck)
        carry_decay = chunk_decay[c]
        outs = []
        for h in range(heads):
            hs = slice(h * HEAD_DIM, (h + 1) * HEAD_DIM)
            scores = jnp.where(causal, _dot_nt(q_in[rs, hs], k_out[rs, hs]), 0.0).astype(BF16)
            outs.append(_dot(scores, v[rs, hs]) + _dot_nt(q_in[rs, hs], states[h].astype(BF16)))
            states[h] = states[h] * carry_decay[:, hs] + _dot_tn(v[rs, hs], k_st[rs, hs])
        tile_out.append(jnp.concatenate(outs, axis=1))
    for h in range(heads):
        st_ref[h] = states[h]
    heads_out = jnp.concatenate(tile_out, axis=0)
    o_heads = [heads_out[:, h * HEAD_DIM:(h + 1) * HEAD_DIM] for h in range(heads)]
    o_ref[...] = _head_norm_gate(o_heads, norm_w, g_ref[...].astype(F32)).astype(o_ref.dtype)


def _hgrn_kernel(q_ref, f_ref, i_ref, g_ref, lbl_ref, nw_ref, tri_ref, wsel_ref, *rest, layer, copies):
    n_copy = len(copies.arrays)
    copy_in, (o_ref, *copy_out, st_ref) = rest[:n_copy], rest[n_copy:]
    step = pl.program_id(0)

    @pl.when(step == 0)
    def _():
        st_ref[...] = jnp.zeros_like(st_ref)

    copies.run(step, copy_in, copy_out)

    logits = lbl_ref[...]
    e = jnp.exp(logits - jnp.max(logits, axis=0, keepdims=True))
    lb = jnp.sum(e[:layer + 1, :], axis=0, keepdims=True) / jnp.sum(e, axis=0, keepdims=True)
    norm_w = nw_ref[...]

    log2_f, kk = _gates(f_ref[...].astype(F32), lb)
    b = _chunk_cumsum(tri_ref[...], log2_f)
    mild = jnp.min(b) >= MILD_DECAY_LOG2

    @pl.when(mild)
    def _():
        _hgrn_tile_mild_decay(b, kk, q_ref, i_ref, g_ref, o_ref, st_ref, norm_w)

    @pl.when(jnp.logical_not(mild))
    def _():
        def body(c, carry):
            _hgrn_chunk_any_decay(c, q_ref, f_ref, i_ref, g_ref, tri_ref, wsel_ref, o_ref, st_ref, lb, norm_w)
            return carry

        lax.fori_loop(0, q_ref.shape[0] // CHUNK, body, 0)


def _hgrn(proj, lb_logits, norm_w, to_bf16, *, bt, layer, q_block, f_block, i_block, g_block):
    s = proj.shape[0]
    d = norm_w.shape[1]
    heads = d // HEAD_DIM
    steps = s // bt
    copies = _Bf16Copies(to_bf16, steps, lambda t: t)
    t_id = lax.broadcasted_iota(jnp.int32, (FAST_CHUNK, FAST_CHUNK), 0)
    s_id = lax.broadcasted_iota(jnp.int32, (FAST_CHUNK, FAST_CHUNK), 1)
    tri = (s_id <= t_id).astype(BF16)
    src = lax.broadcasted_iota(jnp.int32, (SUB * HEAD_DIM, CHUNK), 0) // HEAD_DIM
    col = lax.broadcasted_iota(jnp.int32, (SUB * HEAD_DIM, CHUNK), 1) % SUB
    wsel = (src == col).astype(BF16)
    nl = lb_logits.shape[0]
    og, *as_bf16 = pl.pallas_call(
        functools.partial(_hgrn_kernel, layer=layer, copies=copies),
        grid=(steps,),
        in_specs=[
            pl.BlockSpec((bt, d), lambda t: (t, q_block)),
            pl.BlockSpec((bt, d), lambda t: (t, f_block)),
            pl.BlockSpec((bt, d), lambda t: (t, i_block)),
            pl.BlockSpec((bt, d), lambda t: (t, g_block)),
            _resident((nl, d)),
            _resident((1, d)),
            _resident((FAST_CHUNK, FAST_CHUNK)),
            _resident((SUB * HEAD_DIM, CHUNK)),
            *copies.specs,
        ],
        out_specs=[pl.BlockSpec((bt, d), lambda t: (t, 0)), *copies.specs],
        out_shape=[jax.ShapeDtypeStruct((s, d), BF16), *copies.out_shapes],
        scratch_shapes=[pltpu.VMEM((heads, HEAD_DIM, HEAD_DIM), F32)],
        compiler_params=_params("arbitrary"),
        name="hgrn",
    )(proj, proj, proj, proj, lb_logits, norm_w, tri, wsel, *to_bf16)
    return og, as_bf16


def _merge_kernel(og_ref, gb0_ref, gb1_ref, ya_ref, x_ref, wb_ref, wo_ref, *rest, copies):
    n_copy = len(copies.arrays)
    copy_in, (o_ref, *copy_out) = rest[:n_copy], rest[n_copy:]
    copies.run(pl.program_id(0), copy_in, copy_out)
    y_b = _dot(og_ref[...], wb_ref[...])
    gate_b = jnp.concatenate([gb0_ref[...], gb1_ref[...]], axis=1).astype(F32)
    merged = ya_ref[...].astype(F32) + jax.nn.sigmoid(gate_b) * y_b
    o_ref[...] = x_ref[...] + _dot(merged.astype(BF16), wo_ref[...])


def _merge(og, proj, ya, x, w_b, w_out, to_bf16, *, bm, gb_block0):
    s, d = x.shape
    c = og.shape[1]
    half = d // 2
    copies = _Bf16Copies(to_bf16, s // bm, lambda m: m)
    x_new, *as_bf16 = pl.pallas_call(
        functools.partial(_merge_kernel, copies=copies),
        grid=(s // bm,),
        in_specs=[
            pl.BlockSpec((bm, c), lambda m: (m, 0)),
            pl.BlockSpec((bm, half), lambda m: (m, gb_block0)),
            pl.BlockSpec((bm, half), lambda m: (m, gb_block0 + 1)),
            pl.BlockSpec((bm, d), lambda m: (m, 0)),
            pl.BlockSpec((bm, d), lambda m: (m, 0)),
            _resident((c, d)),
            _resident((d, d)),
            *copies.specs,
        ],
        out_specs=[pl.BlockSpec((bm, d), lambda m: (m, 0)), *copies.specs],
        out_shape=[jax.ShapeDtypeStruct((s, d), F32), *copies.out_shapes],
        compiler_params=_params("arbitrary"),
        name="merge_out",
    )(og, proj, proj, ya, x, w_b, w_out, *to_bf16)
    return x_new, as_bf16


def _causal_conv(z_ref, row0, cols, w_ref, b_ref):
    ext = z_ref[pl.ds(row0, CARRY_ROWS + ACT_ROWS), cols]
    y = b_ref[:, cols] + w_ref[CONV_WIDTH - 1:CONV_WIDTH, cols] * ext[CARRY_ROWS:, :]
    for back in range(1, CONV_WIDTH):
        tap = CONV_WIDTH - 1 - back
        y = y + w_ref[tap:tap + 1, cols] * pltpu.roll(ext, back, axis=0)[CARRY_ROWS:, :]
    return y


def _ffn_kernel(x_ref, lnw_ref, wug_ref, wuv_ref, cwg_ref, cwv_ref, cbg_ref, cbv_ref, wd_ref,
                o_ref, h_ref, zg_ref, zv_ref, a_ref, cg_ref, cv_ref, *, dff):
    m = pl.program_id(0)
    j = pl.program_id(1)
    bm = h_ref.shape[0]
    tf = wd_ref.shape[0]
    shared = j * tf - jnp.minimum(j * tf, dff - tf)

    @pl.when(j == 0)
    def _():
        x = x_ref[...]
        h_ref[...] = _rms(x, lnw_ref[...]).astype(BF16)
        o_ref[...] = x

    @pl.when(m == 0)
    def _():
        cg_ref[j] = jnp.zeros(cg_ref.shape[1:], F32)
        cv_ref[j] = jnp.zeros(cv_ref.shape[1:], F32)

    h = h_ref[...]
    zg_ref[:CARRY_ROWS, :] = cg_ref[j]
    zv_ref[:CARRY_ROWS, :] = cv_ref[j]
    zg_ref[CARRY_ROWS:, :] = _dot(h, wug_ref[...])
    zv_ref[CARRY_ROWS:, :] = _dot(h, wuv_ref[...])
    cg_ref[j] = zg_ref[bm:, :]
    cv_ref[j] = zv_ref[bm:, :]

    for c in range(tf // LANE):
        cols = slice(c * LANE, (c + 1) * LANE)
        fresh = (lax.broadcasted_iota(jnp.int32, (1, LANE), 1) + c * LANE) >= shared
        for r in range(bm // ACT_ROWS):
            ug = _causal_conv(zg_ref, r * ACT_ROWS, cols, cwg_ref, cbg_ref)
            uv = _causal_conv(zv_ref, r * ACT_ROWS, cols, cwv_ref, cbv_ref)
            a_ref[r * ACT_ROWS:(r + 1) * ACT_ROWS, cols] = jnp.where(
                fresh, ug * jax.nn.sigmoid(ug) * uv, 0.0).astype(BF16)

    o_ref[...] += _dot(a_ref[...], wd_ref[...])


def _ffn(x, ln_w, w_up, conv_w, conv_b, w_down, *, bm, tf):
    s, d = x.shape
    dff = w_down.shape[0]
    nj = pl.cdiv(dff, tf)
    assert dff >= tf and dff % LANE == 0

    def start(j, base=0):
        return (base // LANE + jnp.minimum(j * (tf // LANE), (dff - tf) // LANE)) * LANE

    def window(rows, offset):
        return pl.BlockSpec((pl.Element(rows), pl.Element(tf)), lambda m, j: (0, offset(j)))

    return pl.pallas_call(
        functools.partial(_ffn_kernel, dff=dff),
        grid=(s // bm, nj),
        in_specs=[
            pl.BlockSpec((bm, d), lambda m, j: (m, 0)),
            pl.BlockSpec((1, d), lambda m, j: (0, 0)),
            window(d, start),
            window(d, lambda j: start(j, dff)),
            window(CONV_WIDTH, start),
            window(CONV_WIDTH, lambda j: start(j, dff)),
            window(1, start),
            window(1, lambda j: start(j, dff)),
            pl.BlockSpec((pl.Element(tf), pl.Element(d)), lambda m, j: (start(j), 0)),
        ],
        out_specs=pl.BlockSpec((bm, d), lambda m, j: (m, 0)),
        out_shape=jax.ShapeDtypeStruct((s, d), F32),
        scratch_shapes=[
            pltpu.VMEM((bm, d), BF16),
            pltpu.VMEM((CARRY_ROWS + bm, tf), F32),
            pltpu.VMEM((CARRY_ROWS + bm, tf), F32),
            pltpu.VMEM((bm, tf), BF16),
            pltpu.VMEM((nj, CARRY_ROWS, tf), F32),
            pltpu.VMEM((nj, CARRY_ROWS, tf), F32),
        ],
        compiler_params=_params("arbitrary", "arbitrary"),
        name="ffn",
    )(x, ln_w, w_up, w_up, conv_w, conv_w, conv_b, conv_b, w_down)


def _ple_kernel(x_ref, p_ref, lnw_ref, wg_ref, wp_ref, lnf_ref, o_ref, *, final):
    x = x_ref[...]
    h = _rms(x, lnw_ref[...]).astype(BF16)
    gate = jax.nn.sigmoid(_dot(h, wg_ref[...]))
    emb = _dot(p_ref[...].astype(BF16), wp_ref[...])
    y = x + gate * emb
    o_ref[...] = _rms(y, lnf_ref[...]) if final else y


def _ple(x, p, ln_w, w_gate, w_ple, ln_final, *, bm, final):
    s, d = x.shape
    e = p.shape[1]
    return pl.pallas_call(
        functools.partial(_ple_kernel, final=final),
        grid=(s // bm,),
        in_specs=[
            pl.BlockSpec((bm, d), lambda m: (m, 0)),
            pl.BlockSpec((bm, e), lambda m: (m, 0)),
            _resident((1, d)),
            _resident((d, d)),
            _resident((e, d)),
            _resident((1, d)),
        ],
        out_specs=pl.BlockSpec((bm, d), lambda m: (m, 0)),
        out_shape=jax.ShapeDtypeStruct((s, d), F32),
        compiler_params=_params("arbitrary"),
        name="ple_final",
    )(x, p, ln_w, w_gate, w_ple, ln_final)


def kernel(x, p, ln_mix_w, w_in, pool_mix_w, pool_scale, hgrn_lb_logits, hgrn_norm_w, w_branch_a,
           w_branch_b, w_out, ln_ffn_w, w_up, conv_w, conv_b, w_down, ln_ple_w, w_ple_gate, w_ple,
           ln_final_w):
    batch, seq, d = x.shape
    depth = w_in.shape[0]
    c_pool = w_branch_a.shape[1]
    c_hgrn = w_branch_b.shape[1]
    blk = c_hgrn
    assert c_pool == blk and d == 2 * blk, "column blocks of the combined projection must line up"
    assert seq % 1024 == 0
    tf = 512
    outs = []
    for bi in range(batch):
        xb = x[bi]
        for i in range(depth):
            proj = _inproj(xb, ln_mix_w[i][None], w_in[i], bm=1024, bn=blk)
            mix = pool_mix_w[i]
            og, (mix_b, wa_b, wu_b) = _hgrn(
                proj, hgrn_lb_logits, hgrn_norm_w[i][None],
                [mix.reshape(-1, mix.shape[-1]), w_branch_a[i], w_up[i]],
                bt=512, layer=i, q_block=1, f_block=2, i_block=3, g_block=4)
            ya, (wb_b, wo_b) = _pool_branch(
                proj, mix_b.reshape(mix.shape), pool_scale[i][None], wa_b,
                [w_branch_b[i], w_out[i]], bm=1024, bn=blk, ga_block0=5)
            xb, (wd_b, wg_b, wp_b) = _merge(og, proj, ya, xb, wb_b, wo_b,
                                            [w_down[i], w_ple_gate[i], w_ple[i]], bm=512, gb_block0=7)
            xb = _ffn(xb, ln_ffn_w[i][None], wu_b, conv_w[i], conv_b[i][None], wd_b, bm=1024, tf=512)
            xb = _ple(xb, p[i, bi], ln_ple_w[i][None], wg_b, wp_b, ln_final_w[None], bm=1024,
                      final=(i == depth - 1))
        outs.append(xb)
    return jnp.stack(outs, axis=0)
```

```python
import functools

import jax
import jax.numpy as jnp
from jax import lax
from jax.experimental import pallas as pl
from jax.experimental.pallas import tpu as pltpu

F32 = jnp.float32
BF16 = jnp.bfloat16

EPS = 1e-6
POOL_WINDOWS = (2, 4, 8, 16)
POOL_HALO = 16
HEAD_DIM = 128
CHUNK = 64
FAST_CHUNK = 128
SUB = 16
NSUB = CHUNK // SUB
CONV_WIDTH = 3
CARRY_ROWS = 8
ACT_ROWS = 64
LANE = 128
BF16_TILE_ROWS = 16
LOG2_E = 1.4426950408889634
MILD_DECAY_LOG2 = -100.0
V7X_VMEM_BYTES = 64 * 1024 * 1024
VMEM_LIMIT = V7X_VMEM_BYTES - 8 * 1024 * 1024


def _dot(a, b):
    return jnp.dot(a, b, preferred_element_type=F32)


def _dot_nt(a, b):
    return lax.dot_general(a, b, (((1,), (1,)), ((), ())), preferred_element_type=F32)


def _dot_tn(a, b):
    return lax.dot_general(a, b, (((0,), (0,)), ((), ())), preferred_element_type=F32)


def _rms(x, w):
    return x * lax.rsqrt(jnp.mean(x * x, axis=-1, keepdims=True) + EPS) * w


def _params(*sem):
    return pltpu.CompilerParams(dimension_semantics=sem, vmem_limit_bytes=VMEM_LIMIT)


class _Bf16Copies:
    def __init__(self, arrays, steps, step_of):
        self.arrays = list(arrays)
        self.specs, self.counts = [], []
        for a in self.arrays:
            rows, cols = a.shape
            if rows % (steps * BF16_TILE_ROWS) == 0:
                n, block = steps, (rows // steps, cols)
                index = lambda *g: (step_of(*g), 0)
            else:
                assert cols % LANE == 0
                n = max(k for k in range(1, steps + 1) if (cols // LANE) % k == 0)
                block = (rows, cols // n)
                index = lambda *g, n=n: (0, jnp.minimum(step_of(*g), n - 1))
            self.specs.append(pl.BlockSpec(block, index))
            self.counts.append(n)
        self.out_shapes = [jax.ShapeDtypeStruct(a.shape, BF16) for a in self.arrays]

    def run(self, step, srcs, dsts):
        for src, dst, n in zip(srcs, dsts, self.counts):
            @pl.when(step < n)
            def _():
                dst[...] = src[...].astype(dst.dtype)


def _resident(shape):
    return pl.BlockSpec(shape, lambda *_: (0,) * len(shape), pipeline_mode=pl.Buffered(1))


def _inproj_kernel(x_ref, lnw_ref, w_ref, *rest, copies):
    n_copy = len(copies.arrays)
    copy_in, (o_ref, *copy_out, h_ref) = rest[:n_copy], rest[n_copy:]
    copies.run(pl.program_id(0) * pl.num_programs(1) + pl.program_id(1), copy_in, copy_out)

    @pl.when(pl.program_id(1) == 0)
    def _():
        h_ref[...] = _rms(x_ref[...], lnw_ref[...]).astype(BF16)

    o_ref[...] = _dot(h_ref[...], w_ref[...].astype(BF16)).astype(o_ref.dtype)


def _inproj(x, ln_w, w_in, to_bf16, *, bm, bn):
    s, d = x.shape
    d_in = w_in.shape[1]
    n_col = d_in // bn
    copies = _Bf16Copies(to_bf16, (s // bm) * n_col, lambda m, n: m * n_col + n)
    proj, *as_bf16 = pl.pallas_call(
        functools.partial(_inproj_kernel, copies=copies),
        grid=(s // bm, n_col),
        in_specs=[
            pl.BlockSpec((bm, d), lambda m, n: (m, 0)),
            pl.BlockSpec((1, d), lambda m, n: (0, 0)),
            pl.BlockSpec((d, bn), lambda m, n: (0, n)),
            *copies.specs,
        ],
        out_specs=[pl.BlockSpec((bm, bn), lambda m, n: (m, n)), *copies.specs],
        out_shape=[jax.ShapeDtypeStruct((s, d_in), BF16), *copies.out_shapes],
        scratch_shapes=[pltpu.VMEM((bm, d), BF16)],
        compiler_params=_params("arbitrary", "arbitrary"),
        name="inproj",
    )(x, ln_w, w_in, *to_bf16)
    return proj, as_bf16


def _pool_kernel(u_ref, halo_ref, ga_ref, mix_ref, scale_ref, wa_ref, *rest, bm, copies):
    n_copy = len(copies.arrays)
    copy_in, (o_ref, *copy_out, feat_ref) = rest[:n_copy], rest[n_copy:]
    m = pl.program_id(0)
    n = pl.program_id(1)
    copies.run(m * pl.num_programs(1) + n, copy_in, copy_out)

    @pl.when(n == 0)
    def _():
        u = u_ref[...].astype(F32)
        halo = halo_ref[...].astype(F32) * (m > 0).astype(F32)
        ext = jnp.concatenate([halo, u], axis=0)
        gw = u.shape[1] // len(POOL_WINDOWS)
        pos = m * bm + lax.broadcasted_iota(jnp.int32, (bm, 1), 0) + 1
        run = ext
        width = 1
        feats = []
        for g, w in enumerate(POOL_WINDOWS):
            while width < w:
                run = run + pltpu.roll(run, width, axis=0)
                width *= 2
            cnt = jnp.minimum(pos, w).astype(F32)
            win = run[POOL_HALO:, :gw]
            if g + 1 < len(POOL_WINDOWS):
                run = run[:, gw:]
            d = win / cnt - u[:, g * gw:(g + 1) * gw]
            y = _dot(d.astype(BF16), mix_ref[g])
            feats.append(y * scale_ref[:, g * gw:(g + 1) * gw])
        feat_ref[...] = jnp.concatenate(feats, axis=1).astype(BF16)

    y_a = _dot(feat_ref[...], wa_ref[...])
    o_ref[...] = (jax.nn.sigmoid(ga_ref[...].astype(F32)) * y_a).astype(o_ref.dtype)


def _pool_branch(proj, mix_w, scale, w_a, to_bf16, *, bm, bn, ga_block0):
    s = proj.shape[0]
    c = w_a.shape[0]
    d = w_a.shape[1]
    g, gw, _ = mix_w.shape
    halo_per_tile = bm // POOL_HALO
    n_col = d // bn
    copies = _Bf16Copies(to_bf16, (s // bm) * n_col, lambda m, n: m * n_col + n)
    ya, *as_bf16 = pl.pallas_call(
        functools.partial(_pool_kernel, bm=bm, copies=copies),
        grid=(s // bm, n_col),
        in_specs=[
            pl.BlockSpec((bm, c), lambda m, n: (m, 0)),
            pl.BlockSpec((POOL_HALO, c), lambda m, n: (jnp.maximum(m * halo_per_tile - 1, 0), 0)),
            pl.BlockSpec((bm, bn), lambda m, n: (m, ga_block0 + n)),
            _resident((g, gw, gw)),
            _resident((1, c)),
            pl.BlockSpec((c, bn), lambda m, n: (0, n)),
            *copies.specs,
        ],
        out_specs=[pl.BlockSpec((bm, bn), lambda m, n: (m, n)), *copies.specs],
        out_shape=[jax.ShapeDtypeStruct((s, d), BF16), *copies.out_shapes],
        scratch_shapes=[pltpu.VMEM((bm, c), BF16)],
        compiler_params=_params("arbitrary", "arbitrary"),
        name="pool_branch",
    )(proj, proj, proj, mix_w, scale, w_a, *to_bf16)
    return ya, as_bf16


def _split2(x):
    hi = x.astype(BF16)
    return hi, (x - hi.astype(F32)).astype(BF16)


def _gates(fl, lb):
    one_m_lb = 1.0 - lb
    sg = jax.nn.sigmoid(fl)
    log2_f = jnp.log(lb + one_m_lb * sg) * LOG2_E
    return log2_f, one_m_lb * (1.0 - sg)


def _chunk_cumsum(tri, x):
    ck = tri.shape[0]
    hi, lo = _split2(x)
    return jnp.concatenate(
        [_dot(tri, hi[r:r + ck, :]) + _dot(tri, lo[r:r + ck, :]) for r in range(0, x.shape[0], ck)], axis=0)


def _head_norm_gate(o_heads, norm_w, g):
    o = jnp.concatenate(
        [o_h * lax.rsqrt(jnp.mean(o_h * o_h, axis=-1, keepdims=True) + EPS) for o_h in o_heads], axis=1)
    return o * norm_w * (g * jax.nn.sigmoid(g))


def _hgrn_chunk_any_decay(c, q_ref, f_ref, i_ref, g_ref, tri_ref, wsel_ref, o_ref, st_ref, lb, norm_w):
    r0 = pl.multiple_of(c * CHUNK, CHUNK)
    rows = pl.ds(r0, CHUNK)
    d = q_ref.shape[1]
    heads = d // HEAD_DIM

    q = q_ref[rows, :].astype(F32)
    v = i_ref[rows, :]
    log2_f, kk = _gates(f_ref[rows, :].astype(F32), lb)
    b = _chunk_cumsum(tri_ref[:CHUNK, :CHUNK], log2_f)
    b_last = b[CHUNK - 1:CHUNK, :]

    q_in = (q * jnp.exp2(b)).astype(BF16)
    k_st = (kk * jnp.exp2(b_last - b)).astype(BF16)

    b_end = jnp.concatenate(
        [jnp.broadcast_to(b[(j + 1) * SUB - 1:(j + 1) * SUB, :], (SUB, d)) for j in range(NSUB)], axis=0)
    k_hat = kk * jnp.exp2(b_end - b)
    sub_id = lax.broadcasted_iota(jnp.int32, (CHUNK, 1), 0) // SUB
    q_from = []
    for j in range(NSUB - 1):
        lo_row = (j + 1) * SUB
        bj = b[lo_row - 1:lo_row, :]
        part = q[lo_row:, :] * jnp.exp2(b[lo_row:, :] - bj)
        q_from.append(jnp.concatenate([jnp.zeros((lo_row, d), F32), part], axis=0).astype(BF16))
    k_src = [jnp.where(sub_id == j, k_hat, 0.0).astype(BF16) for j in range(NSUB - 1)]

    per_head = [[] for _ in range(heads)]
    for r in range(NSUB):
        qb = q[r * SUB:(r + 1) * SUB, :]
        bb = b[r * SUB:(r + 1) * SUB, :]
        pieces = []
        for s in range(SUB):
            row = r * SUB + s
            e = jnp.exp2(jnp.minimum(bb - b[row:row + 1, :], 0.0))
            pieces.append((qb * kk[row:row + 1, :] * e).astype(BF16))
        for h in range(heads):
            per_head[h].append(jnp.concatenate(
                [p[:, h * HEAD_DIM:(h + 1) * HEAD_DIM] for p in pieces], axis=1))
    a_big = jnp.concatenate([jnp.concatenate(blocks, axis=0) for blocks in per_head], axis=0)
    diag = _dot(a_big, wsel_ref[...])

    t_id = lax.broadcasted_iota(jnp.int32, (CHUNK, CHUNK), 0)
    s_id = lax.broadcasted_iota(jnp.int32, (CHUNK, CHUNK), 1)
    diag_mask = (t_id // SUB == s_id // SUB) & (s_id <= t_id)

    outs = []
    for h in range(heads):
        hs = slice(h * HEAD_DIM, (h + 1) * HEAD_DIM)
        q_cat = jnp.concatenate([qf[:, hs] for qf in q_from], axis=1)
        k_cat = jnp.concatenate([ks[:, hs] for ks in k_src], axis=1)
        scores = _dot_nt(q_cat, k_cat) + jnp.where(diag_mask, diag[h * CHUNK:(h + 1) * CHUNK, :], 0.0)
        v_h = v[:, hs]
        state_t = st_ref[h]
        o_h = _dot_nt(q_in[:, hs], state_t.astype(BF16)) + _dot(scores.astype(BF16), v_h)
        st_ref[h] = state_t * jnp.exp2(b_last[:, hs]) + _dot_tn(v_h, k_st[:, hs])
        outs.append(o_h)
    o_ref[rows, :] = _head_norm_gate(outs, norm_w, g_ref[rows, :].astype(F32)).astype(o_ref.dtype)


def _hgrn_tile_mild_decay(b, kk, q_ref, i_ref, g_ref, o_ref, st_ref, norm_w):
    bt, d = b.shape
    heads = d // HEAD_DIM
    ck = FAST_CHUNK
    n_chunks = bt // ck
    q = q_ref[...].astype(F32)
    v = i_ref[...]
    chunk_decay = [jnp.exp2(b[(c + 1) * ck - 1:(c + 1) * ck, :]) for c in range(n_chunks)]
    q_in = (q * jnp.exp2(b)).astype(BF16)
    k_undecayed = kk * jnp.exp2(-b)
    k_out = k_undecayed.astype(BF16)
    k_st = (k_undecayed * jnp.concatenate(
        [jnp.broadcast_to(r, (ck, d)) for r in chunk_decay], axis=0)).astype(BF16)
    causal = (lax.broadcasted_iota(jnp.int32, (ck, ck), 1) <= lax.broadcasted_iota(jnp.int32, (ck, ck), 0))
    states = [st_ref[h] for h in range(heads)]
    tile_out = []
    for c in range(n_chunks):
        rs = slice(c * ck, (c + 1) * ck)
        carry_decay = chunk_decay[c]
        outs = []
        for h in range(heads):
            hs = slice(h * HEAD_DIM, (h + 1) * HEAD_DIM)
            scores = jnp.where(causal, _dot_nt(q_in[rs, hs], k_out[rs, hs]), 0.0).astype(BF16)
            outs.append(_dot(scores, v[rs, hs]) + _dot_nt(q_in[rs, hs], states[h].astype(BF16)))
            states[h] = states[h] * carry_decay[:, hs] + _dot_tn(v[rs, hs], k_st[rs, hs])
        tile_out.append(jnp.concatenate(outs, axis=1))
    for h in range(heads):
        st_ref[h] = states[h]
    heads_out = jnp.concatenate(tile_out, axis=0)
    o_heads = [heads_out[:, h * HEAD_DIM:(h + 1) * HEAD_DIM] for h in range(heads)]
    o_ref[...] = _head_norm_gate(o_heads, norm_w, g_ref[...].astype(F32)).astype(o_ref.dtype)


def _hgrn_kernel(q_ref, f_ref, i_ref, g_ref, lbl_ref, nw_ref, tri_ref, wsel_ref, *rest, layer, copies):
    n_copy = len(copies.arrays)
    copy_in, (o_ref, *copy_out, st_ref) = rest[:n_copy], rest[n_copy:]
    step = pl.program_id(0)

    @pl.when(step == 0)
    def _():
        st_ref[...] = jnp.zeros_like(st_ref)

    copies.run(step, copy_in, copy_out)

    logits = lbl_ref[...]
    e = jnp.exp(logits - jnp.max(logits, axis=0, keepdims=True))
    lb = jnp.sum(e[:layer + 1, :], axis=0, keepdims=True) / jnp.sum(e, axis=0, keepdims=True)
    norm_w = nw_ref[...]

    log2_f, kk = _gates(f_ref[...].astype(F32), lb)
    b = _chunk_cumsum(tri_ref[...], log2_f)
    mild = jnp.min(b) >= MILD_DECAY_LOG2

    @pl.when(mild)
    def _():
        _hgrn_tile_mild_decay(b, kk, q_ref, i_ref, g_ref, o_ref, st_ref, norm_w)

    @pl.when(jnp.logical_not(mild))
    def _():
        def body(c, carry):
            _hgrn_chunk_any_decay(c, q_ref, f_ref, i_ref, g_ref, tri_ref, wsel_ref, o_ref, st_ref, lb, norm_w)
            return carry

        lax.fori_loop(0, q_ref.shape[0] // CHUNK, body, 0)


def _hgrn(proj, lb_logits, norm_w, to_bf16, *, bt, layer, q_block, f_block, i_block, g_block):
    s = proj.shape[0]
    d = norm_w.shape[1]
    heads = d // HEAD_DIM
    steps = s // bt
    copies = _Bf16Copies(to_bf16, steps, lambda t: t)
    t_id = lax.broadcasted_iota(jnp.int32, (FAST_CHUNK, FAST_CHUNK), 0)
    s_id = lax.broadcasted_iota(jnp.int32, (FAST_CHUNK, FAST_CHUNK), 1)
    tri = (s_id <= t_id).astype(BF16)
    src = lax.broadcasted_iota(jnp.int32, (SUB * HEAD_DIM, CHUNK), 0) // HEAD_DIM
    col = lax.broadcasted_iota(jnp.int32, (SUB * HEAD_DIM, CHUNK), 1) % SUB
    wsel = (src == col).astype(BF16)
    nl = lb_logits.shape[0]
    og, *as_bf16 = pl.pallas_call(
        functools.partial(_hgrn_kernel, layer=layer, copies=copies),
        grid=(steps,),
        in_specs=[
            pl.BlockSpec((bt, d), lambda t: (t, q_block)),
            pl.BlockSpec((bt, d), lambda t: (t, f_block)),
            pl.BlockSpec((bt, d), lambda t: (t, i_block)),
            pl.BlockSpec((bt, d), lambda t: (t, g_block)),
            _resident((nl, d)),
            _resident((1, d)),
            _resident((FAST_CHUNK, FAST_CHUNK)),
            _resident((SUB * HEAD_DIM, CHUNK)),
            *copies.specs,
        ],
        out_specs=[pl.BlockSpec((bt, d), lambda t: (t, 0)), *copies.specs],
        out_shape=[jax.ShapeDtypeStruct((s, d), BF16), *copies.out_shapes],
        scratch_shapes=[pltpu.VMEM((heads, HEAD_DIM, HEAD_DIM), F32)],
        compiler_params=_params("arbitrary"),
        name="hgrn",
    )(proj, proj, proj, proj, lb_logits, norm_w, tri, wsel, *to_bf16)
    return og, as_bf16


def _merge_kernel(og_ref, gb0_ref, gb1_ref, ya_ref, x_ref, wb_ref, wo_ref, *rest, copies):
    n_copy = len(copies.arrays)
    copy_in, (o_ref, *copy_out) = rest[:n_copy], rest[n_copy:]
    copies.run(pl.program_id(0), copy_in, copy_out)
    y_b = _dot(og_ref[...], wb_ref[...])
    gate_b = jnp.concatenate([gb0_ref[...], gb1_ref[...]], axis=1).astype(F32)
    merged = ya_ref[...].astype(F32) + jax.nn.sigmoid(gate_b) * y_b
    o_ref[...] = x_ref[...] + _dot(merged.astype(BF16), wo_ref[...])


def _merge(og, proj, ya, x, w_b, w_out, to_bf16, *, bm, gb_block0):
    s, d = x.shape
    c = og.shape[1]
    half = d // 2
    copies = _Bf16Copies(to_bf16, s // bm, lambda m: m)
    x_new, *as_bf16 = pl.pallas_call(
        functools.partial(_merge_kernel, copies=copies),
        grid=(s // bm,),
        in_specs=[
            pl.BlockSpec((bm, c), lambda m: (m, 0)),
            pl.BlockSpec((bm, half), lambda m: (m, gb_block0)),
            pl.BlockSpec((bm, half), lambda m: (m, gb_block0 + 1)),
            pl.BlockSpec((bm, d), lambda m: (m, 0)),
            pl.BlockSpec((bm, d), lambda m: (m, 0)),
            _resident((c, d)),
            _resident((d, d)),
            *copies.specs,
        ],
        out_specs=[pl.BlockSpec((bm, d), lambda m: (m, 0)), *copies.specs],
        out_shape=[jax.ShapeDtypeStruct((s, d), F32), *copies.out_shapes],
        compiler_params=_params("arbitrary"),
        name="merge_out",
    )(og, proj, proj, ya, x, w_b, w_out, *to_bf16)
    return x_new, as_bf16


def _causal_conv(z_ref, row0, cols, w_ref, b_ref):
    ext = z_ref[pl.ds(row0, CARRY_ROWS + ACT_ROWS), cols]
    y = b_ref[:, cols] + w_ref[CONV_WIDTH - 1:CONV_WIDTH, cols] * ext[CARRY_ROWS:, :]
    for back in range(1, CONV_WIDTH):
        tap = CONV_WIDTH - 1 - back
        y = y + w_ref[tap:tap + 1, cols] * pltpu.roll(ext, back, axis=0)[CARRY_ROWS:, :]
    return y


def _ffn_kernel(x_ref, lnw_ref, wug_ref, wuv_ref, cwg_ref, cwv_ref, cbg_ref, cbv_ref, wd_ref,
                o_ref, h_ref, zg_ref, zv_ref, a_ref, cg_ref, cv_ref, *, dff):
    m = pl.program_id(0)
    j = pl.program_id(1)
    bm = h_ref.shape[0]
    tf = wd_ref.shape[0]
    shared = j * tf - jnp.minimum(j * tf, dff - tf)

    @pl.when(j == 0)
    def _():
        x = x_ref[...]
        h_ref[...] = _rms(x, lnw_ref[...]).astype(BF16)
        o_ref[...] = x

    @pl.when(m == 0)
    def _():
        cg_ref[j] = jnp.zeros(cg_ref.shape[1:], F32)
        cv_ref[j] = jnp.zeros(cv_ref.shape[1:], F32)

    h = h_ref[...]
    zg_ref[:CARRY_ROWS, :] = cg_ref[j]
    zv_ref[:CARRY_ROWS, :] = cv_ref[j]
    zg_ref[CARRY_ROWS:, :] = _dot(h, wug_ref[...])
    zv_ref[CARRY_ROWS:, :] = _dot(h, wuv_ref[...])
    cg_ref[j] = zg_ref[bm:, :]
    cv_ref[j] = zv_ref[bm:, :]

    for c in range(tf // LANE):
        cols = slice(c * LANE, (c + 1) * LANE)
        fresh = (lax.broadcasted_iota(jnp.int32, (1, LANE), 1) + c * LANE) >= shared
        for r in range(bm // ACT_ROWS):
            ug = _causal_conv(zg_ref, r * ACT_ROWS, cols, cwg_ref, cbg_ref)
            uv = _causal_conv(zv_ref, r * ACT_ROWS, cols, cwv_ref, cbv_ref)
            a_ref[r * ACT_ROWS:(r + 1) * ACT_ROWS, cols] = jnp.where(
                fresh, ug * jax.nn.sigmoid(ug) * uv, 0.0).astype(BF16)

    o_ref[...] += _dot(a_ref[...], wd_ref[...])


def _ffn(x, ln_w, w_up, conv_w, conv_b, w_down, *, bm, tf):
    s, d = x.shape
    dff = w_down.shape[0]
    nj = pl.cdiv(dff, tf)
    assert dff >= tf and dff % LANE == 0

    def start(j, base=0):
        return (base // LANE + jnp.minimum(j * (tf // LANE), (dff - tf) // LANE)) * LANE

    def window(rows, offset):
        return pl.BlockSpec((pl.Element(rows), pl.Element(tf)), lambda m, j: (0, offset(j)))

    return pl.pallas_call(
        functools.partial(_ffn_kernel, dff=dff),
        grid=(s // bm, nj),
        in_specs=[
            pl.BlockSpec((bm, d), lambda m, j: (m, 0)),
            pl.BlockSpec((1, d), lambda m, j: (0, 0)),
            window(d, start),
            window(d, lambda j: start(j, dff)),
            window(CONV_WIDTH, start),
            window(CONV_WIDTH, lambda j: start(j, dff)),
            window(1, start),
            window(1, lambda j: start(j, dff)),
            pl.BlockSpec((pl.Element(tf), pl.Element(d)), lambda m, j: (start(j), 0)),
        ],
        out_specs=pl.BlockSpec((bm, d), lambda m, j: (m, 0)),
        out_shape=jax.ShapeDtypeStruct((s, d), F32),
        scratch_shapes=[
            pltpu.VMEM((bm, d), BF16),
            pltpu.VMEM((CARRY_ROWS + bm, tf), F32),
            pltpu.VMEM((CARRY_ROWS + bm, tf), F32),
            pltpu.VMEM((bm, tf), BF16),
            pltpu.VMEM((nj, CARRY_ROWS, tf), F32),
            pltpu.VMEM((nj, CARRY_ROWS, tf), F32),
        ],
        compiler_params=_params("arbitrary", "arbitrary"),
        name="ffn",
    )(x, ln_w, w_up, w_up, conv_w, conv_w, conv_b, conv_b, w_down)


def _ple_kernel(x_ref, p_ref, lnw_ref, wg_ref, wp_ref, lnf_ref, o_ref, *, final):
    x = x_ref[...]
    h = _rms(x, lnw_ref[...]).astype(BF16)
    gate = jax.nn.sigmoid(_dot(h, wg_ref[...]))
    emb = _dot(p_ref[...].astype(BF16), wp_ref[...])
    y = x + gate * emb
    o_ref[...] = _rms(y, lnf_ref[...]) if final else y


def _ple(x, p, ln_w, w_gate, w_ple, ln_final, *, bm, final):
    s, d = x.shape
    e = p.shape[1]
    return pl.pallas_call(
        functools.partial(_ple_kernel, final=final),
        grid=(s // bm,),
        in_specs=[
            pl.BlockSpec((bm, d), lambda m: (m, 0)),
            pl.BlockSpec((bm, e), lambda m: (m, 0)),
            _resident((1, d)),
            _resident((d, d)),
            _resident((e, d)),
            _resident((1, d)),
        ],
        out_specs=pl.BlockSpec((bm, d), lambda m: (m, 0)),
        out_shape=jax.ShapeDtypeStruct((s, d), F32),
        compiler_params=_params("arbitrary"),
        name="ple_final",
    )(x, p, ln_w, w_gate, w_ple, ln_final)


def kernel(x, p, ln_mix_w, w_in, pool_mix_w, pool_scale, hgrn_lb_logits, hgrn_norm_w, w_branch_a,
           w_branch_b, w_out, ln_ffn_w, w_up, conv_w, conv_b, w_down, ln_ple_w, w_ple_gate, w_ple,
           ln_final_w):
    batch, seq, d = x.shape
    depth = w_in.shape[0]
    c_pool = w_branch_a.shape[1]
    c_hgrn = w_branch_b.shape[1]
    blk = c_hgrn
    assert c_pool == blk and d == 2 * blk, "column blocks of the combined projection must line up"
    assert seq % 1024 == 0
    tf = 512
    outs = []
    for bi in range(batch):
        xb = x[bi]
        for i in range(depth):
            proj, (wu_b,) = _inproj(xb, ln_mix_w[i][None], w_in[i], [w_up[i]], bm=1024, bn=blk)
            mix = pool_mix_w[i]
            og, (mix_b, wa_b) = _hgrn(
                proj, hgrn_lb_logits, hgrn_norm_w[i][None],
                [mix.reshape(-1, mix.shape[-1]), w_branch_a[i]],
                bt=512, layer=i, q_block=1, f_block=2, i_block=3, g_block=4)
            ya, (wb_b, wo_b) = _pool_branch(
                proj, mix_b.reshape(mix.shape), pool_scale[i][None], wa_b,
                [w_branch_b[i], w_out[i]], bm=1024, bn=blk, ga_block0=5)
            xb, (wd_b, wg_b, wp_b) = _merge(og, proj, ya, xb, wb_b, wo_b,
                                            [w_down[i], w_ple_gate[i], w_ple[i]], bm=512, gb_block0=7)
            xb = _ffn(xb, ln_ffn_w[i][None], wu_b, conv_w[i], conv_b[i][None], wd_b, bm=1024, tf=512)
            xb = _ple(xb, p[i, bi], ln_ple_w[i][None], wg_b, wp_b, ln_final_w[None], bm=1024,
                      final=(i == depth - 1))
        outs.append(xb)
    return jnp.stack(outs, axis=0)
```

```python
import functools

import jax
import jax.numpy as jnp
from jax import lax
from jax.experimental import pallas as pl
from jax.experimental.pallas import tpu as pltpu

F32 = jnp.float32
BF16 = jnp.bfloat16

EPS = 1e-6
POOL_WINDOWS = (2, 4, 8, 16)
POOL_HALO = 16
HEAD_DIM = 128
CHUNK = 64
FAST_CHUNK = 128
SUB = 16
NSUB = CHUNK // SUB
CONV_WIDTH = 3
CARRY_ROWS = 8
ACT_ROWS = 64
LANE = 128
BF16_TILE_ROWS = 16
LOG2_E = 1.4426950408889634
MILD_DECAY_LOG2 = -100.0
V7X_VMEM_BYTES = 64 * 1024 * 1024
VMEM_LIMIT = V7X_VMEM_BYTES - 8 * 1024 * 1024


def _dot(a, b):
    return jnp.dot(a, b, preferred_element_type=F32)


def _dot_nt(a, b):
    return lax.dot_general(a, b, (((1,), (1,)), ((), ())), preferred_element_type=F32)


def _dot_tn(a, b):
    return lax.dot_general(a, b, (((0,), (0,)), ((), ())), preferred_element_type=F32)


def _rms(x, w):
    return x * lax.rsqrt(jnp.mean(x * x, axis=-1, keepdims=True) + EPS) * w


def _params(*sem):
    return pltpu.CompilerParams(dimension_semantics=sem, vmem_limit_bytes=VMEM_LIMIT)


class _Bf16Copies:
    def __init__(self, arrays, steps, step_of):
        self.arrays = list(arrays)
        self.specs, self.counts = [], []
        for a in self.arrays:
            rows, cols = a.shape
            if rows % (steps * BF16_TILE_ROWS) == 0:
                n, block = steps, (rows // steps, cols)
                index = lambda *g: (step_of(*g), 0)
            else:
                assert cols % LANE == 0
                n = max(k for k in range(1, steps + 1) if (cols // LANE) % k == 0)
                block = (rows, cols // n)
                index = lambda *g, n=n: (0, jnp.minimum(step_of(*g), n - 1))
            self.specs.append(pl.BlockSpec(block, index))
            self.counts.append(n)
        self.out_shapes = [jax.ShapeDtypeStruct(a.shape, BF16) for a in self.arrays]

    def run(self, step, srcs, dsts):
        for src, dst, n in zip(srcs, dsts, self.counts):
            @pl.when(step < n)
            def _():
                dst[...] = src[...].astype(dst.dtype)


def _resident(shape):
    return pl.BlockSpec(shape, lambda *_: (0,) * len(shape), pipeline_mode=pl.Buffered(1))


def _inproj_kernel(x_ref, lnw_ref, w_ref, o_ref, h_ref):
    @pl.when(pl.program_id(1) == 0)
    def _():
        h_ref[...] = _rms(x_ref[...], lnw_ref[...]).astype(BF16)

    o_ref[...] = _dot(h_ref[...], w_ref[...].astype(BF16)).astype(o_ref.dtype)


def _inproj(x, ln_w, w_in, *, bm, bn):
    s, d = x.shape
    d_in = w_in.shape[1]
    return pl.pallas_call(
        _inproj_kernel,
        grid=(s // bm, d_in // bn),
        in_specs=[
            pl.BlockSpec((bm, d), lambda m, n: (m, 0)),
            pl.BlockSpec((1, d), lambda m, n: (0, 0)),
            pl.BlockSpec((d, bn), lambda m, n: (0, n)),
        ],
        out_specs=pl.BlockSpec((bm, bn), lambda m, n: (m, n)),
        out_shape=jax.ShapeDtypeStruct((s, d_in), BF16),
        scratch_shapes=[pltpu.VMEM((bm, d), BF16)],
        compiler_params=_params("arbitrary", "arbitrary"),
        name="inproj",
    )(x, ln_w, w_in)


def _pool_kernel(u_ref, halo_ref, ga_ref, mix_ref, scale_ref, wa_ref, *rest, bm, copies):
    n_copy = len(copies.arrays)
    copy_in, (o_ref, *copy_out, feat_ref) = rest[:n_copy], rest[n_copy:]
    m = pl.program_id(0)
    n = pl.program_id(1)
    copies.run(m * pl.num_programs(1) + n, copy_in, copy_out)

    @pl.when(n == 0)
    def _():
        u = u_ref[...].astype(F32)
        halo = halo_ref[...].astype(F32) * (m > 0).astype(F32)
        ext = jnp.concatenate([halo, u], axis=0)
        gw = u.shape[1] // len(POOL_WINDOWS)
        pos = m * bm + lax.broadcasted_iota(jnp.int32, (bm, 1), 0) + 1
        run = ext
        width = 1
        feats = []
        for g, w in enumerate(POOL_WINDOWS):
            while width < w:
                run = run + pltpu.roll(run, width, axis=0)
                width *= 2
            cnt = jnp.minimum(pos, w).astype(F32)
            win = run[POOL_HALO:, :gw]
            if g + 1 < len(POOL_WINDOWS):
                run = run[:, gw:]
            d = win / cnt - u[:, g * gw:(g + 1) * gw]
            y = _dot(d.astype(BF16), mix_ref[g])
            feats.append(y * scale_ref[:, g * gw:(g + 1) * gw])
        feat_ref[...] = jnp.concatenate(feats, axis=1).astype(BF16)

    y_a = _dot(feat_ref[...], wa_ref[...])
    o_ref[...] = (jax.nn.sigmoid(ga_ref[...].astype(F32)) * y_a).astype(o_ref.dtype)


def _pool_branch(proj, mix_w, scale, w_a, to_bf16, *, bm, bn, ga_block0):
    s = proj.shape[0]
    c = w_a.shape[0]
    d = w_a.shape[1]
    g, gw, _ = mix_w.shape
    halo_per_tile = bm // POOL_HALO
    n_col = d // bn
    copies = _Bf16Copies(to_bf16, (s // bm) * n_col, lambda m, n: m * n_col + n)
    ya, *as_bf16 = pl.pallas_call(
        functools.partial(_pool_kernel, bm=bm, copies=copies),
        grid=(s // bm, n_col),
        in_specs=[
            pl.BlockSpec((bm, c), lambda m, n: (m, 0)),
            pl.BlockSpec((POOL_HALO, c), lambda m, n: (jnp.maximum(m * halo_per_tile - 1, 0), 0)),
            pl.BlockSpec((bm, bn), lambda m, n: (m, ga_block0 + n)),
            _resident((g, gw, gw)),
            _resident((1, c)),
            pl.BlockSpec((c, bn), lambda m, n: (0, n)),
            *copies.specs,
        ],
        out_specs=[pl.BlockSpec((bm, bn), lambda m, n: (m, n)), *copies.specs],
        out_shape=[jax.ShapeDtypeStruct((s, d), BF16), *copies.out_shapes],
        scratch_shapes=[pltpu.VMEM((bm, c), BF16)],
        compiler_params=_params("arbitrary", "arbitrary"),
        name="pool_branch",
    )(proj, proj, proj, mix_w, scale, w_a, *to_bf16)
    return ya, as_bf16


def _split2(x):
    hi = x.astype(BF16)
    return hi, (x - hi.astype(F32)).astype(BF16)


def _gates(fl, lb):
    one_m_lb = 1.0 - lb
    sg = jax.nn.sigmoid(fl)
    log2_f = jnp.log(lb + one_m_lb * sg) * LOG2_E
    return log2_f, one_m_lb * (1.0 - sg)


def _chunk_cumsum(tri, x):
    ck = tri.shape[0]
    hi, lo = _split2(x)
    return jnp.concatenate(
        [_dot(tri, hi[r:r + ck, :]) + _dot(tri, lo[r:r + ck, :]) for r in range(0, x.shape[0], ck)], axis=0)


def _head_norm_gate(o_heads, norm_w, g):
    o = jnp.concatenate(
        [o_h * lax.rsqrt(jnp.mean(o_h * o_h, axis=-1, keepdims=True) + EPS) for o_h in o_heads], axis=1)
    return o * norm_w * (g * jax.nn.sigmoid(g))


def _hgrn_chunk_any_decay(c, q_ref, f_ref, i_ref, g_ref, tri_ref, wsel_ref, o_ref, st_ref, lb, norm_w):
    r0 = pl.multiple_of(c * CHUNK, CHUNK)
    rows = pl.ds(r0, CHUNK)
    d = q_ref.shape[1]
    heads = d // HEAD_DIM

    q = q_ref[rows, :].astype(F32)
    v = i_ref[rows, :]
    log2_f, kk = _gates(f_ref[rows, :].astype(F32), lb)
    b = _chunk_cumsum(tri_ref[:CHUNK, :CHUNK], log2_f)
    b_last = b[CHUNK - 1:CHUNK, :]

    q_in = (q * jnp.exp2(b)).astype(BF16)
    k_st = (kk * jnp.exp2(b_last - b)).astype(BF16)

    b_end = jnp.concatenate(
        [jnp.broadcast_to(b[(j + 1) * SUB - 1:(j + 1) * SUB, :], (SUB, d)) for j in range(NSUB)], axis=0)
    k_hat = kk * jnp.exp2(b_end - b)
    sub_id = lax.broadcasted_iota(jnp.int32, (CHUNK, 1), 0) // SUB
    q_from = []
    for j in range(NSUB - 1):
        lo_row = (j + 1) * SUB
        bj = b[lo_row - 1:lo_row, :]
        part = q[lo_row:, :] * jnp.exp2(b[lo_row:, :] - bj)
        q_from.append(jnp.concatenate([jnp.zeros((lo_row, d), F32), part], axis=0).astype(BF16))
    k_src = [jnp.where(sub_id == j, k_hat, 0.0).astype(BF16) for j in range(NSUB - 1)]

    per_head = [[] for _ in range(heads)]
    for r in range(NSUB):
        qb = q[r * SUB:(r + 1) * SUB, :]
        bb = b[r * SUB:(r + 1) * SUB, :]
        pieces = []
        for s in range(SUB):
            row = r * SUB + s
            e = jnp.exp2(jnp.minimum(bb - b[row:row + 1, :], 0.0))
            pieces.append((qb * kk[row:row + 1, :] * e).astype(BF16))
        for h in range(heads):
            per_head[h].append(jnp.concatenate(
                [p[:, h * HEAD_DIM:(h + 1) * HEAD_DIM] for p in pieces], axis=1))
    a_big = jnp.concatenate([jnp.concatenate(blocks, axis=0) for blocks in per_head], axis=0)
    diag = _dot(a_big, wsel_ref[...])

    t_id = lax.broadcasted_iota(jnp.int32, (CHUNK, CHUNK), 0)
    s_id = lax.broadcasted_iota(jnp.int32, (CHUNK, CHUNK), 1)
    diag_mask = (t_id // SUB == s_id // SUB) & (s_id <= t_id)

    outs = []
    for h in range(heads):
        hs = slice(h * HEAD_DIM, (h + 1) * HEAD_DIM)
        q_cat = jnp.concatenate([qf[:, hs] for qf in q_from], axis=1)
        k_cat = jnp.concatenate([ks[:, hs] for ks in k_src], axis=1)
        scores = _dot_nt(q_cat, k_cat) + jnp.where(diag_mask, diag[h * CHUNK:(h + 1) * CHUNK, :], 0.0)
        v_h = v[:, hs]
        state_t = st_ref[h]
        o_h = _dot_nt(q_in[:, hs], state_t.astype(BF16)) + _dot(scores.astype(BF16), v_h)
        st_ref[h] = state_t * jnp.exp2(b_last[:, hs]) + _dot_tn(v_h, k_st[:, hs])
        outs.append(o_h)
    o_ref[rows, :] = _head_norm_gate(outs, norm_w, g_ref[rows, :].astype(F32)).astype(o_ref.dtype)


def _hgrn_tile_mild_decay(b, kk, q_ref, i_ref, g_ref, o_ref, st_ref, norm_w):
    bt, d = b.shape
    heads = d // HEAD_DIM
    ck = FAST_CHUNK
    n_chunks = bt // ck
    q = q_ref[...].astype(F32)
    v = i_ref[...]
    chunk_decay = [jnp.exp2(b[(c + 1) * ck - 1:(c + 1) * ck, :]) for c in range(n_chunks)]
    q_in = (q * jnp.exp2(b)).astype(BF16)
    k_undecayed = kk * jnp.exp2(-b)
    k_out = k_undecayed.astype(BF16)
    k_st = (k_undecayed * jnp.concatenate(
        [jnp.broadcast_to(r, (ck, d)) for r in chunk_decay], axis=0)).astype(BF16)
    causal = (lax.broadcasted_iota(jnp.int32, (ck, ck), 1) <= lax.broadcasted_iota(jnp.int32, (ck, ck), 0))
    states = [st_ref[h] for h in range(heads)]
    tile_out = []
    for c in range(n_chunks):
        rs = slice(c * ck, (c + 1) * ck)
        carry_decay = chunk_decay[c]
        outs = []
        for h in range(heads):
            hs = slice(h * HEAD_DIM, (h + 1) * HEAD_DIM)
            scores = jnp.where(causal, _dot_nt(q_in[rs, hs], k_out[rs, hs]), 0.0).astype(BF16)
            outs.append(_dot(scores, v[rs, hs]) + _dot_nt(q_in[rs, hs], states[h].astype(BF16)))
            states[h] = states[h] * carry_decay[:, hs] + _dot_tn(v[rs, hs], k_st[rs, hs])
        tile_out.append(jnp.concatenate(outs, axis=1))
    for h in range(heads):
        st_ref[h] = states[h]
    heads_out = jnp.concatenate(tile_out, axis=0)
    o_heads = [heads_out[:, h * HEAD_DIM:(h + 1) * HEAD_DIM] for h in range(heads)]
    o_ref[...] = _head_norm_gate(o_heads, norm_w, g_ref[...].astype(F32)).astype(o_ref.dtype)


def _hgrn_kernel(q_ref, f_ref, i_ref, g_ref, lbl_ref, nw_ref, tri_ref, wsel_ref, *rest, layer, copies):
    n_copy = len(copies.arrays)
    copy_in, (o_ref, *copy_out, st_ref) = rest[:n_copy], rest[n_copy:]
    step = pl.program_id(0)

    @pl.when(step == 0)
    def _():
        st_ref[...] = jnp.zeros_like(st_ref)

    copies.run(step, copy_in, copy_out)

    logits = lbl_ref[...]
    e = jnp.exp(logits - jnp.max(logits, axis=0, keepdims=True))
    lb = jnp.sum(e[:layer + 1, :], axis=0, keepdims=True) / jnp.sum(e, axis=0, keepdims=True)
    norm_w = nw_ref[...]

    log2_f, kk = _gates(f_ref[...].astype(F32), lb)
    b = _chunk_cumsum(tri_ref[...], log2_f)
    mild = jnp.min(b) >= MILD_DECAY_LOG2

    @pl.when(mild)
    def _():
        _hgrn_tile_mild_decay(b, kk, q_ref, i_ref, g_ref, o_ref, st_ref, norm_w)

    @pl.when(jnp.logical_not(mild))
    def _():
        def body(c, carry):
            _hgrn_chunk_any_decay(c, q_ref, f_ref, i_ref, g_ref, tri_ref, wsel_ref, o_ref, st_ref, lb, norm_w)
            return carry

        lax.fori_loop(0, q_ref.shape[0] // CHUNK, body, 0)


def _hgrn(proj, lb_logits, norm_w, to_bf16, *, bt, layer, q_block, f_block, i_block, g_block):
    s = proj.shape[0]
    d = norm_w.shape[1]
    heads = d // HEAD_DIM
    steps = s // bt
    copies = _Bf16Copies(to_bf16, steps, lambda t: t)
    t_id = lax.broadcasted_iota(jnp.int32, (FAST_CHUNK, FAST_CHUNK), 0)
    s_id = lax.broadcasted_iota(jnp.int32, (FAST_CHUNK, FAST_CHUNK), 1)
    tri = (s_id <= t_id).astype(BF16)
    src = lax.broadcasted_iota(jnp.int32, (SUB * HEAD_DIM, CHUNK), 0) // HEAD_DIM
    col = lax.broadcasted_iota(jnp.int32, (SUB * HEAD_DIM, CHUNK), 1) % SUB
    wsel = (src == col).astype(BF16)
    nl = lb_logits.shape[0]
    og, *as_bf16 = pl.pallas_call(
        functools.partial(_hgrn_kernel, layer=layer, copies=copies),
        grid=(steps,),
        in_specs=[
            pl.BlockSpec((bt, d), lambda t: (t, q_block)),
            pl.BlockSpec((bt, d), lambda t: (t, f_block)),
            pl.BlockSpec((bt, d), lambda t: (t, i_block)),
            pl.BlockSpec((bt, d), lambda t: (t, g_block)),
            _resident((nl, d)),
            _resident((1, d)),
            _resident((FAST_CHUNK, FAST_CHUNK)),
            _resident((SUB * HEAD_DIM, CHUNK)),
            *copies.specs,
        ],
        out_specs=[pl.BlockSpec((bt, d), lambda t: (t, 0)), *copies.specs],
        out_shape=[jax.ShapeDtypeStruct((s, d), BF16), *copies.out_shapes],
        scratch_shapes=[pltpu.VMEM((heads, HEAD_DIM, HEAD_DIM), F32)],
        compiler_params=_params("arbitrary"),
        name="hgrn",
    )(proj, proj, proj, proj, lb_logits, norm_w, tri, wsel, *to_bf16)
    return og, as_bf16


def _merge_kernel(og_ref, gb0_ref, gb1_ref, ya_ref, x_ref, wb_ref, wo_ref, *rest, copies):
    n_copy = len(copies.arrays)
    copy_in, (o_ref, *copy_out) = rest[:n_copy], rest[n_copy:]
    copies.run(pl.program_id(0), copy_in, copy_out)
    y_b = _dot(og_ref[...], wb_ref[...])
    gate_b = jnp.concatenate([gb0_ref[...], gb1_ref[...]], axis=1).astype(F32)
    merged = ya_ref[...].astype(F32) + jax.nn.sigmoid(gate_b) * y_b
    o_ref[...] = x_ref[...] + _dot(merged.astype(BF16), wo_ref[...])


def _merge(og, proj, ya, x, w_b, w_out, to_bf16, *, bm, gb_block0):
    s, d = x.shape
    c = og.shape[1]
    half = d // 2
    copies = _Bf16Copies(to_bf16, s // bm, lambda m: m)
    x_new, *as_bf16 = pl.pallas_call(
        functools.partial(_merge_kernel, copies=copies),
        grid=(s // bm,),
        in_specs=[
            pl.BlockSpec((bm, c), lambda m: (m, 0)),
            pl.BlockSpec((bm, half), lambda m: (m, gb_block0)),
            pl.BlockSpec((bm, half), lambda m: (m, gb_block0 + 1)),
            pl.BlockSpec((bm, d), lambda m: (m, 0)),
            pl.BlockSpec((bm, d), lambda m: (m, 0)),
            _resident((c, d)),
            _resident((d, d)),
            *copies.specs,
        ],
        out_specs=[pl.BlockSpec((bm, d), lambda m: (m, 0)), *copies.specs],
        out_shape=[jax.ShapeDtypeStruct((s, d), F32), *copies.out_shapes],
        compiler_params=_params("arbitrary"),
        name="merge_out",
    )(og, proj, proj, ya, x, w_b, w_out, *to_bf16)
    return x_new, as_bf16


def _causal_conv(z_ref, row0, cols, w_ref, b_ref, channels):
    ext = z_ref[pl.ds(row0, CARRY_ROWS + ACT_ROWS), cols]
    y = b_ref[:, channels] + w_ref[CONV_WIDTH - 1:CONV_WIDTH, channels] * ext[CARRY_ROWS:, :]
    for back in range(1, CONV_WIDTH):
        tap = CONV_WIDTH - 1 - back
        y = y + w_ref[tap:tap + 1, channels] * pltpu.roll(ext, back, axis=0)[CARRY_ROWS:, :]
    return y


def _ffn_kernel(x_ref, lnw_ref, wug_ref, wuv_ref, cw_ref, cb_ref, wd_ref,
                o_ref, h_ref, zg_ref, zv_ref, a_ref, cg_ref, cv_ref, *, dff):
    m = pl.program_id(0)
    j = pl.program_id(1)
    bm = h_ref.shape[0]
    tf = wd_ref.shape[0]
    first_tile = jnp.minimum(j * (tf // LANE), (dff - tf) // LANE)
    shared = j * tf - first_tile * LANE

    @pl.when(j == 0)
    def _():
        x = x_ref[...]
        h_ref[...] = _rms(x, lnw_ref[...]).astype(BF16)
        o_ref[...] = x

    @pl.when(m == 0)
    def _():
        cg_ref[j] = jnp.zeros(cg_ref.shape[1:], F32)
        cv_ref[j] = jnp.zeros(cv_ref.shape[1:], F32)

    h = h_ref[...]
    zg_ref[:CARRY_ROWS, :] = cg_ref[j]
    zv_ref[:CARRY_ROWS, :] = cv_ref[j]
    zg_ref[CARRY_ROWS:, :] = _dot(h, wug_ref[...])
    zv_ref[CARRY_ROWS:, :] = _dot(h, wuv_ref[...])
    cg_ref[j] = zg_ref[bm:, :]
    cv_ref[j] = zv_ref[bm:, :]

    for c in range(tf // LANE):
        cols = slice(c * LANE, (c + 1) * LANE)
        fresh = (lax.broadcasted_iota(jnp.int32, (1, LANE), 1) + c * LANE) >= shared
        gate_ch = pl.ds(pl.multiple_of((first_tile + c) * LANE, LANE), LANE)
        val_ch = pl.ds(pl.multiple_of((first_tile + c + dff // LANE) * LANE, LANE), LANE)
        for r in range(bm // ACT_ROWS):
            ug = _causal_conv(zg_ref, r * ACT_ROWS, cols, cw_ref, cb_ref, gate_ch)
            uv = _causal_conv(zv_ref, r * ACT_ROWS, cols, cw_ref, cb_ref, val_ch)
            a_ref[r * ACT_ROWS:(r + 1) * ACT_ROWS, cols] = jnp.where(
                fresh, ug * jax.nn.sigmoid(ug) * uv, 0.0).astype(BF16)

    o_ref[...] += _dot(a_ref[...], wd_ref[...])


def _ffn(x, ln_w, w_up, conv_w, conv_b, w_down, *, bm, tf):
    s, d = x.shape
    dff = w_down.shape[0]
    nj = pl.cdiv(dff, tf)
    assert dff >= tf and dff % LANE == 0

    def start(j, base=0):
        return (base // LANE + jnp.minimum(j * (tf // LANE), (dff - tf) // LANE)) * LANE

    def window(rows, offset):
        return pl.BlockSpec((pl.Element(rows), pl.Element(tf)), lambda m, j: (0, offset(j)))

    return pl.pallas_call(
        functools.partial(_ffn_kernel, dff=dff),
        grid=(s // bm, nj),
        in_specs=[
            pl.BlockSpec((bm, d), lambda m, j: (m, 0)),
            pl.BlockSpec((1, d), lambda m, j: (0, 0)),
            window(d, start),
            window(d, lambda j: start(j, dff)),
            _resident(conv_w.shape),
            _resident(conv_b.shape),
            pl.BlockSpec((pl.Element(tf), pl.Element(d)), lambda m, j: (start(j), 0)),
        ],
        out_specs=pl.BlockSpec((bm, d), lambda m, j: (m, 0)),
        out_shape=jax.ShapeDtypeStruct((s, d), F32),
        scratch_shapes=[
            pltpu.VMEM((bm, d), BF16),
            pltpu.VMEM((CARRY_ROWS + bm, tf), F32),
            pltpu.VMEM((CARRY_ROWS + bm, tf), F32),
            pltpu.VMEM((bm, tf), BF16),
            pltpu.VMEM((nj, CARRY_ROWS, tf), F32),
            pltpu.VMEM((nj, CARRY_ROWS, tf), F32),
        ],
        compiler_params=_params("arbitrary", "arbitrary"),
        name="ffn",
    )(x, ln_w, w_up, w_up, conv_w, conv_b, w_down)


def _ple_kernel(x_ref, p_ref, lnw_ref, wg_ref, wp_ref, lnf_ref, o_ref, *, final):
    x = x_ref[...]
    h = _rms(x, lnw_ref[...]).astype(BF16)
    gate = jax.nn.sigmoid(_dot(h, wg_ref[...]))
    emb = _dot(p_ref[...].astype(BF16), wp_ref[...])
    y = x + gate * emb
    o_ref[...] = _rms(y, lnf_ref[...]) if final else y


def _ple(x, p, ln_w, w_gate, w_ple, ln_final, *, bm, final):
    s, d = x.shape
    e = p.shape[1]
    return pl.pallas_call(
        functools.partial(_ple_kernel, final=final),
        grid=(s // bm,),
        in_specs=[
            pl.BlockSpec((bm, d), lambda m: (m, 0)),
            pl.BlockSpec((bm, e), lambda m: (m, 0)),
            _resident((1, d)),
            _resident((d, d)),
            _resident((e, d)),
            _resident((1, d)),
        ],
        out_specs=pl.BlockSpec((bm, d), lambda m: (m, 0)),
        out_shape=jax.ShapeDtypeStruct((s, d), F32),
        compiler_params=_params("arbitrary"),
        name="ple_final",
    )(x, p, ln_w, w_gate, w_ple, ln_final)


def kernel(x, p, ln_mix_w, w_in, pool_mix_w, pool_scale, hgrn_lb_logits, hgrn_norm_w, w_branch_a,
           w_branch_b, w_out, ln_ffn_w, w_up, conv_w, conv_b, w_down, ln_ple_w, w_ple_gate, w_ple,
           ln_final_w):
    batch, seq, d = x.shape
    depth = w_in.shape[0]
    c_pool = w_branch_a.shape[1]
    c_hgrn = w_branch_b.shape[1]
    blk = c_hgrn
    assert c_pool == blk and d == 2 * blk, "column blocks of the combined projection must line up"
    assert seq % 1024 == 0
    tf = 512
    outs = []
    for bi in range(batch):
        xb = x[bi]
        for i in range(depth):
            proj = _inproj(xb, ln_mix_w[i][None], w_in[i], bm=1024, bn=blk)
            mix = pool_mix_w[i]
            og, (mix_b, wa_b, wu_b) = _hgrn(
                proj, hgrn_lb_logits, hgrn_norm_w[i][None],
                [mix.reshape(-1, mix.shape[-1]), w_branch_a[i], w_up[i]],
                bt=512, layer=i, q_block=1, f_block=2, i_block=3, g_block=4)
            ya, (wb_b, wo_b) = _pool_branch(
                proj, mix_b.reshape(mix.shape), pool_scale[i][None], wa_b,
                [w_branch_b[i], w_out[i]], bm=1024, bn=blk, ga_block0=5)
            xb, (wd_b, wg_b, wp_b) = _merge(og, proj, ya, xb, wb_b, wo_b,
                                            [w_down[i], w_ple_gate[i], w_ple[i]], bm=512, gb_block0=7)
            xb = _ffn(xb, ln_ffn_w[i][None], wu_b, conv_w[i], conv_b[i][None], wd_b, bm=1024, tf=512)
            xb = _ple(xb, p[i, bi], ln_ple_w[i][None], wg_b, wp_b, ln_final_w[None], bm=1024,
                      final=(i == depth - 1))
        outs.append(xb)
    return jnp.stack(outs, axis=0)
```

```python
import functools

import jax
import jax.numpy as jnp
from jax import lax
from jax.experimental import pallas as pl
from jax.experimental.pallas import tpu as pltpu

F32 = jnp.float32
BF16 = jnp.bfloat16

EPS = 1e-6
POOL_WINDOWS = (2, 4, 8, 16)
POOL_HALO = 16
HEAD_DIM = 128
CHUNK = 64
FAST_CHUNK = 128
SUB = 16
NSUB = CHUNK // SUB
CONV_WIDTH = 3
CARRY_ROWS = 8
ACT_ROWS = 64
LANE = 128
BF16_TILE_ROWS = 16
LOG2_E = 1.4426950408889634
MILD_DECAY_LOG2 = -100.0
V7X_VMEM_BYTES = 64 * 1024 * 1024
VMEM_LIMIT = V7X_VMEM_BYTES - 3 * 1024 * 1024


def _dot(a, b):
    return jnp.dot(a, b, preferred_element_type=F32)


def _dot_nt(a, b):
    return lax.dot_general(a, b, (((1,), (1,)), ((), ())), preferred_element_type=F32)


def _dot_tn(a, b):
    return lax.dot_general(a, b, (((0,), (0,)), ((), ())), preferred_element_type=F32)


def _rms(x, w):
    return x * lax.rsqrt(jnp.mean(x * x, axis=-1, keepdims=True) + EPS) * w


def _params(*sem):
    return pltpu.CompilerParams(dimension_semantics=sem, vmem_limit_bytes=VMEM_LIMIT)


class _Bf16Copies:
    def __init__(self, arrays, steps, step_of):
        self.arrays = list(arrays)
        self.specs, self.counts = [], []
        for a in self.arrays:
            rows, cols = a.shape
            if rows % (steps * BF16_TILE_ROWS) == 0:
                n, block = steps, (rows // steps, cols)
                index = lambda *g: (step_of(*g), 0)
            else:
                assert cols % LANE == 0
                n = max(k for k in range(1, steps + 1) if (cols // LANE) % k == 0)
                block = (rows, cols // n)
                index = lambda *g, n=n: (0, jnp.minimum(step_of(*g), n - 1))
            self.specs.append(pl.BlockSpec(block, index))
            self.counts.append(n)
        self.out_shapes = [jax.ShapeDtypeStruct(a.shape, BF16) for a in self.arrays]

    def run(self, step, srcs, dsts):
        for src, dst, n in zip(srcs, dsts, self.counts):
            @pl.when(step < n)
            def _():
                dst[...] = src[...].astype(dst.dtype)


def _resident(shape):
    return pl.BlockSpec(shape, lambda *_: (0,) * len(shape), pipeline_mode=pl.Buffered(1))


def _inproj_kernel(x_ref, lnw_ref, w_ref, o_ref, h_ref):
    @pl.when(pl.program_id(1) == 0)
    def _():
        h_ref[...] = _rms(x_ref[...], lnw_ref[...]).astype(BF16)

    o_ref[...] = _dot(h_ref[...], w_ref[...].astype(BF16)).astype(o_ref.dtype)


def _inproj(x, ln_w, w_in, *, bm, bn):
    s, d = x.shape
    d_in = w_in.shape[1]
    return pl.pallas_call(
        _inproj_kernel,
        grid=(s // bm, d_in // bn),
        in_specs=[
            pl.BlockSpec((bm, d), lambda m, n: (m, 0), pipeline_mode=pl.Buffered(1)),
            pl.BlockSpec((1, d), lambda m, n: (0, 0)),
            pl.BlockSpec((d, bn), lambda m, n: (0, n)),
        ],
        out_specs=pl.BlockSpec((bm, bn), lambda m, n: (m, n)),
        out_shape=jax.ShapeDtypeStruct((s, d_in), BF16),
        scratch_shapes=[pltpu.VMEM((bm, d), BF16)],
        compiler_params=_params("arbitrary", "arbitrary"),
        name="inproj",
    )(x, ln_w, w_in)


def _pool_kernel(u_ref, halo_ref, ga_ref, mix_ref, scale_ref, wa_ref, *rest, bm, copies):
    n_copy = len(copies.arrays)
    copy_in, (o_ref, *copy_out, feat_ref) = rest[:n_copy], rest[n_copy:]
    m = pl.program_id(0)
    n = pl.program_id(1)
    copies.run(m * pl.num_programs(1) + n, copy_in, copy_out)

    @pl.when(n == 0)
    def _():
        u = u_ref[...].astype(F32)
        halo = halo_ref[...].astype(F32) * (m > 0).astype(F32)
        ext = jnp.concatenate([halo, u], axis=0)
        gw = u.shape[1] // len(POOL_WINDOWS)
        pos = m * bm + lax.broadcasted_iota(jnp.int32, (bm, 1), 0) + 1
        run = ext
        width = 1
        feats = []
        for g, w in enumerate(POOL_WINDOWS):
            while width < w:
                run = run + pltpu.roll(run, width, axis=0)
                width *= 2
            cnt = jnp.minimum(pos, w).astype(F32)
            win = run[POOL_HALO:, :gw]
            if g + 1 < len(POOL_WINDOWS):
                run = run[:, gw:]
            d = win / cnt - u[:, g * gw:(g + 1) * gw]
            y = _dot(d.astype(BF16), mix_ref[g])
            feats.append(y * scale_ref[:, g * gw:(g + 1) * gw])
        feat_ref[...] = jnp.concatenate(feats, axis=1).astype(BF16)

    y_a = _dot(feat_ref[...], wa_ref[...])
    o_ref[...] = (jax.nn.sigmoid(ga_ref[...].astype(F32)) * y_a).astype(o_ref.dtype)


def _pool_branch(proj, mix_w, scale, w_a, to_bf16, *, bm, bn, ga_block0):
    s = proj.shape[0]
    c = w_a.shape[0]
    d = w_a.shape[1]
    g, gw, _ = mix_w.shape
    halo_per_tile = bm // POOL_HALO
    n_col = d // bn
    copies = _Bf16Copies(to_bf16, (s // bm) * n_col, lambda m, n: m * n_col + n)
    ya, *as_bf16 = pl.pallas_call(
        functools.partial(_pool_kernel, bm=bm, copies=copies),
        grid=(s // bm, n_col),
        in_specs=[
            pl.BlockSpec((bm, c), lambda m, n: (m, 0)),
            pl.BlockSpec((POOL_HALO, c), lambda m, n: (jnp.maximum(m * halo_per_tile - 1, 0), 0)),
            pl.BlockSpec((bm, bn), lambda m, n: (m, ga_block0 + n)),
            _resident((g, gw, gw)),
            _resident((1, c)),
            pl.BlockSpec((c, bn), lambda m, n: (0, n)),
            *copies.specs,
        ],
        out_specs=[pl.BlockSpec((bm, bn), lambda m, n: (m, n)), *copies.specs],
        out_shape=[jax.ShapeDtypeStruct((s, d), BF16), *copies.out_shapes],
        scratch_shapes=[pltpu.VMEM((bm, c), BF16)],
        compiler_params=_params("arbitrary", "arbitrary"),
        name="pool_branch",
    )(proj, proj, proj, mix_w, scale, w_a, *to_bf16)
    return ya, as_bf16


def _split2(x):
    hi = x.astype(BF16)
    return hi, (x - hi.astype(F32)).astype(BF16)


def _gates(fl, lb):
    one_m_lb = 1.0 - lb
    sg = jax.nn.sigmoid(fl)
    log2_f = jnp.log(lb + one_m_lb * sg) * LOG2_E
    return log2_f, one_m_lb * (1.0 - sg)


def _chunk_cumsum(tri, x):
    ck = tri.shape[0]
    hi, lo = _split2(x)
    return jnp.concatenate(
        [_dot(tri, hi[r:r + ck, :]) + _dot(tri, lo[r:r + ck, :]) for r in range(0, x.shape[0], ck)], axis=0)


def _head_norm_gate(o_heads, norm_w, g):
    o = jnp.concatenate(
        [o_h * lax.rsqrt(jnp.mean(o_h * o_h, axis=-1, keepdims=True) + EPS) for o_h in o_heads], axis=1)
    return o * norm_w * (g * jax.nn.sigmoid(g))


def _hgrn_chunk_any_decay(c, q_ref, f_ref, i_ref, g_ref, tri_ref, wsel_ref, o_ref, st_ref, lb, norm_w):
    r0 = pl.multiple_of(c * CHUNK, CHUNK)
    rows = pl.ds(r0, CHUNK)
    d = q_ref.shape[1]
    heads = d // HEAD_DIM

    q = q_ref[rows, :].astype(F32)
    v = i_ref[rows, :]
    log2_f, kk = _gates(f_ref[rows, :].astype(F32), lb)
    b = _chunk_cumsum(tri_ref[:CHUNK, :CHUNK], log2_f)
    b_last = b[CHUNK - 1:CHUNK, :]

    q_in = (q * jnp.exp2(b)).astype(BF16)
    k_st = (kk * jnp.exp2(b_last - b)).astype(BF16)

    b_end = jnp.concatenate(
        [jnp.broadcast_to(b[(j + 1) * SUB - 1:(j + 1) * SUB, :], (SUB, d)) for j in range(NSUB)], axis=0)
    k_hat = kk * jnp.exp2(b_end - b)
    sub_id = lax.broadcasted_iota(jnp.int32, (CHUNK, 1), 0) // SUB
    q_from = []
    for j in range(NSUB - 1):
        lo_row = (j + 1) * SUB
        bj = b[lo_row - 1:lo_row, :]
        part = q[lo_row:, :] * jnp.exp2(b[lo_row:, :] - bj)
        q_from.append(jnp.concatenate([jnp.zeros((lo_row, d), F32), part], axis=0).astype(BF16))
    k_src = [jnp.where(sub_id == j, k_hat, 0.0).astype(BF16) for j in range(NSUB - 1)]

    per_head = [[] for _ in range(heads)]
    for r in range(NSUB):
        qb = q[r * SUB:(r + 1) * SUB, :]
        bb = b[r * SUB:(r + 1) * SUB, :]
        pieces = []
        for s in range(SUB):
            row = r * SUB + s
            e = jnp.exp2(jnp.minimum(bb - b[row:row + 1, :], 0.0))
            pieces.append((qb * kk[row:row + 1, :] * e).astype(BF16))
        for h in range(heads):
            per_head[h].append(jnp.concatenate(
                [p[:, h * HEAD_DIM:(h + 1) * HEAD_DIM] for p in pieces], axis=1))
    a_big = jnp.concatenate([jnp.concatenate(blocks, axis=0) for blocks in per_head], axis=0)
    diag = _dot(a_big, wsel_ref[...])

    t_id = lax.broadcasted_iota(jnp.int32, (CHUNK, CHUNK), 0)
    s_id = lax.broadcasted_iota(jnp.int32, (CHUNK, CHUNK), 1)
    diag_mask = (t_id // SUB == s_id // SUB) & (s_id <= t_id)

    outs = []
    for h in range(heads):
        hs = slice(h * HEAD_DIM, (h + 1) * HEAD_DIM)
        q_cat = jnp.concatenate([qf[:, hs] for qf in q_from], axis=1)
        k_cat = jnp.concatenate([ks[:, hs] for ks in k_src], axis=1)
        scores = _dot_nt(q_cat, k_cat) + jnp.where(diag_mask, diag[h * CHUNK:(h + 1) * CHUNK, :], 0.0)
        v_h = v[:, hs]
        state_t = st_ref[h]
        o_h = _dot_nt(q_in[:, hs], state_t.astype(BF16)) + _dot(scores.astype(BF16), v_h)
        st_ref[h] = state_t * jnp.exp2(b_last[:, hs]) + _dot_tn(v_h, k_st[:, hs])
        outs.append(o_h)
    o_ref[rows, :] = _head_norm_gate(outs, norm_w, g_ref[rows, :].astype(F32)).astype(o_ref.dtype)


def _hgrn_tile_mild_decay(b, kk, q_ref, i_ref, g_ref, o_ref, st_ref, norm_w):
    bt, d = b.shape
    heads = d // HEAD_DIM
    ck = FAST_CHUNK
    n_chunks = bt // ck
    q = q_ref[...].astype(F32)
    v = i_ref[...]
    chunk_decay = [jnp.exp2(b[(c + 1) * ck - 1:(c + 1) * ck, :]) for c in range(n_chunks)]
    q_in = (q * jnp.exp2(b)).astype(BF16)
    k_undecayed = kk * jnp.exp2(-b)
    k_out = k_undecayed.astype(BF16)
    k_st = (k_undecayed * jnp.concatenate(
        [jnp.broadcast_to(r, (ck, d)) for r in chunk_decay], axis=0)).astype(BF16)
    causal = (lax.broadcasted_iota(jnp.int32, (ck, ck), 1) <= lax.broadcasted_iota(jnp.int32, (ck, ck), 0))
    states = [st_ref[h] for h in range(heads)]
    tile_out = []
    for c in range(n_chunks):
        rs = slice(c * ck, (c + 1) * ck)
        carry_decay = chunk_decay[c]
        outs = []
        for h in range(heads):
            hs = slice(h * HEAD_DIM, (h + 1) * HEAD_DIM)
            scores = jnp.where(causal, _dot_nt(q_in[rs, hs], k_out[rs, hs]), 0.0).astype(BF16)
            outs.append(_dot(scores, v[rs, hs]) + _dot_nt(q_in[rs, hs], states[h].astype(BF16)))
            states[h] = states[h] * carry_decay[:, hs] + _dot_tn(v[rs, hs], k_st[rs, hs])
        tile_out.append(jnp.concatenate(outs, axis=1))
    for h in range(heads):
        st_ref[h] = states[h]
    heads_out = jnp.concatenate(tile_out, axis=0)
    o_heads = [heads_out[:, h * HEAD_DIM:(h + 1) * HEAD_DIM] for h in range(heads)]
    o_ref[...] = _head_norm_gate(o_heads, norm_w, g_ref[...].astype(F32)).astype(o_ref.dtype)


def _hgrn_kernel(q_ref, f_ref, i_ref, g_ref, lbl_ref, nw_ref, tri_ref, wsel_ref, *rest, layer, copies):
    n_copy = len(copies.arrays)
    copy_in, (o_ref, *copy_out, st_ref) = rest[:n_copy], rest[n_copy:]
    step = pl.program_id(0)

    @pl.when(step == 0)
    def _():
        st_ref[...] = jnp.zeros_like(st_ref)

    copies.run(step, copy_in, copy_out)

    logits = lbl_ref[...]
    e = jnp.exp(logits - jnp.max(logits, axis=0, keepdims=True))
    lb = jnp.sum(e[:layer + 1, :], axis=0, keepdims=True) / jnp.sum(e, axis=0, keepdims=True)
    norm_w = nw_ref[...]

    log2_f, kk = _gates(f_ref[...].astype(F32), lb)
    b = _chunk_cumsum(tri_ref[...], log2_f)
    mild = jnp.min(b) >= MILD_DECAY_LOG2

    @pl.when(mild)
    def _():
        _hgrn_tile_mild_decay(b, kk, q_ref, i_ref, g_ref, o_ref, st_ref, norm_w)

    @pl.when(jnp.logical_not(mild))
    def _():
        def body(c, carry):
            _hgrn_chunk_any_decay(c, q_ref, f_ref, i_ref, g_ref, tri_ref, wsel_ref, o_ref, st_ref, lb, norm_w)
            return carry

        lax.fori_loop(0, q_ref.shape[0] // CHUNK, body, 0)


def _hgrn(proj, lb_logits, norm_w, to_bf16, *, bt, layer, q_block, f_block, i_block, g_block):
    s = proj.shape[0]
    d = norm_w.shape[1]
    heads = d // HEAD_DIM
    steps = s // bt
    copies = _Bf16Copies(to_bf16, steps, lambda t: t)
    t_id = lax.broadcasted_iota(jnp.int32, (FAST_CHUNK, FAST_CHUNK), 0)
    s_id = lax.broadcasted_iota(jnp.int32, (FAST_CHUNK, FAST_CHUNK), 1)
    tri = (s_id <= t_id).astype(BF16)
    src = lax.broadcasted_iota(jnp.int32, (SUB * HEAD_DIM, CHUNK), 0) // HEAD_DIM
    col = lax.broadcasted_iota(jnp.int32, (SUB * HEAD_DIM, CHUNK), 1) % SUB
    wsel = (src == col).astype(BF16)
    nl = lb_logits.shape[0]
    og, *as_bf16 = pl.pallas_call(
        functools.partial(_hgrn_kernel, layer=layer, copies=copies),
        grid=(steps,),
        in_specs=[
            pl.BlockSpec((bt, d), lambda t: (t, q_block)),
            pl.BlockSpec((bt, d), lambda t: (t, f_block)),
            pl.BlockSpec((bt, d), lambda t: (t, i_block)),
            pl.BlockSpec((bt, d), lambda t: (t, g_block)),
            _resident((nl, d)),
            _resident((1, d)),
            _resident((FAST_CHUNK, FAST_CHUNK)),
            _resident((SUB * HEAD_DIM, CHUNK)),
            *copies.specs,
        ],
        out_specs=[pl.BlockSpec((bt, d), lambda t: (t, 0)), *copies.specs],
        out_shape=[jax.ShapeDtypeStruct((s, d), BF16), *copies.out_shapes],
        scratch_shapes=[pltpu.VMEM((heads, HEAD_DIM, HEAD_DIM), F32)],
        compiler_params=_params("arbitrary"),
        name="hgrn",
    )(proj, proj, proj, proj, lb_logits, norm_w, tri, wsel, *to_bf16)
    return og, as_bf16


def _merge_kernel(og_ref, gb0_ref, gb1_ref, ya_ref, x_ref, wb_ref, wo_ref, *rest, copies):
    n_copy = len(copies.arrays)
    copy_in, (o_ref, *copy_out) = rest[:n_copy], rest[n_copy:]
    copies.run(pl.program_id(0), copy_in, copy_out)
    y_b = _dot(og_ref[...], wb_ref[...])
    gate_b = jnp.concatenate([gb0_ref[...], gb1_ref[...]], axis=1).astype(F32)
    merged = ya_ref[...].astype(F32) + jax.nn.sigmoid(gate_b) * y_b
    o_ref[...] = x_ref[...] + _dot(merged.astype(BF16), wo_ref[...])


def _merge(og, proj, ya, x, w_b, w_out, to_bf16, *, bm, gb_block0):
    s, d = x.shape
    c = og.shape[1]
    half = d // 2
    copies = _Bf16Copies(to_bf16, s // bm, lambda m: m)
    x_new, *as_bf16 = pl.pallas_call(
        functools.partial(_merge_kernel, copies=copies),
        grid=(s // bm,),
        in_specs=[
            pl.BlockSpec((bm, c), lambda m: (m, 0)),
            pl.BlockSpec((bm, half), lambda m: (m, gb_block0)),
            pl.BlockSpec((bm, half), lambda m: (m, gb_block0 + 1)),
            pl.BlockSpec((bm, d), lambda m: (m, 0)),
            pl.BlockSpec((bm, d), lambda m: (m, 0)),
            _resident((c, d)),
            _resident((d, d)),
            *copies.specs,
        ],
        out_specs=[pl.BlockSpec((bm, d), lambda m: (m, 0)), *copies.specs],
        out_shape=[jax.ShapeDtypeStruct((s, d), F32), *copies.out_shapes],
        compiler_params=_params("arbitrary"),
        name="merge_out",
    )(og, proj, proj, ya, x, w_b, w_out, *to_bf16)
    return x_new, as_bf16


def _causal_conv(z_ref, row0, cols, w_ref, b_ref):
    ext = z_ref[pl.ds(row0, CARRY_ROWS + ACT_ROWS), cols]
    y = b_ref[:, cols] + w_ref[CONV_WIDTH - 1:CONV_WIDTH, cols] * ext[CARRY_ROWS:, :]
    for back in range(1, CONV_WIDTH):
        tap = CONV_WIDTH - 1 - back
        y = y + w_ref[tap:tap + 1, cols] * pltpu.roll(ext, back, axis=0)[CARRY_ROWS:, :]
    return y


def _ffn_kernel(x_ref, lnw_ref, wug_ref, wuv_ref, cwg_ref, cwv_ref, cbg_ref, cbv_ref, wd_ref,
                o_ref, h_ref, zg_ref, zv_ref, a_ref, cg_ref, cv_ref, *, dff):
    m = pl.program_id(0)
    j = pl.program_id(1)
    bm = h_ref.shape[0]
    tf = wd_ref.shape[0]
    shared = j * tf - jnp.minimum(j * tf, dff - tf)

    @pl.when(j == 0)
    def _():
        x = x_ref[...]
        h_ref[...] = _rms(x, lnw_ref[...]).astype(BF16)
        o_ref[...] = x

    @pl.when(m == 0)
    def _():
        cg_ref[j] = jnp.zeros(cg_ref.shape[1:], F32)
        cv_ref[j] = jnp.zeros(cv_ref.shape[1:], F32)

    h = h_ref[...]
    zg_ref[:CARRY_ROWS, :] = cg_ref[j]
    zv_ref[:CARRY_ROWS, :] = cv_ref[j]
    zg_ref[CARRY_ROWS:, :] = _dot(h, wug_ref[...])
    zv_ref[CARRY_ROWS:, :] = _dot(h, wuv_ref[...])
    cg_ref[j] = zg_ref[bm:, :]
    cv_ref[j] = zv_ref[bm:, :]

    for c in range(tf // LANE):
        cols = slice(c * LANE, (c + 1) * LANE)
        fresh = (lax.broadcasted_iota(jnp.int32, (1, LANE), 1) + c * LANE) >= shared
        for r in range(bm // ACT_ROWS):
            ug = _causal_conv(zg_ref, r * ACT_ROWS, cols, cwg_ref, cbg_ref)
            uv = _causal_conv(zv_ref, r * ACT_ROWS, cols, cwv_ref, cbv_ref)
            a_ref[r * ACT_ROWS:(r + 1) * ACT_ROWS, cols] = jnp.where(
                fresh, ug * jax.nn.sigmoid(ug) * uv, 0.0).astype(BF16)

    o_ref[...] += _dot(a_ref[...], wd_ref[...])


def _ffn(x, ln_w, w_up, conv_w, conv_b, w_down, *, bm, tf):
    s, d = x.shape
    dff = w_down.shape[0]
    nj = pl.cdiv(dff, tf)
    assert dff >= tf and dff % LANE == 0

    def start(j, base=0):
        return (base // LANE + jnp.minimum(j * (tf // LANE), (dff - tf) // LANE)) * LANE

    def window(rows, offset):
        return pl.BlockSpec((pl.Element(rows), pl.Element(tf)), lambda m, j: (0, offset(j)))

    return pl.pallas_call(
        functools.partial(_ffn_kernel, dff=dff),
        grid=(s // bm, nj),
        in_specs=[
            pl.BlockSpec((bm, d), lambda m, j: (m, 0)),
            pl.BlockSpec((1, d), lambda m, j: (0, 0)),
            window(d, start),
            window(d, lambda j: start(j, dff)),
            window(CONV_WIDTH, start),
            window(CONV_WIDTH, lambda j: start(j, dff)),
            window(1, start),
            window(1, lambda j: start(j, dff)),
            pl.BlockSpec((pl.Element(tf), pl.Element(d)), lambda m, j: (start(j), 0)),
        ],
        out_specs=pl.BlockSpec((bm, d), lambda m, j: (m, 0)),
        out_shape=jax.ShapeDtypeStruct((s, d), F32),
        scratch_shapes=[
            pltpu.VMEM((bm, d), BF16),
            pltpu.VMEM((CARRY_ROWS + bm, tf), F32),
            pltpu.VMEM((CARRY_ROWS + bm, tf), F32),
            pltpu.VMEM((bm, tf), BF16),
            pltpu.VMEM((nj, CARRY_ROWS, tf), F32),
            pltpu.VMEM((nj, CARRY_ROWS, tf), F32),
        ],
        compiler_params=_params("arbitrary", "arbitrary"),
        name="ffn",
    )(x, ln_w, w_up, w_up, conv_w, conv_w, conv_b, conv_b, w_down)


def _ple_kernel(x_ref, p_ref, lnw_ref, wg_ref, wp_ref, lnf_ref, o_ref, *, final):
    x = x_ref[...]
    h = _rms(x, lnw_ref[...]).astype(BF16)
    gate = jax.nn.sigmoid(_dot(h, wg_ref[...]))
    emb = _dot(p_ref[...].astype(BF16), wp_ref[...])
    y = x + gate * emb
    o_ref[...] = _rms(y, lnf_ref[...]) if final else y


def _ple(x, p, ln_w, w_gate, w_ple, ln_final, *, bm, final):
    s, d = x.shape
    e = p.shape[1]
    return pl.pallas_call(
        functools.partial(_ple_kernel, final=final),
        grid=(s // bm,),
        in_specs=[
            pl.BlockSpec((bm, d), lambda m: (m, 0)),
            pl.BlockSpec((bm, e), lambda m: (m, 0)),
            _resident((1, d)),
            _resident((d, d)),
            _resident((e, d)),
            _resident((1, d)),
        ],
        out_specs=pl.BlockSpec((bm, d), lambda m: (m, 0)),
        out_shape=jax.ShapeDtypeStruct((s, d), F32),
        compiler_params=_params("arbitrary"),
        name="ple_final",
    )(x, p, ln_w, w_gate, w_ple, ln_final)


def kernel(x, p, ln_mix_w, w_in, pool_mix_w, pool_scale, hgrn_lb_logits, hgrn_norm_w, w_branch_a,
           w_branch_b, w_out, ln_ffn_w, w_up, conv_w, conv_b, w_down, ln_ple_w, w_ple_gate, w_ple,
           ln_final_w):
    batch, seq, d = x.shape
    depth = w_in.shape[0]
    c_pool = w_branch_a.shape[1]
    c_hgrn = w_branch_b.shape[1]
    blk = c_hgrn
    assert c_pool == blk and d == 2 * blk, "column blocks of the combined projection must line up"
    assert seq % 1024 == 0
    tf = 512
    outs = []
    for bi in range(batch):
        xb = x[bi]
        for i in range(depth):
            proj = _inproj(xb, ln_mix_w[i][None], w_in[i], bm=2048, bn=blk)
            mix = pool_mix_w[i]
            og, (mix_b, wa_b, wu_b) = _hgrn(
                proj, hgrn_lb_logits, hgrn_norm_w[i][None],
                [mix.reshape(-1, mix.shape[-1]), w_branch_a[i], w_up[i]],
                bt=512, layer=i, q_block=1, f_block=2, i_block=3, g_block=4)
            ya, (wb_b, wo_b) = _pool_branch(
                proj, mix_b.reshape(mix.shape), pool_scale[i][None], wa_b,
                [w_branch_b[i], w_out[i]], bm=1024, bn=blk, ga_block0=5)
            xb, (wd_b, wg_b, wp_b) = _merge(og, proj, ya, xb, wb_b, wo_b,
                                            [w_down[i], w_ple_gate[i], w_ple[i]], bm=512, gb_block0=7)
            xb = _ffn(xb, ln_ffn_w[i][None], wu_b, conv_w[i], conv_b[i][None], wd_b, bm=1024, tf=512)
            xb = _ple(xb, p[i, bi], ln_ple_w[i][None], wg_b, wp_b, ln_final_w[None], bm=1024,
                      final=(i == depth - 1))
        outs.append(xb)
    return jnp.stack(outs, axis=0)
```

```python
import functools

import jax
import jax.numpy as jnp
from jax import lax
from jax.experimental import pallas as pl
from jax.experimental.pallas import tpu as pltpu

F32 = jnp.float32
BF16 = jnp.bfloat16

EPS = 1e-6
POOL_WINDOWS = (2, 4, 8, 16)
POOL_HALO = 16
HEAD_DIM = 128
CHUNK = 64
FAST_CHUNK = 128
SUB = 16
NSUB = CHUNK // SUB
CONV_WIDTH = 3
CARRY_ROWS = 8
ACT_ROWS = 64
LANE = 128
BF16_TILE_ROWS = 16
LOG2_E = 1.4426950408889634
MILD_DECAY_LOG2 = -100.0
V7X_VMEM_BYTES = 64 * 1024 * 1024
VMEM_LIMIT = V7X_VMEM_BYTES - 3 * 1024 * 1024


def _dot(a, b):
    return jnp.dot(a, b, preferred_element_type=F32)


def _dot_nt(a, b):
    return lax.dot_general(a, b, (((1,), (1,)), ((), ())), preferred_element_type=F32)


def _dot_tn(a, b):
    return lax.dot_general(a, b, (((0,), (0,)), ((), ())), preferred_element_type=F32)


def _rms(x, w):
    return x * lax.rsqrt(jnp.mean(x * x, axis=-1, keepdims=True) + EPS) * w


def _params(*sem):
    return pltpu.CompilerParams(dimension_semantics=sem, vmem_limit_bytes=VMEM_LIMIT)


class _Bf16Copies:
    def __init__(self, arrays, steps, step_of):
        self.arrays = list(arrays)
        self.specs, self.counts = [], []
        for a in self.arrays:
            rows, cols = a.shape
            if rows % (steps * BF16_TILE_ROWS) == 0:
                n, block = steps, (rows // steps, cols)
                index = lambda *g: (step_of(*g), 0)
            else:
                assert cols % LANE == 0
                n = max(k for k in range(1, steps + 1) if (cols // LANE) % k == 0)
                block = (rows, cols // n)
                index = lambda *g, n=n: (0, jnp.minimum(step_of(*g), n - 1))
            self.specs.append(pl.BlockSpec(block, index))
            self.counts.append(n)
        self.out_shapes = [jax.ShapeDtypeStruct(a.shape, BF16) for a in self.arrays]

    def run(self, step, srcs, dsts):
        for src, dst, n in zip(srcs, dsts, self.counts):
            @pl.when(step < n)
            def _():
                dst[...] = src[...].astype(dst.dtype)


def _resident(shape):
    return pl.BlockSpec(shape, lambda *_: (0,) * len(shape), pipeline_mode=pl.Buffered(1))


def _inproj_kernel(x_ref, lnw_ref, w_ref, o_ref, h_ref):
    @pl.when(pl.program_id(1) == 0)
    def _():
        h_ref[...] = _rms(x_ref[...], lnw_ref[...]).astype(BF16)

    o_ref[...] = _dot(h_ref[...], w_ref[...].astype(BF16)).astype(o_ref.dtype)


def _inproj(x, ln_w, w_in, *, bm, bn):
    s, d = x.shape
    d_in = w_in.shape[1]
    return pl.pallas_call(
        _inproj_kernel,
        grid=(s // bm, d_in // bn),
        in_specs=[
            pl.BlockSpec((bm, d), lambda m, n: (m, 0), pipeline_mode=pl.Buffered(1)),
            pl.BlockSpec((1, d), lambda m, n: (0, 0)),
            pl.BlockSpec((d, bn), lambda m, n: (0, n)),
        ],
        out_specs=pl.BlockSpec((bm, bn), lambda m, n: (m, n)),
        out_shape=jax.ShapeDtypeStruct((s, d_in), BF16),
        scratch_shapes=[pltpu.VMEM((bm, d), BF16)],
        compiler_params=_params("arbitrary", "arbitrary"),
        name="inproj",
    )(x, ln_w, w_in)


def _pool_kernel(u_ref, halo_ref, ga_ref, mix_ref, scale_ref, wa_ref, *rest, bm, copies):
    n_copy = len(copies.arrays)
    copy_in, (o_ref, *copy_out, feat_ref) = rest[:n_copy], rest[n_copy:]
    m = pl.program_id(0)
    n = pl.program_id(1)
    copies.run(m * pl.num_programs(1) + n, copy_in, copy_out)

    @pl.when(n == 0)
    def _():
        u = u_ref[...].astype(F32)
        halo = halo_ref[...].astype(F32) * (m > 0).astype(F32)
        ext = jnp.concatenate([halo, u], axis=0)
        gw = u.shape[1] // len(POOL_WINDOWS)
        pos = m * bm + lax.broadcasted_iota(jnp.int32, (bm, 1), 0) + 1
        run = ext
        width = 1
        feats = []
        for g, w in enumerate(POOL_WINDOWS):
            while width < w:
                run = run + pltpu.roll(run, width, axis=0)
                width *= 2
            cnt = jnp.minimum(pos, w).astype(F32)
            win = run[POOL_HALO:, :gw]
            if g + 1 < len(POOL_WINDOWS):
                run = run[:, gw:]
            d = win / cnt - u[:, g * gw:(g + 1) * gw]
            y = _dot(d.astype(BF16), mix_ref[g])
            feats.append(y * scale_ref[:, g * gw:(g + 1) * gw])
        feat_ref[...] = jnp.concatenate(feats, axis=1).astype(BF16)

    y_a = _dot(feat_ref[...], wa_ref[...])
    o_ref[...] = (jax.nn.sigmoid(ga_ref[...].astype(F32)) * y_a).astype(o_ref.dtype)


def _pool_branch(proj, mix_w, scale, w_a, to_bf16, *, bm, bn, ga_block0):
    s = proj.shape[0]
    c = w_a.shape[0]
    d = w_a.shape[1]
    g, gw, _ = mix_w.shape
    halo_per_tile = bm // POOL_HALO
    n_col = d // bn
    copies = _Bf16Copies(to_bf16, (s // bm) * n_col, lambda m, n: m * n_col + n)
    ya, *as_bf16 = pl.pallas_call(
        functools.partial(_pool_kernel, bm=bm, copies=copies),
        grid=(s // bm, n_col),
        in_specs=[
            pl.BlockSpec((bm, c), lambda m, n: (m, 0)),
            pl.BlockSpec((POOL_HALO, c), lambda m, n: (jnp.maximum(m * halo_per_tile - 1, 0), 0)),
            pl.BlockSpec((bm, bn), lambda m, n: (m, ga_block0 + n)),
            _resident((g, gw, gw)),
            _resident((1, c)),
            pl.BlockSpec((c, bn), lambda m, n: (0, n)),
            *copies.specs,
        ],
        out_specs=[pl.BlockSpec((bm, bn), lambda m, n: (m, n)), *copies.specs],
        out_shape=[jax.ShapeDtypeStruct((s, d), BF16), *copies.out_shapes],
        scratch_shapes=[pltpu.VMEM((bm, c), BF16)],
        compiler_params=_params("arbitrary", "arbitrary"),
        name="pool_branch",
    )(proj, proj, proj, mix_w, scale, w_a, *to_bf16)
    return ya, as_bf16


def _split2(x):
    hi = x.astype(BF16)
    return hi, (x - hi.astype(F32)).astype(BF16)


def _gates(fl, lb):
    one_m_lb = 1.0 - lb
    sg = jax.nn.sigmoid(fl)
    log2_f = jnp.log(lb + one_m_lb * sg) * LOG2_E
    return log2_f, one_m_lb * (1.0 - sg)


def _chunk_cumsum(tri, x):
    ck = tri.shape[0]
    hi, lo = _split2(x)
    return jnp.concatenate(
        [_dot(tri, hi[r:r + ck, :]) + _dot(tri, lo[r:r + ck, :]) for r in range(0, x.shape[0], ck)], axis=0)


def _head_norm_gate(o_heads, norm_w, g):
    o = jnp.concatenate(
        [o_h * lax.rsqrt(jnp.mean(o_h * o_h, axis=-1, keepdims=True) + EPS) for o_h in o_heads], axis=1)
    return o * norm_w * (g * jax.nn.sigmoid(g))


def _hgrn_chunk_any_decay(c, q_ref, f_ref, i_ref, g_ref, tri_ref, wsel_ref, o_ref, st_ref, lb, norm_w):
    r0 = pl.multiple_of(c * CHUNK, CHUNK)
    rows = pl.ds(r0, CHUNK)
    d = q_ref.shape[1]
    heads = d // HEAD_DIM

    q = q_ref[rows, :].astype(F32)
    v = i_ref[rows, :]
    log2_f, kk = _gates(f_ref[rows, :].astype(F32), lb)
    b = _chunk_cumsum(tri_ref[:CHUNK, :CHUNK], log2_f)
    b_last = b[CHUNK - 1:CHUNK, :]

    q_in = (q * jnp.exp2(b)).astype(BF16)
    k_st = (kk * jnp.exp2(b_last - b)).astype(BF16)

    b_end = jnp.concatenate(
        [jnp.broadcast_to(b[(j + 1) * SUB - 1:(j + 1) * SUB, :], (SUB, d)) for j in range(NSUB)], axis=0)
    k_hat = kk * jnp.exp2(b_end - b)
    sub_id = lax.broadcasted_iota(jnp.int32, (CHUNK, 1), 0) // SUB
    q_from = []
    for j in range(NSUB - 1):
        lo_row = (j + 1) * SUB
        bj = b[lo_row - 1:lo_row, :]
        part = q[lo_row:, :] * jnp.exp2(b[lo_row:, :] - bj)
        q_from.append(jnp.concatenate([jnp.zeros((lo_row, d), F32), part], axis=0).astype(BF16))
    k_src = [jnp.where(sub_id == j, k_hat, 0.0).astype(BF16) for j in range(NSUB - 1)]

    per_head = [[] for _ in range(heads)]
    for r in range(NSUB):
        qb = q[r * SUB:(r + 1) * SUB, :]
        bb = b[r * SUB:(r + 1) * SUB, :]
        pieces = []
        for s in range(SUB):
            row = r * SUB + s
            e = jnp.exp2(jnp.minimum(bb - b[row:row + 1, :], 0.0))
            pieces.append((qb * kk[row:row + 1, :] * e).astype(BF16))
        for h in range(heads):
            per_head[h].append(jnp.concatenate(
                [p[:, h * HEAD_DIM:(h + 1) * HEAD_DIM] for p in pieces], axis=1))
    a_big = jnp.concatenate([jnp.concatenate(blocks, axis=0) for blocks in per_head], axis=0)
    diag = _dot(a_big, wsel_ref[...])

    t_id = lax.broadcasted_iota(jnp.int32, (CHUNK, CHUNK), 0)
    s_id = lax.broadcasted_iota(jnp.int32, (CHUNK, CHUNK), 1)
    diag_mask = (t_id // SUB == s_id // SUB) & (s_id <= t_id)

    outs = []
    for h in range(heads):
        hs = slice(h * HEAD_DIM, (h + 1) * HEAD_DIM)
        q_cat = jnp.concatenate([qf[:, hs] for qf in q_from], axis=1)
        k_cat = jnp.concatenate([ks[:, hs] for ks in k_src], axis=1)
        scores = _dot_nt(q_cat, k_cat) + jnp.where(diag_mask, diag[h * CHUNK:(h + 1) * CHUNK, :], 0.0)
        v_h = v[:, hs]
        state_t = st_ref[h]
        o_h = _dot_nt(q_in[:, hs], state_t.astype(BF16)) + _dot(scores.astype(BF16), v_h)
        st_ref[h] = state_t * jnp.exp2(b_last[:, hs]) + _dot_tn(v_h, k_st[:, hs])
        outs.append(o_h)
    o_ref[rows, :] = _head_norm_gate(outs, norm_w, g_ref[rows, :].astype(F32)).astype(o_ref.dtype)


def _hgrn_tile_mild_decay(b, kk, q_ref, i_ref, g_ref, o_ref, st_ref, norm_w):
    bt, d = b.shape
    heads = d // HEAD_DIM
    ck = FAST_CHUNK
    n_chunks = bt // ck
    q = q_ref[...].astype(F32)
    v = i_ref[...]
    chunk_decay = [jnp.exp2(b[(c + 1) * ck - 1:(c + 1) * ck, :]) for c in range(n_chunks)]
    q_in = (q * jnp.exp2(b)).astype(BF16)
    k_undecayed = kk * jnp.exp2(-b)
    k_out = k_undecayed.astype(BF16)
    k_st = (k_undecayed * jnp.concatenate(
        [jnp.broadcast_to(r, (ck, d)) for r in chunk_decay], axis=0)).astype(BF16)
    causal = (lax.broadcasted_iota(jnp.int32, (ck, ck), 1) <= lax.broadcasted_iota(jnp.int32, (ck, ck), 0))
    states = [st_ref[h] for h in range(heads)]
    tile_out = []
    for c in range(n_chunks):
        rs = slice(c * ck, (c + 1) * ck)
        carry_decay = chunk_decay[c]
        outs = []
        for h in range(heads):
            hs = slice(h * HEAD_DIM, (h + 1) * HEAD_DIM)
            scores = jnp.where(causal, _dot_nt(q_in[rs, hs], k_out[rs, hs]), 0.0).astype(BF16)
            outs.append(_dot(scores, v[rs, hs]) + _dot_nt(q_in[rs, hs], states[h].astype(BF16)))
            states[h] = states[h] * carry_decay[:, hs] + _dot_tn(v[rs, hs], k_st[rs, hs])
        tile_out.append(jnp.concatenate(outs, axis=1))
    for h in range(heads):
        st_ref[h] = states[h]
    heads_out = jnp.concatenate(tile_out, axis=0)
    o_heads = [heads_out[:, h * HEAD_DIM:(h + 1) * HEAD_DIM] for h in range(heads)]
    o_ref[...] = _head_norm_gate(o_heads, norm_w, g_ref[...].astype(F32)).astype(o_ref.dtype)


def _hgrn_kernel(q_ref, f_ref, i_ref, g_ref, lbl_ref, nw_ref, tri_ref, wsel_ref, *rest, layer, copies):
    n_copy = len(copies.arrays)
    copy_in, (o_ref, *copy_out, st_ref) = rest[:n_copy], rest[n_copy:]
    step = pl.program_id(0)

    @pl.when(step == 0)
    def _():
        st_ref[...] = jnp.zeros_like(st_ref)

    copies.run(step, copy_in, copy_out)

    logits = lbl_ref[...]
    e = jnp.exp(logits - jnp.max(logits, axis=0, keepdims=True))
    lb = jnp.sum(e[:layer + 1, :], axis=0, keepdims=True) / jnp.sum(e, axis=0, keepdims=True)
    norm_w = nw_ref[...]

    log2_f, kk = _gates(f_ref[...].astype(F32), lb)
    b = _chunk_cumsum(tri_ref[...], log2_f)
    mild = jnp.min(b) >= MILD_DECAY_LOG2

    @pl.when(mild)
    def _():
        _hgrn_tile_mild_decay(b, kk, q_ref, i_ref, g_ref, o_ref, st_ref, norm_w)

    @pl.when(jnp.logical_not(mild))
    def _():
        def body(c, carry):
            _hgrn_chunk_any_decay(c, q_ref, f_ref, i_ref, g_ref, tri_ref, wsel_ref, o_ref, st_ref, lb, norm_w)
            return carry

        lax.fori_loop(0, q_ref.shape[0] // CHUNK, body, 0)


def _hgrn(proj, lb_logits, norm_w, to_bf16, *, bt, layer, q_block, f_block, i_block, g_block):
    s = proj.shape[0]
    d = norm_w.shape[1]
    heads = d // HEAD_DIM
    steps = s // bt
    copies = _Bf16Copies(to_bf16, steps, lambda t: t)
    t_id = lax.broadcasted_iota(jnp.int32, (FAST_CHUNK, FAST_CHUNK), 0)
    s_id = lax.broadcasted_iota(jnp.int32, (FAST_CHUNK, FAST_CHUNK), 1)
    tri = (s_id <= t_id).astype(BF16)
    src = lax.broadcasted_iota(jnp.int32, (SUB * HEAD_DIM, CHUNK), 0) // HEAD_DIM
    col = lax.broadcasted_iota(jnp.int32, (SUB * HEAD_DIM, CHUNK), 1) % SUB
    wsel = (src == col).astype(BF16)
    nl = lb_logits.shape[0]
    og, *as_bf16 = pl.pallas_call(
        functools.partial(_hgrn_kernel, layer=layer, copies=copies),
        grid=(steps,),
        in_specs=[
            pl.BlockSpec((bt, d), lambda t: (t, q_block)),
            pl.BlockSpec((bt, d), lambda t: (t, f_block)),
            pl.BlockSpec((bt, d), lambda t: (t, i_block)),
            pl.BlockSpec((bt, d), lambda t: (t, g_block)),
            _resident((nl, d)),
            _resident((1, d)),
            _resident((FAST_CHUNK, FAST_CHUNK)),
            _resident((SUB * HEAD_DIM, CHUNK)),
            *copies.specs,
        ],
        out_specs=[pl.BlockSpec((bt, d), lambda t: (t, 0)), *copies.specs],
        out_shape=[jax.ShapeDtypeStruct((s, d), BF16), *copies.out_shapes],
        scratch_shapes=[pltpu.VMEM((heads, HEAD_DIM, HEAD_DIM), F32)],
        compiler_params=_params("arbitrary"),
        name="hgrn",
    )(proj, proj, proj, proj, lb_logits, norm_w, tri, wsel, *to_bf16)
    return og, as_bf16


def _merge_kernel(og_ref, gb0_ref, gb1_ref, ya_ref, x_ref, wb_ref, wo_ref, *rest, copies):
    n_copy = len(copies.arrays)
    copy_in, (o_ref, *copy_out) = rest[:n_copy], rest[n_copy:]
    copies.run(pl.program_id(0), copy_in, copy_out)
    y_b = _dot(og_ref[...], wb_ref[...])
    gate_b = jnp.concatenate([gb0_ref[...], gb1_ref[...]], axis=1).astype(F32)
    merged = ya_ref[...].astype(F32) + jax.nn.sigmoid(gate_b) * y_b
    o_ref[...] = x_ref[...] + _dot(merged.astype(BF16), wo_ref[...])


def _merge(og, proj, ya, x, w_b, w_out, to_bf16, *, bm, gb_block0):
    s, d = x.shape
    c = og.shape[1]
    half = d // 2
    copies = _Bf16Copies(to_bf16, s // bm, lambda m: m)
    x_new, *as_bf16 = pl.pallas_call(
        functools.partial(_merge_kernel, copies=copies),
        grid=(s // bm,),
        in_specs=[
            pl.BlockSpec((bm, c), lambda m: (m, 0)),
            pl.BlockSpec((bm, half), lambda m: (m, gb_block0)),
            pl.BlockSpec((bm, half), lambda m: (m, gb_block0 + 1)),
            pl.BlockSpec((bm, d), lambda m: (m, 0)),
            pl.BlockSpec((bm, d), lambda m: (m, 0)),
            _resident((c, d)),
            _resident((d, d)),
            *copies.specs,
        ],
        out_specs=[pl.BlockSpec((bm, d), lambda m: (m, 0)), *copies.specs],
        out_shape=[jax.ShapeDtypeStruct((s, d), F32), *copies.out_shapes],
        compiler_params=_params("arbitrary"),
        name="merge_out",
    )(og, proj, proj, ya, x, w_b, w_out, *to_bf16)
    return x_new, as_bf16


def _causal_conv(z_ref, row0, cols, w_ref, b_ref):
    ext = z_ref[pl.ds(row0, CARRY_ROWS + ACT_ROWS), cols]
    y = b_ref[:, cols] + w_ref[CONV_WIDTH - 1:CONV_WIDTH, cols] * ext[CARRY_ROWS:, :]
    for back in range(1, CONV_WIDTH):
        tap = CONV_WIDTH - 1 - back
        y = y + w_ref[tap:tap + 1, cols] * pltpu.roll(ext, back, axis=0)[CARRY_ROWS:, :]
    return y


def _ffn_kernel(x_ref, lnw_ref, wug_ref, wuv_ref, cwg_ref, cwv_ref, cbg_ref, cbv_ref, wd_ref,
                o_ref, h_ref, zg_ref, zv_ref, a_ref, cg_ref, cv_ref, *, dff):
    m = pl.program_id(0)
    j = pl.program_id(1)
    bm = h_ref.shape[0]
    tf = wd_ref.shape[0]
    shared = j * tf - jnp.minimum(j * tf, dff - tf)

    @pl.when(j == 0)
    def _():
        x = x_ref[...]
        h_ref[...] = _rms(x, lnw_ref[...]).astype(BF16)
        o_ref[...] = x

    @pl.when(m == 0)
    def _():
        cg_ref[j] = jnp.zeros(cg_ref.shape[1:], F32)
        cv_ref[j] = jnp.zeros(cv_ref.shape[1:], F32)

    h = h_ref[...]
    zg_ref[:CARRY_ROWS, :] = cg_ref[j]
    zv_ref[:CARRY_ROWS, :] = cv_ref[j]
    zg_ref[CARRY_ROWS:, :] = _dot(h, wug_ref[...])
    zv_ref[CARRY_ROWS:, :] = _dot(h, wuv_ref[...])
    cg_ref[j] = zg_ref[bm:, :]
    cv_ref[j] = zv_ref[bm:, :]

    for c in range(tf // LANE):
        cols = slice(c * LANE, (c + 1) * LANE)
        fresh = (lax.broadcasted_iota(jnp.int32, (1, LANE), 1) + c * LANE) >= shared
        for r in range(bm // ACT_ROWS):
            ug = _causal_conv(zg_ref, r * ACT_ROWS, cols, cwg_ref, cbg_ref)
            uv = _causal_conv(zv_ref, r * ACT_ROWS, cols, cwv_ref, cbv_ref)
            a_ref[r * ACT_ROWS:(r + 1) * ACT_ROWS, cols] = jnp.where(
                fresh, ug * jax.nn.sigmoid(ug) * uv, 0.0).astype(BF16)

    o_ref[...] += _dot(a_ref[...], wd_ref[...])


def _ffn(x, ln_w, w_up, conv_w, conv_b, w_down, *, bm, tf):
    s, d = x.shape
    dff = w_down.shape[0]
    nj = pl.cdiv(dff, tf)
    assert dff >= tf and dff % LANE == 0

    def start(j, base=0):
        return (base // LANE + jnp.minimum(j * (tf // LANE), (dff - tf) // LANE)) * LANE

    def window(rows, offset):
        return pl.BlockSpec((pl.Element(rows), pl.Element(tf)), lambda m, j: (0, offset(j)))

    return pl.pallas_call(
        functools.partial(_ffn_kernel, dff=dff),
        grid=(s // bm, nj),
        in_specs=[
            pl.BlockSpec((bm, d), lambda m, j: (m, 0)),
            pl.BlockSpec((1, d), lambda m, j: (0, 0)),
            window(d, start),
            window(d, lambda j: start(j, dff)),
            window(CONV_WIDTH, start),
            window(CONV_WIDTH, lambda j: start(j, dff)),
            window(1, start),
            window(1, lambda j: start(j, dff)),
            pl.BlockSpec((pl.Element(tf), pl.Element(d)), lambda m, j: (start(j), 0)),
        ],
        out_specs=pl.BlockSpec((bm, d), lambda m, j: (m, 0)),
        out_shape=jax.ShapeDtypeStruct((s, d), F32),
        scratch_shapes=[
            pltpu.VMEM((bm, d), BF16),
            pltpu.VMEM((CARRY_ROWS + bm, tf), F32),
            pltpu.VMEM((CARRY_ROWS + bm, tf), F32),
            pltpu.VMEM((bm, tf), BF16),
            pltpu.VMEM((nj, CARRY_ROWS, tf), F32),
            pltpu.VMEM((nj, CARRY_ROWS, tf), F32),
        ],
        compiler_params=_params("arbitrary", "arbitrary"),
        name="ffn",
    )(x, ln_w, w_up, w_up, conv_w, conv_w, conv_b, conv_b, w_down)


def _ple_kernel(x_ref, p_ref, lnw_ref, wg_ref, wp_ref, lnf_ref, o_ref, *, final):
    x = x_ref[...]
    h = _rms(x, lnw_ref[...]).astype(BF16)
    gate = jax.nn.sigmoid(_dot(h, wg_ref[...]))
    emb = _dot(p_ref[...].astype(BF16), wp_ref[...])
    y = x + gate * emb
    o_ref[...] = _rms(y, lnf_ref[...]) if final else y


def _ple(x, p, ln_w, w_gate, w_ple, ln_final, *, bm, final):
    s, d = x.shape
    e = p.shape[1]
    return pl.pallas_call(
        functools.partial(_ple_kernel, final=final),
        grid=(s // bm,),
        in_specs=[
            pl.BlockSpec((bm, d), lambda m: (m, 0)),
            pl.BlockSpec((bm, e), lambda m: (m, 0)),
            _resident((1, d)),
            _resident((d, d)),
            _resident((e, d)),
            _resident((1, d)),
        ],
        out_specs=pl.BlockSpec((bm, d), lambda m: (m, 0)),
        out_shape=jax.ShapeDtypeStruct((s, d), F32),
        compiler_params=_params("arbitrary"),
        name="ple_final",
    )(x, p, ln_w, w_gate, w_ple, ln_final)


def kernel(x, p, ln_mix_w, w_in, pool_mix_w, pool_scale, hgrn_lb_logits, hgrn_norm_w, w_branch_a,
           w_branch_b, w_out, ln_ffn_w, w_up, conv_w, conv_b, w_down, ln_ple_w, w_ple_gate, w_ple,
           ln_final_w):
    batch, seq, d = x.shape
    depth = w_in.shape[0]
    c_pool = w_branch_a.shape[1]
    c_hgrn = w_branch_b.shape[1]
    blk = c_hgrn
    assert c_pool == blk and d == 2 * blk, "column blocks of the combined projection must line up"
    assert seq % 1024 == 0
    tf = 512
    outs = []
    for bi in range(batch):
        xb = x[bi]
        for i in range(depth):
            proj = _inproj(xb, ln_mix_w[i][None], w_in[i], bm=2048, bn=blk)
            mix = pool_mix_w[i]
            og, (mix_b, wa_b, wd_b) = _hgrn(
                proj, hgrn_lb_logits, hgrn_norm_w[i][None],
                [mix.reshape(-1, mix.shape[-1]), w_branch_a[i], w_down[i]],
                bt=512, layer=i, q_block=1, f_block=2, i_block=3, g_block=4)
            ya, (wb_b, wo_b) = _pool_branch(
                proj, mix_b.reshape(mix.shape), pool_scale[i][None], wa_b,
                [w_branch_b[i], w_out[i]], bm=1024, bn=blk, ga_block0=5)
            xb, (wu_b, wg_b, wp_b) = _merge(og, proj, ya, xb, wb_b, wo_b,
                                            [w_up[i], w_ple_gate[i], w_ple[i]], bm=512, gb_block0=7)
            xb = _ffn(xb, ln_ffn_w[i][None], wu_b, conv_w[i], conv_b[i][None], wd_b, bm=1024, tf=512)
            xb = _ple(xb, p[i, bi], ln_ple_w[i][None], wg_b, wp_b, ln_final_w[None], bm=1024,
                      final=(i == depth - 1))
        outs.append(xb)
    return jnp.stack(outs, axis=0)
```

```python
import functools

import jax
import jax.numpy as jnp
from jax import lax
from jax.experimental import pallas as pl
from jax.experimental.pallas import tpu as pltpu

F32 = jnp.float32
BF16 = jnp.bfloat16

EPS = 1e-6
POOL_WINDOWS = (2, 4, 8, 16)
POOL_HALO = 16
HEAD_DIM = 128
CHUNK = 64
FAST_CHUNK = 128
SUB = 16
NSUB = CHUNK // SUB
CONV_WIDTH = 3
CARRY_ROWS = 8
ACT_ROWS = 64
LANE = 128
BF16_TILE_ROWS = 16
LOG2_E = 1.4426950408889634
MILD_DECAY_LOG2 = -100.0
V7X_VMEM_BYTES = 64 * 1024 * 1024
VMEM_LIMIT = V7X_VMEM_BYTES - 3 * 1024 * 1024


def _dot(a, b):
    return jnp.dot(a, b, preferred_element_type=F32)


def _dot_nt(a, b):
    return lax.dot_general(a, b, (((1,), (1,)), ((), ())), preferred_element_type=F32)


def _dot_tn(a, b):
    return lax.dot_general(a, b, (((0,), (0,)), ((), ())), preferred_element_type=F32)


def _rms(x, w):
    return x * lax.rsqrt(jnp.mean(x * x, axis=-1, keepdims=True) + EPS) * w


def _params(*sem):
    return pltpu.CompilerParams(dimension_semantics=sem, vmem_limit_bytes=VMEM_LIMIT)


class _Bf16Copies:
    def __init__(self, arrays, steps, step_of):
        self.arrays = list(arrays)
        self.specs, self.counts = [], []
        for a in self.arrays:
            rows, cols = a.shape
            if rows % (steps * BF16_TILE_ROWS) == 0:
                n, block = steps, (rows // steps, cols)
                index = lambda *g: (step_of(*g), 0)
            else:
                assert cols % LANE == 0
                n = max(k for k in range(1, steps + 1) if (cols // LANE) % k == 0)
                block = (rows, cols // n)
                index = lambda *g, n=n: (0, jnp.minimum(step_of(*g), n - 1))
            self.specs.append(pl.BlockSpec(block, index))
            self.counts.append(n)
        self.out_shapes = [jax.ShapeDtypeStruct(a.shape, BF16) for a in self.arrays]

    def run(self, step, srcs, dsts):
        for src, dst, n in zip(srcs, dsts, self.counts):
            @pl.when(step < n)
            def _():
                dst[...] = src[...].astype(dst.dtype)


def _resident(shape):
    return pl.BlockSpec(shape, lambda *_: (0,) * len(shape), pipeline_mode=pl.Buffered(1))


def _inproj_kernel(x_ref, lnw_ref, w_ref, o_ref, h_ref):
    @pl.when(pl.program_id(1) == 0)
    def _():
        h_ref[...] = _rms(x_ref[...], lnw_ref[...]).astype(BF16)

    o_ref[...] = _dot(h_ref[...], w_ref[...].astype(BF16)).astype(o_ref.dtype)


def _inproj(x, ln_w, w_in, *, bm, bn):
    s, d = x.shape
    d_in = w_in.shape[1]
    return pl.pallas_call(
        _inproj_kernel,
        grid=(s // bm, d_in // bn),
        in_specs=[
            pl.BlockSpec((bm, d), lambda m, n: (m, 0), pipeline_mode=pl.Buffered(1)),
            pl.BlockSpec((1, d), lambda m, n: (0, 0)),
            pl.BlockSpec((d, bn), lambda m, n: (0, n)),
        ],
        out_specs=pl.BlockSpec((bm, bn), lambda m, n: (m, n)),
        out_shape=jax.ShapeDtypeStruct((s, d_in), BF16),
        scratch_shapes=[pltpu.VMEM((bm, d), BF16)],
        compiler_params=_params("arbitrary", "arbitrary"),
        name="inproj",
    )(x, ln_w, w_in)


def _pool_kernel(u_ref, halo_ref, ga_ref, mix_ref, scale_ref, wa_ref, *rest, bm, copies):
    n_copy = len(copies.arrays)
    copy_in, (o_ref, *copy_out, feat_ref) = rest[:n_copy], rest[n_copy:]
    m = pl.program_id(0)
    n = pl.program_id(1)
    copies.run(m * pl.num_programs(1) + n, copy_in, copy_out)

    @pl.when(n == 0)
    def _():
        u = u_ref[...].astype(F32)
        halo = halo_ref[...].astype(F32) * (m > 0).astype(F32)
        ext = jnp.concatenate([halo, u], axis=0)
        gw = u.shape[1] // len(POOL_WINDOWS)
        pos = m * bm + lax.broadcasted_iota(jnp.int32, (bm, 1), 0) + 1
        run = ext
        width = 1
        feats = []
        for g, w in enumerate(POOL_WINDOWS):
            while width < w:
                run = run + pltpu.roll(run, width, axis=0)
                width *= 2
            cnt = jnp.minimum(pos, w).astype(F32)
            win = run[POOL_HALO:, :gw]
            if g + 1 < len(POOL_WINDOWS):
                run = run[:, gw:]
            d = win / cnt - u[:, g * gw:(g + 1) * gw]
            y = _dot(d.astype(BF16), mix_ref[g])
            feats.append(y * scale_ref[:, g * gw:(g + 1) * gw])
        feat_ref[...] = jnp.concatenate(feats, axis=1).astype(BF16)

    y_a = _dot(feat_ref[...], wa_ref[...])
    o_ref[...] = (jax.nn.sigmoid(ga_ref[...].astype(F32)) * y_a).astype(o_ref.dtype)


def _pool_branch(proj, mix_w, scale, w_a, to_bf16, *, bm, bn, ga_block0):
    s = proj.shape[0]
    c = w_a.shape[0]
    d = w_a.shape[1]
    g, gw, _ = mix_w.shape
    halo_per_tile = bm // POOL_HALO
    n_col = d // bn
    copies = _Bf16Copies(to_bf16, (s // bm) * n_col, lambda m, n: m * n_col + n)
    ya, *as_bf16 = pl.pallas_call(
        functools.partial(_pool_kernel, bm=bm, copies=copies),
        grid=(s // bm, n_col),
        in_specs=[
            pl.BlockSpec((bm, c), lambda m, n: (m, 0)),
            pl.BlockSpec((POOL_HALO, c), lambda m, n: (jnp.maximum(m * halo_per_tile - 1, 0), 0)),
            pl.BlockSpec((bm, bn), lambda m, n: (m, ga_block0 + n)),
            _resident((g, gw, gw)),
            _resident((1, c)),
            pl.BlockSpec((c, bn), lambda m, n: (0, n)),
            *copies.specs,
        ],
        out_specs=[pl.BlockSpec((bm, bn), lambda m, n: (m, n)), *copies.specs],
        out_shape=[jax.ShapeDtypeStruct((s, d), BF16), *copies.out_shapes],
        scratch_shapes=[pltpu.VMEM((bm, c), BF16)],
        compiler_params=_params("arbitrary", "arbitrary"),
        name="pool_branch",
    )(proj, proj, proj, mix_w, scale, w_a, *to_bf16)
    return ya, as_bf16


def _split2(x):
    hi = x.astype(BF16)
    return hi, (x - hi.astype(F32)).astype(BF16)


def _gates(fl, lb):
    one_m_lb = 1.0 - lb
    sg = jax.nn.sigmoid(fl)
    log2_f = jnp.log(lb + one_m_lb * sg) * LOG2_E
    return log2_f, one_m_lb * (1.0 - sg)


def _chunk_cumsum(tri, x):
    ck = tri.shape[0]
    hi, lo = _split2(x)
    return jnp.concatenate(
        [_dot(tri, hi[r:r + ck, :]) + _dot(tri, lo[r:r + ck, :]) for r in range(0, x.shape[0], ck)], axis=0)


def _head_norm_gate(o_heads, norm_w, g):
    o = jnp.concatenate(
        [o_h * lax.rsqrt(jnp.mean(o_h * o_h, axis=-1, keepdims=True) + EPS) for o_h in o_heads], axis=1)
    return o * norm_w * (g * jax.nn.sigmoid(g))


def _hgrn_chunk_any_decay(c, q_ref, f_ref, i_ref, g_ref, tri_ref, wsel_ref, o_ref, st_ref, lb, norm_w):
    r0 = pl.multiple_of(c * CHUNK, CHUNK)
    rows = pl.ds(r0, CHUNK)
    d = q_ref.shape[1]
    heads = d // HEAD_DIM

    q = q_ref[rows, :].astype(F32)
    v = i_ref[rows, :]
    log2_f, kk = _gates(f_ref[rows, :].astype(F32), lb)
    b = _chunk_cumsum(tri_ref[:CHUNK, :CHUNK], log2_f)
    b_last = b[CHUNK - 1:CHUNK, :]

    q_in = (q * jnp.exp2(b)).astype(BF16)
    k_st = (kk * jnp.exp2(b_last - b)).astype(BF16)

    b_end = jnp.concatenate(
        [jnp.broadcast_to(b[(j + 1) * SUB - 1:(j + 1) * SUB, :], (SUB, d)) for j in range(NSUB)], axis=0)
    k_hat = kk * jnp.exp2(b_end - b)
    sub_id = lax.broadcasted_iota(jnp.int32, (CHUNK, 1), 0) // SUB
    q_from = []
    for j in range(NSUB - 1):
        lo_row = (j + 1) * SUB
        bj = b[lo_row - 1:lo_row, :]
        part = q[lo_row:, :] * jnp.exp2(b[lo_row:, :] - bj)
        q_from.append(jnp.concatenate([jnp.zeros((lo_row, d), F32), part], axis=0).astype(BF16))
    k_src = [jnp.where(sub_id == j, k_hat, 0.0).astype(BF16) for j in range(NSUB - 1)]

    per_head = [[] for _ in range(heads)]
    for r in range(NSUB):
        qb = q[r * SUB:(r + 1) * SUB, :]
        bb = b[r * SUB:(r + 1) * SUB, :]
        pieces = []
        for s in range(SUB):
            row = r * SUB + s
            e = jnp.exp2(jnp.minimum(bb - b[row:row + 1, :], 0.0))
            pieces.append((qb * kk[row:row + 1, :] * e).astype(BF16))
        for h in range(heads):
            per_head[h].append(jnp.concatenate(
                [p[:, h * HEAD_DIM:(h + 1) * HEAD_DIM] for p in pieces], axis=1))
    a_big = jnp.concatenate([jnp.concatenate(blocks, axis=0) for blocks in per_head], axis=0)
    diag = _dot(a_big, wsel_ref[...])

    t_id = lax.broadcasted_iota(jnp.int32, (CHUNK, CHUNK), 0)
    s_id = lax.broadcasted_iota(jnp.int32, (CHUNK, CHUNK), 1)
    diag_mask = (t_id // SUB == s_id // SUB) & (s_id <= t_id)

    outs = []
    for h in range(heads):
        hs = slice(h * HEAD_DIM, (h + 1) * HEAD_DIM)
        q_cat = jnp.concatenate([qf[:, hs] for qf in q_from], axis=1)
        k_cat = jnp.concatenate([ks[:, hs] for ks in k_src], axis=1)
        scores = _dot_nt(q_cat, k_cat) + jnp.where(diag_mask, diag[h * CHUNK:(h + 1) * CHUNK, :], 0.0)
        v_h = v[:, hs]
        state_t = st_ref[h]
        o_h = _dot_nt(q_in[:, hs], state_t.astype(BF16)) + _dot(scores.astype(BF16), v_h)
        st_ref[h] = state_t * jnp.exp2(b_last[:, hs]) + _dot_tn(v_h, k_st[:, hs])
        outs.append(o_h)
    o_ref[rows, :] = _head_norm_gate(outs, norm_w, g_ref[rows, :].astype(F32)).astype(o_ref.dtype)


def _hgrn_tile_mild_decay(b, kk, q_ref, i_ref, g_ref, o_ref, st_ref, norm_w):
    bt, d = b.shape
    heads = d // HEAD_DIM
    ck = FAST_CHUNK
    n_chunks = bt // ck
    q = q_ref[...].astype(F32)
    v = i_ref[...]
    chunk_decay = [jnp.exp2(b[(c + 1) * ck - 1:(c + 1) * ck, :]) for c in range(n_chunks)]
    q_in = (q * jnp.exp2(b)).astype(BF16)
    k_undecayed = kk * jnp.exp2(-b)
    k_out = k_undecayed.astype(BF16)
    k_st = (k_undecayed * jnp.concatenate(
        [jnp.broadcast_to(r, (ck, d)) for r in chunk_decay], axis=0)).astype(BF16)
    causal = (lax.broadcasted_iota(jnp.int32, (ck, ck), 1) <= lax.broadcasted_iota(jnp.int32, (ck, ck), 0))
    states = [st_ref[h] for h in range(heads)]
    tile_out = []
    for c in range(n_chunks):
        rs = slice(c * ck, (c + 1) * ck)
        carry_decay = chunk_decay[c]
        outs = []
        for h in range(heads):
            hs = slice(h * HEAD_DIM, (h + 1) * HEAD_DIM)
            scores = jnp.where(causal, _dot_nt(q_in[rs, hs], k_out[rs, hs]), 0.0).astype(BF16)
            outs.append(_dot(scores, v[rs, hs]) + _dot_nt(q_in[rs, hs], states[h].astype(BF16)))
            states[h] = states[h] * carry_decay[:, hs] + _dot_tn(v[rs, hs], k_st[rs, hs])
        tile_out.append(jnp.concatenate(outs, axis=1))
    for h in range(heads):
        st_ref[h] = states[h]
    heads_out = jnp.concatenate(tile_out, axis=0)
    o_heads = [heads_out[:, h * HEAD_DIM:(h + 1) * HEAD_DIM] for h in range(heads)]
    o_ref[...] = _head_norm_gate(o_heads, norm_w, g_ref[...].astype(F32)).astype(o_ref.dtype)


def _hgrn_kernel(q_ref, f_ref, i_ref, g_ref, lbl_ref, nw_ref, tri_ref, wsel_ref, *rest, layer, copies):
    n_copy = len(copies.arrays)
    copy_in, (o_ref, *copy_out, st_ref) = rest[:n_copy], rest[n_copy:]
    step = pl.program_id(0)

    @pl.when(step == 0)
    def _():
        st_ref[...] = jnp.zeros_like(st_ref)

    copies.run(step, copy_in, copy_out)

    logits = lbl_ref[...]
    e = jnp.exp(logits - jnp.max(logits, axis=0, keepdims=True))
    lb = jnp.sum(e[:layer + 1, :], axis=0, keepdims=True) / jnp.sum(e, axis=0, keepdims=True)
    norm_w = nw_ref[...]

    log2_f, kk = _gates(f_ref[...].astype(F32), lb)
    b = _chunk_cumsum(tri_ref[...], log2_f)
    mild = jnp.min(b) >= MILD_DECAY_LOG2

    @pl.when(mild)
    def _():
        _hgrn_tile_mild_decay(b, kk, q_ref, i_ref, g_ref, o_ref, st_ref, norm_w)

    @pl.when(jnp.logical_not(mild))
    def _():
        def body(c, carry):
            _hgrn_chunk_any_decay(c, q_ref, f_ref, i_ref, g_ref, tri_ref, wsel_ref, o_ref, st_ref, lb, norm_w)
            return carry

        lax.fori_loop(0, q_ref.shape[0] // CHUNK, body, 0)


def _hgrn(proj, lb_logits, norm_w, to_bf16, *, bt, layer, q_block, f_block, i_block, g_block):
    s = proj.shape[0]
    d = norm_w.shape[1]
    heads = d // HEAD_DIM
    steps = s // bt
    copies = _Bf16Copies(to_bf16, steps, lambda t: t)
    t_id = lax.broadcasted_iota(jnp.int32, (FAST_CHUNK, FAST_CHUNK), 0)
    s_id = lax.broadcasted_iota(jnp.int32, (FAST_CHUNK, FAST_CHUNK), 1)
    tri = (s_id <= t_id).astype(BF16)
    src = lax.broadcasted_iota(jnp.int32, (SUB * HEAD_DIM, CHUNK), 0) // HEAD_DIM
    col = lax.broadcasted_iota(jnp.int32, (SUB * HEAD_DIM, CHUNK), 1) % SUB
    wsel = (src == col).astype(BF16)
    nl = lb_logits.shape[0]
    og, *as_bf16 = pl.pallas_call(
        functools.partial(_hgrn_kernel, layer=layer, copies=copies),
        grid=(steps,),
        in_specs=[
            pl.BlockSpec((bt, d), lambda t: (t, q_block)),
            pl.BlockSpec((bt, d), lambda t: (t, f_block)),
            pl.BlockSpec((bt, d), lambda t: (t, i_block)),
            pl.BlockSpec((bt, d), lambda t: (t, g_block)),
            _resident((nl, d)),
            _resident((1, d)),
            _resident((FAST_CHUNK, FAST_CHUNK)),
            _resident((SUB * HEAD_DIM, CHUNK)),
            *copies.specs,
        ],
        out_specs=[pl.BlockSpec((bt, d), lambda t: (t, 0)), *copies.specs],
        out_shape=[jax.ShapeDtypeStruct((s, d), BF16), *copies.out_shapes],
        scratch_shapes=[pltpu.VMEM((heads, HEAD_DIM, HEAD_DIM), F32)],
        compiler_params=_params("arbitrary"),
        name="hgrn",
    )(proj, proj, proj, proj, lb_logits, norm_w, tri, wsel, *to_bf16)
    return og, as_bf16


def _merge_kernel(og_ref, gb0_ref, gb1_ref, ya_ref, x_ref, wb_ref, wo_ref, *rest, copies):
    n_copy = len(copies.arrays)
    copy_in, (o_ref, *copy_out) = rest[:n_copy], rest[n_copy:]
    copies.run(pl.program_id(0), copy_in, copy_out)
    y_b = _dot(og_ref[...], wb_ref[...])
    gate_b = jnp.concatenate([gb0_ref[...], gb1_ref[...]], axis=1).astype(F32)
    merged = ya_ref[...].astype(F32) + jax.nn.sigmoid(gate_b) * y_b
    o_ref[...] = x_ref[...] + _dot(merged.astype(BF16), wo_ref[...])


def _merge(og, proj, ya, x, w_b, w_out, to_bf16, *, bm, gb_block0):
    s, d = x.shape
    c = og.shape[1]
    half = d // 2
    copies = _Bf16Copies(to_bf16, s // bm, lambda m: m)
    x_new, *as_bf16 = pl.pallas_call(
        functools.partial(_merge_kernel, copies=copies),
        grid=(s // bm,),
        in_specs=[
            pl.BlockSpec((bm, c), lambda m: (m, 0)),
            pl.BlockSpec((bm, half), lambda m: (m, gb_block0)),
            pl.BlockSpec((bm, half), lambda m: (m, gb_block0 + 1)),
            pl.BlockSpec((bm, d), lambda m: (m, 0)),
            pl.BlockSpec((bm, d), lambda m: (m, 0)),
            _resident((c, d)),
            _resident((d, d)),
            *copies.specs,
        ],
        out_specs=[pl.BlockSpec((bm, d), lambda m: (m, 0)), *copies.specs],
        out_shape=[jax.ShapeDtypeStruct((s, d), F32), *copies.out_shapes],
        compiler_params=_params("arbitrary"),
        name="merge_out",
    )(og, proj, proj, ya, x, w_b, w_out, *to_bf16)
    return x_new, as_bf16


def _causal_conv(z_ref, row0, cols, w_ref, b_ref):
    ext = z_ref[pl.ds(row0, CARRY_ROWS + ACT_ROWS), cols]
    y = b_ref[:, cols] + w_ref[CONV_WIDTH - 1:CONV_WIDTH, cols] * ext[CARRY_ROWS:, :]
    for back in range(1, CONV_WIDTH):
        tap = CONV_WIDTH - 1 - back
        y = y + w_ref[tap:tap + 1, cols] * pltpu.roll(ext, back, axis=0)[CARRY_ROWS:, :]
    return y


def _ffn_kernel(x_ref, lnw_ref, wug_ref, wuv_ref, cwg_ref, cwv_ref, cbg_ref, cbv_ref, wd_ref,
                o_ref, h_ref, zg_ref, zv_ref, a_ref, cg_ref, cv_ref, *, dff):
    m = pl.program_id(0)
    j = pl.program_id(1)
    bm = h_ref.shape[0]
    tf = wd_ref.shape[0]
    shared = j * tf - jnp.minimum(j * tf, dff - tf)

    @pl.when(j == 0)
    def _():
        x = x_ref[...]
        h_ref[...] = _rms(x, lnw_ref[...]).astype(BF16)
        o_ref[...] = x

    @pl.when(m == 0)
    def _():
        cg_ref[j] = jnp.zeros(cg_ref.shape[1:], F32)
        cv_ref[j] = jnp.zeros(cv_ref.shape[1:], F32)

    h = h_ref[...]
    zg_ref[:CARRY_ROWS, :] = cg_ref[j]
    zv_ref[:CARRY_ROWS, :] = cv_ref[j]
    zg_ref[CARRY_ROWS:, :] = _dot(h, wug_ref[...])
    zv_ref[CARRY_ROWS:, :] = _dot(h, wuv_ref[...])
    cg_ref[j] = zg_ref[bm:, :]
    cv_ref[j] = zv_ref[bm:, :]

    for c in range(tf // LANE):
        cols = slice(c * LANE, (c + 1) * LANE)
        fresh = (lax.broadcasted_iota(jnp.int32, (1, LANE), 1) + c * LANE) >= shared
        for r in range(bm // ACT_ROWS):
            ug = _causal_conv(zg_ref, r * ACT_ROWS, cols, cwg_ref, cbg_ref)
            uv = _causal_conv(zv_ref, r * ACT_ROWS, cols, cwv_ref, cbv_ref)
            a_ref[r * ACT_ROWS:(r + 1) * ACT_ROWS, cols] = jnp.where(
                fresh, ug * jax.nn.sigmoid(ug) * uv, 0.0).astype(BF16)

    o_ref[...] += _dot(a_ref[...], wd_ref[...])


def _ffn(x, ln_w, w_up, conv_w, conv_b, w_down, *, bm, tf):
    s, d = x.shape
    dff = w_down.shape[0]
    nj = pl.cdiv(dff, tf)
    assert dff >= tf and dff % LANE == 0

    def start(j, base=0):
        return (base // LANE + jnp.minimum(j * (tf // LANE), (dff - tf) // LANE)) * LANE

    def window(rows, offset):
        return pl.BlockSpec((pl.Element(rows), pl.Element(tf)), lambda m, j: (0, offset(j)))

    return pl.pallas_call(
        functools.partial(_ffn_kernel, dff=dff),
        grid=(s // bm, nj),
        in_specs=[
            pl.BlockSpec((bm, d), lambda m, j: (m, 0)),
            pl.BlockSpec((1, d), lambda m, j: (0, 0)),
            window(d, start),
            window(d, lambda j: start(j, dff)),
            window(CONV_WIDTH, start),
            window(CONV_WIDTH, lambda j: start(j, dff)),
            window(1, start),
            window(1, lambda j: start(j, dff)),
            pl.BlockSpec((pl.Element(tf), pl.Element(d)), lambda m, j: (start(j), 0)),
        ],
        out_specs=pl.BlockSpec((bm, d), lambda m, j: (m, 0)),
        out_shape=jax.ShapeDtypeStruct((s, d), F32),
        scratch_shapes=[
            pltpu.VMEM((bm, d), BF16),
            pltpu.VMEM((CARRY_ROWS + bm, tf), F32),
            pltpu.VMEM((CARRY_ROWS + bm, tf), F32),
            pltpu.VMEM((bm, tf), BF16),
            pltpu.VMEM((nj, CARRY_ROWS, tf), F32),
            pltpu.VMEM((nj, CARRY_ROWS, tf), F32),
        ],
        compiler_params=_params("arbitrary", "arbitrary"),
        name="ffn",
    )(x, ln_w, w_up, w_up, conv_w, conv_w, conv_b, conv_b, w_down)


def _ple_kernel(x_ref, p_ref, lnw_ref, wg_ref, wp_ref, lnf_ref, o_ref, *, final):
    x = x_ref[...]
    h = _rms(x, lnw_ref[...]).astype(BF16)
    gate = jax.nn.sigmoid(_dot(h, wg_ref[...]))
    emb = _dot(p_ref[...].astype(BF16), wp_ref[...])
    y = x + gate * emb
    o_ref[...] = _rms(y, lnf_ref[...]) if final else y


def _ple(x, p, ln_w, w_gate, w_ple, ln_final, *, bm, final):
    s, d = x.shape
    e = p.shape[1]
    return pl.pallas_call(
        functools.partial(_ple_kernel, final=final),
        grid=(s // bm,),
        in_specs=[
            pl.BlockSpec((bm, d), lambda m: (m, 0)),
            pl.BlockSpec((bm, e), lambda m: (m, 0)),
            _resident((1, d)),
            _resident((d, d)),
            _resident((e, d)),
            _resident((1, d)),
        ],
        out_specs=pl.BlockSpec((bm, d), lambda m: (m, 0)),
        out_shape=jax.ShapeDtypeStruct((s, d), F32),
        compiler_params=_params("arbitrary"),
        name="ple_final",
    )(x, p, ln_w, w_gate, w_ple, ln_final)


def kernel(x, p, ln_mix_w, w_in, pool_mix_w, pool_scale, hgrn_lb_logits, hgrn_norm_w, w_branch_a,
           w_branch_b, w_out, ln_ffn_w, w_up, conv_w, conv_b, w_down, ln_ple_w, w_ple_gate, w_ple,
           ln_final_w):
    batch, seq, d = x.shape
    depth = w_in.shape[0]
    c_pool = w_branch_a.shape[1]
    c_hgrn = w_branch_b.shape[1]
    blk = c_hgrn
    assert c_pool == blk and d == 2 * blk, "column blocks of the combined projection must line up"
    assert seq % 2048 == 0, "row tiles of up to 2048 rows must divide the sequence"
    tf = 512
    outs = []
    for bi in range(batch):
        xb = x[bi]
        for i in range(depth):
            proj = _inproj(xb, ln_mix_w[i][None], w_in[i], bm=2048, bn=blk)
            mix = pool_mix_w[i]
            og, (mix_b, wa_b, wu_b) = _hgrn(
                proj, hgrn_lb_logits, hgrn_norm_w[i][None],
                [mix.reshape(-1, mix.shape[-1]), w_branch_a[i], w_up[i]],
                bt=512, layer=i, q_block=1, f_block=2, i_block=3, g_block=4)
            ya, (wb_b, wo_b) = _pool_branch(
                proj, mix_b.reshape(mix.shape), pool_scale[i][None], wa_b,
                [w_branch_b[i], w_out[i]], bm=1024, bn=blk, ga_block0=5)
            xb, (wd_b, wg_b, wp_b) = _merge(og, proj, ya, xb, wb_b, wo_b,
                                            [w_down[i], w_ple_gate[i], w_ple[i]], bm=512, gb_block0=7)
            xb = _ffn(xb, ln_ffn_w[i][None], wu_b, conv_w[i], conv_b[i][None], wd_b, bm=1024, tf=512)
            xb = _ple(xb, p[i, bi], ln_ple_w[i][None], wg_b, wp_b, ln_final_w[None], bm=1024,
                      final=(i == depth - 1))
        outs.append(xb)
    return jnp.stack(outs, axis=0)
```

```python
import functools

import jax
import jax.numpy as jnp
from jax import lax
from jax.experimental import pallas as pl
from jax.experimental.pallas import tpu as pltpu

F32 = jnp.float32
BF16 = jnp.bfloat16

EPS = 1e-6
POOL_WINDOWS = (2, 4, 8, 16)
POOL_HALO = 16
HEAD_DIM = 128
CHUNK = 64
FAST_CHUNK = 128
SUB = 16
NSUB = CHUNK // SUB
CONV_WIDTH = 3
CARRY_ROWS = 8
ACT_ROWS = 64
LANE = 128
BF16_TILE_ROWS = 16
LOG2_E = 1.4426950408889634
MILD_DECAY_LOG2 = -100.0
V7X_VMEM_BYTES = 64 * 1024 * 1024
VMEM_LIMIT = V7X_VMEM_BYTES - 3 * 1024 * 1024


def _dot(a, b):
    return jnp.dot(a, b, preferred_element_type=F32)


def _dot_nt(a, b):
    return lax.dot_general(a, b, (((1,), (1,)), ((), ())), preferred_element_type=F32)


def _dot_tn(a, b):
    return lax.dot_general(a, b, (((0,), (0,)), ((), ())), preferred_element_type=F32)


def _rms(x, w):
    return x * lax.rsqrt(jnp.mean(x * x, axis=-1, keepdims=True) + EPS) * w


def _params(*sem):
    return pltpu.CompilerParams(dimension_semantics=sem, vmem_limit_bytes=VMEM_LIMIT)


class _Bf16Copies:
    def __init__(self, arrays, steps, step_of):
        self.arrays = list(arrays)
        self.specs, self.counts = [], []
        for a in self.arrays:
            rows, cols = a.shape
            if rows % (steps * BF16_TILE_ROWS) == 0:
                n, block = steps, (rows // steps, cols)
                index = lambda *g: (step_of(*g), 0)
            else:
                assert cols % LANE == 0
                n = max(k for k in range(1, steps + 1) if (cols // LANE) % k == 0)
                block = (rows, cols // n)
                index = lambda *g, n=n: (0, jnp.minimum(step_of(*g), n - 1))
            self.specs.append(pl.BlockSpec(block, index))
            self.counts.append(n)
        self.out_shapes = [jax.ShapeDtypeStruct(a.shape, BF16) for a in self.arrays]

    def run(self, step, srcs, dsts):
        for src, dst, n in zip(srcs, dsts, self.counts):
            @pl.when(step < n)
            def _():
                dst[...] = src[...].astype(dst.dtype)


def _resident(shape):
    return pl.BlockSpec(shape, lambda *_: (0,) * len(shape), pipeline_mode=pl.Buffered(1))


def _inproj_kernel(x_ref, lnw_ref, w_ref, o_ref, h_ref):
    @pl.when(pl.program_id(1) == 0)
    def _():
        h_ref[...] = _rms(x_ref[...], lnw_ref[...]).astype(BF16)

    o_ref[...] = _dot(h_ref[...], w_ref[...].astype(BF16)).astype(o_ref.dtype)


def _inproj(x, ln_w, w_in, *, bm, bn):
    s, d = x.shape
    d_in = w_in.shape[1]
    return pl.pallas_call(
        _inproj_kernel,
        grid=(s // bm, d_in // bn),
        in_specs=[
            pl.BlockSpec((bm, d), lambda m, n: (m, 0), pipeline_mode=pl.Buffered(1)),
            pl.BlockSpec((1, d), lambda m, n: (0, 0)),
            pl.BlockSpec((d, bn), lambda m, n: (0, n)),
        ],
        out_specs=pl.BlockSpec((bm, bn), lambda m, n: (m, n)),
        out_shape=jax.ShapeDtypeStruct((s, d_in), BF16),
        scratch_shapes=[pltpu.VMEM((bm, d), BF16)],
        compiler_params=_params("arbitrary", "arbitrary"),
        name="inproj",
    )(x, ln_w, w_in)


def _pool_kernel(u_ref, halo_ref, ga0_ref, ga1_ref, mix_ref, scale_ref, wa_ref, *rest, bm, copies):
    n_copy = len(copies.arrays)
    copy_in, (o_ref, *copy_out, feat_ref) = rest[:n_copy], rest[n_copy:]
    m = pl.program_id(0)
    copies.run(m, copy_in, copy_out)

    u = u_ref[...].astype(F32)
    halo = halo_ref[...].astype(F32) * (m > 0).astype(F32)
    ext = jnp.concatenate([halo, u], axis=0)
    gw = u.shape[1] // len(POOL_WINDOWS)
    pos = m * bm + lax.broadcasted_iota(jnp.int32, (bm, 1), 0) + 1
    run = ext
    width = 1
    feats = []
    for g, w in enumerate(POOL_WINDOWS):
        while width < w:
            run = run + pltpu.roll(run, width, axis=0)
            width *= 2
        cnt = jnp.minimum(pos, w).astype(F32)
        win = run[POOL_HALO:, :gw]
        if g + 1 < len(POOL_WINDOWS):
            run = run[:, gw:]
        d = win / cnt - u[:, g * gw:(g + 1) * gw]
        y = _dot(d.astype(BF16), mix_ref[g])
        feats.append(y * scale_ref[:, g * gw:(g + 1) * gw])
    feat_ref[...] = jnp.concatenate(feats, axis=1).astype(BF16)

    y_a = _dot(feat_ref[...], wa_ref[...])
    gate_a = jnp.concatenate([ga0_ref[...], ga1_ref[...]], axis=1).astype(F32)
    o_ref[...] = (jax.nn.sigmoid(gate_a) * y_a).astype(o_ref.dtype)


def _pool_branch(proj, mix_w, scale, w_a, to_bf16, *, bm, bn, ga_block0):
    s = proj.shape[0]
    c = w_a.shape[0]
    d = w_a.shape[1]
    g, gw, _ = mix_w.shape
    halo_per_tile = bm // POOL_HALO
    assert d == 2 * bn
    copies = _Bf16Copies(to_bf16, s // bm, lambda m: m)
    ya, *as_bf16 = pl.pallas_call(
        functools.partial(_pool_kernel, bm=bm, copies=copies),
        grid=(s // bm,),
        in_specs=[
            pl.BlockSpec((bm, c), lambda m: (m, 0)),
            pl.BlockSpec((POOL_HALO, c), lambda m: (jnp.maximum(m * halo_per_tile - 1, 0), 0)),
            pl.BlockSpec((bm, bn), lambda m: (m, ga_block0)),
            pl.BlockSpec((bm, bn), lambda m: (m, ga_block0 + 1)),
            _resident((g, gw, gw)),
            _resident((1, c)),
            _resident((c, d)),
            *copies.specs,
        ],
        out_specs=[pl.BlockSpec((bm, d), lambda m: (m, 0)), *copies.specs],
        out_shape=[jax.ShapeDtypeStruct((s, d), BF16), *copies.out_shapes],
        scratch_shapes=[pltpu.VMEM((bm, c), BF16)],
        compiler_params=_params("arbitrary"),
        name="pool_branch",
    )(proj, proj, proj, proj, mix_w, scale, w_a, *to_bf16)
    return ya, as_bf16


def _split2(x):
    hi = x.astype(BF16)
    return hi, (x - hi.astype(F32)).astype(BF16)


def _gates(fl, lb):
    one_m_lb = 1.0 - lb
    sg = jax.nn.sigmoid(fl)
    log2_f = jnp.log(lb + one_m_lb * sg) * LOG2_E
    return log2_f, one_m_lb * (1.0 - sg)


def _chunk_cumsum(tri, x):
    ck = tri.shape[0]
    hi, lo = _split2(x)
    return jnp.concatenate(
        [_dot(tri, hi[r:r + ck, :]) + _dot(tri, lo[r:r + ck, :]) for r in range(0, x.shape[0], ck)], axis=0)


def _head_norm_gate(o_heads, norm_w, g):
    o = jnp.concatenate(
        [o_h * lax.rsqrt(jnp.mean(o_h * o_h, axis=-1, keepdims=True) + EPS) for o_h in o_heads], axis=1)
    return o * norm_w * (g * jax.nn.sigmoid(g))


def _hgrn_chunk_any_decay(c, q_ref, f_ref, i_ref, g_ref, tri_ref, wsel_ref, o_ref, st_ref, lb, norm_w):
    r0 = pl.multiple_of(c * CHUNK, CHUNK)
    rows = pl.ds(r0, CHUNK)
    d = q_ref.shape[1]
    heads = d // HEAD_DIM

    q = q_ref[rows, :].astype(F32)
    v = i_ref[rows, :]
    log2_f, kk = _gates(f_ref[rows, :].astype(F32), lb)
    b = _chunk_cumsum(tri_ref[:CHUNK, :CHUNK], log2_f)
    b_last = b[CHUNK - 1:CHUNK, :]

    q_in = (q * jnp.exp2(b)).astype(BF16)
    k_st = (kk * jnp.exp2(b_last - b)).astype(BF16)

    b_end = jnp.concatenate(
        [jnp.broadcast_to(b[(j + 1) * SUB - 1:(j + 1) * SUB, :], (SUB, d)) for j in range(NSUB)], axis=0)
    k_hat = kk * jnp.exp2(b_end - b)
    sub_id = lax.broadcasted_iota(jnp.int32, (CHUNK, 1), 0) // SUB
    q_from = []
    for j in range(NSUB - 1):
        lo_row = (j + 1) * SUB
        bj = b[lo_row - 1:lo_row, :]
        part = q[lo_row:, :] * jnp.exp2(b[lo_row:, :] - bj)
        q_from.append(jnp.concatenate([jnp.zeros((lo_row, d), F32), part], axis=0).astype(BF16))
    k_src = [jnp.where(sub_id == j, k_hat, 0.0).astype(BF16) for j in range(NSUB - 1)]

    per_head = [[] for _ in range(heads)]
    for r in range(NSUB):
        qb = q[r * SUB:(r + 1) * SUB, :]
        bb = b[r * SUB:(r + 1) * SUB, :]
        pieces = []
        for s in range(SUB):
            row = r * SUB + s
            e = jnp.exp2(jnp.minimum(bb - b[row:row + 1, :], 0.0))
            pieces.append((qb * kk[row:row + 1, :] * e).astype(BF16))
        for h in range(heads):
            per_head[h].append(jnp.concatenate(
                [p[:, h * HEAD_DIM:(h + 1) * HEAD_DIM] for p in pieces], axis=1))
    a_big = jnp.concatenate([jnp.concatenate(blocks, axis=0) for blocks in per_head], axis=0)
    diag = _dot(a_big, wsel_ref[...])

    t_id = lax.broadcasted_iota(jnp.int32, (CHUNK, CHUNK), 0)
    s_id = lax.broadcasted_iota(jnp.int32, (CHUNK, CHUNK), 1)
    diag_mask = (t_id // SUB == s_id // SUB) & (s_id <= t_id)

    outs = []
    for h in range(heads):
        hs = slice(h * HEAD_DIM, (h + 1) * HEAD_DIM)
        q_cat = jnp.concatenate([qf[:, hs] for qf in q_from], axis=1)
        k_cat = jnp.concatenate([ks[:, hs] for ks in k_src], axis=1)
        scores = _dot_nt(q_cat, k_cat) + jnp.where(diag_mask, diag[h * CHUNK:(h + 1) * CHUNK, :], 0.0)
        v_h = v[:, hs]
        state_t = st_ref[h]
        o_h = _dot_nt(q_in[:, hs], state_t.astype(BF16)) + _dot(scores.astype(BF16), v_h)
        st_ref[h] = state_t * jnp.exp2(b_last[:, hs]) + _dot_tn(v_h, k_st[:, hs])
        outs.append(o_h)
    o_ref[rows, :] = _head_norm_gate(outs, norm_w, g_ref[rows, :].astype(F32)).astype(o_ref.dtype)


def _hgrn_tile_mild_decay(b, kk, q_ref, i_ref, g_ref, o_ref, st_ref, norm_w):
    bt, d = b.shape
    heads = d // HEAD_DIM
    ck = FAST_CHUNK
    n_chunks = bt // ck
    q = q_ref[...].astype(F32)
    v = i_ref[...]
    chunk_decay = [jnp.exp2(b[(c + 1) * ck - 1:(c + 1) * ck, :]) for c in range(n_chunks)]
    q_in = (q * jnp.exp2(b)).astype(BF16)
    k_undecayed = kk * jnp.exp2(-b)
    k_out = k_undecayed.astype(BF16)
    k_st = (k_undecayed * jnp.concatenate(
        [jnp.broadcast_to(r, (ck, d)) for r in chunk_decay], axis=0)).astype(BF16)
    causal = (lax.broadcasted_iota(jnp.int32, (ck, ck), 1) <= lax.broadcasted_iota(jnp.int32, (ck, ck), 0))
    states = [st_ref[h] for h in range(heads)]
    tile_out = []
    for c in range(n_chunks):
        rs = slice(c * ck, (c + 1) * ck)
        carry_decay = chunk_decay[c]
        outs = []
        for h in range(heads):
            hs = slice(h * HEAD_DIM, (h + 1) * HEAD_DIM)
            scores = jnp.where(causal, _dot_nt(q_in[rs, hs], k_out[rs, hs]), 0.0).astype(BF16)
            outs.append(_dot(scores, v[rs, hs]) + _dot_nt(q_in[rs, hs], states[h].astype(BF16)))
            states[h] = states[h] * carry_decay[:, hs] + _dot_tn(v[rs, hs], k_st[rs, hs])
        tile_out.append(jnp.concatenate(outs, axis=1))
    for h in range(heads):
        st_ref[h] = states[h]
    heads_out = jnp.concatenate(tile_out, axis=0)
    o_heads = [heads_out[:, h * HEAD_DIM:(h + 1) * HEAD_DIM] for h in range(heads)]
    o_ref[...] = _head_norm_gate(o_heads, norm_w, g_ref[...].astype(F32)).astype(o_ref.dtype)


def _hgrn_kernel(q_ref, f_ref, i_ref, g_ref, lbl_ref, nw_ref, tri_ref, wsel_ref, *rest, layer, copies):
    n_copy = len(copies.arrays)
    copy_in, (o_ref, *copy_out, st_ref) = rest[:n_copy], rest[n_copy:]
    step = pl.program_id(0)

    @pl.when(step == 0)
    def _():
        st_ref[...] = jnp.zeros_like(st_ref)

    copies.run(step, copy_in, copy_out)

    logits = lbl_ref[...]
    e = jnp.exp(logits - jnp.max(logits, axis=0, keepdims=True))
    lb = jnp.sum(e[:layer + 1, :], axis=0, keepdims=True) / jnp.sum(e, axis=0, keepdims=True)
    norm_w = nw_ref[...]

    log2_f, kk = _gates(f_ref[...].astype(F32), lb)
    b = _chunk_cumsum(tri_ref[...], log2_f)
    mild = jnp.min(b) >= MILD_DECAY_LOG2

    @pl.when(mild)
    def _():
        _hgrn_tile_mild_decay(b, kk, q_ref, i_ref, g_ref, o_ref, st_ref, norm_w)

    @pl.when(jnp.logical_not(mild))
    def _():
        def body(c, carry):
            _hgrn_chunk_any_decay(c, q_ref, f_ref, i_ref, g_ref, tri_ref, wsel_ref, o_ref, st_ref, lb, norm_w)
            return carry

        lax.fori_loop(0, q_ref.shape[0] // CHUNK, body, 0)


def _hgrn(proj, lb_logits, norm_w, to_bf16, *, bt, layer, q_block, f_block, i_block, g_block):
    s = proj.shape[0]
    d = norm_w.shape[1]
    heads = d // HEAD_DIM
    steps = s // bt
    copies = _Bf16Copies(to_bf16, steps, lambda t: t)
    t_id = lax.broadcasted_iota(jnp.int32, (FAST_CHUNK, FAST_CHUNK), 0)
    s_id = lax.broadcasted_iota(jnp.int32, (FAST_CHUNK, FAST_CHUNK), 1)
    tri = (s_id <= t_id).astype(BF16)
    src = lax.broadcasted_iota(jnp.int32, (SUB * HEAD_DIM, CHUNK), 0) // HEAD_DIM
    col = lax.broadcasted_iota(jnp.int32, (SUB * HEAD_DIM, CHUNK), 1) % SUB
    wsel = (src == col).astype(BF16)
    nl = lb_logits.shape[0]
    og, *as_bf16 = pl.pallas_call(
        functools.partial(_hgrn_kernel, layer=layer, copies=copies),
        grid=(steps,),
        in_specs=[
            pl.BlockSpec((bt, d), lambda t: (t, q_block)),
            pl.BlockSpec((bt, d), lambda t: (t, f_block)),
            pl.BlockSpec((bt, d), lambda t: (t, i_block)),
            pl.BlockSpec((bt, d), lambda t: (t, g_block)),
            _resident((nl, d)),
            _resident((1, d)),
            _resident((FAST_CHUNK, FAST_CHUNK)),
            _resident((SUB * HEAD_DIM, CHUNK)),
            *copies.specs,
        ],
        out_specs=[pl.BlockSpec((bt, d), lambda t: (t, 0)), *copies.specs],
        out_shape=[jax.ShapeDtypeStruct((s, d), BF16), *copies.out_shapes],
        scratch_shapes=[pltpu.VMEM((heads, HEAD_DIM, HEAD_DIM), F32)],
        compiler_params=_params("arbitrary"),
        name="hgrn",
    )(proj, proj, proj, proj, lb_logits, norm_w, tri, wsel, *to_bf16)
    return og, as_bf16


def _merge_kernel(og_ref, gb0_ref, gb1_ref, ya_ref, x_ref, wb_ref, wo_ref, *rest, copies):
    n_copy = len(copies.arrays)
    copy_in, (o_ref, *copy_out) = rest[:n_copy], rest[n_copy:]
    copies.run(pl.program_id(0), copy_in, copy_out)
    y_b = _dot(og_ref[...], wb_ref[...])
    gate_b = jnp.concatenate([gb0_ref[...], gb1_ref[...]], axis=1).astype(F32)
    merged = ya_ref[...].astype(F32) + jax.nn.sigmoid(gate_b) * y_b
    o_ref[...] = x_ref[...] + _dot(merged.astype(BF16), wo_ref[...])


def _merge(og, proj, ya, x, w_b, w_out, to_bf16, *, bm, gb_block0):
    s, d = x.shape
    c = og.shape[1]
    half = d // 2
    copies = _Bf16Copies(to_bf16, s // bm, lambda m: m)
    x_new, *as_bf16 = pl.pallas_call(
        functools.partial(_merge_kernel, copies=copies),
        grid=(s // bm,),
        in_specs=[
            pl.BlockSpec((bm, c), lambda m: (m, 0)),
            pl.BlockSpec((bm, half), lambda m: (m, gb_block0)),
            pl.BlockSpec((bm, half), lambda m: (m, gb_block0 + 1)),
            pl.BlockSpec((bm, d), lambda m: (m, 0)),
            pl.BlockSpec((bm, d), lambda m: (m, 0)),
            _resident((c, d)),
            _resident((d, d)),
            *copies.specs,
        ],
        out_specs=[pl.BlockSpec((bm, d), lambda m: (m, 0)), *copies.specs],
        out_shape=[jax.ShapeDtypeStruct((s, d), F32), *copies.out_shapes],
        compiler_params=_params("arbitrary"),
        name="merge_out",
    )(og, proj, proj, ya, x, w_b, w_out, *to_bf16)
    return x_new, as_bf16


def _causal_conv(z_ref, row0, cols, w_ref, b_ref):
    ext = z_ref[pl.ds(row0, CARRY_ROWS + ACT_ROWS), cols]
    y = b_ref[:, cols] + w_ref[CONV_WIDTH - 1:CONV_WIDTH, cols] * ext[CARRY_ROWS:, :]
    for back in range(1, CONV_WIDTH):
        tap = CONV_WIDTH - 1 - back
        y = y + w_ref[tap:tap + 1, cols] * pltpu.roll(ext, back, axis=0)[CARRY_ROWS:, :]
    return y


def _ffn_kernel(x_ref, lnw_ref, wug_ref, wuv_ref, cwg_ref, cwv_ref, cbg_ref, cbv_ref, wd_ref,
                o_ref, h_ref, zg_ref, zv_ref, a_ref, cg_ref, cv_ref, *, dff):
    m = pl.program_id(0)
    j = pl.program_id(1)
    bm = h_ref.shape[0]
    tf = wd_ref.shape[0]
    shared = j * tf - jnp.minimum(j * tf, dff - tf)

    @pl.when(j == 0)
    def _():
        x = x_ref[...]
        h_ref[...] = _rms(x, lnw_ref[...]).astype(BF16)
        o_ref[...] = x

    @pl.when(m == 0)
    def _():
        cg_ref[j] = jnp.zeros(cg_ref.shape[1:], F32)
        cv_ref[j] = jnp.zeros(cv_ref.shape[1:], F32)

    h = h_ref[...]
    zg_ref[:CARRY_ROWS, :] = cg_ref[j]
    zv_ref[:CARRY_ROWS, :] = cv_ref[j]
    zg_ref[CARRY_ROWS:, :] = _dot(h, wug_ref[...])
    zv_ref[CARRY_ROWS:, :] = _dot(h, wuv_ref[...])
    cg_ref[j] = zg_ref[bm:, :]
    cv_ref[j] = zv_ref[bm:, :]

    for c in range(tf // LANE):
        cols = slice(c * LANE, (c + 1) * LANE)
        fresh = (lax.broadcasted_iota(jnp.int32, (1, LANE), 1) + c * LANE) >= shared
        for r in range(bm // ACT_ROWS):
            ug = _causal_conv(zg_ref, r * ACT_ROWS, cols, cwg_ref, cbg_ref)
            uv = _causal_conv(zv_ref, r * ACT_ROWS, cols, cwv_ref, cbv_ref)
            a_ref[r * ACT_ROWS:(r + 1) * ACT_ROWS, cols] = jnp.where(
                fresh, ug * jax.nn.sigmoid(ug) * uv, 0.0).astype(BF16)

    o_ref[...] += _dot(a_ref[...], wd_ref[...])


def _ffn(x, ln_w, w_up, conv_w, conv_b, w_down, *, bm, tf):
    s, d = x.shape
    dff = w_down.shape[0]
    nj = pl.cdiv(dff, tf)
    assert dff >= tf and dff % LANE == 0

    def start(j, base=0):
        return (base // LANE + jnp.minimum(j * (tf // LANE), (dff - tf) // LANE)) * LANE

    def window(rows, offset):
        return pl.BlockSpec((pl.Element(rows), pl.Element(tf)), lambda m, j: (0, offset(j)))

    return pl.pallas_call(
        functools.partial(_ffn_kernel, dff=dff),
        grid=(s // bm, nj),
        in_specs=[
            pl.BlockSpec((bm, d), lambda m, j: (m, 0)),
            pl.BlockSpec((1, d), lambda m, j: (0, 0)),
            window(d, start),
            window(d, lambda j: start(j, dff)),
            window(CONV_WIDTH, start),
            window(CONV_WIDTH, lambda j: start(j, dff)),
            window(1, start),
            window(1, lambda j: start(j, dff)),
            pl.BlockSpec((pl.Element(tf), pl.Element(d)), lambda m, j: (start(j), 0)),
        ],
        out_specs=pl.BlockSpec((bm, d), lambda m, j: (m, 0)),
        out_shape=jax.ShapeDtypeStruct((s, d), F32),
        scratch_shapes=[
            pltpu.VMEM((bm, d), BF16),
            pltpu.VMEM((CARRY_ROWS + bm, tf), F32),
            pltpu.VMEM((CARRY_ROWS + bm, tf), F32),
            pltpu.VMEM((bm, tf), BF16),
            pltpu.VMEM((nj, CARRY_ROWS, tf), F32),
            pltpu.VMEM((nj, CARRY_ROWS, tf), F32),
        ],
        compiler_params=_params("arbitrary", "arbitrary"),
        name="ffn",
    )(x, ln_w, w_up, w_up, conv_w, conv_w, conv_b, conv_b, w_down)


def _ple_kernel(x_ref, p_ref, lnw_ref, wg_ref, wp_ref, lnf_ref, o_ref, *, final):
    x = x_ref[...]
    h = _rms(x, lnw_ref[...]).astype(BF16)
    gate = jax.nn.sigmoid(_dot(h, wg_ref[...]))
    emb = _dot(p_ref[...].astype(BF16), wp_ref[...])
    y = x + gate * emb
    o_ref[...] = _rms(y, lnf_ref[...]) if final else y


def _ple(x, p, ln_w, w_gate, w_ple, ln_final, *, bm, final):
    s, d = x.shape
    e = p.shape[1]
    return pl.pallas_call(
        functools.partial(_ple_kernel, final=final),
        grid=(s // bm,),
        in_specs=[
            pl.BlockSpec((bm, d), lambda m: (m, 0)),
            pl.BlockSpec((bm, e), lambda m: (m, 0)),
            _resident((1, d)),
            _resident((d, d)),
            _resident((e, d)),
            _resident((1, d)),
        ],
        out_specs=pl.BlockSpec((bm, d), lambda m: (m, 0)),
        out_shape=jax.ShapeDtypeStruct((s, d), F32),
        compiler_params=_params("arbitrary"),
        name="ple_final",
    )(x, p, ln_w, w_gate, w_ple, ln_final)


def kernel(x, p, ln_mix_w, w_in, pool_mix_w, pool_scale, hgrn_lb_logits, hgrn_norm_w, w_branch_a,
           w_branch_b, w_out, ln_ffn_w, w_up, conv_w, conv_b, w_down, ln_ple_w, w_ple_gate, w_ple,
           ln_final_w):
    batch, seq, d = x.shape
    depth = w_in.shape[0]
    c_pool = w_branch_a.shape[1]
    c_hgrn = w_branch_b.shape[1]
    blk = c_hgrn
    assert c_pool == blk and d == 2 * blk, "column blocks of the combined projection must line up"
    assert seq % 2048 == 0, "row tiles of up to 2048 rows must divide the sequence"
    tf = 512
    outs = []
    for bi in range(batch):
        xb = x[bi]
        for i in range(depth):
            proj = _inproj(xb, ln_mix_w[i][None], w_in[i], bm=2048, bn=blk)
            mix = pool_mix_w[i]
            og, (mix_b, wa_b, wu_b) = _hgrn(
                proj, hgrn_lb_logits, hgrn_norm_w[i][None],
                [mix.reshape(-1, mix.shape[-1]), w_branch_a[i], w_up[i]],
                bt=512, layer=i, q_block=1, f_block=2, i_block=3, g_block=4)
            ya, (wb_b, wo_b) = _pool_branch(
                proj, mix_b.reshape(mix.shape), pool_scale[i][None], wa_b,
                [w_branch_b[i], w_out[i]], bm=1024, bn=blk, ga_block0=5)
            xb, (wd_b, wg_b, wp_b) = _merge(og, proj, ya, xb, wb_b, wo_b,
                                            [w_down[i], w_ple_gate[i], w_ple[i]], bm=512, gb_block0=7)
            xb = _ffn(xb, ln_ffn_w[i][None], wu_b, conv_w[i], conv_b[i][None], wd_b, bm=1024, tf=512)
            xb = _ple(xb, p[i, bi], ln_ple_w[i][None], wg_b, wp_b, ln_final_w[None], bm=1024,
                      final=(i == depth - 1))
        outs.append(xb)
    return jnp.stack(outs, axis=0)
```
